```python
import math
import jax, jax.numpy as jnp
from jax import lax
import numpy as np

D_MODEL = 1024
BATCH = 8
SEQ = 4096
DEPTH = 2

N_EVEN = (DEPTH + 1) // 2
N_ODD = DEPTH // 2
CHUNK = 64
CONV_W = 4
A_HEADS = 4
A_DK = 128
A_DV = D_MODEL // 2 // A_HEADS
A_KW = A_HEADS * A_DK
A_VW = A_HEADS * A_DV
B_HEADS = 4
B_DH = D_MODEL // 2 // B_HEADS
B_WIDTH = B_HEADS * B_DH
QK_BLOCK = 4
AB_SIZES = (A_KW, A_KW, A_VW, A_VW, B_WIDTH, B_WIDTH, B_WIDTH, 2 * B_HEADS)
AB_IN = 2 * A_KW + 2 * A_VW + 3 * B_WIDTH + 2 * B_HEADS
AB_MIX = A_VW + B_WIDTH
D_RNN = D_MODEL
C_BLOCKS = 8
C_BLOCK = D_RNN // C_BLOCKS
RG_C = 8.0
D_FF = 256 * math.ceil(8 * D_MODEL / 3 / 256)
FFN_RES_W = 0.5
ALPHA = (2 * DEPTH) ** 0.25
BETA = (8 * DEPTH) ** -0.25
N_ADA = 9

kernel_name = 'hybrid_hgrn2_mlstm_rglru_macaron_deepnorm'


def layer_norm(x, g, b, eps=1e-5):
    xf = x.astype(jnp.float32)
    mu = jnp.mean(xf, -1, keepdims=True)
    var = jnp.mean(jnp.square(xf - mu), -1, keepdims=True)
    return ((xf - mu) * lax.rsqrt(var + eps) * g + b).astype(x.dtype)


def head_norm(x, g, n_heads, center, eps=1e-6):
    bsz, s, w = x.shape
    xf = x.astype(jnp.float32).reshape(bsz, s, n_heads, w // n_heads)
    if center:
        xf = xf - jnp.mean(xf, -1, keepdims=True)
    xf = xf * lax.rsqrt(jnp.mean(jnp.square(xf), -1, keepdims=True) + eps)
    return (xf.reshape(bsz, s, w) * g).astype(x.dtype)


def causal_conv(x, w, b):
    ch = x.shape[-1]
    y = lax.conv_general_dilated(x, w[:, None, :], window_strides=(1,), padding=[(CONV_W - 1, 0)],
                                 dimension_numbers=('NWC', 'WIO', 'NWC'), feature_group_count=ch)
    return y + b


def to_chunks(x, n_heads):
    bsz, s, w = x.shape
    return x.reshape(bsz, s // CHUNK, CHUNK, n_heads, w // n_heads).transpose(1, 0, 3, 2, 4)


def gate_to_chunks(g):
    bsz, s, h = g.shape
    return g.reshape(bsz, s // CHUNK, CHUNK, h).transpose(1, 0, 3, 2)


def from_chunks(y):
    n, bsz, h, cl, d = y.shape
    return y.transpose(1, 0, 3, 2, 4).reshape(bsz, n * cl, h * d)


def hgrn2_scan(q, k, v, logf):
    _, bsz, h, cl, dk = q.shape
    dv = v.shape[-1]
    mask = jnp.tril(jnp.ones((cl, cl), bool))[:, :, None]

    def step(state, inp):
        q_c, k_c, v_c, lf = inp
        b = jnp.cumsum(lf, axis=2)
        inter = jnp.einsum('bhtk,bhkv->bhtv', q_c * jnp.exp(b), state)
        decay = jnp.exp(jnp.where(mask, b[:, :, :, None, :] - b[:, :, None, :, :], -jnp.inf))
        att = jnp.einsum('bhtk,bhsk,bhtsk->bhts', q_c, k_c, decay)
        intra = jnp.einsum('bhts,bhsv->bhtv', att, v_c)
        b_end = b[:, :, -1:, :]
        new_state = jnp.exp(b_end[:, :, 0])[..., None] * state + jnp.einsum(
            'bhsk,bhsv->bhkv', k_c * jnp.exp(b_end - b), v_c)
        return new_state, inter + intra

    s0 = jnp.zeros((bsz, h, dk, dv), jnp.float32)
    _, o = lax.scan(step, s0, (q, k, v, logf))
    return o


def mlstm_scan(q, k, v, logi, logf):
    _, bsz, h, cl, dk = q.shape
    dv = v.shape[-1]
    mask = jnp.tril(jnp.ones((cl, cl), bool))

    def step(carry, inp):
        c_st, n_st, m_st = carry
        q_c, k_c, v_c, li, lf = inp
        b = jnp.cumsum(lf, axis=-1)
        d_mat = jnp.where(mask, b[..., :, None] - b[..., None, :] + li[..., None, :], -jnp.inf)
        g_inter = b + m_st[..., None]
        m_t = jnp.maximum(g_inter, jnp.max(d_mat, -1))
        w_inter = jnp.exp(g_inter - m_t)
        aw = jnp.exp(d_mat - m_t[..., None]) * jnp.einsum('bhtd,bhsd->bhts', q_c, k_c)
        num = w_inter[..., None] * jnp.einsum('bhtd,bhde->bhte', q_c, c_st) + jnp.einsum('bhts,bhse->bhte', aw, v_c)
        den = w_inter * jnp.einsum('bhtd,bhd->bht', q_c, n_st) + jnp.sum(aw, -1)
        h_out = num / jnp.maximum(jnp.abs(den), jnp.exp(-m_t))[..., None]
        g_state = b[..., -1] + m_st
        s_w = b[..., -1:] - b + li
        m_new = jnp.maximum(g_state, jnp.max(s_w, -1))
        w_s = jnp.exp(s_w - m_new[..., None])
        dec = jnp.exp(g_state - m_new)
        c_new = dec[..., None, None] * c_st + jnp.einsum('bhs,bhsd,bhse->bhde', w_s, k_c, v_c)
        n_new = dec[..., None] * n_st + jnp.einsum('bhs,bhsd->bhd', w_s, k_c)
        return (c_new, n_new, m_new), h_out

    init = (jnp.zeros((bsz, h, dk, dv), jnp.float32), jnp.zeros((bsz, h, dk), jnp.float32),
            jnp.zeros((bsz, h), jnp.float32))
    _, hs = lax.scan(step, init, (q, k, v, logi, logf))
    return hs


def hgrn2_mlstm_mixer(t, lb, w_in, w_out, hgrn_g, conv_w, conv_b, wq, wk, gate_b, skip, mnorm_g):
    bsz, s, _ = t.shape
    f32 = jnp.float32
    idx = list(np.cumsum(AB_SIZES)[:-1])
    a_q, a_f, a_i, a_g, b_x, b_v, b_z, b_gate = jnp.split(t @ w_in, idx, axis=-1)
    f = lb + (1.0 - lb) * jax.nn.sigmoid(a_f.astype(f32))
    q_a = jax.nn.silu(a_q.astype(f32))
    o_a = from_chunks(hgrn2_scan(to_chunks(q_a, A_HEADS), to_chunks(1.0 - f, A_HEADS),
                                 to_chunks(a_i.astype(f32), A_HEADS), to_chunks(jnp.log(f), A_HEADS)))
    y_a = head_norm(o_a.astype(t.dtype), hgrn_g, A_HEADS, center=False) * jax.nn.silu(a_g)
    xc = jax.nn.silu(causal_conv(b_x, conv_w, conv_b))
    xcb = xc.reshape(bsz, s, B_WIDTH // QK_BLOCK, QK_BLOCK)
    q_b = jnp.einsum('bsnj,nij->bsni', xcb, wq).reshape(bsz, s, B_WIDTH).astype(f32)
    k_b = (jnp.einsum('bsnj,nij->bsni', xcb, wk).reshape(bsz, s, B_WIDTH) * B_DH ** -0.5).astype(f32)
    gates = b_gate.astype(f32) + gate_b
    logi = gates[..., :B_HEADS]
    logf = jax.nn.log_sigmoid(gates[..., B_HEADS:])
    h_b = from_chunks(mlstm_scan(to_chunks(q_b, B_HEADS), to_chunks(k_b, B_HEADS),
                                 to_chunks(b_v.astype(f32), B_HEADS), gate_to_chunks(logi), gate_to_chunks(logf)))
    y_b = (head_norm(h_b.astype(t.dtype), mnorm_g, B_HEADS, center=True) + skip * xc) * jax.nn.silu(b_z)
    return jnp.concatenate([y_a, y_b], axis=-1) @ w_out


def rglru_mixer(t, w_in, conv_w, conv_b, wa, ba, wx, bx, lam, w_out):
    bsz, s, _ = t.shape
    f32 = jnp.float32
    y_br, x_br = jnp.split(t @ w_in, 2, axis=-1)
    gate = jax.nn.gelu(y_br)
    xr = causal_conv(x_br, conv_w, conv_b)
    xb = xr.reshape(bsz, s, C_BLOCKS, C_BLOCK)
    r = jax.nn.sigmoid((jnp.einsum('bsnj,nij->bsni', xb, wa).reshape(bsz, s, D_RNN) + ba).astype(f32))
    i = jax.nn.sigmoid((jnp.einsum('bsnj,nij->bsni', xb, wx).reshape(bsz, s, D_RNN) + bx).astype(f32))
    log_a = -RG_C * r * jax.nn.softplus(-lam.astype(f32))
    a = jnp.exp(log_a)
    u = jnp.sqrt(-jnp.expm1(2.0 * log_a)) * i * xr.astype(f32)

    def combine(left, right):
        a1, b1 = left
        a2, b2 = right
        return a1 * a2, a2 * b1 + b2

    _, hs = lax.associative_scan(combine, (a, u), axis=1)
    return (hs.astype(t.dtype) * gate) @ w_out


def swiglu(t, w1, w3, w2):
    return (jax.nn.silu(t @ w1) * (t @ w3)) @ w2


def residual_sublayer(x, mod, fn, weight, g, b):
    shift, scale, gate = mod[:, 0, None, :], mod[:, 1, None, :], mod[:, 2, None, :]
    y = fn(x * (1.0 + scale) + shift)
    return layer_norm(ALPHA * x + weight * (1.0 + gate) * y, g, b)


def _fwd_setup_inputs(seed: int = 0) -> dict:
    key = jax.random.key(seed)
    ks = jax.random.split(key, 32)
    f32 = jnp.float32

    def nrm(k, shape, scale):
        return jax.random.normal(k, shape, f32) * scale

    d = D_MODEL
    u = jax.random.uniform(ks[28], (N_ODD, D_RNN), f32, minval=0.9, maxval=0.999)
    p = u ** (1.0 / RG_C)
    return {
        'x': nrm(ks[0], (BATCH, SEQ, d), 1.0),
        'c': nrm(ks[1], (BATCH, d), 1.0),
        'ada_w': nrm(ks[2], (DEPTH, d, N_ADA * d), 0.1 * d ** -0.5),
        'ada_b': nrm(ks[3], (DEPTH, N_ADA * d), 0.02),
        'ln_g': 1.0 + nrm(ks[4], (DEPTH, 3, d), 0.02),
        'ln_b': nrm(ks[5], (DEPTH, 3, d), 0.02),
        'ffn_w1': nrm(ks[6], (DEPTH, 2, d, D_FF), d ** -0.5),
        'ffn_w3': nrm(ks[7], (DEPTH, 2, d, D_FF), d ** -0.5),
        'ffn_w2': nrm(ks[8], (DEPTH, 2, D_FF, d), BETA * D_FF ** -0.5),
        'hgrn_lb_logits': nrm(ks[9], (DEPTH + 1, A_KW), 0.1),
        'ab_w_in': nrm(ks[10], (N_EVEN, d, AB_IN), d ** -0.5),
        'ab_w_out': nrm(ks[11], (N_EVEN, AB_MIX, d), BETA * AB_MIX ** -0.5),
        'hgrn_norm_g': 1.0 + nrm(ks[12], (N_EVEN, A_VW), 0.02),
        'mlstm_conv_w': nrm(ks[13], (N_EVEN, CONV_W, B_WIDTH), CONV_W ** -0.5),
        'mlstm_conv_b': nrm(ks[14], (N_EVEN, B_WIDTH), 0.02),
        'mlstm_wq': nrm(ks[15], (N_EVEN, B_WIDTH // QK_BLOCK, QK_BLOCK, QK_BLOCK), QK_BLOCK ** -0.5),
        'mlstm_wk': nrm(ks[16], (N_EVEN, B_WIDTH // QK_BLOCK, QK_BLOCK, QK_BLOCK), QK_BLOCK ** -0.5),
        'mlstm_gate_b': jnp.concatenate([nrm(ks[17], (N_EVEN, B_HEADS), 0.1),
                                         jnp.linspace(3.0, 6.0, B_HEADS, dtype=f32)[None, :]
                                         + nrm(ks[18], (N_EVEN, B_HEADS), 0.1)], axis=-1),
        'mlstm_skip': 1.0 + nrm(ks[19], (N_EVEN, B_WIDTH), 0.02),
        'mlstm_norm_g': 1.0 + nrm(ks[20], (N_EVEN, B_WIDTH), 0.02),
        'rglru_w_in': nrm(ks[21], (N_ODD, d, 2 * D_RNN), d ** -0.5),
        'rglru_conv_w': nrm(ks[22], (N_ODD, CONV_W, D_RNN), CONV_W ** -0.5),
        'rglru_conv_b': nrm(ks[23], (N_ODD, D_RNN), 0.02),
        'rglru_wa': nrm(ks[24], (N_ODD, C_BLOCKS, C_BLOCK, C_BLOCK), C_BLOCK ** -0.5),
        'rglru_ba': nrm(ks[25], (N_ODD, D_RNN), 0.02),
        'rglru_wx': nrm(ks[26], (N_ODD, C_BLOCKS, C_BLOCK, C_BLOCK), C_BLOCK ** -0.5),
        'rglru_bx': nrm(ks[27], (N_ODD, D_RNN), 0.02),
        'rglru_lambda': jnp.log(p) - jnp.log1p(-p),
        'rglru_w_out': nrm(ks[29], (N_ODD, D_RNN, d), BETA * D_RNN ** -0.5),
    }


def _fwd_reference(x, c, ada_w, ada_b, ln_g, ln_b, ffn_w1, ffn_w3, ffn_w2, hgrn_lb_logits,
              ab_w_in, ab_w_out, hgrn_norm_g, mlstm_conv_w, mlstm_conv_b, mlstm_wq, mlstm_wk,
              mlstm_gate_b, mlstm_skip, mlstm_norm_g, rglru_w_in, rglru_conv_w, rglru_conv_b,
              rglru_wa, rglru_ba, rglru_wx, rglru_bx, rglru_lambda, rglru_w_out):
    bsz = x.shape[0]
    lb_all = jnp.cumsum(jax.nn.softmax(hgrn_lb_logits.astype(jnp.float32), axis=0), axis=0)
    c_act = jax.nn.silu(c)
    for layer in range(DEPTH):
        ada = (c_act @ ada_w[layer] + ada_b[layer]).reshape(bsz, 3, 3, D_MODEL)
        x = residual_sublayer(x, ada[:, 0], lambda t: swiglu(t, ffn_w1[layer, 0], ffn_w3[layer, 0], ffn_w2[layer, 0]),
                              FFN_RES_W, ln_g[layer, 0], ln_b[layer, 0])
        e = layer // 2
        if layer % 2 == 0:
            mixer = lambda t: hgrn2_mlstm_mixer(t, lb_all[layer], ab_w_in[e], ab_w_out[e], hgrn_norm_g[e],
                                                mlstm_conv_w[e], mlstm_conv_b[e], mlstm_wq[e], mlstm_wk[e],
                                                mlstm_gate_b[e], mlstm_skip[e], mlstm_norm_g[e])
        else:
            mixer = lambda t: rglru_mixer(t, rglru_w_in[e], rglru_conv_w[e], rglru_conv_b[e], rglru_wa[e],
                                          rglru_ba[e], rglru_wx[e], rglru_bx[e], rglru_lambda[e], rglru_w_out[e])
        x = residual_sublayer(x, ada[:, 1], mixer, 1.0, ln_g[layer, 1], ln_b[layer, 1])
        x = residual_sublayer(x, ada[:, 2], lambda t: swiglu(t, ffn_w1[layer, 1], ffn_w3[layer, 1], ffn_w2[layer, 1]),
                              FFN_RES_W, ln_g[layer, 2], ln_b[layer, 2])
    return x


import jax as _jax
import jax.numpy as _jnp

TWIN_FORMAT = 'train_step'
FWD_PARAMS = ['x', 'c', 'ada_w', 'ada_b', 'ln_g', 'ln_b', 'ffn_w1', 'ffn_w3', 'ffn_w2', 'hgrn_lb_logits', 'ab_w_in', 'ab_w_out', 'hgrn_norm_g', 'mlstm_conv_w', 'mlstm_conv_b', 'mlstm_wq', 'mlstm_wk', 'mlstm_gate_b', 'mlstm_skip', 'mlstm_norm_g', 'rglru_w_in', 'rglru_conv_w', 'rglru_conv_b', 'rglru_wa', 'rglru_ba', 'rglru_wx', 'rglru_bx', 'rglru_lambda', 'rglru_w_out']
TWIN_WEIGHTS = ['ada_w', 'ada_b', 'ln_g', 'ln_b', 'ffn_w1', 'ffn_w3', 'ffn_w2', 'hgrn_lb_logits', 'ab_w_in', 'ab_w_out', 'hgrn_norm_g', 'mlstm_conv_w', 'mlstm_conv_b', 'mlstm_wq', 'mlstm_wk', 'mlstm_gate_b', 'mlstm_skip', 'mlstm_norm_g', 'rglru_w_in', 'rglru_conv_w', 'rglru_conv_b', 'rglru_wa', 'rglru_ba', 'rglru_wx', 'rglru_bx', 'rglru_lambda', 'rglru_w_out']
TWIN_DIFF_INPUT = 'x'
TWIN_INPUTS = ['x', 'c', 'ada_w', 'ada_b', 'ln_g', 'ln_b', 'ffn_w1', 'ffn_w3', 'ffn_w2', 'hgrn_lb_logits', 'ab_w_in', 'ab_w_out', 'hgrn_norm_g', 'mlstm_conv_w', 'mlstm_conv_b', 'mlstm_wq', 'mlstm_wk', 'mlstm_gate_b', 'mlstm_skip', 'mlstm_norm_g', 'rglru_w_in', 'rglru_conv_w', 'rglru_conv_b', 'rglru_wa', 'rglru_ba', 'rglru_wx', 'rglru_bx', 'rglru_lambda', 'rglru_w_out', 'loss_target', 'm_ada_w', 'm_ada_b', 'm_ln_g', 'm_ln_b', 'm_ffn_w1', 'm_ffn_w3', 'm_ffn_w2', 'm_hgrn_lb_logits', 'm_ab_w_in', 'm_ab_w_out', 'm_hgrn_norm_g', 'm_mlstm_conv_w', 'm_mlstm_conv_b', 'm_mlstm_wq', 'm_mlstm_wk', 'm_mlstm_gate_b', 'm_mlstm_skip', 'm_mlstm_norm_g', 'm_rglru_w_in', 'm_rglru_conv_w', 'm_rglru_conv_b', 'm_rglru_wa', 'm_rglru_ba', 'm_rglru_wx', 'm_rglru_bx', 'm_rglru_lambda', 'm_rglru_w_out', 'v_ada_w', 'v_ada_b', 'v_ln_g', 'v_ln_b', 'v_ffn_w1', 'v_ffn_w3', 'v_ffn_w2', 'v_hgrn_lb_logits', 'v_ab_w_in', 'v_ab_w_out', 'v_hgrn_norm_g', 'v_mlstm_conv_w', 'v_mlstm_conv_b', 'v_mlstm_wq', 'v_mlstm_wk', 'v_mlstm_gate_b', 'v_mlstm_skip', 'v_mlstm_norm_g', 'v_rglru_w_in', 'v_rglru_conv_w', 'v_rglru_conv_b', 'v_rglru_wa', 'v_rglru_ba', 'v_rglru_wx', 'v_rglru_bx', 'v_rglru_lambda', 'v_rglru_w_out']
TWIN_OUTPUTS = ['loss', 'grad_x', 'grad_ada_w', 'grad_ada_b', 'grad_ln_g', 'grad_ln_b', 'grad_ffn_w1', 'grad_ffn_w3', 'grad_ffn_w2', 'grad_hgrn_lb_logits', 'grad_ab_w_in', 'grad_ab_w_out', 'grad_hgrn_norm_g', 'grad_mlstm_conv_w', 'grad_mlstm_conv_b', 'grad_mlstm_wq', 'grad_mlstm_wk', 'grad_mlstm_gate_b', 'grad_mlstm_skip', 'grad_mlstm_norm_g', 'grad_rglru_w_in', 'grad_rglru_conv_w', 'grad_rglru_conv_b', 'grad_rglru_wa', 'grad_rglru_ba', 'grad_rglru_wx', 'grad_rglru_bx', 'grad_rglru_lambda', 'grad_rglru_w_out', 'delta_ada_w', 'delta_ada_b', 'delta_ln_g', 'delta_ln_b', 'delta_ffn_w1', 'delta_ffn_w3', 'delta_ffn_w2', 'delta_hgrn_lb_logits', 'delta_ab_w_in', 'delta_ab_w_out', 'delta_hgrn_norm_g', 'delta_mlstm_conv_w', 'delta_mlstm_conv_b', 'delta_mlstm_wq', 'delta_mlstm_wk', 'delta_mlstm_gate_b', 'delta_mlstm_skip', 'delta_mlstm_norm_g', 'delta_rglru_w_in', 'delta_rglru_conv_w', 'delta_rglru_conv_b', 'delta_rglru_wa', 'delta_rglru_ba', 'delta_rglru_wx', 'delta_rglru_bx', 'delta_rglru_lambda', 'delta_rglru_w_out', 'new_m_ada_w', 'new_m_ada_b', 'new_m_ln_g', 'new_m_ln_b', 'new_m_ffn_w1', 'new_m_ffn_w3', 'new_m_ffn_w2', 'new_m_hgrn_lb_logits', 'new_m_ab_w_in', 'new_m_ab_w_out', 'new_m_hgrn_norm_g', 'new_m_mlstm_conv_w', 'new_m_mlstm_conv_b', 'new_m_mlstm_wq', 'new_m_mlstm_wk', 'new_m_mlstm_gate_b', 'new_m_mlstm_skip', 'new_m_mlstm_norm_g', 'new_m_rglru_w_in', 'new_m_rglru_conv_w', 'new_m_rglru_conv_b', 'new_m_rglru_wa', 'new_m_rglru_ba', 'new_m_rglru_wx', 'new_m_rglru_bx', 'new_m_rglru_lambda', 'new_m_rglru_w_out', 'new_v_ada_w', 'new_v_ada_b', 'new_v_ln_g', 'new_v_ln_b', 'new_v_ffn_w1', 'new_v_ffn_w3', 'new_v_ffn_w2', 'new_v_hgrn_lb_logits', 'new_v_ab_w_in', 'new_v_ab_w_out', 'new_v_hgrn_norm_g', 'new_v_mlstm_conv_w', 'new_v_mlstm_conv_b', 'new_v_mlstm_wq', 'new_v_mlstm_wk', 'new_v_mlstm_gate_b', 'new_v_mlstm_skip', 'new_v_mlstm_norm_g', 'new_v_rglru_w_in', 'new_v_rglru_conv_w', 'new_v_rglru_conv_b', 'new_v_rglru_wa', 'new_v_rglru_ba', 'new_v_rglru_wx', 'new_v_rglru_bx', 'new_v_rglru_lambda', 'new_v_rglru_w_out']
TWIN_LEAF_KINDS = {'loss': 'loss', 'grad_x': 'grad_x', 'grad_ada_w': 'grad_w', 'grad_ada_b': 'grad_w', 'grad_ln_g': 'grad_w', 'grad_ln_b': 'grad_w', 'grad_ffn_w1': 'grad_w', 'grad_ffn_w3': 'grad_w', 'grad_ffn_w2': 'grad_w', 'grad_hgrn_lb_logits': 'grad_w', 'grad_ab_w_in': 'grad_w', 'grad_ab_w_out': 'grad_w', 'grad_hgrn_norm_g': 'grad_w', 'grad_mlstm_conv_w': 'grad_w', 'grad_mlstm_conv_b': 'grad_w', 'grad_mlstm_wq': 'grad_w', 'grad_mlstm_wk': 'grad_w', 'grad_mlstm_gate_b': 'grad_w', 'grad_mlstm_skip': 'grad_w', 'grad_mlstm_norm_g': 'grad_w', 'grad_rglru_w_in': 'grad_w', 'grad_rglru_conv_w': 'grad_w', 'grad_rglru_conv_b': 'grad_w', 'grad_rglru_wa': 'grad_w', 'grad_rglru_ba': 'grad_w', 'grad_rglru_wx': 'grad_w', 'grad_rglru_bx': 'grad_w', 'grad_rglru_lambda': 'grad_w', 'grad_rglru_w_out': 'grad_w', 'delta_ada_w': 'delta_w', 'delta_ada_b': 'delta_w', 'delta_ln_g': 'delta_w', 'delta_ln_b': 'delta_w', 'delta_ffn_w1': 'delta_w', 'delta_ffn_w3': 'delta_w', 'delta_ffn_w2': 'delta_w', 'delta_hgrn_lb_logits': 'delta_w', 'delta_ab_w_in': 'delta_w', 'delta_ab_w_out': 'delta_w', 'delta_hgrn_norm_g': 'delta_w', 'delta_mlstm_conv_w': 'delta_w', 'delta_mlstm_conv_b': 'delta_w', 'delta_mlstm_wq': 'delta_w', 'delta_mlstm_wk': 'delta_w', 'delta_mlstm_gate_b': 'delta_w', 'delta_mlstm_skip': 'delta_w', 'delta_mlstm_norm_g': 'delta_w', 'delta_rglru_w_in': 'delta_w', 'delta_rglru_conv_w': 'delta_w', 'delta_rglru_conv_b': 'delta_w', 'delta_rglru_wa': 'delta_w', 'delta_rglru_ba': 'delta_w', 'delta_rglru_wx': 'delta_w', 'delta_rglru_bx': 'delta_w', 'delta_rglru_lambda': 'delta_w', 'delta_rglru_w_out': 'delta_w', 'new_m_ada_w': 'new_m', 'new_m_ada_b': 'new_m', 'new_m_ln_g': 'new_m', 'new_m_ln_b': 'new_m', 'new_m_ffn_w1': 'new_m', 'new_m_ffn_w3': 'new_m', 'new_m_ffn_w2': 'new_m', 'new_m_hgrn_lb_logits': 'new_m', 'new_m_ab_w_in': 'new_m', 'new_m_ab_w_out': 'new_m', 'new_m_hgrn_norm_g': 'new_m', 'new_m_mlstm_conv_w': 'new_m', 'new_m_mlstm_conv_b': 'new_m', 'new_m_mlstm_wq': 'new_m', 'new_m_mlstm_wk': 'new_m', 'new_m_mlstm_gate_b': 'new_m', 'new_m_mlstm_skip': 'new_m', 'new_m_mlstm_norm_g': 'new_m', 'new_m_rglru_w_in': 'new_m', 'new_m_rglru_conv_w': 'new_m', 'new_m_rglru_conv_b': 'new_m', 'new_m_rglru_wa': 'new_m', 'new_m_rglru_ba': 'new_m', 'new_m_rglru_wx': 'new_m', 'new_m_rglru_bx': 'new_m', 'new_m_rglru_lambda': 'new_m', 'new_m_rglru_w_out': 'new_m', 'new_v_ada_w': 'new_v', 'new_v_ada_b': 'new_v', 'new_v_ln_g': 'new_v', 'new_v_ln_b': 'new_v', 'new_v_ffn_w1': 'new_v', 'new_v_ffn_w3': 'new_v', 'new_v_ffn_w2': 'new_v', 'new_v_hgrn_lb_logits': 'new_v', 'new_v_ab_w_in': 'new_v', 'new_v_ab_w_out': 'new_v', 'new_v_hgrn_norm_g': 'new_v', 'new_v_mlstm_conv_w': 'new_v', 'new_v_mlstm_conv_b': 'new_v', 'new_v_mlstm_wq': 'new_v', 'new_v_mlstm_wk': 'new_v', 'new_v_mlstm_gate_b': 'new_v', 'new_v_mlstm_skip': 'new_v', 'new_v_mlstm_norm_g': 'new_v', 'new_v_rglru_w_in': 'new_v', 'new_v_rglru_conv_w': 'new_v', 'new_v_rglru_conv_b': 'new_v', 'new_v_rglru_wa': 'new_v', 'new_v_rglru_ba': 'new_v', 'new_v_rglru_wx': 'new_v', 'new_v_rglru_bx': 'new_v', 'new_v_rglru_lambda': 'new_v', 'new_v_rglru_w_out': 'new_v'}


def _forward(args):
    return _fwd_reference(*[args[k] for k in FWD_PARAMS])


def _output_shape():
    out = _jax.eval_shape(lambda: _forward(_fwd_setup_inputs(0)))
    return out.shape, out.dtype

N_MICROBATCH = 1
ADAM_LR = 0.001
ADAM_B1 = 0.9
ADAM_B2 = 0.999
ADAM_EPS = 1e-08
ADAM_WD = 0.01
ADAM_STEP = 10
PER_EXAMPLE_BATCH_AXIS = {'x': 0, 'c': 0, 'loss_target': 0}
SHARED_INPUTS = []
_WEIGHT_DTYPES = {'ada_w': _jnp.float32, 'ada_b': _jnp.float32, 'ln_g': _jnp.float32, 'ln_b': _jnp.float32, 'ffn_w1': _jnp.float32, 'ffn_w3': _jnp.float32, 'ffn_w2': _jnp.float32, 'hgrn_lb_logits': _jnp.float32, 'ab_w_in': _jnp.float32, 'ab_w_out': _jnp.float32, 'hgrn_norm_g': _jnp.float32, 'mlstm_conv_w': _jnp.float32, 'mlstm_conv_b': _jnp.float32, 'mlstm_wq': _jnp.float32, 'mlstm_wk': _jnp.float32, 'mlstm_gate_b': _jnp.float32, 'mlstm_skip': _jnp.float32, 'mlstm_norm_g': _jnp.float32, 'rglru_w_in': _jnp.float32, 'rglru_conv_w': _jnp.float32, 'rglru_conv_b': _jnp.float32, 'rglru_wa': _jnp.float32, 'rglru_ba': _jnp.float32, 'rglru_wx': _jnp.float32, 'rglru_bx': _jnp.float32, 'rglru_lambda': _jnp.float32, 'rglru_w_out': _jnp.float32}
MOMENT_SCALE = {'ada_w': 5.617043e-02, 'ada_b': 9.888114e-02, 'ln_g': 1.311350e+01, 'ln_b': 8.551775e-01, 'ffn_w1': 1.205933e-02, 'ffn_w3': 1.169844e-02, 'ffn_w2': 3.890035e-02, 'hgrn_lb_logits': 1.792872e-03, 'ab_w_in': 4.142719e-02, 'ab_w_out': 8.437658e-02, 'hgrn_norm_g': 3.868345e-02, 'mlstm_conv_w': 6.504469e-02, 'mlstm_conv_b': 7.131413e-02, 'mlstm_wq': 4.647870e-02, 'mlstm_wk': 4.435401e-02, 'mlstm_gate_b': 1.538533e-01, 'mlstm_skip': 2.406341e-02, 'mlstm_norm_g': 3.894297e-02, 'rglru_w_in': 3.700162e-02, 'rglru_conv_w': 4.265652e-02, 'rglru_conv_b': 4.622794e-01, 'rglru_wa': 1.042616e-02, 'rglru_ba': 8.627019e-03, 'rglru_wx': 1.861863e-02, 'rglru_bx': 1.454578e-02, 'rglru_lambda': 1.875005e-02, 'rglru_w_out': 7.734023e-02}


def _to_microbatches(a, axis):
    t = _jnp.moveaxis(a, axis, 0)
    t = t.reshape((N_MICROBATCH, t.shape[0] // N_MICROBATCH) + t.shape[1:])
    return _jnp.moveaxis(t, 1, axis + 1)


def setup_inputs(seed: int = 0) -> dict:
    inp = _fwd_setup_inputs(seed)
    key = _jax.random.fold_in(_jax.random.key(seed), 7919)
    shape, _ = _output_shape()
    out = dict(inp)
    out["loss_target"] = _jax.random.normal(_jax.random.fold_in(key, 0), shape, _jnp.float32)
    for i, name in enumerate(TWIN_WEIGHTS):
        w = inp[name].astype(_jnp.float32)
        if MOMENT_SCALE is None:
            s = _jnp.sqrt(_jnp.mean(_jnp.square(w)) + 1e-30)
        else:
            s = MOMENT_SCALE[name]
        km, kv = _jax.random.split(_jax.random.fold_in(key, i + 1))
        out[name] = w
        out["m_" + name] = s * _jax.random.normal(km, w.shape, _jnp.float32)
        out["v_" + name] = (s * s) * _jax.random.uniform(kv, w.shape, _jnp.float32, 0.5, 1.5)
    if N_MICROBATCH > 1:
        for name, axis in PER_EXAMPLE_BATCH_AXIS.items():
            out[name] = _to_microbatches(out[name], axis)
    return {'x': out['x'], 'c': out['c'], 'ada_w': out['ada_w'], 'ada_b': out['ada_b'], 'ln_g': out['ln_g'], 'ln_b': out['ln_b'], 'ffn_w1': out['ffn_w1'], 'ffn_w3': out['ffn_w3'], 'ffn_w2': out['ffn_w2'], 'hgrn_lb_logits': out['hgrn_lb_logits'], 'ab_w_in': out['ab_w_in'], 'ab_w_out': out['ab_w_out'], 'hgrn_norm_g': out['hgrn_norm_g'], 'mlstm_conv_w': out['mlstm_conv_w'], 'mlstm_conv_b': out['mlstm_conv_b'], 'mlstm_wq': out['mlstm_wq'], 'mlstm_wk': out['mlstm_wk'], 'mlstm_gate_b': out['mlstm_gate_b'], 'mlstm_skip': out['mlstm_skip'], 'mlstm_norm_g': out['mlstm_norm_g'], 'rglru_w_in': out['rglru_w_in'], 'rglru_conv_w': out['rglru_conv_w'], 'rglru_conv_b': out['rglru_conv_b'], 'rglru_wa': out['rglru_wa'], 'rglru_ba': out['rglru_ba'], 'rglru_wx': out['rglru_wx'], 'rglru_bx': out['rglru_bx'], 'rglru_lambda': out['rglru_lambda'], 'rglru_w_out': out['rglru_w_out'], 'loss_target': out['loss_target'], 'm_ada_w': out['m_ada_w'], 'm_ada_b': out['m_ada_b'], 'm_ln_g': out['m_ln_g'], 'm_ln_b': out['m_ln_b'], 'm_ffn_w1': out['m_ffn_w1'], 'm_ffn_w3': out['m_ffn_w3'], 'm_ffn_w2': out['m_ffn_w2'], 'm_hgrn_lb_logits': out['m_hgrn_lb_logits'], 'm_ab_w_in': out['m_ab_w_in'], 'm_ab_w_out': out['m_ab_w_out'], 'm_hgrn_norm_g': out['m_hgrn_norm_g'], 'm_mlstm_conv_w': out['m_mlstm_conv_w'], 'm_mlstm_conv_b': out['m_mlstm_conv_b'], 'm_mlstm_wq': out['m_mlstm_wq'], 'm_mlstm_wk': out['m_mlstm_wk'], 'm_mlstm_gate_b': out['m_mlstm_gate_b'], 'm_mlstm_skip': out['m_mlstm_skip'], 'm_mlstm_norm_g': out['m_mlstm_norm_g'], 'm_rglru_w_in': out['m_rglru_w_in'], 'm_rglru_conv_w': out['m_rglru_conv_w'], 'm_rglru_conv_b': out['m_rglru_conv_b'], 'm_rglru_wa': out['m_rglru_wa'], 'm_rglru_ba': out['m_rglru_ba'], 'm_rglru_wx': out['m_rglru_wx'], 'm_rglru_bx': out['m_rglru_bx'], 'm_rglru_lambda': out['m_rglru_lambda'], 'm_rglru_w_out': out['m_rglru_w_out'], 'v_ada_w': out['v_ada_w'], 'v_ada_b': out['v_ada_b'], 'v_ln_g': out['v_ln_g'], 'v_ln_b': out['v_ln_b'], 'v_ffn_w1': out['v_ffn_w1'], 'v_ffn_w3': out['v_ffn_w3'], 'v_ffn_w2': out['v_ffn_w2'], 'v_hgrn_lb_logits': out['v_hgrn_lb_logits'], 'v_ab_w_in': out['v_ab_w_in'], 'v_ab_w_out': out['v_ab_w_out'], 'v_hgrn_norm_g': out['v_hgrn_norm_g'], 'v_mlstm_conv_w': out['v_mlstm_conv_w'], 'v_mlstm_conv_b': out['v_mlstm_conv_b'], 'v_mlstm_wq': out['v_mlstm_wq'], 'v_mlstm_wk': out['v_mlstm_wk'], 'v_mlstm_gate_b': out['v_mlstm_gate_b'], 'v_mlstm_skip': out['v_mlstm_skip'], 'v_mlstm_norm_g': out['v_mlstm_norm_g'], 'v_rglru_w_in': out['v_rglru_w_in'], 'v_rglru_conv_w': out['v_rglru_conv_w'], 'v_rglru_conv_b': out['v_rglru_conv_b'], 'v_rglru_wa': out['v_rglru_wa'], 'v_rglru_ba': out['v_rglru_ba'], 'v_rglru_wx': out['v_rglru_wx'], 'v_rglru_bx': out['v_rglru_bx'], 'v_rglru_lambda': out['v_rglru_lambda'], 'v_rglru_w_out': out['v_rglru_w_out']}


def _loss(weights, diff, rest, loss_target):
    with _jax.named_scope("forward"):
        args = {**rest, TWIN_DIFF_INPUT: diff, **{k: w.astype(_WEIGHT_DTYPES[k]) for k, w in weights.items()}}
        y = _forward(args)
    with _jax.named_scope("loss_head"):
        err = _jnp.square(y.astype(_jnp.float32) - loss_target)
        return 0.5 * _jnp.sum(_jnp.mean(err, axis=-1)) if err.ndim else 0.5 * err


def _adamw(w, g, m, v):
    m = ADAM_B1 * m + (1.0 - ADAM_B1) * g
    v = ADAM_B2 * v + (1.0 - ADAM_B2) * _jnp.square(g)
    m_hat = m / (1.0 - ADAM_B1 ** ADAM_STEP)
    v_hat = v / (1.0 - ADAM_B2 ** ADAM_STEP)
    delta = -ADAM_LR * (m_hat / (_jnp.sqrt(v_hat) + ADAM_EPS) + ADAM_WD * w)
    return delta, m, v


def reference(x, c, ada_w, ada_b, ln_g, ln_b, ffn_w1, ffn_w3, ffn_w2, hgrn_lb_logits, ab_w_in, ab_w_out, hgrn_norm_g, mlstm_conv_w, mlstm_conv_b, mlstm_wq, mlstm_wk, mlstm_gate_b, mlstm_skip, mlstm_norm_g, rglru_w_in, rglru_conv_w, rglru_conv_b, rglru_wa, rglru_ba, rglru_wx, rglru_bx, rglru_lambda, rglru_w_out, loss_target, m_ada_w, m_ada_b, m_ln_g, m_ln_b, m_ffn_w1, m_ffn_w3, m_ffn_w2, m_hgrn_lb_logits, m_ab_w_in, m_ab_w_out, m_hgrn_norm_g, m_mlstm_conv_w, m_mlstm_conv_b, m_mlstm_wq, m_mlstm_wk, m_mlstm_gate_b, m_mlstm_skip, m_mlstm_norm_g, m_rglru_w_in, m_rglru_conv_w, m_rglru_conv_b, m_rglru_wa, m_rglru_ba, m_rglru_wx, m_rglru_bx, m_rglru_lambda, m_rglru_w_out, v_ada_w, v_ada_b, v_ln_g, v_ln_b, v_ffn_w1, v_ffn_w3, v_ffn_w2, v_hgrn_lb_logits, v_ab_w_in, v_ab_w_out, v_hgrn_norm_g, v_mlstm_conv_w, v_mlstm_conv_b, v_mlstm_wq, v_mlstm_wk, v_mlstm_gate_b, v_mlstm_skip, v_mlstm_norm_g, v_rglru_w_in, v_rglru_conv_w, v_rglru_conv_b, v_rglru_wa, v_rglru_ba, v_rglru_wx, v_rglru_bx, v_rglru_lambda, v_rglru_w_out):
    given = dict(x=x, c=c, ada_w=ada_w, ada_b=ada_b, ln_g=ln_g, ln_b=ln_b, ffn_w1=ffn_w1, ffn_w3=ffn_w3, ffn_w2=ffn_w2, hgrn_lb_logits=hgrn_lb_logits, ab_w_in=ab_w_in, ab_w_out=ab_w_out, hgrn_norm_g=hgrn_norm_g, mlstm_conv_w=mlstm_conv_w, mlstm_conv_b=mlstm_conv_b, mlstm_wq=mlstm_wq, mlstm_wk=mlstm_wk, mlstm_gate_b=mlstm_gate_b, mlstm_skip=mlstm_skip, mlstm_norm_g=mlstm_norm_g, rglru_w_in=rglru_w_in, rglru_conv_w=rglru_conv_w, rglru_conv_b=rglru_conv_b, rglru_wa=rglru_wa, rglru_ba=rglru_ba, rglru_wx=rglru_wx, rglru_bx=rglru_bx, rglru_lambda=rglru_lambda, rglru_w_out=rglru_w_out, loss_target=loss_target, m_ada_w=m_ada_w, m_ada_b=m_ada_b, m_ln_g=m_ln_g, m_ln_b=m_ln_b, m_ffn_w1=m_ffn_w1, m_ffn_w3=m_ffn_w3, m_ffn_w2=m_ffn_w2, m_hgrn_lb_logits=m_hgrn_lb_logits, m_ab_w_in=m_ab_w_in, m_ab_w_out=m_ab_w_out, m_hgrn_norm_g=m_hgrn_norm_g, m_mlstm_conv_w=m_mlstm_conv_w, m_mlstm_conv_b=m_mlstm_conv_b, m_mlstm_wq=m_mlstm_wq, m_mlstm_wk=m_mlstm_wk, m_mlstm_gate_b=m_mlstm_gate_b, m_mlstm_skip=m_mlstm_skip, m_mlstm_norm_g=m_mlstm_norm_g, m_rglru_w_in=m_rglru_w_in, m_rglru_conv_w=m_rglru_conv_w, m_rglru_conv_b=m_rglru_conv_b, m_rglru_wa=m_rglru_wa, m_rglru_ba=m_rglru_ba, m_rglru_wx=m_rglru_wx, m_rglru_bx=m_rglru_bx, m_rglru_lambda=m_rglru_lambda, m_rglru_w_out=m_rglru_w_out, v_ada_w=v_ada_w, v_ada_b=v_ada_b, v_ln_g=v_ln_g, v_ln_b=v_ln_b, v_ffn_w1=v_ffn_w1, v_ffn_w3=v_ffn_w3, v_ffn_w2=v_ffn_w2, v_hgrn_lb_logits=v_hgrn_lb_logits, v_ab_w_in=v_ab_w_in, v_ab_w_out=v_ab_w_out, v_hgrn_norm_g=v_hgrn_norm_g, v_mlstm_conv_w=v_mlstm_conv_w, v_mlstm_conv_b=v_mlstm_conv_b, v_mlstm_wq=v_mlstm_wq, v_mlstm_wk=v_mlstm_wk, v_mlstm_gate_b=v_mlstm_gate_b, v_mlstm_skip=v_mlstm_skip, v_mlstm_norm_g=v_mlstm_norm_g, v_rglru_w_in=v_rglru_w_in, v_rglru_conv_w=v_rglru_conv_w, v_rglru_conv_b=v_rglru_conv_b, v_rglru_wa=v_rglru_wa, v_rglru_ba=v_rglru_ba, v_rglru_wx=v_rglru_wx, v_rglru_bx=v_rglru_bx, v_rglru_lambda=v_rglru_lambda, v_rglru_w_out=v_rglru_w_out)
    weights = {n: given[n] for n in TWIN_WEIGHTS}
    shared = {n: given[n] for n in SHARED_INPUTS}
    per_example = {n: given[n] for n in ['x', 'c']}
    grad_fn = _jax.value_and_grad(_loss, argnums=(0, 1))

    def one_microbatch(ex, loss_target):
        ex = dict(ex)
        diff = ex.pop(TWIN_DIFF_INPUT)
        return grad_fn(weights, diff, {**shared, **ex}, loss_target)

    if N_MICROBATCH == 1:
        loss, (grad_w, grad_x) = one_microbatch(per_example, given["loss_target"])
    else:
        def body(carry, xs):
            loss_sum, grad_sum = carry
            l_k, (gw_k, gx_k) = one_microbatch(xs[0], xs[1])
            with _jax.named_scope("update"):
                return (loss_sum + l_k, _jax.tree.map(_jnp.add, grad_sum, gw_k)), gx_k

        init = (_jnp.zeros((), _jnp.float32), _jax.tree.map(_jnp.zeros_like, weights))
        (loss, grad_w), grad_x = _jax.lax.scan(body, init, (per_example, given["loss_target"]))
    with _jax.named_scope("update"):
        delta_w, new_m, new_v = {}, {}, {}
        for n in TWIN_WEIGHTS:
            delta_w[n], new_m[n], new_v[n] = _adamw(weights[n], grad_w[n], given["m_" + n], given["v_" + n])
    return (loss, grad_x, *[grad_w[n] for n in TWIN_WEIGHTS], *[delta_w[n] for n in TWIN_WEIGHTS],
            *[new_m[n] for n in TWIN_WEIGHTS], *[new_v[n] for n in TWIN_WEIGHTS])
```

```python
import functools

import jax
import jax.numpy as jnp
import numpy as np
from jax import lax
from jax.experimental import pallas as pl
from jax.experimental.pallas import tpu as pltpu

F32 = jnp.float32
BF16 = jnp.bfloat16
HIGHEST = lax.Precision.HIGHEST

NDEV = 8
D_MODEL = 1024
D_FF = 2816
DEPTH = 2
CHUNK = 64
SUB = 16
HEADS = 4
HEAD_W = 128
MIX_W = HEADS * HEAD_W
AB_MAIN = 7 * MIX_W
AB_ALL = AB_MAIN + 128
CONV_W = 4
C_BLOCKS = 8
RG_C = 8.0
ALPHA = (2 * DEPTH) ** 0.25
FFN_RES_W = 0.5
NEG = -1e30

ADAM_LR = 0.001
ADAM_B1 = 0.9
ADAM_B2 = 0.999
ADAM_EPS = 1e-08
ADAM_WD = 0.01
ADAM_STEP = 10

VMEM_LIMIT = 56 * 1024 * 1024
GRAD_WIRE = jnp.bfloat16

NN = ((1,), (0,))
NT = ((1,), (1,))
TN = ((0,), (0,))


def _dot(a, b, dims, precision=None):
    return lax.dot_general(a, b, (dims, ((), ())), precision=precision, preferred_element_type=F32)


def _make_mm(dims, d_lhs, d_rhs, swap_lhs, swap_rhs, prec):
    def cast(v):
        return v.astype(BF16) if prec is None else v.astype(F32)

    @jax.custom_vjp
    def mm(a, b):
        return _dot(cast(a), cast(b), dims, prec)

    def fwd(a, b):
        return mm(a, b), (a, b)

    def bwd(res, g):
        a, b = res
        g = cast(g)
        da = _dot(cast(b), g, d_lhs, prec) if swap_lhs else _dot(g, cast(b), d_lhs, prec)
        db = _dot(g, cast(a), d_rhs, prec) if swap_rhs else _dot(cast(a), g, d_rhs, prec)
        return da.astype(a.dtype), db.astype(b.dtype)

    mm.defvjp(fwd, bwd)
    return mm


def _mm_family(prec):
    return (_make_mm(NN, NT, TN, False, False, prec), _make_mm(NT, NN, TN, False, True, prec),
            _make_mm(TN, NT, NN, True, False, prec))


def _round(v):
    return v.astype(BF16).astype(F32)


@jax.custom_vjp
def row_dot(a, n):
    return jnp.sum(_round(a) * _round(n), axis=1, keepdims=True)


def _row_dot_fwd(a, n):
    return row_dot(a, n), (a, n)


def _row_dot_bwd(res, g):
    a, n = res
    g = _round(g)
    return g * _round(n), jnp.sum(g * _round(a), axis=0, keepdims=True)


row_dot.defvjp(_row_dot_fwd, _row_dot_bwd)


@jax.custom_vjp
def col_dot(s, a):
    return jnp.sum(_round(s) * _round(a), axis=0, keepdims=True)


def _col_dot_fwd(s, a):
    return col_dot(s, a), (s, a)


def _col_dot_bwd(res, g):
    s, a = res
    g = _round(g)
    return jnp.sum(g * _round(a), axis=1, keepdims=True), _round(s) * g


col_dot.defvjp(_col_dot_fwd, _col_dot_bwd)

mm_nn, mm_nt, mm_tn = _mm_family(None)
mid_nn, mid_nt, mid_tn = _mm_family(lax.Precision.HIGH)
hi_nn, hi_nt, hi_tn = _mm_family(HIGHEST)


def _silu(v):
    return v * jax.nn.sigmoid(v)


def _log_sigmoid(v):
    return jnp.minimum(v, 0.0) - jnp.log1p(jnp.exp(-jnp.abs(v)))


def _softplus(v):
    return jnp.maximum(v, 0.0) + jnp.log1p(jnp.exp(-jnp.abs(v)))


def _neg_expm1(v):
    series = -v * (1.0 + v * (0.5 + v * (1.0 / 6.0 + v * (1.0 / 24.0 + v * (1.0 / 120.0)))))
    return jnp.where(v > -0.05, series, 1.0 - jnp.exp(v))


def _params(**kw):
    return pltpu.CompilerParams(vmem_limit_bytes=VMEM_LIMIT, **kw)


class Rows:
    def __init__(self, arr, block=None, split=None):
        self.arr = arr
        self.block = block
        self.split = split

    @property
    def width(self):
        return self.block[1] if self.block else self.arr.shape[1]


def _load(ref, split):
    if ref.ndim == 3:
        return [ref[k].astype(F32) for k in range(ref.shape[0])]
    if split is None:
        return ref[...].astype(F32)
    return [ref[:, k * split:(k + 1) * split].astype(F32) for k in range(ref.shape[1] // split)]


def _store(ref, val, accumulate=False):
    if isinstance(val, (list, tuple)):
        if ref.ndim == 3:
            for k, v in enumerate(val):
                ref[k] = (ref[k] + v if accumulate else v).astype(ref.dtype)
            return
        w = ref.shape[1] // len(val)
        for k, v in enumerate(val):
            sl = slice(k * w, (k + 1) * w)
            ref[:, sl] = (ref[:, sl] + v if accumulate else v).astype(ref.dtype)
    else:
        ref[...] = (ref[...] + val if accumulate else val).astype(ref.dtype)


def rowwise(fn, rows, params, out_rows, out_sums, *, tile, name):
    n_rows = rows[0].arr.shape[0]
    n_r, n_p, n_o = len(rows), len(params), len(out_rows)
    splits = [r.split for r in rows]

    def body(*refs):
        r_refs, p_refs = refs[:n_r], refs[n_r:n_r + n_p]
        o_refs, s_refs = refs[n_r + n_p:n_r + n_p + n_o], refs[n_r + n_p + n_o:]
        row_out, sum_out = fn([_load(r, s) for r, s in zip(r_refs, splits)], [_load(p, None) for p in p_refs])
        for ref, val in zip(o_refs, row_out):
            _store(ref, val)
        if s_refs:
            @pl.when(pl.program_id(0) == 0)
            def _():
                for ref in s_refs:
                    ref[...] = jnp.zeros(ref.shape, ref.dtype)

            for ref, val in zip(s_refs, sum_out):
                _store(ref, val, accumulate=True)

    in_specs = []
    for r in rows:
        blk = r.block[0] if r.block else 0
        in_specs.append(pl.BlockSpec((tile, r.width), functools.partial(lambda i, b: (i, b), b=blk)))
    for p in params:
        in_specs.append(pl.BlockSpec(p.shape, functools.partial(lambda i, n: (0,) * n, n=p.ndim)))
    out_shape = [jax.ShapeDtypeStruct((n_rows, w), dt) for w, dt in out_rows]
    out_specs = [pl.BlockSpec((tile, w), lambda i: (i, 0)) for w, _ in out_rows]
    for shp in out_sums:
        out_shape.append(jax.ShapeDtypeStruct(shp, F32))
        out_specs.append(pl.BlockSpec(shp, functools.partial(lambda i, n: (0,) * n, n=len(shp))))
    res = pl.pallas_call(
        body, name=name, grid=(n_rows // tile,), in_specs=in_specs, out_specs=out_specs, out_shape=out_shape,
        compiler_params=_params(dimension_semantics=("arbitrary",)),
    )(*[r.arr for r in rows], *params)
    return res[:n_o], res[n_o:]


def rowwise_bwd(fn, rows, cots, params, want_rows, want_params, out_dtypes, *, tile, name):
    n = len(rows)

    def bwd(row_vals, param_vals):
        prim, cot = row_vals[:n], row_vals[n:]
        _, vjp = jax.vjp(lambda r, p: fn(r, p)[0], prim, param_vals)
        d_rows, d_params = vjp(cot)
        return [d_rows[i] for i in want_rows], [d_params[j] for j in want_params]

    out_rows = [(rows[i].width, dt) for i, dt in zip(want_rows, out_dtypes)]
    out_sums = [params[j].shape for j in want_params]
    return rowwise(bwd, list(rows) + list(cots), params, out_rows, out_sums, tile=tile, name=name)


def matmul(groups, mode, outs, *, tm, tn, tk, extras=(), epi=None, name):
    a0, b0 = groups[0][0]
    if mode == "tn":
        k_dim, m_dim = a0.shape
    else:
        m_dim, k_dim = a0.shape
    n_dim = b0.shape[0] if mode == "nt" else b0.shape[1]
    tm, tn, tk = min(tm, m_dim), min(tn, n_dim), min(tk, k_dim)
    assert m_dim % tm == 0 and n_dim % tn == 0 and k_dim % tk == 0, (name, m_dim, n_dim, k_dim)
    nk = k_dim // tk
    pairs = [p for g in groups for p in g]
    n_pairs, n_groups, n_ex, n_out = len(pairs), len(groups), len(extras), len(outs)
    dims = {"nn": NN, "nt": NT, "tn": TN}[mode]

    def body(*refs):
        ab = refs[:2 * n_pairs]
        ex = refs[2 * n_pairs:2 * n_pairs + n_ex]
        o_refs = refs[2 * n_pairs + n_ex:2 * n_pairs + n_ex + n_out]
        accs = refs[2 * n_pairs + n_ex + n_out:]

        def partial_sums():
            sums, p = [], 0
            for g in groups:
                tot = None
                for _ in g:
                    d = _dot(ab[2 * p][...].astype(BF16), ab[2 * p + 1][...].astype(BF16), dims)
                    tot = d if tot is None else tot + d
                    p += 1
                sums.append(tot)
            return sums

        def finish(vals):
            res = epi(vals, [e[...] for e in ex]) if epi else vals
            for ref, v in zip(o_refs, res):
                ref[...] = v.astype(ref.dtype)

        if nk == 1:
            finish(partial_sums())
        else:
            k = pl.program_id(2)

            @pl.when(k == 0)
            def _():
                for acc in accs:
                    acc[...] = jnp.zeros(acc.shape, F32)

            for acc, s in zip(accs, partial_sums()):
                acc[...] += s

            @pl.when(k == nk - 1)
            def _():
                finish([acc[...] for acc in accs])

    if mode == "nn":
        a_spec = pl.BlockSpec((tm, tk), lambda i, j, k: (i, k))
        b_spec = pl.BlockSpec((tk, tn), lambda i, j, k: (k, j))
    elif mode == "nt":
        a_spec = pl.BlockSpec((tm, tk), lambda i, j, k: (i, k))
        b_spec = pl.BlockSpec((tn, tk), lambda i, j, k: (j, k))
    else:
        a_spec = pl.BlockSpec((tk, tm), lambda i, j, k: (k, i))
        b_spec = pl.BlockSpec((tk, tn), lambda i, j, k: (k, j))
    mn_spec = pl.BlockSpec((tm, tn), lambda i, j, k: (i, j))
    return pl.pallas_call(
        body, name=name, grid=(m_dim // tm, n_dim // tn, nk),
        in_specs=[a_spec, b_spec] * n_pairs + [mn_spec] * n_ex,
        out_specs=[mn_spec] * n_out,
        out_shape=[jax.ShapeDtypeStruct((m_dim, n_dim), dt) for dt in outs],
        scratch_shapes=[pltpu.VMEM((tm, tn), F32)] * (n_groups if nk > 1 else 0),
        compiler_params=_params(dimension_semantics=("parallel", "parallel", "arbitrary")),
    )(*[x for p in pairs for x in p], *extras)


def conv_fwd(x, w, b, *, tile, name):
    n_rows, width = x.arr.shape[0], x.width
    blk = x.block[0] if x.block else 0

    def body(x_ref, halo_ref, w_ref, b_ref, y_ref, buf):
        i = pl.program_id(0)
        halo = halo_ref[...]
        buf[0:8, :] = jnp.where(i == 0, jnp.zeros_like(halo), _round(halo))
        buf[8:, :] = _round(x_ref[...])
        acc = jnp.zeros((tile, width), F32)
        for j in range(CONV_W):
            s = CONV_W - 1 - j
            acc = acc + _round(w_ref[j:j + 1, :]) * buf[8 - s:8 - s + tile, :]
        y_ref[...] = acc + b_ref[...]

    hb = tile // 8
    return pl.pallas_call(
        body, name=name, grid=(n_rows // tile,),
        in_specs=[pl.BlockSpec((tile, width), lambda i: (i, blk)),
                  pl.BlockSpec((8, width), lambda i: (jnp.maximum(i * hb - 1, 0), blk)),
                  pl.BlockSpec((CONV_W, width), lambda i: (0, 0)),
                  pl.BlockSpec((1, width), lambda i: (0, 0))],
        out_specs=pl.BlockSpec((tile, width), lambda i: (i, 0)),
        out_shape=jax.ShapeDtypeStruct((n_rows, width), F32),
        scratch_shapes=[pltpu.VMEM((tile + 8, width), F32)],
        compiler_params=_params(dimension_semantics=("arbitrary",)),
    )(x.arr, x.arr, w, b)


def conv_bwd(x, dy, w, *, tile, name):
    n_rows, width = x.arr.shape[0], x.width
    blk = x.block[0] if x.block else 0
    n_tiles = n_rows // tile

    def body(x_ref, xh_ref, dy_ref, dyh_ref, w_ref, dx_ref, dw_ref, db_ref, xbuf, dbuf):
        i = pl.program_id(0)
        xh, dyh = xh_ref[...], dyh_ref[...]
        xbuf[0:8, :] = jnp.where(i == 0, jnp.zeros_like(xh), _round(xh))
        xbuf[8:, :] = _round(x_ref[...])
        dy_t = dy_ref[...]
        dy_r = _round(dy_t)
        dbuf[0:tile, :] = dy_r
        dbuf[tile:, :] = jnp.where(i == n_tiles - 1, jnp.zeros_like(dyh), _round(dyh))

        @pl.when(i == 0)
        def _():
            dw_ref[...] = jnp.zeros(dw_ref.shape, F32)
            db_ref[...] = jnp.zeros(db_ref.shape, F32)

        acc = jnp.zeros((tile, width), F32)
        for j in range(CONV_W):
            s = CONV_W - 1 - j
            acc = acc + _round(w_ref[j:j + 1, :]) * dbuf[s:s + tile, :]
            dw_ref[j:j + 1, :] += jnp.sum(dy_r * xbuf[8 - s:8 - s + tile, :], axis=0, keepdims=True)
        dx_ref[...] = acc.astype(dx_ref.dtype)
        db_ref[...] += jnp.sum(dy_t, axis=0, keepdims=True)

    hb = tile // 8
    return pl.pallas_call(
        body, name=name, grid=(n_tiles,),
        in_specs=[pl.BlockSpec((tile, width), lambda i: (i, blk)),
                  pl.BlockSpec((8, width), lambda i: (jnp.maximum(i * hb - 1, 0), blk)),
                  pl.BlockSpec((tile, width), lambda i: (i, 0)),
                  pl.BlockSpec((8, width), lambda i: (jnp.minimum((i + 1) * hb, n_tiles * hb - 1), 0)),
                  pl.BlockSpec((CONV_W, width), lambda i: (0, 0))],
        out_specs=[pl.BlockSpec((tile, width), lambda i: (i, 0)),
                   pl.BlockSpec((CONV_W, width), lambda i: (0, 0)),
                   pl.BlockSpec((1, width), lambda i: (0, 0))],
        out_shape=[jax.ShapeDtypeStruct((n_rows, width), BF16),
                   jax.ShapeDtypeStruct((CONV_W, width), F32),
                   jax.ShapeDtypeStruct((1, width), F32)],
        scratch_shapes=[pltpu.VMEM((tile + 8, width), F32), pltpu.VMEM((tile + 8, width), F32)],
        compiler_params=_params(dimension_semantics=("arbitrary",)),
    )(x.arr, x.arr, dy, dy, w)


def _pieces(ref, col_w, row_h):
    n_c, n_r = ref.shape[1] // col_w, ref.shape[0] // row_h
    return [[ref[r * row_h:(r + 1) * row_h, c * col_w:(c + 1) * col_w].astype(F32) for r in range(n_r)]
            for c in range(n_c)]


def _store_pieces(ref, vals, col_w, row_h):
    for c, col in enumerate(vals):
        for r, v in enumerate(col):
            ref[r * row_h:(r + 1) * row_h, c * col_w:(c + 1) * col_w] = v.astype(ref.dtype)


def _x_spec(x, n_chunks, reverse):
    blk = x.block[0] if x.block else 0
    if reverse:
        return pl.BlockSpec((CHUNK, x.width), functools.partial(lambda n, b: (n_chunks - 1 - n, b), b=blk))
    return pl.BlockSpec((CHUNK, x.width), functools.partial(lambda n, b: (n, b), b=blk))


def chunk_scan_fwd(step, xs, piece, state_shapes, out_widths, *, name):
    n_rows = xs[0].arr.shape[0]
    n_chunks = n_rows // CHUNK
    n_x, n_s, n_o = len(xs), len(state_shapes), len(out_widths)

    def body(*refs):
        x_refs, o_refs = refs[:n_x], refs[n_x:n_x + n_o]
        keep_refs, st_refs = refs[n_x + n_o:n_x + n_o + n_s], refs[n_x + n_o + n_s:]

        @pl.when(pl.program_id(0) == 0)
        def _():
            for st in st_refs:
                st[...] = jnp.zeros(st.shape, F32)

        states = [st[...] for st in st_refs]
        for keep, s in zip(keep_refs, states):
            keep[...] = s
        new_states, outs = step(states, [_pieces(x, *p) for x, p in zip(x_refs, piece)])
        for st, s in zip(st_refs, new_states):
            st[...] = s
        for o, v in zip(o_refs, outs):
            o[...] = v

    out_shape = [jax.ShapeDtypeStruct((n_rows, w), F32) for w in out_widths]
    out_specs = [pl.BlockSpec((CHUNK, w), lambda n: (n, 0)) for w in out_widths]
    for shp in state_shapes:
        out_shape.append(jax.ShapeDtypeStruct((n_chunks,) + shp, F32))
        out_specs.append(pl.BlockSpec((None,) + shp, lambda n: (n, 0, 0)))
    res = pl.pallas_call(
        body, name=name, grid=(n_chunks,),
        in_specs=[_x_spec(x, n_chunks, False) for x in xs],
        out_specs=out_specs, out_shape=out_shape,
        scratch_shapes=[pltpu.VMEM(shp, F32) for shp in state_shapes],
        compiler_params=_params(dimension_semantics=("arbitrary",)),
    )(*[x.arr for x in xs])
    return res[:n_o], res[n_o:]


def chunk_scan_bwd(step, xs, piece, kept, d_outs, *, name):
    n_rows = xs[0].arr.shape[0]
    n_chunks = n_rows // CHUNK
    n_x, n_s, n_o = len(xs), len(kept), len(d_outs)
    state_shapes = [k.shape[1:] for k in kept]

    def body(*refs):
        x_refs, k_refs = refs[:n_x], refs[n_x:n_x + n_s]
        do_refs = refs[n_x + n_s:n_x + n_s + n_o]
        dx_refs = refs[n_x + n_s + n_o:2 * n_x + n_s + n_o]
        ds_refs = refs[2 * n_x + n_s + n_o:]

        @pl.when(pl.program_id(0) == 0)
        def _():
            for ds in ds_refs:
                ds[...] = jnp.zeros(ds.shape, F32)

        states = [k[...] for k in k_refs]
        inputs = [_pieces(x, *p) for x, p in zip(x_refs, piece)]
        _, vjp = jax.vjp(step, states, inputs)
        d_states, d_inputs = vjp(([ds[...] for ds in ds_refs], [do[...] for do in do_refs]))
        for ds, v in zip(ds_refs, d_states):
            ds[...] = v
        for dx, v, p in zip(dx_refs, d_inputs, piece):
            _store_pieces(dx, v, *p)

    rev3 = lambda n: (n_chunks - 1 - n, 0, 0)
    rev2 = lambda n: (n_chunks - 1 - n, 0)
    return pl.pallas_call(
        body, name=name, grid=(n_chunks,),
        in_specs=[_x_spec(x, n_chunks, True) for x in xs]
        + [pl.BlockSpec((None,) + shp, rev3) for shp in state_shapes]
        + [pl.BlockSpec((CHUNK, d.shape[1]), rev2) for d in d_outs],
        out_specs=[pl.BlockSpec((CHUNK, x.width), rev2) for x in xs],
        out_shape=[jax.ShapeDtypeStruct((n_rows, x.width), F32) for x in xs],
        scratch_shapes=[pltpu.VMEM(shp, F32) for shp in state_shapes],
        compiler_params=_params(dimension_semantics=("arbitrary",)),
    )(*[x.arr for x in xs], *kept, *d_outs)


def _tri(n, strict=False):
    r = lax.broadcasted_iota(jnp.int32, (n, n), 0)
    c = lax.broadcasted_iota(jnp.int32, (n, n), 1)
    return (r > c) if strict else (r >= c)


def hgrn2_step(states, inputs):
    q_all, k_all, v_all, lf_all = inputs
    n_sub = CHUNK // SUB
    low = _tri(SUB).astype(F32)
    ones_sub = jnp.ones((SUB, SUB), F32)
    ones_chunk = jnp.ones((CHUNK, HEAD_W), F32)
    new_states, outs = [], []
    for h in range(HEADS):
        state = states[h]
        q, k, v, lf = q_all[h], k_all[h], v_all[h], lf_all[h]
        cum = [hi_nn(low, lf[i]) for i in range(n_sub)]
        tot = [hi_nn(ones_sub, lf[i]) for i in range(n_sub)]
        start = [jnp.zeros((SUB, HEAD_W), F32)]
        for i in range(n_sub):
            start.append(start[-1] + tot[i])
        q_in = [q[i] * jnp.exp(cum[i]) for i in range(n_sub)]
        intra = []
        for i in range(n_sub):
            keys = [k[j] * jnp.exp(start[i] - start[j] - cum[j]) for j in range(i)]
            keys.append(k[i] * jnp.exp(jnp.minimum(-cum[i], 80.0)))
            att = mid_nt(q_in[i], jnp.concatenate(keys, axis=0))
            r_id = lax.broadcasted_iota(jnp.int32, att.shape, 0)
            c_id = lax.broadcasted_iota(jnp.int32, att.shape, 1)
            att = jnp.where(c_id - SUB * i <= r_id, att, 0.0)
            intra.append(mm_nn(att, jnp.concatenate(v[:i + 1], axis=0)))
        q_state = jnp.concatenate([q_in[i] * jnp.exp(start[i]) for i in range(n_sub)], axis=0)
        out = mm_nn(q_state, state) + jnp.concatenate(intra, axis=0)
        k_end = jnp.concatenate([k[j] * jnp.exp(start[n_sub] - start[j] - cum[j]) for j in range(n_sub)], axis=0)
        decay = jnp.exp(hi_tn(jnp.concatenate(lf, axis=0), ones_chunk))
        new_states.append(decay * state + mm_tn(k_end, jnp.concatenate(v, axis=0)))
        outs.append(out)
    return new_states, [jnp.concatenate(outs, axis=1)]


def mlstm_step(states, inputs):
    q_all, k_all, v_all, gates = inputs
    gates = gates[0][0]
    c_st, n_st, m_st = states[:HEADS], states[HEADS:2 * HEADS], states[2 * HEADS:]
    lane = lax.broadcasted_iota(jnp.int32, (CHUNK, 128), 1)
    low = _tri(CHUNK).astype(F32)
    causal = _tri(CHUNK)
    gates_cum = hi_nn(low, gates)
    new_c, new_n, new_m, outs = [], [], [], []
    for h in range(HEADS):
        q, k, v = q_all[h][0], k_all[h][0], v_all[h][0]
        pick_i = (lane == h).astype(F32)
        pick_f = (lane == HEADS + h).astype(F32)
        li_col = jnp.sum(gates * pick_i, axis=1, keepdims=True)
        lf_col = jnp.sum(gates * pick_f, axis=1, keepdims=True)
        b_col = jnp.sum(gates_cum * pick_f, axis=1, keepdims=True)
        by_key = hi_nt(pick_i, gates) - hi_nt(pick_f, gates_cum)
        d_mat = jnp.where(causal, b_col + by_key, NEG)
        m_prev = lax.stop_gradient(jnp.max(m_st[h], axis=1, keepdims=True))
        g_inter = b_col + m_prev
        m_t = lax.stop_gradient(jnp.maximum(g_inter, jnp.max(d_mat, axis=1, keepdims=True)))
        w_inter = jnp.exp(g_inter - m_t)
        aw = jnp.exp(d_mat - m_t) * mm_nt(q, k)
        num = w_inter * mm_nn(q, c_st[h]) + mm_nn(aw, v)
        den = w_inter * row_dot(q, n_st[h]) + jnp.sum(aw, axis=1, keepdims=True)
        outs.append(num / jnp.maximum(jnp.abs(den), jnp.exp(-m_t)))
        b_end = jnp.sum(lf_col, axis=0, keepdims=True)
        g_state = b_end + m_prev
        s_w = b_end - b_col + li_col
        m_next = lax.stop_gradient(jnp.maximum(g_state, jnp.max(s_w, axis=0, keepdims=True)))
        dec = jnp.exp(g_state - m_next)
        w_s = jnp.exp(s_w - m_next)
        kw = k * w_s
        new_c.append(dec * c_st[h] + mm_tn(kw, v))
        new_n.append(dec * n_st[h] + col_dot(w_s, k))
        new_m.append(jnp.broadcast_to(m_next, (1, HEAD_W)))
    return new_c + new_n + new_m, [jnp.concatenate(outs, axis=1)]


def lru_fwd(a, u, *, tile, name):
    n_rows, width = a.shape

    def body(a_ref, u_ref, h_ref, carry):
        @pl.when(pl.program_id(0) == 0)
        def _():
            carry[...] = jnp.zeros(carry.shape, F32)

        h = carry[...]
        for t in range(tile):
            h = a_ref[t:t + 1, :] * h + u_ref[t:t + 1, :]
            h_ref[t:t + 1, :] = h
        carry[...] = h

    spec = pl.BlockSpec((tile, width), lambda i: (i, 0))
    return pl.pallas_call(
        body, name=name, grid=(n_rows // tile,), in_specs=[spec, spec], out_specs=spec,
        out_shape=jax.ShapeDtypeStruct((n_rows, width), F32),
        scratch_shapes=[pltpu.VMEM((1, width), F32)],
        compiler_params=_params(dimension_semantics=("arbitrary",)),
    )(a, u)


def lru_bwd(a, h, dh, *, tile, name):
    n_rows, width = a.shape
    n_tiles = n_rows // tile
    hb = tile // 8

    def body(a_ref, h_ref, hh_ref, dh_ref, da_ref, du_ref, carry):
        i = pl.program_id(0)

        @pl.when(i == 0)
        def _():
            carry[...] = jnp.zeros(carry.shape, F32)

        c = carry[...]
        for t in range(tile - 1, -1, -1):
            g = dh_ref[t:t + 1, :] + c
            du_ref[t:t + 1, :] = g
            if t:
                h_prev = h_ref[t - 1:t, :]
            else:
                h_prev = jnp.where(i == n_tiles - 1, 0.0, hh_ref[7:8, :])
            da_ref[t:t + 1, :] = g * h_prev
            c = a_ref[t:t + 1, :] * g
        carry[...] = c

    rev = lambda i: (n_tiles - 1 - i, 0)
    spec = pl.BlockSpec((tile, width), rev)
    halo = pl.BlockSpec((8, width), lambda i: (jnp.maximum((n_tiles - 1 - i) * hb - 1, 0), 0))
    return pl.pallas_call(
        body, name=name, grid=(n_tiles,), in_specs=[spec, spec, halo, spec], out_specs=[spec, spec],
        out_shape=[jax.ShapeDtypeStruct((n_rows, width), F32)] * 2,
        scratch_shapes=[pltpu.VMEM((1, width), F32)],
        compiler_params=_params(dimension_semantics=("arbitrary",)),
    )(a, h, h, dh)


def _layer_norm(z, g, b):
    mu = jnp.mean(z, axis=-1, keepdims=True)
    zc = z - mu
    var = jnp.mean(zc * zc, axis=-1, keepdims=True)
    return zc * lax.rsqrt(var + 1e-5) * g + b


def pre_fn(rows, params):
    (x,), (scale, shift) = rows, params
    return [x * (1.0 + scale) + shift], []


def make_post_fn(weight, with_next):
    def fn(rows, params):
        x, y = rows
        gate, g, b = params[:3]
        xo = _layer_norm(ALPHA * x + weight * (1.0 + gate) * y, g, b)
        if with_next:
            return [xo, xo * (1.0 + params[3]) + params[4]], []
        return [xo], []

    return fn


def make_last_fn(weight):
    post = make_post_fn(weight, False)

    def fn(rows, params):
        x, y, target = rows
        err = post([x, y], params)[0][0] - target
        loss = 0.5 * jnp.sum(jnp.mean(err * err, axis=-1, keepdims=True), axis=0, keepdims=True)
        return [err * (1.0 / D_MODEL)], [jnp.broadcast_to(loss, (1, 128))]

    return fn


def mix_a_fn(rows, params):
    a_q, a_f, xconv, graw = rows
    l0, l1, l2, wq, wk, gate_b = params
    mx = jnp.maximum(jnp.maximum(l0, l1), l2)
    e0, e1, e2 = jnp.exp(l0 - mx), jnp.exp(l1 - mx), jnp.exp(l2 - mx)
    lb = e0 / (e0 + e1 + e2)
    f = lb + (1.0 - lb) * jax.nn.sigmoid(a_f)
    xc = _silu(xconv)
    q_b = mm_nn(xc, wq)
    k_b = mm_nn(xc, wk) * (HEAD_W ** -0.5)
    g = graw + gate_b
    lane = lax.broadcasted_iota(jnp.int32, g.shape, 1)
    gates = jnp.where(lane < HEADS, g, _log_sigmoid(g))
    return [_silu(a_q), 1.0 - f, jnp.log(f), xc, q_b, k_b, gates], []


def _head_norm(v, g, center):
    if center:
        v = v - jnp.mean(v, axis=-1, keepdims=True)
    return v * lax.rsqrt(jnp.mean(v * v, axis=-1, keepdims=True) + 1e-6) * g


def mix_b_fn(rows, params):
    o_a, a_g, h_b, xc, b_z = rows
    hg, mg, skip = params
    y_a = [_head_norm(o_a[h], hg[h], False) * _silu(a_g[h]) for h in range(HEADS)]
    y_b = [(_head_norm(h_b[h], mg[h], True) + skip[h] * xc[h]) * _silu(b_z[h]) for h in range(HEADS)]
    return [y_a + y_b], []


def lru_a_fn(rows, params):
    (xr,) = rows
    wa, wx, ba, bx, lam = params
    a_out, u_out = [], []
    for n in range(C_BLOCKS):
        r = jax.nn.sigmoid(mm_nt(xr[n], wa[n]) + ba[n])
        i = jax.nn.sigmoid(mm_nt(xr[n], wx[n]) + bx[n])
        log_a = -RG_C * r * _softplus(-lam[n])
        a_out.append(jnp.exp(log_a))
        u_out.append(jnp.sqrt(_neg_expm1(2.0 * log_a)) * i * xr[n])
    return [a_out, u_out], []


def lru_b_fn(rows, params):
    h, y_br = rows
    return [h * jax.nn.gelu(y_br)], []


def swiglu_epi(accs, extras):
    h1, h3 = accs
    return [h1, h3, _silu(h1) * h3]


def swiglu_bwd_epi(accs, extras):
    (da,), (h1, h3) = accs, extras
    h1, h3 = h1.astype(F32), h3.astype(F32)
    sig = jax.nn.sigmoid(h1)
    return [da * h3 * sig * (1.0 + h1 * (1.0 - sig)), da * h1 * sig]


ROW_TILE = 256


def _chunks(v, n):
    return v.reshape(n, 1, v.shape[-1] // n)


def _dense_blocks(w):
    n, b, _ = w.shape
    eye = jnp.eye(n, dtype=w.dtype)
    return (eye[:, None, :, None] * jnp.swapaxes(w, 1, 2)[:, :, None, :]).reshape(n * b, n * b)


def _block_diag_of(m, n, b):
    m4 = m.reshape(n, b, n, b)
    idx = jnp.arange(n)
    return jnp.swapaxes(m4[idx, :, idx, :], 1, 2)


def local_step(x, target, mod, w):
    s_len = x.shape[0]
    tile = min(ROW_TILE, s_len)
    row = lambda v: v.reshape(1, -1)
    mrow = lambda l, j, k: mod[l, 3 * j + k].reshape(1, D_MODEL)
    g = {}
    d_mod = [[None] * 9 for _ in range(DEPTH)]
    d_ln_g = [[None] * 3 for _ in range(DEPTH)]
    d_ln_b = [[None] * 3 for _ in range(DEPTH)]
    subs = [(l, j) for l in range(DEPTH) for j in range(3)]
    weight_of = lambda j: 1.0 if j == 1 else FFN_RES_W

    wq_d = _dense_blocks(w["mlstm_wq"])
    wk_d = _dense_blocks(w["mlstm_wk"])
    wa_b, wx_b = w["rglru_wa"].astype(BF16), w["rglru_wx"].astype(BF16)
    gate_b = jnp.pad(w["mlstm_gate_b"].reshape(1, 8), ((0, 0), (0, 120)))
    lb_rows = [row(w["hgrn_lb_logits"][k]) for k in range(3)]
    mix_a_params = lb_rows + [wq_d, wk_d, gate_b]
    mix_b_params = [_chunks(row(w["hgrn_norm_g"]), HEADS), _chunks(row(w["mlstm_norm_g"]), HEADS),
                    _chunks(row(w["mlstm_skip"]), HEADS)]
    lru_a_params = [wa_b, wx_b, _chunks(row(w["rglru_ba"]), C_BLOCKS), _chunks(row(w["rglru_bx"]), C_BLOCKS),
                    _chunks(row(w["rglru_lambda"]), C_BLOCKS)]
    mconv_w, mconv_b = w["mlstm_conv_w"], row(w["mlstm_conv_b"])
    rconv_w, rconv_b = w["rglru_conv_w"], row(w["rglru_conv_b"])
    hg_piece = [(HEAD_W, SUB)] * 4
    ml_piece = [(HEAD_W, CHUNK)] * 3 + [(128, CHUNK)]

    (t,), _ = rowwise(pre_fn, [Rows(x)], [mrow(0, 0, 1), mrow(0, 0, 0)], [(D_MODEL, BF16)], [],
                      tile=tile, name="pre")
    saved = {}
    x_in = x
    for idx, (l, j) in enumerate(subs):
        sv = {"x": x_in, "t": t}
        if j != 1:
            w1, w3, w2 = w["ffn_w1"][l][j // 2], w["ffn_w3"][l][j // 2], w["ffn_w2"][l][j // 2]
            h1, h3, act = matmul([[(t, w1)], [(t, w3)]], "nn", [F32, F32, BF16], tm=512, tn=1408, tk=D_MODEL,
                                 epi=swiglu_epi, name="ffn_up")
            (y,) = matmul([[(act, w2)]], "nn", [F32], tm=512, tn=1024, tk=1408, name="ffn_down")
            sv.update(h1=h1, h3=h3, act=act)
        elif l == 0:
            (proj,) = matmul([[(t, w["ab_w_in"])]], "nn", [F32], tm=256, tn=AB_ALL, tk=D_MODEL, name="ab_in")
            xconv = conv_fwd(Rows(proj, (4, MIX_W)), mconv_w, mconv_b, tile=tile, name="mconv")
            a_rows = [Rows(proj, (0, MIX_W)), Rows(proj, (1, MIX_W)), Rows(xconv), Rows(proj, (AB_MAIN // 128, 128))]
            (q_a, k_a, lf_a, xc, q_b, k_b, gates), _ = rowwise(
                mix_a_fn, a_rows, mix_a_params, [(MIX_W, F32)] * 6 + [(128, F32)], [], tile=tile, name="mix_a")
            hg_xs = [Rows(q_a), Rows(k_a), Rows(proj, (2, MIX_W)), Rows(lf_a)]
            (o_a,), hg_kept = chunk_scan_fwd(hgrn2_step, hg_xs, hg_piece, [(HEAD_W, HEAD_W)] * HEADS, [MIX_W],
                                             name="hgrn2_fwd")
            ml_xs = [Rows(q_b), Rows(k_b), Rows(proj, (5, MIX_W)), Rows(gates)]
            ml_states = [(HEAD_W, HEAD_W)] * HEADS + [(1, HEAD_W)] * (2 * HEADS)
            (h_b,), ml_kept = chunk_scan_fwd(mlstm_step, ml_xs, ml_piece, ml_states, [MIX_W], name="mlstm_fwd")
            b_rows = [Rows(o_a, split=HEAD_W), Rows(proj, (3, MIX_W), HEAD_W), Rows(h_b, split=HEAD_W),
                      Rows(xc, split=HEAD_W), Rows(proj, (6, MIX_W), HEAD_W)]
            (ycat,), _ = rowwise(mix_b_fn, b_rows, mix_b_params, [(2 * MIX_W, BF16)], [], tile=tile, name="mix_b")
            (y,) = matmul([[(ycat, w["ab_w_out"])]], "nn", [F32], tm=512, tn=1024, tk=D_MODEL, name="ab_out")
            sv.update(proj=proj, xconv=xconv, a_rows=a_rows, hg_xs=hg_xs, hg_kept=hg_kept, ml_xs=ml_xs,
                      ml_kept=ml_kept, b_rows=b_rows, ycat=ycat)
        else:
            (proj,) = matmul([[(t, w["rglru_w_in"])]], "nn", [F32], tm=512, tn=1024, tk=D_MODEL, name="rg_in")
            xr = conv_fwd(Rows(proj, (1, D_MODEL)), rconv_w, rconv_b, tile=tile, name="rconv")
            (a_t, u_t), _ = rowwise(lru_a_fn, [Rows(xr, split=128)], lru_a_params, [(D_MODEL, F32)] * 2, [],
                                    tile=tile, name="lru_a")
            h = lru_fwd(a_t, u_t, tile=min(128, s_len), name="lru_fwd")
            b_rows = [Rows(h), Rows(proj, (0, D_MODEL))]
            (hgate,), _ = rowwise(lru_b_fn, b_rows, [], [(D_MODEL, BF16)], [], tile=tile, name="lru_b")
            (y,) = matmul([[(hgate, w["rglru_w_out"])]], "nn", [F32], tm=512, tn=1024, tk=D_MODEL, name="rg_out")
            sv.update(proj=proj, xr=xr, a_t=a_t, h=h, b_rows=b_rows, hgate=hgate)
        sv["y"] = y
        post_params = [mrow(l, j, 2), row(w["ln_g"][l, j]), row(w["ln_b"][l, j])]
        if idx + 1 < len(subs):
            nl, nj = subs[idx + 1]
            post_params += [mrow(nl, nj, 1), mrow(nl, nj, 0)]
            (x_out, t), _ = rowwise(make_post_fn(weight_of(j), True), [Rows(x_in), Rows(y)], post_params,
                                    [(D_MODEL, F32), (D_MODEL, BF16)], [], tile=tile, name="post")
        else:
            (d_xo,), (loss_row,) = rowwise(make_last_fn(weight_of(j)), [Rows(x_in), Rows(y), Rows(target)],
                                           post_params, [(D_MODEL, F32)], [(1, 128)], tile=tile, name="post_loss")
            x_out = None
        sv["post_params"] = post_params
        saved[(l, j)] = sv
        x_in = x_out
    loss = loss_row[0, 0]

    d_t_next = None
    for idx in range(len(subs) - 1, -1, -1):
        l, j = subs[idx]
        sv = saved[(l, j)]
        has_next = idx + 1 < len(subs)
        cots = [Rows(d_xo)] + ([Rows(d_t_next)] if has_next else [])
        want_p = [0, 1, 2] + ([3, 4] if has_next else [])
        (d_xres, d_y), d_par = rowwise_bwd(
            make_post_fn(weight_of(j), has_next), [Rows(sv["x"]), Rows(sv["y"])], cots, sv["post_params"],
            [0, 1], want_p, [F32, BF16], tile=tile, name="post_bwd")
        d_mod[l][3 * j + 2], d_ln_g[l][j], d_ln_b[l][j] = d_par[:3]
        if has_next:
            nl, nj = subs[idx + 1]
            d_mod[nl][3 * nj + 1], d_mod[nl][3 * nj] = d_par[3:]
        t = sv["t"]
        if j != 1:
            w1, w3, w2 = w["ffn_w1"][l][j // 2], w["ffn_w3"][l][j // 2], w["ffn_w2"][l][j // 2]
            d_h1, d_h3 = matmul([[(d_y, w2)]], "nt", [BF16, BF16], tm=512, tn=1408, tk=D_MODEL,
                                extras=[sv["h1"], sv["h3"]], epi=swiglu_bwd_epi, name="ffn_down_bwd")
            (g[f"ffn_w2.{l}.{j // 2}"],) = matmul([[(sv["act"], d_y)]], "tn", [GRAD_WIRE], tm=1408, tn=1024, tk=512,
                                                 name="ffn_dw2")
            (g[f"ffn_w1.{l}.{j // 2}"],) = matmul([[(t, d_h1)]], "tn", [GRAD_WIRE], tm=1024, tn=1408, tk=512,
                                                 name="ffn_dw1")
            (g[f"ffn_w3.{l}.{j // 2}"],) = matmul([[(t, d_h3)]], "tn", [GRAD_WIRE], tm=1024, tn=1408, tk=512,
                                                 name="ffn_dw3")
            (d_t,) = matmul([[(d_h1, w1), (d_h3, w3)]], "nt", [F32], tm=512, tn=1024, tk=1408, name="ffn_up_bwd")
        elif l == 0:
            (d_ycat,) = matmul([[(d_y, w["ab_w_out"])]], "nt", [F32], tm=512, tn=1024, tk=D_MODEL, name="ab_out_bwd")
            (g["ab_w_out"],) = matmul([[(sv["ycat"], d_y)]], "tn", [GRAD_WIRE], tm=1024, tn=1024, tk=512, name="ab_dwout")
            (d_oa, d_ag, d_hb, d_xc, d_bz), (d_hg, d_mg, d_skip) = rowwise_bwd(
                mix_b_fn, sv["b_rows"], [Rows(d_ycat, split=HEAD_W)], mix_b_params, [0, 1, 2, 3, 4], [0, 1, 2],
                [F32, BF16, F32, F32, BF16], tile=tile, name="mix_b_bwd")
            g["hgrn_norm_g"], g["mlstm_norm_g"], g["mlstm_skip"] = (v.reshape(1, MIX_W) for v in (d_hg, d_mg, d_skip))
            d_qb, d_kb, d_bv, d_gates = chunk_scan_bwd(mlstm_step, sv["ml_xs"], ml_piece, sv["ml_kept"], [d_hb],
                                                       name="mlstm_bwd")
            d_qa, d_ka, d_ai, d_lf = chunk_scan_bwd(hgrn2_step, sv["hg_xs"], hg_piece, sv["hg_kept"], [d_oa],
                                                    name="hgrn2_bwd")
            a_cots = [Rows(v) for v in (d_qa, d_ka, d_lf, d_xc, d_qb, d_kb, d_gates)]
            (d_aq, d_af, d_xconv, d_graw), (d_l0, d_l1, d_l2, d_wq, d_wk, d_gb) = rowwise_bwd(
                mix_a_fn, sv["a_rows"], a_cots, mix_a_params, [0, 1, 2, 3], [0, 1, 2, 3, 4, 5],
                [BF16, BF16, F32, BF16], tile=tile, name="mix_a_bwd")
            g["hgrn_lb_logits"] = jnp.concatenate([d_l0, d_l1, d_l2], axis=0)
            g["mlstm_wq"] = _block_diag_of(d_wq, MIX_W // 4, 4)
            g["mlstm_wk"] = _block_diag_of(d_wk, MIX_W // 4, 4)
            g["mlstm_gate_b"] = d_gb[:, :8]
            d_bx, g["mlstm_conv_w"], g["mlstm_conv_b"] = conv_bwd(Rows(sv["proj"], (4, MIX_W)), d_xconv, mconv_w,
                                                                  tile=tile, name="mconv_bwd")
            d_proj = jnp.concatenate([d_aq, d_af, d_ai.astype(BF16), d_ag, d_bx, d_bv.astype(BF16), d_bz, d_graw],
                                     axis=1)
            (g["ab_w_in"],) = matmul([[(t, d_proj)]], "tn", [GRAD_WIRE], tm=256, tn=AB_ALL, tk=512, name="ab_dwin")
            (d_t,) = matmul([[(d_proj, w["ab_w_in"])]], "nt", [F32], tm=512, tn=1024, tk=AB_ALL, name="ab_in_bwd")
        else:
            (d_hgate,) = matmul([[(d_y, w["rglru_w_out"])]], "nt", [F32], tm=512, tn=1024, tk=D_MODEL,
                                name="rg_out_bwd")
            (g["rglru_w_out"],) = matmul([[(sv["hgate"], d_y)]], "tn", [GRAD_WIRE], tm=1024, tn=1024, tk=512,
                                         name="rg_dwout")
            (d_h, d_ybr), _ = rowwise_bwd(lru_b_fn, sv["b_rows"], [Rows(d_hgate)], [], [0, 1], [], [F32, BF16],
                                          tile=tile, name="lru_b_bwd")
            d_a, d_u = lru_bwd(sv["a_t"], sv["h"], d_h, tile=min(128, s_len), name="lru_bwd")
            (d_xr,), (d_wa, d_wx, d_ba, d_bx_, d_lam) = rowwise_bwd(
                lru_a_fn, [Rows(sv["xr"], split=128)], [Rows(d_a, split=128), Rows(d_u, split=128)], lru_a_params,
                [0], [0, 1, 2, 3, 4], [F32], tile=tile, name="lru_a_bwd")
            g["rglru_wa"], g["rglru_wx"] = d_wa, d_wx
            g["rglru_ba"], g["rglru_bx"], g["rglru_lambda"] = (v.reshape(1, D_MODEL) for v in (d_ba, d_bx_, d_lam))
            d_xbr, g["rglru_conv_w"], g["rglru_conv_b"] = conv_bwd(Rows(sv["proj"], (1, D_MODEL)), d_xr, rconv_w,
                                                                   tile=tile, name="rconv_bwd")
            d_proj = jnp.concatenate([d_ybr, d_xbr], axis=1)
            (g["rglru_w_in"],) = matmul([[(t, d_proj)]], "tn", [GRAD_WIRE], tm=1024, tn=1024, tk=512, name="rg_dwin")
            (d_t,) = matmul([[(d_proj, w["rglru_w_in"])]], "nt", [F32], tm=512, tn=1024, tk=1024, name="rg_in_bwd")
        d_xo, d_t_next = d_xres, d_t

    def first_bwd(rows, params):
        x0, d_res, d_t0 = rows
        _, vjp = jax.vjp(lambda r, p: pre_fn(r, p)[0], [x0], params)
        (d_x0,), d_p = vjp([d_t0])
        return [d_res + d_x0], d_p

    (grad_x,), (d_mod[0][1], d_mod[0][0]) = rowwise(
        first_bwd, [Rows(x), Rows(d_xo), Rows(d_t_next)], [mrow(0, 0, 1), mrow(0, 0, 0)], [(D_MODEL, F32)],
        [(1, D_MODEL)] * 2, tile=tile, name="pre_bwd")
    g["ln_g"] = jnp.stack([jnp.concatenate(r, axis=0) for r in d_ln_g])
    g["ln_b"] = jnp.stack([jnp.concatenate(r, axis=0) for r in d_ln_b])
    d_mod = jnp.stack([jnp.concatenate(r, axis=0) for r in d_mod])
    return loss, grad_x, d_mod, g


MESH_ID = pl.DeviceIdType.MESH
ANY_SPEC = pl.BlockSpec(memory_space=pl.ANY)


def _my_position():
    return lax.axis_index("x"), lax.axis_index("y"), lax.axis_index("c")


def _flat_index(pos):
    return 4 * pos[0] + 2 * pos[1] + pos[2]


def _peer_position(pos, k):
    return tuple(lax.rem(p + ((k >> s) & 1), 2) for p, s in zip(pos, (2, 1, 0)))


def _exchange(x, gather, name):
    out_shape = (NDEV,) + x.shape if gather else x.shape

    def body(x_ref, o_ref, send_sems, recv_sems, local_sem):
        pos = _my_position()
        me = _flat_index(pos)
        local = pltpu.make_async_copy(x_ref if gather else x_ref.at[me], o_ref.at[me], local_sem)
        local.start()
        copies = []
        for k in range(1, NDEV):
            peer = _peer_position(pos, k)
            src = x_ref if gather else x_ref.at[_flat_index(peer)]
            copies.append(pltpu.make_async_remote_copy(
                src_ref=src, dst_ref=o_ref.at[me], send_sem=send_sems.at[k - 1], recv_sem=recv_sems.at[k - 1],
                device_id=peer, device_id_type=MESH_ID))
            copies[-1].start()
        for cp in copies:
            cp.wait()
        local.wait()

    return pl.pallas_call(
        body, name=name, in_specs=[ANY_SPEC], out_specs=ANY_SPEC,
        out_shape=jax.ShapeDtypeStruct(out_shape, x.dtype),
        scratch_shapes=[pltpu.SemaphoreType.DMA((NDEV - 1,)), pltpu.SemaphoreType.DMA((NDEV - 1,)),
                        pltpu.SemaphoreType.DMA],
    )(x)


def all_gather(x, name):
    return _exchange(x, True, name)


def all_to_all(x, name):
    return _exchange(x, False, name)


def _row_tile(n_rows, cap):
    best = None
    for t in range(8, min(n_rows, cap) + 1, 8):
        if n_rows % t == 0:
            best = t
    return best if best else n_rows


def adamw(w, m, v, slots, *, name):
    n_rows, width = w.shape
    n_slots = slots.shape[0]
    lanes = -(-width // 128) * 128
    tile = _row_tile(n_rows, max(8, (1 << 20) // (4 * lanes) // 8 * 8))
    bc1 = 1.0 - ADAM_B1 ** ADAM_STEP
    bc2 = 1.0 - ADAM_B2 ** ADAM_STEP

    def body(w_ref, m_ref, v_ref, s_ref, g_ref, d_ref, nm_ref, nv_ref):
        g = s_ref[0].astype(F32)
        for k in range(1, n_slots):
            g = g + s_ref[k].astype(F32)
        wv = w_ref[...]
        nm = ADAM_B1 * m_ref[...] + (1.0 - ADAM_B1) * g
        nv = ADAM_B2 * v_ref[...] + (1.0 - ADAM_B2) * (g * g)
        g_ref[...] = g
        nm_ref[...] = nm
        nv_ref[...] = nv
        d_ref[...] = -ADAM_LR * ((nm / bc1) / (jnp.sqrt(nv / bc2) + ADAM_EPS) + ADAM_WD * wv)

    spec = pl.BlockSpec((tile, width), lambda i: (i, 0))
    return pl.pallas_call(
        body, name=name, grid=(n_rows // tile,),
        in_specs=[spec, spec, spec, pl.BlockSpec((n_slots, tile, width), lambda i: (0, i, 0))],
        out_specs=[spec] * 4, out_shape=[jax.ShapeDtypeStruct((n_rows, width), F32)] * 4,
        compiler_params=_params(dimension_semantics=("parallel",)),
    )(w, m, v, slots)


def adamw_nd(w, m, v, slots, *, name):
    shp = w.shape
    two = (-1, shp[-1])
    res = adamw(w.reshape(two), m.reshape(two), v.reshape(two), slots.reshape((slots.shape[0],) + (w.size // shp[-1], shp[-1])),
                name=name)
    return [r.reshape(shp) for r in res]


def _pack(arrs):
    parts = []
    for a in arrs:
        flat = a.reshape(-1).astype(F32)
        parts.append(jnp.pad(flat, (0, (-flat.shape[0]) % 1024)))
    return jnp.concatenate(parts).reshape(-1, 128)


def _unpack(buf, shapes):
    outs, at = [], 0
    flat = buf.reshape(-1)
    for shp in shapes:
        n = int(np.prod(shp))
        outs.append(flat[at:at + n].reshape(shp))
        at += n + (-n) % 1024
    return outs


ARG_NAMES = ["x", "c", "ada_w", "ada_b", "ln_g", "ln_b", "ffn_w1", "ffn_w3", "ffn_w2", "hgrn_lb_logits", "ab_w_in",
             "ab_w_out", "hgrn_norm_g", "mlstm_conv_w", "mlstm_conv_b", "mlstm_wq", "mlstm_wk", "mlstm_gate_b",
             "mlstm_skip", "mlstm_norm_g", "rglru_w_in", "rglru_conv_w", "rglru_conv_b", "rglru_wa", "rglru_ba",
             "rglru_wx", "rglru_bx", "rglru_lambda", "rglru_w_out", "loss_target"]
WEIGHTS = ARG_NAMES[2:-1]
BIG = ["ffn_w1", "ffn_w3", "ffn_w2", "ab_w_in", "ab_w_out", "rglru_w_in", "rglru_w_out"]
REPLICATED = ["ada_b", "hgrn_lb_logits", "hgrn_norm_g", "mlstm_conv_b", "mlstm_wq", "mlstm_wk", "mlstm_gate_b",
              "mlstm_skip", "mlstm_norm_g", "rglru_wa", "rglru_wx"]
SHARDED_SMALL = ["ln_g", "ln_b", "mlstm_conv_w", "rglru_conv_w", "rglru_conv_b", "rglru_ba", "rglru_bx", "rglru_lambda"]


def _unshard_last(gathered):
    moved = jnp.moveaxis(gathered, 0, -2)
    return moved.reshape(moved.shape[:-2] + (NDEV * moved.shape[-1],))


def _shard_last(full):
    split = full.reshape(full.shape[:-1] + (NDEV, full.shape[-1] // NDEV))
    return jnp.moveaxis(split, -2, 0)


def kernel(x, c, ada_w, ada_b, ln_g, ln_b, ffn_w1, ffn_w3, ffn_w2, hgrn_lb_logits, ab_w_in, ab_w_out, hgrn_norm_g, mlstm_conv_w, mlstm_conv_b, mlstm_wq, mlstm_wk, mlstm_gate_b, mlstm_skip, mlstm_norm_g, rglru_w_in, rglru_conv_w, rglru_conv_b, rglru_wa, rglru_ba, rglru_wx, rglru_bx, rglru_lambda, rglru_w_out, loss_target, m_ada_w, m_ada_b, m_ln_g, m_ln_b, m_ffn_w1, m_ffn_w3, m_ffn_w2, m_hgrn_lb_logits, m_ab_w_in, m_ab_w_out, m_hgrn_norm_g, m_mlstm_conv_w, m_mlstm_conv_b, m_mlstm_wq, m_mlstm_wk, m_mlstm_gate_b, m_mlstm_skip, m_mlstm_norm_g, m_rglru_w_in, m_rglru_conv_w, m_rglru_conv_b, m_rglru_wa, m_rglru_ba, m_rglru_wx, m_rglru_bx, m_rglru_lambda, m_rglru_w_out, v_ada_w, v_ada_b, v_ln_g, v_ln_b, v_ffn_w1, v_ffn_w3, v_ffn_w2, v_hgrn_lb_logits, v_ab_w_in, v_ab_w_out, v_hgrn_norm_g, v_mlstm_conv_w, v_mlstm_conv_b, v_mlstm_wq, v_mlstm_wk, v_mlstm_gate_b, v_mlstm_skip, v_mlstm_norm_g, v_rglru_w_in, v_rglru_conv_w, v_rglru_conv_b, v_rglru_wa, v_rglru_ba, v_rglru_wx, v_rglru_bx, v_rglru_lambda, v_rglru_w_out):
    args = locals()
    p = {n: args[n] for n in ARG_NAMES}
    mom = {n: (args["m_" + n], args["v_" + n]) for n in WEIGHTS}
    me = _flat_index(_my_position())

    sharded_shapes = [p[n].shape for n in SHARDED_SMALL]
    small = all_gather(_pack([p[n] for n in SHARDED_SMALL] + [c]), "gather_small")
    per_dev = [_unpack(small[d], sharded_shapes + [c.shape]) for d in range(NDEV)]
    full_small = {n: _unshard_last(jnp.stack([per_dev[d][i] for d in range(NDEV)]))
                  for i, n in enumerate(SHARDED_SMALL)}
    c_all = jnp.concatenate([per_dev[d][-1] for d in range(NDEV)], axis=0)

    c16 = jnp.pad(c_all, ((0, 8), (0, 0)))
    (c_act,), _ = rowwise(lambda r, q: ([_silu(r[0])], []), [Rows(c16)], [], [(D_MODEL, BF16)], [], tile=16,
                          name="cond_act")
    n_ada = ada_w.shape[-1]
    ada_b_mine = lax.dynamic_slice_in_dim(ada_b, me * n_ada, n_ada, axis=1)
    ada_cols = []
    for l in range(DEPTH):
        bias = jnp.broadcast_to(ada_b_mine[l][None, :], (16, n_ada))
        (cols,) = matmul([[(c_act, ada_w[l])]], "nn", [F32], tm=16, tn=n_ada, tk=D_MODEL, extras=[bias],
                         epi=lambda accs, ex: [accs[0] + ex[0]], name="ada_fwd")
        ada_cols.append(cols[:8])
    ada_mine = all_to_all(jnp.stack(ada_cols, axis=1), "ada_to_owner")
    mod = jnp.moveaxis(ada_mine, 0, 1).reshape(DEPTH, 9, D_MODEL)

    gathered = {n: all_gather(p[n].astype(BF16), "gather_" + n) for n in BIG}
    w = {}
    for n in ("ffn_w1", "ffn_w3"):
        full = _unshard_last(gathered[n])
        w[n] = [[full[l, j] for j in range(2)] for l in range(DEPTH)]
    full = jnp.moveaxis(gathered["ffn_w2"], 0, 2).reshape(DEPTH, 2, D_FF, D_MODEL)
    w["ffn_w2"] = [[full[l, j] for j in range(2)] for l in range(DEPTH)]
    ab_in = _unshard_last(gathered["ab_w_in"])[0]
    w["ab_w_in"] = jnp.concatenate([ab_in[:, :AB_MAIN], jnp.pad(ab_in[:, AB_MAIN:], ((0, 0), (0, 120)))], axis=1)
    w["ab_w_out"] = gathered["ab_w_out"].reshape(D_MODEL, D_MODEL)
    w["rglru_w_in"] = _unshard_last(gathered["rglru_w_in"])[0]
    w["rglru_w_out"] = gathered["rglru_w_out"].reshape(D_MODEL, D_MODEL)
    w["ln_g"], w["ln_b"] = full_small["ln_g"], full_small["ln_b"]
    w["hgrn_lb_logits"] = hgrn_lb_logits
    for n in ("hgrn_norm_g", "mlstm_conv_b", "mlstm_wq", "mlstm_wk", "mlstm_gate_b", "mlstm_skip", "mlstm_norm_g",
              "rglru_wa", "rglru_wx"):
        w[n] = p[n][0]
    for n in ("mlstm_conv_w", "rglru_conv_w", "rglru_conv_b", "rglru_ba", "rglru_bx", "rglru_lambda"):
        w[n] = full_small[n][0]

    loss, grad_x, d_mod, g = local_step(x[0], loss_target[0], mod, w)
    loss = lax.psum(loss, ("x", "y", "c"))

    outs = {}
    g_full = {
        "ffn_w1": jnp.stack([jnp.stack([g[f"ffn_w1.{l}.{j}"] for j in range(2)]) for l in range(DEPTH)]),
        "ffn_w3": jnp.stack([jnp.stack([g[f"ffn_w3.{l}.{j}"] for j in range(2)]) for l in range(DEPTH)]),
        "ab_w_in": g["ab_w_in"][None, :, :AB_MAIN + 8],
        "rglru_w_in": g["rglru_w_in"][None],
    }
    for n in ("ffn_w1", "ffn_w3", "ab_w_in", "rglru_w_in"):
        slots = all_to_all(_shard_last(g_full[n]), "scatter_" + n)
        outs[n] = adamw_nd(p[n], *mom[n], slots, name="adamw_" + n)
    g_w2 = jnp.stack([jnp.stack([g[f"ffn_w2.{l}.{j}"] for j in range(2)]) for l in range(DEPTH)])
    g_w2 = jnp.moveaxis(g_w2.reshape(DEPTH, 2, NDEV, D_FF // NDEV, D_MODEL), 2, 0)
    outs["ffn_w2"] = adamw_nd(p["ffn_w2"], *mom["ffn_w2"], all_to_all(g_w2, "scatter_ffn_w2"), name="adamw_ffn_w2")
    for n in ("ab_w_out", "rglru_w_out"):
        slots = all_to_all(g[n].reshape(NDEV, 1, D_MODEL // NDEV, D_MODEL), "scatter_" + n)
        outs[n] = adamw_nd(p[n], *mom[n], slots, name="adamw_" + n)

    g["ada_b"] = d_mod.reshape(DEPTH, 9 * D_MODEL)
    small_names = REPLICATED + SHARDED_SMALL
    full_shapes = [p[n].shape for n in REPLICATED] + [full_small[n].shape for n in SHARDED_SMALL]
    all_small = all_gather(_pack([g[n] for n in small_names]), "gather_small_grads")
    zeros = jnp.zeros(all_small.shape[1:], F32)
    summed = adamw(zeros, zeros, zeros, all_small, name="sum_small_grads")[0]
    g_small = dict(zip(small_names, _unpack(summed, full_shapes)))
    rep = adamw(*[_pack([t[n] for n in REPLICATED]) for t in (p, {n: mom[n][0] for n in WEIGHTS},
                                                               {n: mom[n][1] for n in WEIGHTS})],
                _pack([g_small[n] for n in REPLICATED])[None], name="adamw_replicated")
    rep = [_unpack(r, [p[n].shape for n in REPLICATED]) for r in rep]
    for i, n in enumerate(REPLICATED):
        outs[n] = [r[i] for r in rep]
    g_mine = {n: lax.dynamic_slice_in_dim(g_small[n], me * p[n].shape[-1], p[n].shape[-1], axis=-1)
              for n in SHARDED_SMALL}
    shd = adamw(*[_pack([t[n] for n in SHARDED_SMALL]) for t in (p, {n: mom[n][0] for n in WEIGHTS},
                                                                  {n: mom[n][1] for n in WEIGHTS})],
                _pack([g_mine[n] for n in SHARDED_SMALL])[None], name="adamw_sharded_small")
    shd = [_unpack(r, sharded_shapes) for r in shd]
    for i, n in enumerate(SHARDED_SMALL):
        outs[n] = [r[i] for r in shd]

    d_ada = all_small[:, :DEPTH * 9 * D_MODEL // 128].reshape(NDEV, DEPTH, 9 * D_MODEL)
    d_mine = lax.dynamic_slice_in_dim(d_ada, me * n_ada, n_ada, axis=2)
    g_ada = []
    for l in range(DEPTH):
        d16 = jnp.pad(d_mine[:, l], ((0, 8), (0, 0)))
        (gl,) = matmul([[(c_act, d16)]], "tn", [F32], tm=D_MODEL, tn=n_ada, tk=16, name="ada_bwd")
        g_ada.append(gl)
    outs["ada_w"] = adamw_nd(ada_w, *mom["ada_w"], jnp.stack(g_ada)[None], name="adamw_ada_w")

    result = [loss, grad_x[None]]
    for k in range(4):
        result += [outs[n][k].reshape(p[n].shape) for n in WEIGHTS]
    return tuple(result)
```

```python
import functools

import jax
import jax.numpy as jnp
import numpy as np
from jax import lax
from jax.experimental import pallas as pl
from jax.experimental.pallas import tpu as pltpu

F32 = jnp.float32
BF16 = jnp.bfloat16
HIGHEST = lax.Precision.HIGHEST

NDEV = 8
D_MODEL = 1024
D_FF = 2816
DEPTH = 2
CHUNK = 64
SUB = 16
HEADS = 4
HEAD_W = 128
MIX_W = HEADS * HEAD_W
AB_MAIN = 7 * MIX_W
AB_ALL = AB_MAIN + 128
CONV_W = 4
C_BLOCKS = 8
RG_C = 8.0
ALPHA = (2 * DEPTH) ** 0.25
FFN_RES_W = 0.5
NEG = -1e30

ADAM_LR = 0.001
ADAM_B1 = 0.9
ADAM_B2 = 0.999
ADAM_EPS = 1e-08
ADAM_WD = 0.01
ADAM_STEP = 10

VMEM_LIMIT = 56 * 1024 * 1024
GRAD_WIRE = jnp.bfloat16

NN = ((1,), (0,))
NT = ((1,), (1,))
TN = ((0,), (0,))


def _dot(a, b, dims, precision=None):
    return lax.dot_general(a, b, (dims, ((), ())), precision=precision, preferred_element_type=F32)


def _make_mm(dims, d_lhs, d_rhs, swap_lhs, swap_rhs, prec):
    def cast(v):
        return v.astype(BF16) if prec is None else v.astype(F32)

    @jax.custom_vjp
    def mm(a, b):
        return _dot(cast(a), cast(b), dims, prec)

    def fwd(a, b):
        return mm(a, b), (a, b)

    def bwd(res, g):
        a, b = res
        g = cast(g)
        da = _dot(cast(b), g, d_lhs, prec) if swap_lhs else _dot(g, cast(b), d_lhs, prec)
        db = _dot(g, cast(a), d_rhs, prec) if swap_rhs else _dot(cast(a), g, d_rhs, prec)
        return da.astype(a.dtype), db.astype(b.dtype)

    mm.defvjp(fwd, bwd)
    return mm


def _mm_family(prec):
    return (_make_mm(NN, NT, TN, False, False, prec), _make_mm(NT, NN, TN, False, True, prec),
            _make_mm(TN, NT, NN, True, False, prec))


def _round(v):
    return v.astype(BF16).astype(F32)


@jax.custom_vjp
def row_dot(a, n):
    return jnp.sum(_round(a) * _round(n), axis=1, keepdims=True)


def _row_dot_fwd(a, n):
    return row_dot(a, n), (a, n)


def _row_dot_bwd(res, g):
    a, n = res
    g = _round(g)
    return g * _round(n), jnp.sum(g * _round(a), axis=0, keepdims=True)


row_dot.defvjp(_row_dot_fwd, _row_dot_bwd)


@jax.custom_vjp
def col_dot(s, a):
    return jnp.sum(_round(s) * _round(a), axis=0, keepdims=True)


def _col_dot_fwd(s, a):
    return col_dot(s, a), (s, a)


def _col_dot_bwd(res, g):
    s, a = res
    g = _round(g)
    return jnp.sum(g * _round(a), axis=1, keepdims=True), _round(s) * g


col_dot.defvjp(_col_dot_fwd, _col_dot_bwd)

mm_nn, mm_nt, mm_tn = _mm_family(None)
mid_nn, mid_nt, mid_tn = _mm_family(lax.Precision.HIGH)
hi_nn, hi_nt, hi_tn = _mm_family(HIGHEST)


def _silu(v):
    return v * jax.nn.sigmoid(v)


def _log_sigmoid(v):
    return jnp.minimum(v, 0.0) - jnp.log1p(jnp.exp(-jnp.abs(v)))


def _softplus(v):
    return jnp.maximum(v, 0.0) + jnp.log1p(jnp.exp(-jnp.abs(v)))


def _neg_expm1(v):
    series = -v * (1.0 + v * (0.5 + v * (1.0 / 6.0 + v * (1.0 / 24.0 + v * (1.0 / 120.0)))))
    return jnp.where(v > -0.05, series, 1.0 - jnp.exp(v))


def _params(**kw):
    return pltpu.CompilerParams(vmem_limit_bytes=VMEM_LIMIT, **kw)


class Rows:
    def __init__(self, arr, block=None, split=None):
        self.arr = arr
        self.block = block
        self.split = split

    @property
    def width(self):
        return self.block[1] if self.block else self.arr.shape[1]


def _load(ref, split):
    if ref.ndim == 3:
        return [ref[k].astype(F32) for k in range(ref.shape[0])]
    if split is None:
        return ref[...].astype(F32)
    return [ref[:, k * split:(k + 1) * split].astype(F32) for k in range(ref.shape[1] // split)]


def _store(ref, val, accumulate=False):
    if isinstance(val, (list, tuple)):
        if ref.ndim == 3:
            for k, v in enumerate(val):
                ref[k] = (ref[k] + v if accumulate else v).astype(ref.dtype)
            return
        w = ref.shape[1] // len(val)
        for k, v in enumerate(val):
            sl = slice(k * w, (k + 1) * w)
            ref[:, sl] = (ref[:, sl] + v if accumulate else v).astype(ref.dtype)
    else:
        ref[...] = (ref[...] + val if accumulate else val).astype(ref.dtype)


def rowwise(fn, rows, params, out_rows, out_sums, *, tile, name):
    n_rows = rows[0].arr.shape[0]
    n_r, n_p, n_o = len(rows), len(params), len(out_rows)
    splits = [r.split for r in rows]

    def body(*refs):
        r_refs, p_refs = refs[:n_r], refs[n_r:n_r + n_p]
        o_refs, s_refs = refs[n_r + n_p:n_r + n_p + n_o], refs[n_r + n_p + n_o:]
        row_out, sum_out = fn([_load(r, s) for r, s in zip(r_refs, splits)], [_load(p, None) for p in p_refs])
        for ref, val in zip(o_refs, row_out):
            _store(ref, val)
        if s_refs:
            @pl.when(pl.program_id(0) == 0)
            def _():
                for ref in s_refs:
                    ref[...] = jnp.zeros(ref.shape, ref.dtype)

            for ref, val in zip(s_refs, sum_out):
                _store(ref, val, accumulate=True)

    in_specs = []
    for r in rows:
        blk = r.block[0] if r.block else 0
        in_specs.append(pl.BlockSpec((tile, r.width), functools.partial(lambda i, b: (i, b), b=blk)))
    for p in params:
        in_specs.append(pl.BlockSpec(p.shape, functools.partial(lambda i, n: (0,) * n, n=p.ndim)))
    out_shape = [jax.ShapeDtypeStruct((n_rows, w), dt) for w, dt in out_rows]
    out_specs = [pl.BlockSpec((tile, w), lambda i: (i, 0)) for w, _ in out_rows]
    for shp in out_sums:
        out_shape.append(jax.ShapeDtypeStruct(shp, F32))
        out_specs.append(pl.BlockSpec(shp, functools.partial(lambda i, n: (0,) * n, n=len(shp))))
    res = pl.pallas_call(
        body, name=name, grid=(n_rows // tile,), in_specs=in_specs, out_specs=out_specs, out_shape=out_shape,
        compiler_params=_params(dimension_semantics=("arbitrary",)),
    )(*[r.arr for r in rows], *params)
    return res[:n_o], res[n_o:]


def rowwise_bwd(fn, rows, cots, params, want_rows, want_params, out_dtypes, *, tile, name):
    n = len(rows)

    def bwd(row_vals, param_vals):
        prim, cot = row_vals[:n], row_vals[n:]
        _, vjp = jax.vjp(lambda r, p: fn(r, p)[0], prim, param_vals)
        d_rows, d_params = vjp(cot)
        return [d_rows[i] for i in want_rows], [d_params[j] for j in want_params]

    out_rows = [(rows[i].width, dt) for i, dt in zip(want_rows, out_dtypes)]
    out_sums = [params[j].shape for j in want_params]
    return rowwise(bwd, list(rows) + list(cots), params, out_rows, out_sums, tile=tile, name=name)


def matmul(groups, mode, outs, *, tm, tn, tk, extras=(), epi=None, name):
    a0, b0 = groups[0][0]
    if mode == "tn":
        k_dim, m_dim = a0.shape
    else:
        m_dim, k_dim = a0.shape
    n_dim = b0.shape[0] if mode == "nt" else b0.shape[1]
    tm, tn, tk = min(tm, m_dim), min(tn, n_dim), min(tk, k_dim)
    assert m_dim % tm == 0 and n_dim % tn == 0 and k_dim % tk == 0, (name, m_dim, n_dim, k_dim)
    nk = k_dim // tk
    pairs = [p for g in groups for p in g]
    n_pairs, n_groups, n_ex, n_out = len(pairs), len(groups), len(extras), len(outs)
    dims = {"nn": NN, "nt": NT, "tn": TN}[mode]

    def body(*refs):
        ab = refs[:2 * n_pairs]
        ex = refs[2 * n_pairs:2 * n_pairs + n_ex]
        o_refs = refs[2 * n_pairs + n_ex:2 * n_pairs + n_ex + n_out]
        accs = refs[2 * n_pairs + n_ex + n_out:]

        def partial_sums():
            sums, p = [], 0
            for g in groups:
                tot = None
                for _ in g:
                    d = _dot(ab[2 * p][...].astype(BF16), ab[2 * p + 1][...].astype(BF16), dims)
                    tot = d if tot is None else tot + d
                    p += 1
                sums.append(tot)
            return sums

        def finish(vals):
            res = epi(vals, [e[...] for e in ex]) if epi else vals
            for ref, v in zip(o_refs, res):
                ref[...] = v.astype(ref.dtype)

        if nk == 1:
            finish(partial_sums())
        else:
            k = pl.program_id(2)

            @pl.when(k == 0)
            def _():
                for acc in accs:
                    acc[...] = jnp.zeros(acc.shape, F32)

            for acc, s in zip(accs, partial_sums()):
                acc[...] += s

            @pl.when(k == nk - 1)
            def _():
                finish([acc[...] for acc in accs])

    if mode == "nn":
        a_spec = pl.BlockSpec((tm, tk), lambda i, j, k: (i, k))
        b_spec = pl.BlockSpec((tk, tn), lambda i, j, k: (k, j))
    elif mode == "nt":
        a_spec = pl.BlockSpec((tm, tk), lambda i, j, k: (i, k))
        b_spec = pl.BlockSpec((tn, tk), lambda i, j, k: (j, k))
    else:
        a_spec = pl.BlockSpec((tk, tm), lambda i, j, k: (k, i))
        b_spec = pl.BlockSpec((tk, tn), lambda i, j, k: (k, j))
    mn_spec = pl.BlockSpec((tm, tn), lambda i, j, k: (i, j))
    return pl.pallas_call(
        body, name=name, grid=(m_dim // tm, n_dim // tn, nk),
        in_specs=[a_spec, b_spec] * n_pairs + [mn_spec] * n_ex,
        out_specs=[mn_spec] * n_out,
        out_shape=[jax.ShapeDtypeStruct((m_dim, n_dim), dt) for dt in outs],
        scratch_shapes=[pltpu.VMEM((tm, tn), F32)] * (n_groups if nk > 1 else 0),
        compiler_params=_params(dimension_semantics=("parallel", "parallel", "arbitrary")),
    )(*[x for p in pairs for x in p], *extras)


def conv_fwd(x, w, b, *, tile, name):
    n_rows, width = x.arr.shape[0], x.width
    blk = x.block[0] if x.block else 0

    def body(x_ref, halo_ref, w_ref, b_ref, y_ref, buf):
        i = pl.program_id(0)
        halo = halo_ref[...]
        buf[0:8, :] = jnp.where(i == 0, jnp.zeros_like(halo), _round(halo))
        buf[8:, :] = _round(x_ref[...])
        acc = jnp.zeros((tile, width), F32)
        for j in range(CONV_W):
            s = CONV_W - 1 - j
            acc = acc + _round(w_ref[j:j + 1, :]) * buf[8 - s:8 - s + tile, :]
        y_ref[...] = acc + b_ref[...]

    hb = tile // 8
    return pl.pallas_call(
        body, name=name, grid=(n_rows // tile,),
        in_specs=[pl.BlockSpec((tile, width), lambda i: (i, blk)),
                  pl.BlockSpec((8, width), lambda i: (jnp.maximum(i * hb - 1, 0), blk)),
                  pl.BlockSpec((CONV_W, width), lambda i: (0, 0)),
                  pl.BlockSpec((1, width), lambda i: (0, 0))],
        out_specs=pl.BlockSpec((tile, width), lambda i: (i, 0)),
        out_shape=jax.ShapeDtypeStruct((n_rows, width), F32),
        scratch_shapes=[pltpu.VMEM((tile + 8, width), F32)],
        compiler_params=_params(dimension_semantics=("arbitrary",)),
    )(x.arr, x.arr, w, b)


def conv_bwd(x, dy, w, *, tile, name):
    n_rows, width = x.arr.shape[0], x.width
    blk = x.block[0] if x.block else 0
    n_tiles = n_rows // tile

    def body(x_ref, xh_ref, dy_ref, dyh_ref, w_ref, dx_ref, dw_ref, db_ref, xbuf, dbuf):
        i = pl.program_id(0)
        xh, dyh = xh_ref[...], dyh_ref[...]
        xbuf[0:8, :] = jnp.where(i == 0, jnp.zeros_like(xh), _round(xh))
        xbuf[8:, :] = _round(x_ref[...])
        dy_t = dy_ref[...]
        dy_r = _round(dy_t)
        dbuf[0:tile, :] = dy_r
        dbuf[tile:, :] = jnp.where(i == n_tiles - 1, jnp.zeros_like(dyh), _round(dyh))

        @pl.when(i == 0)
        def _():
            dw_ref[...] = jnp.zeros(dw_ref.shape, F32)
            db_ref[...] = jnp.zeros(db_ref.shape, F32)

        acc = jnp.zeros((tile, width), F32)
        for j in range(CONV_W):
            s = CONV_W - 1 - j
            acc = acc + _round(w_ref[j:j + 1, :]) * dbuf[s:s + tile, :]
            dw_ref[j:j + 1, :] += jnp.sum(dy_r * xbuf[8 - s:8 - s + tile, :], axis=0, keepdims=True)
        dx_ref[...] = acc.astype(dx_ref.dtype)
        db_ref[...] += jnp.sum(dy_t, axis=0, keepdims=True)

    hb = tile // 8
    return pl.pallas_call(
        body, name=name, grid=(n_tiles,),
        in_specs=[pl.BlockSpec((tile, width), lambda i: (i, blk)),
                  pl.BlockSpec((8, width), lambda i: (jnp.maximum(i * hb - 1, 0), blk)),
                  pl.BlockSpec((tile, width), lambda i: (i, 0)),
                  pl.BlockSpec((8, width), lambda i: (jnp.minimum((i + 1) * hb, n_tiles * hb - 1), 0)),
                  pl.BlockSpec((CONV_W, width), lambda i: (0, 0))],
        out_specs=[pl.BlockSpec((tile, width), lambda i: (i, 0)),
                   pl.BlockSpec((CONV_W, width), lambda i: (0, 0)),
                   pl.BlockSpec((1, width), lambda i: (0, 0))],
        out_shape=[jax.ShapeDtypeStruct((n_rows, width), BF16),
                   jax.ShapeDtypeStruct((CONV_W, width), F32),
                   jax.ShapeDtypeStruct((1, width), F32)],
        scratch_shapes=[pltpu.VMEM((tile + 8, width), F32), pltpu.VMEM((tile + 8, width), F32)],
        compiler_params=_params(dimension_semantics=("arbitrary",)),
    )(x.arr, x.arr, dy, dy, w)


def _pieces(ref, col_w, row_h):
    n_c, n_r = ref.shape[1] // col_w, ref.shape[0] // row_h
    return [[ref[r * row_h:(r + 1) * row_h, c * col_w:(c + 1) * col_w].astype(F32) for r in range(n_r)]
            for c in range(n_c)]


def _store_pieces(ref, vals, col_w, row_h):
    for c, col in enumerate(vals):
        for r, v in enumerate(col):
            ref[r * row_h:(r + 1) * row_h, c * col_w:(c + 1) * col_w] = v.astype(ref.dtype)


def _x_spec(x, n_chunks, reverse):
    blk = x.block[0] if x.block else 0
    if reverse:
        return pl.BlockSpec((CHUNK, x.width), functools.partial(lambda n, b: (n_chunks - 1 - n, b), b=blk))
    return pl.BlockSpec((CHUNK, x.width), functools.partial(lambda n, b: (n, b), b=blk))


def chunk_scan_fwd(step, xs, piece, state_shapes, out_widths, *, name):
    n_rows = xs[0].arr.shape[0]
    n_chunks = n_rows // CHUNK
    n_x, n_s, n_o = len(xs), len(state_shapes), len(out_widths)

    def body(*refs):
        x_refs, o_refs = refs[:n_x], refs[n_x:n_x + n_o]
        keep_refs, st_refs = refs[n_x + n_o:n_x + n_o + n_s], refs[n_x + n_o + n_s:]

        @pl.when(pl.program_id(0) == 0)
        def _():
            for st in st_refs:
                st[...] = jnp.zeros(st.shape, F32)

        states = [st[...] for st in st_refs]
        for keep, s in zip(keep_refs, states):
            keep[...] = s
        new_states, outs = step(states, [_pieces(x, *p) for x, p in zip(x_refs, piece)])
        for st, s in zip(st_refs, new_states):
            st[...] = s
        for o, v in zip(o_refs, outs):
            o[...] = v

    out_shape = [jax.ShapeDtypeStruct((n_rows, w), F32) for w in out_widths]
    out_specs = [pl.BlockSpec((CHUNK, w), lambda n: (n, 0)) for w in out_widths]
    for shp in state_shapes:
        out_shape.append(jax.ShapeDtypeStruct((n_chunks,) + shp, F32))
        out_specs.append(pl.BlockSpec((None,) + shp, lambda n: (n, 0, 0)))
    res = pl.pallas_call(
        body, name=name, grid=(n_chunks,),
        in_specs=[_x_spec(x, n_chunks, False) for x in xs],
        out_specs=out_specs, out_shape=out_shape,
        scratch_shapes=[pltpu.VMEM(shp, F32) for shp in state_shapes],
        compiler_params=_params(dimension_semantics=("arbitrary",)),
    )(*[x.arr for x in xs])
    return res[:n_o], res[n_o:]


def chunk_scan_bwd(step, xs, piece, kept, d_outs, *, name):
    n_rows = xs[0].arr.shape[0]
    n_chunks = n_rows // CHUNK
    n_x, n_s, n_o = len(xs), len(kept), len(d_outs)
    state_shapes = [k.shape[1:] for k in kept]

    def body(*refs):
        x_refs, k_refs = refs[:n_x], refs[n_x:n_x + n_s]
        do_refs = refs[n_x + n_s:n_x + n_s + n_o]
        dx_refs = refs[n_x + n_s + n_o:2 * n_x + n_s + n_o]
        ds_refs = refs[2 * n_x + n_s + n_o:]

        @pl.when(pl.program_id(0) == 0)
        def _():
            for ds in ds_refs:
                ds[...] = jnp.zeros(ds.shape, F32)

        states = [k[...] for k in k_refs]
        inputs = [_pieces(x, *p) for x, p in zip(x_refs, piece)]
        _, vjp = jax.vjp(step, states, inputs)
        d_states, d_inputs = vjp(([ds[...] for ds in ds_refs], [do[...] for do in do_refs]))
        for ds, v in zip(ds_refs, d_states):
            ds[...] = v
        for dx, v, p in zip(dx_refs, d_inputs, piece):
            _store_pieces(dx, v, *p)

    rev3 = lambda n: (n_chunks - 1 - n, 0, 0)
    rev2 = lambda n: (n_chunks - 1 - n, 0)
    return pl.pallas_call(
        body, name=name, grid=(n_chunks,),
        in_specs=[_x_spec(x, n_chunks, True) for x in xs]
        + [pl.BlockSpec((None,) + shp, rev3) for shp in state_shapes]
        + [pl.BlockSpec((CHUNK, d.shape[1]), rev2) for d in d_outs],
        out_specs=[pl.BlockSpec((CHUNK, x.width), rev2) for x in xs],
        out_shape=[jax.ShapeDtypeStruct((n_rows, x.width), F32) for x in xs],
        scratch_shapes=[pltpu.VMEM(shp, F32) for shp in state_shapes],
        compiler_params=_params(dimension_semantics=("arbitrary",)),
    )(*[x.arr for x in xs], *kept, *d_outs)


def _tri(n, strict=False):
    r = lax.broadcasted_iota(jnp.int32, (n, n), 0)
    c = lax.broadcasted_iota(jnp.int32, (n, n), 1)
    return (r > c) if strict else (r >= c)


def hgrn2_step(states, inputs):
    q_all, k_all, v_all, lf_all = inputs
    n_sub = CHUNK // SUB
    low = _tri(SUB).astype(F32)
    ones_sub = jnp.ones((SUB, SUB), F32)
    ones_chunk = jnp.ones((CHUNK, HEAD_W), F32)
    new_states, outs = [], []
    for h in range(HEADS):
        state = states[h]
        q, k, v, lf = q_all[h], k_all[h], v_all[h], lf_all[h]
        cum = [hi_nn(low, lf[i]) for i in range(n_sub)]
        tot = [hi_nn(ones_sub, lf[i]) for i in range(n_sub)]
        start = [jnp.zeros((SUB, HEAD_W), F32)]
        for i in range(n_sub):
            start.append(start[-1] + tot[i])
        q_in = [q[i] * jnp.exp(cum[i]) for i in range(n_sub)]
        intra = []
        for i in range(n_sub):
            keys = [k[j] * jnp.exp(start[i] - start[j] - cum[j]) for j in range(i)]
            keys.append(k[i] * jnp.exp(jnp.minimum(-cum[i], 80.0)))
            att = mid_nt(q_in[i], jnp.concatenate(keys, axis=0))
            r_id = lax.broadcasted_iota(jnp.int32, att.shape, 0)
            c_id = lax.broadcasted_iota(jnp.int32, att.shape, 1)
            att = jnp.where(c_id - SUB * i <= r_id, att, 0.0)
            intra.append(mm_nn(att, jnp.concatenate(v[:i + 1], axis=0)))
        q_state = jnp.concatenate([q_in[i] * jnp.exp(start[i]) for i in range(n_sub)], axis=0)
        out = mm_nn(q_state, state) + jnp.concatenate(intra, axis=0)
        k_end = jnp.concatenate([k[j] * jnp.exp(start[n_sub] - start[j] - cum[j]) for j in range(n_sub)], axis=0)
        decay = jnp.exp(hi_tn(jnp.concatenate(lf, axis=0), ones_chunk))
        new_states.append(decay * state + mm_tn(k_end, jnp.concatenate(v, axis=0)))
        outs.append(out)
    return new_states, [jnp.concatenate(outs, axis=1)]


def mlstm_step(states, inputs):
    q_all, k_all, v_all, gates = inputs
    gates = gates[0][0]
    c_st, n_st, m_st = states[:HEADS], states[HEADS:2 * HEADS], states[2 * HEADS:]
    lane = lax.broadcasted_iota(jnp.int32, (CHUNK, 128), 1)
    low = _tri(CHUNK).astype(F32)
    causal = _tri(CHUNK)
    gates_cum = hi_nn(low, gates)
    new_c, new_n, new_m, outs = [], [], [], []
    for h in range(HEADS):
        q, k, v = q_all[h][0], k_all[h][0], v_all[h][0]
        pick_i = (lane == h).astype(F32)
        pick_f = (lane == HEADS + h).astype(F32)
        li_col = jnp.sum(gates * pick_i, axis=1, keepdims=True)
        lf_col = jnp.sum(gates * pick_f, axis=1, keepdims=True)
        b_col = jnp.sum(gates_cum * pick_f, axis=1, keepdims=True)
        by_key = hi_nt(pick_i, gates) - hi_nt(pick_f, gates_cum)
        d_mat = jnp.where(causal, b_col + by_key, NEG)
        m_prev = lax.stop_gradient(jnp.max(m_st[h], axis=1, keepdims=True))
        g_inter = b_col + m_prev
        m_t = lax.stop_gradient(jnp.maximum(g_inter, jnp.max(d_mat, axis=1, keepdims=True)))
        w_inter = jnp.exp(g_inter - m_t)
        aw = jnp.exp(d_mat - m_t) * mm_nt(q, k)
        num = w_inter * mm_nn(q, c_st[h]) + mm_nn(aw, v)
        den = w_inter * row_dot(q, n_st[h]) + jnp.sum(aw, axis=1, keepdims=True)
        outs.append(num / jnp.maximum(jnp.abs(den), jnp.exp(-m_t)))
        b_end = jnp.sum(lf_col, axis=0, keepdims=True)
        g_state = b_end + m_prev
        s_w = b_end - b_col + li_col
        m_next = lax.stop_gradient(jnp.maximum(g_state, jnp.max(s_w, axis=0, keepdims=True)))
        dec = jnp.exp(g_state - m_next)
        w_s = jnp.exp(s_w - m_next)
        kw = k * w_s
        new_c.append(dec * c_st[h] + mm_tn(kw, v))
        new_n.append(dec * n_st[h] + col_dot(w_s, k))
        new_m.append(jnp.broadcast_to(m_next, (1, HEAD_W)))
    return new_c + new_n + new_m, [jnp.concatenate(outs, axis=1)]


def lru_fwd(a, u, *, tile, name):
    n_rows, width = a.shape

    def body(a_ref, u_ref, h_ref, carry):
        @pl.when(pl.program_id(0) == 0)
        def _():
            carry[...] = jnp.zeros(carry.shape, F32)

        h = carry[...]
        for t in range(tile):
            h = a_ref[t:t + 1, :] * h + u_ref[t:t + 1, :]
            h_ref[t:t + 1, :] = h
        carry[...] = h

    spec = pl.BlockSpec((tile, width), lambda i: (i, 0))
    return pl.pallas_call(
        body, name=name, grid=(n_rows // tile,), in_specs=[spec, spec], out_specs=spec,
        out_shape=jax.ShapeDtypeStruct((n_rows, width), F32),
        scratch_shapes=[pltpu.VMEM((1, width), F32)],
        compiler_params=_params(dimension_semantics=("arbitrary",)),
    )(a, u)


def lru_bwd(a, h, dh, *, tile, name):
    n_rows, width = a.shape
    n_tiles = n_rows // tile
    hb = tile // 8

    def body(a_ref, h_ref, hh_ref, dh_ref, da_ref, du_ref, carry):
        i = pl.program_id(0)

        @pl.when(i == 0)
        def _():
            carry[...] = jnp.zeros(carry.shape, F32)

        c = carry[...]
        for t in range(tile - 1, -1, -1):
            g = dh_ref[t:t + 1, :] + c
            du_ref[t:t + 1, :] = g
            if t:
                h_prev = h_ref[t - 1:t, :]
            else:
                h_prev = jnp.where(i == n_tiles - 1, 0.0, hh_ref[7:8, :])
            da_ref[t:t + 1, :] = g * h_prev
            c = a_ref[t:t + 1, :] * g
        carry[...] = c

    rev = lambda i: (n_tiles - 1 - i, 0)
    spec = pl.BlockSpec((tile, width), rev)
    halo = pl.BlockSpec((8, width), lambda i: (jnp.maximum((n_tiles - 1 - i) * hb - 1, 0), 0))
    return pl.pallas_call(
        body, name=name, grid=(n_tiles,), in_specs=[spec, spec, halo, spec], out_specs=[spec, spec],
        out_shape=[jax.ShapeDtypeStruct((n_rows, width), F32)] * 2,
        scratch_shapes=[pltpu.VMEM((1, width), F32)],
        compiler_params=_params(dimension_semantics=("arbitrary",)),
    )(a, h, h, dh)


def _layer_norm(z, g, b):
    mu = jnp.mean(z, axis=-1, keepdims=True)
    zc = z - mu
    var = jnp.mean(zc * zc, axis=-1, keepdims=True)
    return zc * lax.rsqrt(var + 1e-5) * g + b


def pre_fn(rows, params):
    (x,), (scale, shift) = rows, params
    return [x * (1.0 + scale) + shift], []


def make_post_fn(weight, with_next):
    def fn(rows, params):
        x, y = rows
        gate, g, b = params[:3]
        xo = _layer_norm(ALPHA * x + weight * (1.0 + gate) * y, g, b)
        if with_next:
            return [xo, xo * (1.0 + params[3]) + params[4]], []
        return [xo], []

    return fn


def make_last_fn(weight):
    post = make_post_fn(weight, False)

    def fn(rows, params):
        x, y, target = rows
        err = post([x, y], params)[0][0] - target
        loss = 0.5 * jnp.sum(jnp.mean(err * err, axis=-1, keepdims=True), axis=0, keepdims=True)
        return [err * (1.0 / D_MODEL)], [jnp.broadcast_to(loss, (1, 128))]

    return fn


def mix_a_fn(rows, params):
    a_q, a_f, xconv, graw = rows
    l0, l1, l2, wq, wk, gate_b = params
    mx = jnp.maximum(jnp.maximum(l0, l1), l2)
    e0, e1, e2 = jnp.exp(l0 - mx), jnp.exp(l1 - mx), jnp.exp(l2 - mx)
    lb = e0 / (e0 + e1 + e2)
    f = lb + (1.0 - lb) * jax.nn.sigmoid(a_f)
    xc = _silu(xconv)
    q_b = mm_nn(xc, wq)
    k_b = mm_nn(xc, wk) * (HEAD_W ** -0.5)
    g = graw + gate_b
    lane = lax.broadcasted_iota(jnp.int32, g.shape, 1)
    gates = jnp.where(lane < HEADS, g, _log_sigmoid(g))
    return [_silu(a_q), 1.0 - f, jnp.log(f), xc, q_b, k_b, gates], []


def _head_norm(v, g, center):
    if center:
        v = v - jnp.mean(v, axis=-1, keepdims=True)
    return v * lax.rsqrt(jnp.mean(v * v, axis=-1, keepdims=True) + 1e-6) * g


def mix_b_fn(rows, params):
    o_a, a_g, h_b, xc, b_z = rows
    hg, mg, skip = params
    y_a = [_head_norm(o_a[h], hg[h], False) * _silu(a_g[h]) for h in range(HEADS)]
    y_b = [(_head_norm(h_b[h], mg[h], True) + skip[h] * xc[h]) * _silu(b_z[h]) for h in range(HEADS)]
    return [y_a + y_b], []


def lru_a_fn(rows, params):
    (xr,) = rows
    wa, wx, ba, bx, lam = params
    a_out, u_out = [], []
    for n in range(C_BLOCKS):
        r = jax.nn.sigmoid(mm_nt(xr[n], wa[n]) + ba[n])
        i = jax.nn.sigmoid(mm_nt(xr[n], wx[n]) + bx[n])
        log_a = -RG_C * r * _softplus(-lam[n])
        a_out.append(jnp.exp(log_a))
        u_out.append(jnp.sqrt(_neg_expm1(2.0 * log_a)) * i * xr[n])
    return [a_out, u_out], []


def lru_b_fn(rows, params):
    h, y_br = rows
    return [h * jax.nn.gelu(y_br)], []


def swiglu_epi(accs, extras):
    h1, h3 = accs
    return [h1, h3, _silu(h1) * h3]


def swiglu_bwd_epi(accs, extras):
    (da,), (h1, h3) = accs, extras
    h1, h3 = h1.astype(F32), h3.astype(F32)
    sig = jax.nn.sigmoid(h1)
    return [da * h3 * sig * (1.0 + h1 * (1.0 - sig)), da * h1 * sig]


ROW_TILE = 256


def _chunks(v, n):
    return v.reshape(n, 1, v.shape[-1] // n)


def _dense_blocks(w):
    n, b, _ = w.shape
    eye = jnp.eye(n, dtype=w.dtype)
    return (eye[:, None, :, None] * jnp.swapaxes(w, 1, 2)[:, :, None, :]).reshape(n * b, n * b)


def _block_diag_of(m, n, b):
    m4 = m.reshape(n, b, n, b)
    idx = jnp.arange(n)
    return jnp.swapaxes(m4[idx, :, idx, :], 1, 2)


def local_step(x, target, mod, w, get_weights, put_grads):
    s_len = x.shape[0]
    tile = min(ROW_TILE, s_len)
    row = lambda v: v.reshape(1, -1)
    mrow = lambda l, j, k: mod[l, 3 * j + k].reshape(1, D_MODEL)
    g = {}
    d_mod = [[None] * 9 for _ in range(DEPTH)]
    d_ln_g = [[None] * 3 for _ in range(DEPTH)]
    d_ln_b = [[None] * 3 for _ in range(DEPTH)]
    subs = [(l, j) for l in range(DEPTH) for j in range(3)]
    weight_of = lambda j: 1.0 if j == 1 else FFN_RES_W

    wq_d = _dense_blocks(w["mlstm_wq"])
    wk_d = _dense_blocks(w["mlstm_wk"])
    wa_b, wx_b = w["rglru_wa"].astype(BF16), w["rglru_wx"].astype(BF16)
    gate_b = jnp.pad(w["mlstm_gate_b"].reshape(1, 8), ((0, 0), (0, 120)))
    lb_rows = [row(w["hgrn_lb_logits"][k]) for k in range(3)]
    mix_a_params = lb_rows + [wq_d, wk_d, gate_b]
    mix_b_params = [_chunks(row(w["hgrn_norm_g"]), HEADS), _chunks(row(w["mlstm_norm_g"]), HEADS),
                    _chunks(row(w["mlstm_skip"]), HEADS)]
    lru_a_params = [wa_b, wx_b, _chunks(row(w["rglru_ba"]), C_BLOCKS), _chunks(row(w["rglru_bx"]), C_BLOCKS),
                    _chunks(row(w["rglru_lambda"]), C_BLOCKS)]
    mconv_w, mconv_b = w["mlstm_conv_w"], row(w["mlstm_conv_b"])
    rconv_w, rconv_b = w["rglru_conv_w"], row(w["rglru_conv_b"])
    hg_piece = [(HEAD_W, SUB)] * 4
    ml_piece = [(HEAD_W, CHUNK)] * 3 + [(128, CHUNK)]

    (t,), _ = rowwise(pre_fn, [Rows(x)], [mrow(0, 0, 1), mrow(0, 0, 0)], [(D_MODEL, BF16)], [],
                      tile=tile, name="pre")
    saved = {}
    x_in = x
    for idx, (l, j) in enumerate(subs):
        sv = {"x": x_in, "t": t}
        if j != 1:
            w1, w3, w2 = get_weights(("ffn", l, j // 2), t)
            h1, h3, act = matmul([[(t, w1)], [(t, w3)]], "nn", [F32, F32, BF16], tm=512, tn=1408, tk=D_MODEL,
                                 epi=swiglu_epi, name="ffn_up")
            (y,) = matmul([[(act, w2)]], "nn", [F32], tm=512, tn=1024, tk=1408, name="ffn_down")
            sv.update(h1=h1, h3=h3, act=act, big=(w1, w3, w2))
        elif l == 0:
            ab_w_in, ab_w_out = get_weights(("ab",), t)
            sv["big"] = (ab_w_in, ab_w_out)
            (proj,) = matmul([[(t, ab_w_in)]], "nn", [F32], tm=256, tn=AB_ALL, tk=D_MODEL, name="ab_in")
            xconv = conv_fwd(Rows(proj, (4, MIX_W)), mconv_w, mconv_b, tile=tile, name="mconv")
            a_rows = [Rows(proj, (0, MIX_W)), Rows(proj, (1, MIX_W)), Rows(xconv), Rows(proj, (AB_MAIN // 128, 128))]
            (q_a, k_a, lf_a, xc, q_b, k_b, gates), _ = rowwise(
                mix_a_fn, a_rows, mix_a_params, [(MIX_W, F32)] * 6 + [(128, F32)], [], tile=tile, name="mix_a")
            hg_xs = [Rows(q_a), Rows(k_a), Rows(proj, (2, MIX_W)), Rows(lf_a)]
            (o_a,), hg_kept = chunk_scan_fwd(hgrn2_step, hg_xs, hg_piece, [(HEAD_W, HEAD_W)] * HEADS, [MIX_W],
                                             name="hgrn2_fwd")
            ml_xs = [Rows(q_b), Rows(k_b), Rows(proj, (5, MIX_W)), Rows(gates)]
            ml_states = [(HEAD_W, HEAD_W)] * HEADS + [(1, HEAD_W)] * (2 * HEADS)
            (h_b,), ml_kept = chunk_scan_fwd(mlstm_step, ml_xs, ml_piece, ml_states, [MIX_W], name="mlstm_fwd")
            b_rows = [Rows(o_a, split=HEAD_W), Rows(proj, (3, MIX_W), HEAD_W), Rows(h_b, split=HEAD_W),
                      Rows(xc, split=HEAD_W), Rows(proj, (6, MIX_W), HEAD_W)]
            (ycat,), _ = rowwise(mix_b_fn, b_rows, mix_b_params, [(2 * MIX_W, BF16)], [], tile=tile, name="mix_b")
            (y,) = matmul([[(ycat, ab_w_out)]], "nn", [F32], tm=512, tn=1024, tk=D_MODEL, name="ab_out")
            sv.update(proj=proj, xconv=xconv, a_rows=a_rows, hg_xs=hg_xs, hg_kept=hg_kept, ml_xs=ml_xs,
                      ml_kept=ml_kept, b_rows=b_rows, ycat=ycat)
        else:
            rg_w_in, rg_w_out = get_weights(("rg",), t)
            sv["big"] = (rg_w_in, rg_w_out)
            (proj,) = matmul([[(t, rg_w_in)]], "nn", [F32], tm=512, tn=1024, tk=D_MODEL, name="rg_in")
            xr = conv_fwd(Rows(proj, (1, D_MODEL)), rconv_w, rconv_b, tile=tile, name="rconv")
            (a_t, u_t), _ = rowwise(lru_a_fn, [Rows(xr, split=128)], lru_a_params, [(D_MODEL, F32)] * 2, [],
                                    tile=tile, name="lru_a")
            h = lru_fwd(a_t, u_t, tile=min(128, s_len), name="lru_fwd")
            b_rows = [Rows(h), Rows(proj, (0, D_MODEL))]
            (hgate,), _ = rowwise(lru_b_fn, b_rows, [], [(D_MODEL, BF16)], [], tile=tile, name="lru_b")
            (y,) = matmul([[(hgate, rg_w_out)]], "nn", [F32], tm=512, tn=1024, tk=D_MODEL, name="rg_out")
            sv.update(proj=proj, xr=xr, a_t=a_t, h=h, b_rows=b_rows, hgate=hgate)
        sv["y"] = y
        post_params = [mrow(l, j, 2), row(w["ln_g"][l, j]), row(w["ln_b"][l, j])]
        if idx + 1 < len(subs):
            nl, nj = subs[idx + 1]
            post_params += [mrow(nl, nj, 1), mrow(nl, nj, 0)]
            (x_out, t), _ = rowwise(make_post_fn(weight_of(j), True), [Rows(x_in), Rows(y)], post_params,
                                    [(D_MODEL, F32), (D_MODEL, BF16)], [], tile=tile, name="post")
        else:
            (d_xo,), (loss_row,) = rowwise(make_last_fn(weight_of(j)), [Rows(x_in), Rows(y), Rows(target)],
                                           post_params, [(D_MODEL, F32)], [(1, 128)], tile=tile, name="post_loss")
            x_out = None
        sv["post_params"] = post_params
        saved[(l, j)] = sv
        x_in = x_out
    loss = loss_row[0, 0]

    d_t_next = None
    for idx in range(len(subs) - 1, -1, -1):
        l, j = subs[idx]
        sv = saved[(l, j)]
        has_next = idx + 1 < len(subs)
        cots = [Rows(d_xo)] + ([Rows(d_t_next)] if has_next else [])
        want_p = [0, 1, 2] + ([3, 4] if has_next else [])
        (d_xres, d_y), d_par = rowwise_bwd(
            make_post_fn(weight_of(j), has_next), [Rows(sv["x"]), Rows(sv["y"])], cots, sv["post_params"],
            [0, 1], want_p, [F32, BF16], tile=tile, name="post_bwd")
        d_mod[l][3 * j + 2], d_ln_g[l][j], d_ln_b[l][j] = d_par[:3]
        if has_next:
            nl, nj = subs[idx + 1]
            d_mod[nl][3 * nj + 1], d_mod[nl][3 * nj] = d_par[3:]
        t = sv["t"]
        if j != 1:
            w1, w3, w2 = sv["big"]
            d_h1, d_h3 = matmul([[(d_y, w2)]], "nt", [BF16, BF16], tm=512, tn=1408, tk=D_MODEL,
                                extras=[sv["h1"], sv["h3"]], epi=swiglu_bwd_epi, name="ffn_down_bwd")
            (g_w2,) = matmul([[(sv["act"], d_y)]], "tn", [GRAD_WIRE], tm=1408, tn=1024, tk=512,
                                                 name="ffn_dw2")
            (g_w1,) = matmul([[(t, d_h1)]], "tn", [GRAD_WIRE], tm=1024, tn=1408, tk=512,
                                                 name="ffn_dw1")
            (g_w3,) = matmul([[(t, d_h3)]], "tn", [GRAD_WIRE], tm=1024, tn=1408, tk=512,
                                                 name="ffn_dw3")
            put_grads(("ffn", l, j // 2), [g_w1, g_w3, g_w2])
            (d_t,) = matmul([[(d_h1, w1), (d_h3, w3)]], "nt", [F32], tm=512, tn=1024, tk=1408, name="ffn_up_bwd")
        elif l == 0:
            ab_w_in, ab_w_out = sv["big"]
            (d_ycat,) = matmul([[(d_y, ab_w_out)]], "nt", [F32], tm=512, tn=1024, tk=D_MODEL, name="ab_out_bwd")
            (g_out,) = matmul([[(sv["ycat"], d_y)]], "tn", [GRAD_WIRE], tm=1024, tn=1024, tk=512, name="ab_dwout")
            (d_oa, d_ag, d_hb, d_xc, d_bz), (d_hg, d_mg, d_skip) = rowwise_bwd(
                mix_b_fn, sv["b_rows"], [Rows(d_ycat, split=HEAD_W)], mix_b_params, [0, 1, 2, 3, 4], [0, 1, 2],
                [F32, BF16, F32, F32, BF16], tile=tile, name="mix_b_bwd")
            g["hgrn_norm_g"], g["mlstm_norm_g"], g["mlstm_skip"] = (v.reshape(1, MIX_W) for v in (d_hg, d_mg, d_skip))
            d_qb, d_kb, d_bv, d_gates = chunk_scan_bwd(mlstm_step, sv["ml_xs"], ml_piece, sv["ml_kept"], [d_hb],
                                                       name="mlstm_bwd")
            d_qa, d_ka, d_ai, d_lf = chunk_scan_bwd(hgrn2_step, sv["hg_xs"], hg_piece, sv["hg_kept"], [d_oa],
                                                    name="hgrn2_bwd")
            a_cots = [Rows(v) for v in (d_qa, d_ka, d_lf, d_xc, d_qb, d_kb, d_gates)]
            (d_aq, d_af, d_xconv, d_graw), (d_l0, d_l1, d_l2, d_wq, d_wk, d_gb) = rowwise_bwd(
                mix_a_fn, sv["a_rows"], a_cots, mix_a_params, [0, 1, 2, 3], [0, 1, 2, 3, 4, 5],
                [BF16, BF16, F32, BF16], tile=tile, name="mix_a_bwd")
            g["hgrn_lb_logits"] = jnp.concatenate([d_l0, d_l1, d_l2], axis=0)
            g["mlstm_wq"] = _block_diag_of(d_wq, MIX_W // 4, 4)
            g["mlstm_wk"] = _block_diag_of(d_wk, MIX_W // 4, 4)
            g["mlstm_gate_b"] = d_gb[:, :8]
            d_bx, g["mlstm_conv_w"], g["mlstm_conv_b"] = conv_bwd(Rows(sv["proj"], (4, MIX_W)), d_xconv, mconv_w,
                                                                  tile=tile, name="mconv_bwd")
            d_proj = jnp.concatenate([d_aq, d_af, d_ai.astype(BF16), d_ag, d_bx, d_bv.astype(BF16), d_bz, d_graw],
                                     axis=1)
            (g_in,) = matmul([[(t, d_proj)]], "tn", [GRAD_WIRE], tm=256, tn=AB_ALL, tk=512, name="ab_dwin")
            put_grads(("ab",), [g_in, g_out])
            (d_t,) = matmul([[(d_proj, ab_w_in)]], "nt", [F32], tm=512, tn=1024, tk=AB_ALL, name="ab_in_bwd")
        else:
            rg_w_in, rg_w_out = sv["big"]
            (d_hgate,) = matmul([[(d_y, rg_w_out)]], "nt", [F32], tm=512, tn=1024, tk=D_MODEL,
                                name="rg_out_bwd")
            (g_out,) = matmul([[(sv["hgate"], d_y)]], "tn", [GRAD_WIRE], tm=1024, tn=1024, tk=512,
                                         name="rg_dwout")
            (d_h, d_ybr), _ = rowwise_bwd(lru_b_fn, sv["b_rows"], [Rows(d_hgate)], [], [0, 1], [], [F32, BF16],
                                          tile=tile, name="lru_b_bwd")
            d_a, d_u = lru_bwd(sv["a_t"], sv["h"], d_h, tile=min(128, s_len), name="lru_bwd")
            (d_xr,), (d_wa, d_wx, d_ba, d_bx_, d_lam) = rowwise_bwd(
                lru_a_fn, [Rows(sv["xr"], split=128)], [Rows(d_a, split=128), Rows(d_u, split=128)], lru_a_params,
                [0], [0, 1, 2, 3, 4], [F32], tile=tile, name="lru_a_bwd")
            g["rglru_wa"], g["rglru_wx"] = d_wa, d_wx
            g["rglru_ba"], g["rglru_bx"], g["rglru_lambda"] = (v.reshape(1, D_MODEL) for v in (d_ba, d_bx_, d_lam))
            d_xbr, g["rglru_conv_w"], g["rglru_conv_b"] = conv_bwd(Rows(sv["proj"], (1, D_MODEL)), d_xr, rconv_w,
                                                                   tile=tile, name="rconv_bwd")
            d_proj = jnp.concatenate([d_ybr, d_xbr], axis=1)
            (g_in,) = matmul([[(t, d_proj)]], "tn", [GRAD_WIRE], tm=1024, tn=1024, tk=512, name="rg_dwin")
            put_grads(("rg",), [g_in, g_out])
            (d_t,) = matmul([[(d_proj, rg_w_in)]], "nt", [F32], tm=512, tn=1024, tk=1024, name="rg_in_bwd")
        d_xo, d_t_next = d_xres, d_t

    def first_bwd(rows, params):
        x0, d_res, d_t0 = rows
        _, vjp = jax.vjp(lambda r, p: pre_fn(r, p)[0], [x0], params)
        (d_x0,), d_p = vjp([d_t0])
        return [d_res + d_x0], d_p

    (grad_x,), (d_mod[0][1], d_mod[0][0]) = rowwise(
        first_bwd, [Rows(x), Rows(d_xo), Rows(d_t_next)], [mrow(0, 0, 1), mrow(0, 0, 0)], [(D_MODEL, F32)],
        [(1, D_MODEL)] * 2, tile=tile, name="pre_bwd")
    g["ln_g"] = jnp.stack([jnp.concatenate(r, axis=0) for r in d_ln_g])
    g["ln_b"] = jnp.stack([jnp.concatenate(r, axis=0) for r in d_ln_b])
    d_mod = jnp.stack([jnp.concatenate(r, axis=0) for r in d_mod])
    return loss, grad_x, d_mod, g


MESH_ID = pl.DeviceIdType.MESH
ANY_SPEC = pl.BlockSpec(memory_space=pl.ANY)


def _my_position():
    return lax.axis_index("x"), lax.axis_index("y"), lax.axis_index("c")


def _flat_index(pos):
    return 4 * pos[0] + 2 * pos[1] + pos[2]


def _peer_position(pos, k):
    return tuple(lax.rem(p + ((k >> s) & 1), 2) for p, s in zip(pos, (2, 1, 0)))


def _exchange(x, gather, name):
    out_shape = (NDEV,) + x.shape if gather else x.shape

    def body(x_ref, o_ref, send_sems, recv_sems, local_sem):
        pos = _my_position()
        me = _flat_index(pos)
        local = pltpu.make_async_copy(x_ref if gather else x_ref.at[me], o_ref.at[me], local_sem)
        local.start()
        copies = []
        for k in range(1, NDEV):
            peer = _peer_position(pos, k)
            src = x_ref if gather else x_ref.at[_flat_index(peer)]
            copies.append(pltpu.make_async_remote_copy(
                src_ref=src, dst_ref=o_ref.at[me], send_sem=send_sems.at[k - 1], recv_sem=recv_sems.at[k - 1],
                device_id=peer, device_id_type=MESH_ID))
            copies[-1].start()
        for cp in copies:
            cp.wait()
        local.wait()

    return pl.pallas_call(
        body, name=name, in_specs=[ANY_SPEC], out_specs=ANY_SPEC,
        out_shape=jax.ShapeDtypeStruct(out_shape, x.dtype),
        scratch_shapes=[pltpu.SemaphoreType.DMA((NDEV - 1,)), pltpu.SemaphoreType.DMA((NDEV - 1,)),
                        pltpu.SemaphoreType.DMA],
    )(x)


HBM_SPEC = pl.BlockSpec(memory_space=pltpu.HBM)
SEM_SPEC = pl.BlockSpec(memory_space=pltpu.SEMAPHORE)
SIDE_EFFECT = pltpu.SideEffectType.DATAFLOW_SIDE_EFFECTING


def _exchange_copies(x_refs, land_refs, send_sems, recv_sems, gather):
    pos = _my_position()
    me = _flat_index(pos)
    copies = []
    for k in range(1, NDEV):
        peer = _peer_position(pos, k)
        for x_ref, land_ref, s_sem, r_sem in zip(x_refs, land_refs, send_sems, recv_sems):
            src = x_ref if gather else x_ref.at[_flat_index(peer)]
            copies.append(pltpu.make_async_remote_copy(src_ref=src, dst_ref=land_ref.at[me], send_sem=s_sem,
                                                       recv_sem=r_sem, device_id=peer, device_id_type=MESH_ID))
    return copies


def exchange_start(xs, gather, name):
    n = len(xs)
    land_shapes = [(NDEV,) + x.shape if gather else x.shape for x in xs]

    def body(*refs):
        x_refs, land_refs = refs[:n], refs[n:2 * n]
        send_sems, recv_sems = refs[2 * n:3 * n], refs[3 * n:4 * n]
        token = refs[-1]
        for cp in _exchange_copies(x_refs, land_refs, send_sems, recv_sems, gather):
            cp.start()
        token[...] = jnp.zeros(token.shape, token.dtype)

    sem = pltpu.SemaphoreType.DMA(())
    res = pl.pallas_call(
        body, name=name,
        out_shape=[sem] * (2 * n) + [pltpu.HBM(x.shape, x.dtype) for x in xs]
        + [pltpu.HBM(s, x.dtype) for s, x in zip(land_shapes, xs)] + [jax.ShapeDtypeStruct((8, 128), F32)],
        in_specs=[HBM_SPEC] * (2 * n),
        out_specs=[SEM_SPEC] * (2 * n) + [HBM_SPEC] * (2 * n) + [pl.BlockSpec(memory_space=pltpu.VMEM)],
        input_output_aliases={i: 2 * n + i for i in range(2 * n)},
        compiler_params=pltpu.CompilerParams(has_side_effects=SIDE_EFFECT),
    )(*[pltpu.with_memory_space_constraint(x, pltpu.HBM) for x in xs],
      *[pltpu.with_memory_space_constraint(lax.empty(s, x.dtype), pltpu.HBM) for s, x in zip(land_shapes, xs)])
    return (res[:n], res[n:2 * n], res[2 * n:3 * n], res[3 * n:4 * n]), res[-1]


def exchange_wait(handles, after, name):
    send_sems, recv_sems, x_thru, land_thru = handles
    n = len(x_thru)

    def body(*refs):
        land_refs = refs[n:2 * n]
        s_sems, r_sems = refs[2 * n:3 * n], refs[3 * n:4 * n]
        pos = _my_position()
        for land_ref, s_sem, r_sem in zip(land_refs, s_sems, r_sems):
            seven = land_ref.at[pl.ds(0, NDEV - 1)]
            all_seven = pltpu.make_async_remote_copy(src_ref=seven, dst_ref=seven, send_sem=s_sem, recv_sem=r_sem,
                                                     device_id=pos, device_id_type=MESH_ID)
            all_seven.wait_send()
            all_seven.wait_recv()

    res = pl.pallas_call(
        body, name=name,
        out_shape=[pltpu.HBM(x.shape, x.dtype) for x in x_thru] + [pltpu.HBM(x.shape, x.dtype) for x in land_thru],
        in_specs=[HBM_SPEC] * (2 * n) + [SEM_SPEC] * (2 * n) + [ANY_SPEC],
        out_specs=[HBM_SPEC] * (2 * n),
        input_output_aliases={i: i for i in range(2 * n)},
        compiler_params=pltpu.CompilerParams(has_side_effects=SIDE_EFFECT),
    )(*x_thru, *land_thru, *send_sems, *recv_sems, after)
    return res[:n], res[n:]


def all_gather(x, name):
    return _exchange(x, True, name)


def all_to_all(x, name):
    return _exchange(x, False, name)


def _row_tile(n_rows, cap):
    best = None
    for t in range(8, min(n_rows, cap) + 1, 8):
        if n_rows % t == 0:
            best = t
    return best if best else n_rows


def adamw(w, m, v, slots, *, name):
    n_rows, width = w.shape
    n_slots = slots.shape[0]
    lanes = -(-width // 128) * 128
    tile = _row_tile(n_rows, max(8, (1 << 20) // (4 * lanes) // 8 * 8))
    bc1 = 1.0 - ADAM_B1 ** ADAM_STEP
    bc2 = 1.0 - ADAM_B2 ** ADAM_STEP

    def body(w_ref, m_ref, v_ref, s_ref, g_ref, d_ref, nm_ref, nv_ref):
        g = s_ref[0].astype(F32)
        for k in range(1, n_slots):
            g = g + s_ref[k].astype(F32)
        wv = w_ref[...]
        nm = ADAM_B1 * m_ref[...] + (1.0 - ADAM_B1) * g
        nv = ADAM_B2 * v_ref[...] + (1.0 - ADAM_B2) * (g * g)
        g_ref[...] = g
        nm_ref[...] = nm
        nv_ref[...] = nv
        d_ref[...] = -ADAM_LR * ((nm / bc1) / (jnp.sqrt(nv / bc2) + ADAM_EPS) + ADAM_WD * wv)

    spec = pl.BlockSpec((tile, width), lambda i: (i, 0))
    return pl.pallas_call(
        body, name=name, grid=(n_rows // tile,),
        in_specs=[spec, spec, spec, pl.BlockSpec((n_slots, tile, width), lambda i: (0, i, 0))],
        out_specs=[spec] * 4, out_shape=[jax.ShapeDtypeStruct((n_rows, width), F32)] * 4,
        compiler_params=_params(dimension_semantics=("parallel",)),
    )(w, m, v, slots)


def adamw_nd(w, m, v, slots, *, name):
    shp = w.shape
    two = (-1, shp[-1])
    res = adamw(w.reshape(two), m.reshape(two), v.reshape(two), slots.reshape((slots.shape[0],) + (w.size // shp[-1], shp[-1])),
                name=name)
    return [r.reshape(shp) for r in res]


def _pack(arrs):
    parts = []
    for a in arrs:
        flat = a.reshape(-1).astype(F32)
        parts.append(jnp.pad(flat, (0, (-flat.shape[0]) % 1024)))
    return jnp.concatenate(parts).reshape(-1, 128)


def _unpack(buf, shapes):
    outs, at = [], 0
    flat = buf.reshape(-1)
    for shp in shapes:
        n = int(np.prod(shp))
        outs.append(flat[at:at + n].reshape(shp))
        at += n + (-n) % 1024
    return outs


ARG_NAMES = ["x", "c", "ada_w", "ada_b", "ln_g", "ln_b", "ffn_w1", "ffn_w3", "ffn_w2", "hgrn_lb_logits", "ab_w_in",
             "ab_w_out", "hgrn_norm_g", "mlstm_conv_w", "mlstm_conv_b", "mlstm_wq", "mlstm_wk", "mlstm_gate_b",
             "mlstm_skip", "mlstm_norm_g", "rglru_w_in", "rglru_conv_w", "rglru_conv_b", "rglru_wa", "rglru_ba",
             "rglru_wx", "rglru_bx", "rglru_lambda", "rglru_w_out", "loss_target"]
WEIGHTS = ARG_NAMES[2:-1]
BIG = ["ffn_w1", "ffn_w3", "ffn_w2", "ab_w_in", "ab_w_out", "rglru_w_in", "rglru_w_out"]
REPLICATED = ["ada_b", "hgrn_lb_logits", "hgrn_norm_g", "mlstm_conv_b", "mlstm_wq", "mlstm_wk", "mlstm_gate_b",
              "mlstm_skip", "mlstm_norm_g", "rglru_wa", "rglru_wx"]
SHARDED_SMALL = ["ln_g", "ln_b", "mlstm_conv_w", "rglru_conv_w", "rglru_conv_b", "rglru_ba", "rglru_bx", "rglru_lambda"]


def _unshard_last(gathered):
    moved = jnp.moveaxis(gathered, 0, -2)
    return moved.reshape(moved.shape[:-2] + (NDEV * moved.shape[-1],))


def _shard_last(full):
    split = full.reshape(full.shape[:-1] + (NDEV, full.shape[-1] // NDEV))
    return jnp.moveaxis(split, -2, 0)


def kernel(x, c, ada_w, ada_b, ln_g, ln_b, ffn_w1, ffn_w3, ffn_w2, hgrn_lb_logits, ab_w_in, ab_w_out, hgrn_norm_g, mlstm_conv_w, mlstm_conv_b, mlstm_wq, mlstm_wk, mlstm_gate_b, mlstm_skip, mlstm_norm_g, rglru_w_in, rglru_conv_w, rglru_conv_b, rglru_wa, rglru_ba, rglru_wx, rglru_bx, rglru_lambda, rglru_w_out, loss_target, m_ada_w, m_ada_b, m_ln_g, m_ln_b, m_ffn_w1, m_ffn_w3, m_ffn_w2, m_hgrn_lb_logits, m_ab_w_in, m_ab_w_out, m_hgrn_norm_g, m_mlstm_conv_w, m_mlstm_conv_b, m_mlstm_wq, m_mlstm_wk, m_mlstm_gate_b, m_mlstm_skip, m_mlstm_norm_g, m_rglru_w_in, m_rglru_conv_w, m_rglru_conv_b, m_rglru_wa, m_rglru_ba, m_rglru_wx, m_rglru_bx, m_rglru_lambda, m_rglru_w_out, v_ada_w, v_ada_b, v_ln_g, v_ln_b, v_ffn_w1, v_ffn_w3, v_ffn_w2, v_hgrn_lb_logits, v_ab_w_in, v_ab_w_out, v_hgrn_norm_g, v_mlstm_conv_w, v_mlstm_conv_b, v_mlstm_wq, v_mlstm_wk, v_mlstm_gate_b, v_mlstm_skip, v_mlstm_norm_g, v_rglru_w_in, v_rglru_conv_w, v_rglru_conv_b, v_rglru_wa, v_rglru_ba, v_rglru_wx, v_rglru_bx, v_rglru_lambda, v_rglru_w_out):
    args = locals()
    p = {n: args[n] for n in ARG_NAMES}
    mom = {n: (args["m_" + n], args["v_" + n]) for n in WEIGHTS}
    me = _flat_index(_my_position())

    keys = [("ffn", 0, 0), ("ab",), ("ffn", 0, 1), ("ffn", 1, 0), ("rg",), ("ffn", 1, 1)]
    names = {("ab",): ("ab_w_in", "ab_w_out"), ("rg",): ("rglru_w_in", "rglru_w_out")}
    for l in range(DEPTH):
        for i in range(2):
            names[("ffn", l, i)] = ("ffn_w1", "ffn_w3", "ffn_w2")

    def part(key):
        return (lambda a: a[key[1], key[2]]) if key[0] == "ffn" else (lambda a: a[0])

    shards = {key: [part(key)(p[n]).astype(BF16) for n in names[key]] for key in keys}
    gather_handles, token = {}, jnp.zeros((), F32)
    for key in keys:
        gather_handles[key], tok = exchange_start(shards[key], True, "gather_start_" + "_".join(map(str, key)))
        token = token + tok[0, 0]

    def landed(handles, own_of, after, name):
        sources, lands = exchange_wait(handles, after, name)
        return [lax.dynamic_update_index_in_dim(ld, own_of(src), me, 0) for src, ld in zip(sources, lands)]

    def get_weights(key, after):
        got = landed(gather_handles[key], lambda src: src, after, "gather_wait_" + "_".join(map(str, key)))
        if key[0] == "ffn":
            return _unshard_last(got[0]), _unshard_last(got[1]), got[2].reshape(D_FF, D_MODEL)
        w_in = _unshard_last(got[0])
        if key[0] == "ab":
            w_in = jnp.concatenate([w_in[:, :AB_MAIN], jnp.pad(w_in[:, AB_MAIN:], ((0, 0), (0, 120)))], axis=1)
        return w_in, got[1].reshape(D_MODEL, D_MODEL)

    scatter_handles = {}

    def put_grads(key, grads):
        if key[0] == "ffn":
            slots = [_shard_last(grads[0]), _shard_last(grads[1]), grads[2].reshape(NDEV, D_FF // NDEV, D_MODEL)]
        else:
            g_in = grads[0][:, :AB_MAIN + 8] if key[0] == "ab" else grads[0]
            slots = [_shard_last(g_in), grads[1].reshape(NDEV, D_MODEL // NDEV, D_MODEL)]
        scatter_handles[key], _ = exchange_start(slots, False, "scatter_start_" + "_".join(map(str, key)))

    sharded_shapes = [p[n].shape for n in SHARDED_SMALL]
    small = all_gather(_pack([p[n] for n in SHARDED_SMALL] + [c]) + token, "gather_small")
    per_dev = [_unpack(small[d], sharded_shapes + [c.shape]) for d in range(NDEV)]
    full_small = {n: _unshard_last(jnp.stack([per_dev[d][i] for d in range(NDEV)]))
                  for i, n in enumerate(SHARDED_SMALL)}
    c_all = jnp.concatenate([per_dev[d][-1] for d in range(NDEV)], axis=0)

    c16 = jnp.pad(c_all, ((0, 8), (0, 0)))
    (c_act,), _ = rowwise(lambda r, q: ([_silu(r[0])], []), [Rows(c16)], [], [(D_MODEL, BF16)], [], tile=16,
                          name="cond_act")
    n_ada = ada_w.shape[-1]
    ada_b_mine = lax.dynamic_slice_in_dim(ada_b, me * n_ada, n_ada, axis=1)
    ada_cols = []
    for l in range(DEPTH):
        bias = jnp.broadcast_to(ada_b_mine[l][None, :], (16, n_ada))
        (cols,) = matmul([[(c_act, ada_w[l])]], "nn", [F32], tm=16, tn=n_ada, tk=D_MODEL, extras=[bias],
                         epi=lambda accs, ex: [accs[0] + ex[0]], name="ada_fwd")
        ada_cols.append(cols[:8])
    ada_mine = all_to_all(jnp.stack(ada_cols, axis=1), "ada_to_owner")
    mod = jnp.moveaxis(ada_mine, 0, 1).reshape(DEPTH, 9, D_MODEL)

    w = {"ln_g": full_small["ln_g"], "ln_b": full_small["ln_b"], "hgrn_lb_logits": hgrn_lb_logits}
    for n in ("hgrn_norm_g", "mlstm_conv_b", "mlstm_wq", "mlstm_wk", "mlstm_gate_b", "mlstm_skip", "mlstm_norm_g",
              "rglru_wa", "rglru_wx"):
        w[n] = p[n][0]
    for n in ("mlstm_conv_w", "rglru_conv_w", "rglru_conv_b", "rglru_ba", "rglru_bx", "rglru_lambda"):
        w[n] = full_small[n][0]

    loss, grad_x, d_mod, g = local_step(x[0], loss_target[0], mod, w, get_weights, put_grads)
    loss = lax.psum(loss, ("x", "y", "c"))

    outs = {}
    parts = {n: {} for n in BIG}
    for key in keys:
        slots = landed(scatter_handles[key], lambda src: lax.dynamic_index_in_dim(src, me, 0, keepdims=False), grad_x,
                       "scatter_wait_" + "_".join(map(str, key)))
        for n, sl in zip(names[key], slots):
            sel = part(key)
            parts[n][key] = adamw_nd(sel(p[n]), sel(mom[n][0]), sel(mom[n][1]), sl, name="adamw_" + n)
    for n in BIG:
        if n.startswith("ffn"):
            outs[n] = [jnp.stack([jnp.stack([parts[n][("ffn", l, i)][k] for i in range(2)]) for l in range(DEPTH)])
                       for k in range(4)]
        else:
            (only,) = parts[n].values()
            outs[n] = [r[None] for r in only]

    g["ada_b"] = d_mod.reshape(DEPTH, 9 * D_MODEL)
    small_names = REPLICATED + SHARDED_SMALL
    full_shapes = [p[n].shape for n in REPLICATED] + [full_small[n].shape for n in SHARDED_SMALL]
    all_small = all_gather(_pack([g[n] for n in small_names]), "gather_small_grads")
    zeros = jnp.zeros(all_small.shape[1:], F32)
    summed = adamw(zeros, zeros, zeros, all_small, name="sum_small_grads")[0]
    g_small = dict(zip(small_names, _unpack(summed, full_shapes)))
    rep = adamw(*[_pack([t[n] for n in REPLICATED]) for t in (p, {n: mom[n][0] for n in WEIGHTS},
                                                               {n: mom[n][1] for n in WEIGHTS})],
                _pack([g_small[n] for n in REPLICATED])[None], name="adamw_replicated")
    rep = [_unpack(r, [p[n].shape for n in REPLICATED]) for r in rep]
    for i, n in enumerate(REPLICATED):
        outs[n] = [r[i] for r in rep]
    g_mine = {n: lax.dynamic_slice_in_dim(g_small[n], me * p[n].shape[-1], p[n].shape[-1], axis=-1)
              for n in SHARDED_SMALL}
    shd = adamw(*[_pack([t[n] for n in SHARDED_SMALL]) for t in (p, {n: mom[n][0] for n in WEIGHTS},
                                                                  {n: mom[n][1] for n in WEIGHTS})],
                _pack([g_mine[n] for n in SHARDED_SMALL])[None], name="adamw_sharded_small")
    shd = [_unpack(r, sharded_shapes) for r in shd]
    for i, n in enumerate(SHARDED_SMALL):
        outs[n] = [r[i] for r in shd]

    d_ada = all_small[:, :DEPTH * 9 * D_MODEL // 128].reshape(NDEV, DEPTH, 9 * D_MODEL)
    d_mine = lax.dynamic_slice_in_dim(d_ada, me * n_ada, n_ada, axis=2)
    g_ada = []
    for l in range(DEPTH):
        d16 = jnp.pad(d_mine[:, l], ((0, 8), (0, 0)))
        (gl,) = matmul([[(c_act, d16)]], "tn", [F32], tm=D_MODEL, tn=n_ada, tk=16, name="ada_bwd")
        g_ada.append(gl)
    outs["ada_w"] = adamw_nd(ada_w, *mom["ada_w"], jnp.stack(g_ada)[None], name="adamw_ada_w")

    result = [loss, grad_x[None]]
    for k in range(4):
        result += [outs[n][k].reshape(p[n].shape) for n in WEIGHTS]
    return tuple(result)
```

```python
import functools

import jax
import jax.numpy as jnp
import numpy as np
from jax import lax
from jax.experimental import pallas as pl
from jax.experimental.pallas import tpu as pltpu

F32 = jnp.float32
BF16 = jnp.bfloat16
HIGHEST = lax.Precision.HIGHEST

NDEV = 8
D_MODEL = 1024
D_FF = 2816
DEPTH = 2
CHUNK = 64
SUB = 16
HEADS = 4
HEAD_W = 128
MIX_W = HEADS * HEAD_W
AB_MAIN = 7 * MIX_W
AB_ALL = AB_MAIN + 128
CONV_W = 4
C_BLOCKS = 8
RG_C = 8.0
ALPHA = (2 * DEPTH) ** 0.25
FFN_RES_W = 0.5
NEG = -1e30

ADAM_LR = 0.001
ADAM_B1 = 0.9
ADAM_B2 = 0.999
ADAM_EPS = 1e-08
ADAM_WD = 0.01
ADAM_STEP = 10

VMEM_LIMIT = 56 * 1024 * 1024
GRAD_WIRE = jnp.bfloat16

NN = ((1,), (0,))
NT = ((1,), (1,))
TN = ((0,), (0,))


def _dot(a, b, dims, precision=None):
    return lax.dot_general(a, b, (dims, ((), ())), precision=precision, preferred_element_type=F32)


def _make_mm(dims, d_lhs, d_rhs, swap_lhs, swap_rhs, prec):
    def cast(v):
        return v.astype(BF16) if prec is None else v.astype(F32)

    @jax.custom_vjp
    def mm(a, b):
        return _dot(cast(a), cast(b), dims, prec)

    def fwd(a, b):
        return mm(a, b), (a, b)

    def bwd(res, g):
        a, b = res
        g = cast(g)
        da = _dot(cast(b), g, d_lhs, prec) if swap_lhs else _dot(g, cast(b), d_lhs, prec)
        db = _dot(g, cast(a), d_rhs, prec) if swap_rhs else _dot(cast(a), g, d_rhs, prec)
        return da.astype(a.dtype), db.astype(b.dtype)

    mm.defvjp(fwd, bwd)
    return mm


def _mm_family(prec):
    return (_make_mm(NN, NT, TN, False, False, prec), _make_mm(NT, NN, TN, False, True, prec),
            _make_mm(TN, NT, NN, True, False, prec))


def _round(v):
    return v.astype(BF16).astype(F32)


@jax.custom_vjp
def row_dot(a, n):
    return jnp.sum(_round(a) * _round(n), axis=1, keepdims=True)


def _row_dot_fwd(a, n):
    return row_dot(a, n), (a, n)


def _row_dot_bwd(res, g):
    a, n = res
    g = _round(g)
    return g * _round(n), jnp.sum(g * _round(a), axis=0, keepdims=True)


row_dot.defvjp(_row_dot_fwd, _row_dot_bwd)


@jax.custom_vjp
def col_dot(s, a):
    return jnp.sum(_round(s) * _round(a), axis=0, keepdims=True)


def _col_dot_fwd(s, a):
    return col_dot(s, a), (s, a)


def _col_dot_bwd(res, g):
    s, a = res
    g = _round(g)
    return jnp.sum(g * _round(a), axis=1, keepdims=True), _round(s) * g


col_dot.defvjp(_col_dot_fwd, _col_dot_bwd)

mm_nn, mm_nt, mm_tn = _mm_family(None)
mid_nn, mid_nt, mid_tn = _mm_family(lax.Precision.HIGH)
hi_nn, hi_nt, hi_tn = _mm_family(HIGHEST)


def _silu(v):
    return v * jax.nn.sigmoid(v)


def _log_sigmoid(v):
    return jnp.minimum(v, 0.0) - jnp.log1p(jnp.exp(-jnp.abs(v)))


def _softplus(v):
    return jnp.maximum(v, 0.0) + jnp.log1p(jnp.exp(-jnp.abs(v)))


def _neg_expm1(v):
    series = -v * (1.0 + v * (0.5 + v * (1.0 / 6.0 + v * (1.0 / 24.0 + v * (1.0 / 120.0)))))
    return jnp.where(v > -0.05, series, 1.0 - jnp.exp(v))


def _params(**kw):
    return pltpu.CompilerParams(vmem_limit_bytes=VMEM_LIMIT, **kw)


class Rows:
    def __init__(self, arr, block=None, split=None):
        self.arr = arr
        self.block = block
        self.split = split

    @property
    def width(self):
        return self.block[1] if self.block else self.arr.shape[1]


def _load(ref, split):
    if ref.ndim == 3:
        return [ref[k].astype(F32) for k in range(ref.shape[0])]
    if split is None:
        return ref[...].astype(F32)
    return [ref[:, k * split:(k + 1) * split].astype(F32) for k in range(ref.shape[1] // split)]


def _store(ref, val, accumulate=False):
    if isinstance(val, (list, tuple)):
        if ref.ndim == 3:
            for k, v in enumerate(val):
                ref[k] = (ref[k] + v if accumulate else v).astype(ref.dtype)
            return
        w = ref.shape[1] // len(val)
        for k, v in enumerate(val):
            sl = slice(k * w, (k + 1) * w)
            ref[:, sl] = (ref[:, sl] + v if accumulate else v).astype(ref.dtype)
    else:
        ref[...] = (ref[...] + val if accumulate else val).astype(ref.dtype)


def rowwise(fn, rows, params, out_rows, out_sums, *, tile, name, after=()):
    n_rows = rows[0].arr.shape[0]
    n_r, n_p, n_o = len(rows), len(params), len(out_rows)
    n_in = n_r + n_p + len(after)
    splits = [r.split for r in rows]

    def body(*refs):
        r_refs, p_refs = refs[:n_r], refs[n_r:n_r + n_p]
        o_refs, s_refs = refs[n_in:n_in + n_o], refs[n_in + n_o:]
        row_out, sum_out = fn([_load(r, s) for r, s in zip(r_refs, splits)], [_load(p, None) for p in p_refs])
        for ref, val in zip(o_refs, row_out):
            _store(ref, val)
        if s_refs:
            @pl.when(pl.program_id(0) == 0)
            def _():
                for ref in s_refs:
                    ref[...] = jnp.zeros(ref.shape, ref.dtype)

            for ref, val in zip(s_refs, sum_out):
                _store(ref, val, accumulate=True)

    in_specs = []
    for r in rows:
        blk = r.block[0] if r.block else 0
        in_specs.append(pl.BlockSpec((tile, r.width), functools.partial(lambda i, b: (i, b), b=blk)))
    for p in params:
        in_specs.append(pl.BlockSpec(p.shape, functools.partial(lambda i, n: (0,) * n, n=p.ndim)))
    in_specs += [pl.BlockSpec(memory_space=pl.ANY)] * len(after)
    out_shape = [jax.ShapeDtypeStruct((n_rows, w), dt) for w, dt in out_rows]
    out_specs = [pl.BlockSpec((tile, w), lambda i: (i, 0)) for w, _ in out_rows]
    for shp in out_sums:
        out_shape.append(jax.ShapeDtypeStruct(shp, F32))
        out_specs.append(pl.BlockSpec(shp, functools.partial(lambda i, n: (0,) * n, n=len(shp))))
    res = pl.pallas_call(
        body, name=name, grid=(n_rows // tile,), in_specs=in_specs, out_specs=out_specs, out_shape=out_shape,
        compiler_params=_params(dimension_semantics=("arbitrary",)),
    )(*[r.arr for r in rows], *params, *after)
    return res[:n_o], res[n_o:]


def rowwise_bwd(fn, rows, cots, params, want_rows, want_params, out_dtypes, *, tile, name, after=()):
    n = len(rows)

    def bwd(row_vals, param_vals):
        prim, cot = row_vals[:n], row_vals[n:]
        _, vjp = jax.vjp(lambda r, p: fn(r, p)[0], prim, param_vals)
        d_rows, d_params = vjp(cot)
        return [d_rows[i] for i in want_rows], [d_params[j] for j in want_params]

    out_rows = [(rows[i].width, dt) for i, dt in zip(want_rows, out_dtypes)]
    out_sums = [params[j].shape for j in want_params]
    return rowwise(bwd, list(rows) + list(cots), params, out_rows, out_sums, tile=tile, name=name, after=after)


def matmul(groups, mode, outs, *, tm, tn, tk, extras=(), epi=None, name):
    a0, b0 = groups[0][0]
    if mode == "tn":
        k_dim, m_dim = a0.shape
    else:
        m_dim, k_dim = a0.shape
    n_dim = b0.shape[0] if mode == "nt" else b0.shape[1]
    tm, tn, tk = min(tm, m_dim), min(tn, n_dim), min(tk, k_dim)
    assert m_dim % tm == 0 and n_dim % tn == 0 and k_dim % tk == 0, (name, m_dim, n_dim, k_dim)
    nk = k_dim // tk
    pairs = [p for g in groups for p in g]
    n_pairs, n_groups, n_ex, n_out = len(pairs), len(groups), len(extras), len(outs)
    dims = {"nn": NN, "nt": NT, "tn": TN}[mode]

    def body(*refs):
        ab = refs[:2 * n_pairs]
        ex = refs[2 * n_pairs:2 * n_pairs + n_ex]
        o_refs = refs[2 * n_pairs + n_ex:2 * n_pairs + n_ex + n_out]
        accs = refs[2 * n_pairs + n_ex + n_out:]

        def partial_sums():
            sums, p = [], 0
            for g in groups:
                tot = None
                for _ in g:
                    d = _dot(ab[2 * p][...].astype(BF16), ab[2 * p + 1][...].astype(BF16), dims)
                    tot = d if tot is None else tot + d
                    p += 1
                sums.append(tot)
            return sums

        def finish(vals):
            res = epi(vals, [e[...] for e in ex]) if epi else vals
            for ref, v in zip(o_refs, res):
                ref[...] = v.astype(ref.dtype)

        if nk == 1:
            finish(partial_sums())
        else:
            k = pl.program_id(2)

            @pl.when(k == 0)
            def _():
                for acc in accs:
                    acc[...] = jnp.zeros(acc.shape, F32)

            for acc, s in zip(accs, partial_sums()):
                acc[...] += s

            @pl.when(k == nk - 1)
            def _():
                finish([acc[...] for acc in accs])

    if mode == "nn":
        a_spec = pl.BlockSpec((tm, tk), lambda i, j, k: (i, k))
        b_spec = pl.BlockSpec((tk, tn), lambda i, j, k: (k, j))
    elif mode == "nt":
        a_spec = pl.BlockSpec((tm, tk), lambda i, j, k: (i, k))
        b_spec = pl.BlockSpec((tn, tk), lambda i, j, k: (j, k))
    else:
        a_spec = pl.BlockSpec((tk, tm), lambda i, j, k: (k, i))
        b_spec = pl.BlockSpec((tk, tn), lambda i, j, k: (k, j))
    mn_spec = pl.BlockSpec((tm, tn), lambda i, j, k: (i, j))
    return pl.pallas_call(
        body, name=name, grid=(m_dim // tm, n_dim // tn, nk),
        in_specs=[a_spec, b_spec] * n_pairs + [mn_spec] * n_ex,
        out_specs=[mn_spec] * n_out,
        out_shape=[jax.ShapeDtypeStruct((m_dim, n_dim), dt) for dt in outs],
        scratch_shapes=[pltpu.VMEM((tm, tn), F32)] * (n_groups if nk > 1 else 0),
        compiler_params=_params(dimension_semantics=("parallel", "parallel", "arbitrary")),
    )(*[x for p in pairs for x in p], *extras)


def conv_fwd(x, w, b, *, tile, name):
    n_rows, width = x.arr.shape[0], x.width
    blk = x.block[0] if x.block else 0

    def body(x_ref, halo_ref, w_ref, b_ref, y_ref, buf):
        i = pl.program_id(0)
        halo = halo_ref[...]
        buf[0:8, :] = jnp.where(i == 0, jnp.zeros_like(halo), _round(halo))
        buf[8:, :] = _round(x_ref[...])
        acc = jnp.zeros((tile, width), F32)
        for j in range(CONV_W):
            s = CONV_W - 1 - j
            acc = acc + _round(w_ref[j:j + 1, :]) * buf[8 - s:8 - s + tile, :]
        y_ref[...] = acc + b_ref[...]

    hb = tile // 8
    return pl.pallas_call(
        body, name=name, grid=(n_rows // tile,),
        in_specs=[pl.BlockSpec((tile, width), lambda i: (i, blk)),
                  pl.BlockSpec((8, width), lambda i: (jnp.maximum(i * hb - 1, 0), blk)),
                  pl.BlockSpec((CONV_W, width), lambda i: (0, 0)),
                  pl.BlockSpec((1, width), lambda i: (0, 0))],
        out_specs=pl.BlockSpec((tile, width), lambda i: (i, 0)),
        out_shape=jax.ShapeDtypeStruct((n_rows, width), F32),
        scratch_shapes=[pltpu.VMEM((tile + 8, width), F32)],
        compiler_params=_params(dimension_semantics=("arbitrary",)),
    )(x.arr, x.arr, w, b)


def conv_bwd(x, dy, w, *, tile, name):
    n_rows, width = x.arr.shape[0], x.width
    blk = x.block[0] if x.block else 0
    n_tiles = n_rows // tile

    def body(x_ref, xh_ref, dy_ref, dyh_ref, w_ref, dx_ref, dw_ref, db_ref, xbuf, dbuf):
        i = pl.program_id(0)
        xh, dyh = xh_ref[...], dyh_ref[...]
        xbuf[0:8, :] = jnp.where(i == 0, jnp.zeros_like(xh), _round(xh))
        xbuf[8:, :] = _round(x_ref[...])
        dy_t = dy_ref[...]
        dy_r = _round(dy_t)
        dbuf[0:tile, :] = dy_r
        dbuf[tile:, :] = jnp.where(i == n_tiles - 1, jnp.zeros_like(dyh), _round(dyh))

        @pl.when(i == 0)
        def _():
            dw_ref[...] = jnp.zeros(dw_ref.shape, F32)
            db_ref[...] = jnp.zeros(db_ref.shape, F32)

        acc = jnp.zeros((tile, width), F32)
        for j in range(CONV_W):
            s = CONV_W - 1 - j
            acc = acc + _round(w_ref[j:j + 1, :]) * dbuf[s:s + tile, :]
            dw_ref[j:j + 1, :] += jnp.sum(dy_r * xbuf[8 - s:8 - s + tile, :], axis=0, keepdims=True)
        dx_ref[...] = acc.astype(dx_ref.dtype)
        db_ref[...] += jnp.sum(dy_t, axis=0, keepdims=True)

    hb = tile // 8
    return pl.pallas_call(
        body, name=name, grid=(n_tiles,),
        in_specs=[pl.BlockSpec((tile, width), lambda i: (i, blk)),
                  pl.BlockSpec((8, width), lambda i: (jnp.maximum(i * hb - 1, 0), blk)),
                  pl.BlockSpec((tile, width), lambda i: (i, 0)),
                  pl.BlockSpec((8, width), lambda i: (jnp.minimum((i + 1) * hb, n_tiles * hb - 1), 0)),
                  pl.BlockSpec((CONV_W, width), lambda i: (0, 0))],
        out_specs=[pl.BlockSpec((tile, width), lambda i: (i, 0)),
                   pl.BlockSpec((CONV_W, width), lambda i: (0, 0)),
                   pl.BlockSpec((1, width), lambda i: (0, 0))],
        out_shape=[jax.ShapeDtypeStruct((n_rows, width), BF16),
                   jax.ShapeDtypeStruct((CONV_W, width), F32),
                   jax.ShapeDtypeStruct((1, width), F32)],
        scratch_shapes=[pltpu.VMEM((tile + 8, width), F32), pltpu.VMEM((tile + 8, width), F32)],
        compiler_params=_params(dimension_semantics=("arbitrary",)),
    )(x.arr, x.arr, dy, dy, w)


def _pieces(ref, col_w, row_h):
    n_c, n_r = ref.shape[1] // col_w, ref.shape[0] // row_h
    return [[ref[r * row_h:(r + 1) * row_h, c * col_w:(c + 1) * col_w].astype(F32) for r in range(n_r)]
            for c in range(n_c)]


def _store_pieces(ref, vals, col_w, row_h):
    for c, col in enumerate(vals):
        for r, v in enumerate(col):
            ref[r * row_h:(r + 1) * row_h, c * col_w:(c + 1) * col_w] = v.astype(ref.dtype)


def _x_spec(x, n_chunks, reverse):
    blk = x.block[0] if x.block else 0
    if reverse:
        return pl.BlockSpec((CHUNK, x.width), functools.partial(lambda n, b: (n_chunks - 1 - n, b), b=blk))
    return pl.BlockSpec((CHUNK, x.width), functools.partial(lambda n, b: (n, b), b=blk))


def chunk_scan_fwd(step, xs, piece, state_shapes, out_widths, *, name):
    n_rows = xs[0].arr.shape[0]
    n_chunks = n_rows // CHUNK
    n_x, n_s, n_o = len(xs), len(state_shapes), len(out_widths)

    def body(*refs):
        x_refs, o_refs = refs[:n_x], refs[n_x:n_x + n_o]
        keep_refs, st_refs = refs[n_x + n_o:n_x + n_o + n_s], refs[n_x + n_o + n_s:]

        @pl.when(pl.program_id(0) == 0)
        def _():
            for st in st_refs:
                st[...] = jnp.zeros(st.shape, F32)

        states = [st[...] for st in st_refs]
        for keep, s in zip(keep_refs, states):
            keep[...] = s
        new_states, outs = step(states, [_pieces(x, *p) for x, p in zip(x_refs, piece)])
        for st, s in zip(st_refs, new_states):
            st[...] = s
        for o, v in zip(o_refs, outs):
            o[...] = v

    out_shape = [jax.ShapeDtypeStruct((n_rows, w), F32) for w in out_widths]
    out_specs = [pl.BlockSpec((CHUNK, w), lambda n: (n, 0)) for w in out_widths]
    for shp in state_shapes:
        out_shape.append(jax.ShapeDtypeStruct((n_chunks,) + shp, F32))
        out_specs.append(pl.BlockSpec((None,) + shp, lambda n: (n, 0, 0)))
    res = pl.pallas_call(
        body, name=name, grid=(n_chunks,),
        in_specs=[_x_spec(x, n_chunks, False) for x in xs],
        out_specs=out_specs, out_shape=out_shape,
        scratch_shapes=[pltpu.VMEM(shp, F32) for shp in state_shapes],
        compiler_params=_params(dimension_semantics=("arbitrary",)),
    )(*[x.arr for x in xs])
    return res[:n_o], res[n_o:]


def chunk_scan_bwd(step, xs, piece, kept, d_outs, *, name):
    n_rows = xs[0].arr.shape[0]
    n_chunks = n_rows // CHUNK
    n_x, n_s, n_o = len(xs), len(kept), len(d_outs)
    state_shapes = [k.shape[1:] for k in kept]

    def body(*refs):
        x_refs, k_refs = refs[:n_x], refs[n_x:n_x + n_s]
        do_refs = refs[n_x + n_s:n_x + n_s + n_o]
        dx_refs = refs[n_x + n_s + n_o:2 * n_x + n_s + n_o]
        ds_refs = refs[2 * n_x + n_s + n_o:]

        @pl.when(pl.program_id(0) == 0)
        def _():
            for ds in ds_refs:
                ds[...] = jnp.zeros(ds.shape, F32)

        states = [k[...] for k in k_refs]
        inputs = [_pieces(x, *p) for x, p in zip(x_refs, piece)]
        _, vjp = jax.vjp(step, states, inputs)
        d_states, d_inputs = vjp(([ds[...] for ds in ds_refs], [do[...] for do in do_refs]))
        for ds, v in zip(ds_refs, d_states):
            ds[...] = v
        for dx, v, p in zip(dx_refs, d_inputs, piece):
            _store_pieces(dx, v, *p)

    rev3 = lambda n: (n_chunks - 1 - n, 0, 0)
    rev2 = lambda n: (n_chunks - 1 - n, 0)
    return pl.pallas_call(
        body, name=name, grid=(n_chunks,),
        in_specs=[_x_spec(x, n_chunks, True) for x in xs]
        + [pl.BlockSpec((None,) + shp, rev3) for shp in state_shapes]
        + [pl.BlockSpec((CHUNK, d.shape[1]), rev2) for d in d_outs],
        out_specs=[pl.BlockSpec((CHUNK, x.width), rev2) for x in xs],
        out_shape=[jax.ShapeDtypeStruct((n_rows, x.width), F32) for x in xs],
        scratch_shapes=[pltpu.VMEM(shp, F32) for shp in state_shapes],
        compiler_params=_params(dimension_semantics=("arbitrary",)),
    )(*[x.arr for x in xs], *kept, *d_outs)


def _tri(n, strict=False):
    r = lax.broadcasted_iota(jnp.int32, (n, n), 0)
    c = lax.broadcasted_iota(jnp.int32, (n, n), 1)
    return (r > c) if strict else (r >= c)


def hgrn2_step(states, inputs):
    q_all, k_all, v_all, lf_all = inputs
    n_sub = CHUNK // SUB
    low = _tri(SUB).astype(F32)
    ones_sub = jnp.ones((SUB, SUB), F32)
    ones_chunk = jnp.ones((CHUNK, HEAD_W), F32)
    new_states, outs = [], []
    for h in range(HEADS):
        state = states[h]
        q, k, v, lf = q_all[h], k_all[h], v_all[h], lf_all[h]
        cum = [hi_nn(low, lf[i]) for i in range(n_sub)]
        tot = [hi_nn(ones_sub, lf[i]) for i in range(n_sub)]
        start = [jnp.zeros((SUB, HEAD_W), F32)]
        for i in range(n_sub):
            start.append(start[-1] + tot[i])
        q_in = [q[i] * jnp.exp(cum[i]) for i in range(n_sub)]
        intra = []
        for i in range(n_sub):
            keys = [k[j] * jnp.exp(start[i] - start[j] - cum[j]) for j in range(i)]
            keys.append(k[i] * jnp.exp(jnp.minimum(-cum[i], 80.0)))
            att = mid_nt(q_in[i], jnp.concatenate(keys, axis=0))
            r_id = lax.broadcasted_iota(jnp.int32, att.shape, 0)
            c_id = lax.broadcasted_iota(jnp.int32, att.shape, 1)
            att = jnp.where(c_id - SUB * i <= r_id, att, 0.0)
            intra.append(mm_nn(att, jnp.concatenate(v[:i + 1], axis=0)))
        q_state = jnp.concatenate([q_in[i] * jnp.exp(start[i]) for i in range(n_sub)], axis=0)
        out = mm_nn(q_state, state) + jnp.concatenate(intra, axis=0)
        k_end = jnp.concatenate([k[j] * jnp.exp(start[n_sub] - start[j] - cum[j]) for j in range(n_sub)], axis=0)
        decay = jnp.exp(hi_tn(jnp.concatenate(lf, axis=0), ones_chunk))
        new_states.append(decay * state + mm_tn(k_end, jnp.concatenate(v, axis=0)))
        outs.append(out)
    return new_states, [jnp.concatenate(outs, axis=1)]


def mlstm_step(states, inputs):
    q_all, k_all, v_all, gates = inputs
    gates = gates[0][0]
    c_st, n_st, m_st = states[:HEADS], states[HEADS:2 * HEADS], states[2 * HEADS:]
    lane = lax.broadcasted_iota(jnp.int32, (CHUNK, 128), 1)
    low = _tri(CHUNK).astype(F32)
    causal = _tri(CHUNK)
    gates_cum = hi_nn(low, gates)
    new_c, new_n, new_m, outs = [], [], [], []
    for h in range(HEADS):
        q, k, v = q_all[h][0], k_all[h][0], v_all[h][0]
        pick_i = (lane == h).astype(F32)
        pick_f = (lane == HEADS + h).astype(F32)
        li_col = jnp.sum(gates * pick_i, axis=1, keepdims=True)
        lf_col = jnp.sum(gates * pick_f, axis=1, keepdims=True)
        b_col = jnp.sum(gates_cum * pick_f, axis=1, keepdims=True)
        by_key = hi_nt(pick_i, gates) - hi_nt(pick_f, gates_cum)
        d_mat = jnp.where(causal, b_col + by_key, NEG)
        m_prev = lax.stop_gradient(jnp.max(m_st[h], axis=1, keepdims=True))
        g_inter = b_col + m_prev
        m_t = lax.stop_gradient(jnp.maximum(g_inter, jnp.max(d_mat, axis=1, keepdims=True)))
        w_inter = jnp.exp(g_inter - m_t)
        aw = jnp.exp(d_mat - m_t) * mm_nt(q, k)
        num = w_inter * mm_nn(q, c_st[h]) + mm_nn(aw, v)
        den = w_inter * row_dot(q, n_st[h]) + jnp.sum(aw, axis=1, keepdims=True)
        outs.append(num / jnp.maximum(jnp.abs(den), jnp.exp(-m_t)))
        b_end = jnp.sum(lf_col, axis=0, keepdims=True)
        g_state = b_end + m_prev
        s_w = b_end - b_col + li_col
        m_next = lax.stop_gradient(jnp.maximum(g_state, jnp.max(s_w, axis=0, keepdims=True)))
        dec = jnp.exp(g_state - m_next)
        w_s = jnp.exp(s_w - m_next)
        kw = k * w_s
        new_c.append(dec * c_st[h] + mm_tn(kw, v))
        new_n.append(dec * n_st[h] + col_dot(w_s, k))
        new_m.append(jnp.broadcast_to(m_next, (1, HEAD_W)))
    return new_c + new_n + new_m, [jnp.concatenate(outs, axis=1)]


def lru_fwd(a, u, *, tile, name):
    n_rows, width = a.shape

    def body(a_ref, u_ref, h_ref, carry):
        @pl.when(pl.program_id(0) == 0)
        def _():
            carry[...] = jnp.zeros(carry.shape, F32)

        h = carry[...]
        for t in range(tile):
            h = a_ref[t:t + 1, :] * h + u_ref[t:t + 1, :]
            h_ref[t:t + 1, :] = h
        carry[...] = h

    spec = pl.BlockSpec((tile, width), lambda i: (i, 0))
    return pl.pallas_call(
        body, name=name, grid=(n_rows // tile,), in_specs=[spec, spec], out_specs=spec,
        out_shape=jax.ShapeDtypeStruct((n_rows, width), F32),
        scratch_shapes=[pltpu.VMEM((1, width), F32)],
        compiler_params=_params(dimension_semantics=("arbitrary",)),
    )(a, u)


def lru_bwd(a, h, dh, *, tile, name):
    n_rows, width = a.shape
    n_tiles = n_rows // tile
    hb = tile // 8

    def body(a_ref, h_ref, hh_ref, dh_ref, da_ref, du_ref, carry):
        i = pl.program_id(0)

        @pl.when(i == 0)
        def _():
            carry[...] = jnp.zeros(carry.shape, F32)

        c = carry[...]
        for t in range(tile - 1, -1, -1):
            g = dh_ref[t:t + 1, :] + c
            du_ref[t:t + 1, :] = g
            if t:
                h_prev = h_ref[t - 1:t, :]
            else:
                h_prev = jnp.where(i == n_tiles - 1, 0.0, hh_ref[7:8, :])
            da_ref[t:t + 1, :] = g * h_prev
            c = a_ref[t:t + 1, :] * g
        carry[...] = c

    rev = lambda i: (n_tiles - 1 - i, 0)
    spec = pl.BlockSpec((tile, width), rev)
    halo = pl.BlockSpec((8, width), lambda i: (jnp.maximum((n_tiles - 1 - i) * hb - 1, 0), 0))
    return pl.pallas_call(
        body, name=name, grid=(n_tiles,), in_specs=[spec, spec, halo, spec], out_specs=[spec, spec],
        out_shape=[jax.ShapeDtypeStruct((n_rows, width), F32)] * 2,
        scratch_shapes=[pltpu.VMEM((1, width), F32)],
        compiler_params=_params(dimension_semantics=("arbitrary",)),
    )(a, h, h, dh)


def _layer_norm(z, g, b):
    mu = jnp.mean(z, axis=-1, keepdims=True)
    zc = z - mu
    var = jnp.mean(zc * zc, axis=-1, keepdims=True)
    return zc * lax.rsqrt(var + 1e-5) * g + b


def pre_fn(rows, params):
    (x,), (scale, shift) = rows, params
    return [x * (1.0 + scale) + shift], []


def make_post_fn(weight, with_next):
    def fn(rows, params):
        x, y = rows
        gate, g, b = params[:3]
        xo = _layer_norm(ALPHA * x + weight * (1.0 + gate) * y, g, b)
        if with_next:
            return [xo, xo * (1.0 + params[3]) + params[4]], []
        return [xo], []

    return fn


def make_last_fn(weight):
    post = make_post_fn(weight, False)

    def fn(rows, params):
        x, y, target = rows
        err = post([x, y], params)[0][0] - target
        loss = 0.5 * jnp.sum(jnp.mean(err * err, axis=-1, keepdims=True), axis=0, keepdims=True)
        return [err * (1.0 / D_MODEL)], [jnp.broadcast_to(loss, (1, 128))]

    return fn


def mix_a_fn(rows, params):
    a_q, a_f, xconv, graw = rows
    l0, l1, l2, wq, wk, gate_b = params
    mx = jnp.maximum(jnp.maximum(l0, l1), l2)
    e0, e1, e2 = jnp.exp(l0 - mx), jnp.exp(l1 - mx), jnp.exp(l2 - mx)
    lb = e0 / (e0 + e1 + e2)
    f = lb + (1.0 - lb) * jax.nn.sigmoid(a_f)
    xc = _silu(xconv)
    q_b = mm_nn(xc, wq)
    k_b = mm_nn(xc, wk) * (HEAD_W ** -0.5)
    g = graw + gate_b
    lane = lax.broadcasted_iota(jnp.int32, g.shape, 1)
    gates = jnp.where(lane < HEADS, g, _log_sigmoid(g))
    return [_silu(a_q), 1.0 - f, jnp.log(f), xc, q_b, k_b, gates], []


def _head_norm(v, g, center):
    if center:
        v = v - jnp.mean(v, axis=-1, keepdims=True)
    return v * lax.rsqrt(jnp.mean(v * v, axis=-1, keepdims=True) + 1e-6) * g


def mix_b_fn(rows, params):
    o_a, a_g, h_b, xc, b_z = rows
    hg, mg, skip = params
    y_a = [_head_norm(o_a[h], hg[h], False) * _silu(a_g[h]) for h in range(HEADS)]
    y_b = [(_head_norm(h_b[h], mg[h], True) + skip[h] * xc[h]) * _silu(b_z[h]) for h in range(HEADS)]
    return [y_a + y_b], []


def lru_a_fn(rows, params):
    (xr,) = rows
    wa, wx, ba, bx, lam = params
    a_out, u_out = [], []
    for n in range(C_BLOCKS):
        r = jax.nn.sigmoid(mm_nt(xr[n], wa[n]) + ba[n])
        i = jax.nn.sigmoid(mm_nt(xr[n], wx[n]) + bx[n])
        log_a = -RG_C * r * _softplus(-lam[n])
        a_out.append(jnp.exp(log_a))
        u_out.append(jnp.sqrt(_neg_expm1(2.0 * log_a)) * i * xr[n])
    return [a_out, u_out], []


def lru_b_fn(rows, params):
    h, y_br = rows
    return [h * jax.nn.gelu(y_br)], []


def swiglu_epi(accs, extras):
    h1, h3 = accs
    return [h1, h3, _silu(h1) * h3]


def swiglu_bwd_epi(accs, extras):
    (da,), (h1, h3) = accs, extras
    h1, h3 = h1.astype(F32), h3.astype(F32)
    sig = jax.nn.sigmoid(h1)
    return [da * h3 * sig * (1.0 + h1 * (1.0 - sig)), da * h1 * sig]


ROW_TILE = 256


def _chunks(v, n):
    return v.reshape(n, 1, v.shape[-1] // n)


def _dense_blocks(w):
    n, b, _ = w.shape
    by_row = jnp.swapaxes(w, 1, 2).reshape(n * b, b)
    spread = jnp.dot(by_row, _column_picker(n, b).T, precision=HIGHEST)
    return spread * _block_mask(n, b)


def _column_picker(n, b):
    return jnp.asarray(np.tile(np.eye(b, dtype=np.float32), (n, 1)))


def _block_mask(n, b):
    return jnp.asarray(np.kron(np.eye(n, dtype=np.float32), np.ones((b, b), np.float32)))


def _block_diag_of(m, n, b):
    by_row = jnp.dot(m * _block_mask(n, b), _column_picker(n, b), precision=HIGHEST)
    return jnp.swapaxes(by_row.reshape(n, b, b), 1, 2)


def local_step(x, target, mod, w, get_weights, put_grads, first_after=()):
    s_len = x.shape[0]
    tile = min(ROW_TILE, s_len)
    row = lambda v: v.reshape(1, -1)
    mrow = lambda l, j, k: mod[l, 3 * j + k].reshape(1, D_MODEL)
    g = {}
    d_mod = [[None] * 9 for _ in range(DEPTH)]
    d_ln_g = [[None] * 3 for _ in range(DEPTH)]
    d_ln_b = [[None] * 3 for _ in range(DEPTH)]
    subs = [(l, j) for l in range(DEPTH) for j in range(3)]
    weight_of = lambda j: 1.0 if j == 1 else FFN_RES_W

    wq_d = _dense_blocks(w["mlstm_wq"])
    wk_d = _dense_blocks(w["mlstm_wk"])
    wa_b, wx_b = w["rglru_wa"].astype(BF16), w["rglru_wx"].astype(BF16)
    gate_b = jnp.pad(w["mlstm_gate_b"].reshape(1, 8), ((0, 0), (0, 120)))
    lb_rows = [row(w["hgrn_lb_logits"][k]) for k in range(3)]
    mix_a_params = lb_rows + [wq_d, wk_d, gate_b]
    mix_b_params = [_chunks(row(w["hgrn_norm_g"]), HEADS), _chunks(row(w["mlstm_norm_g"]), HEADS),
                    _chunks(row(w["mlstm_skip"]), HEADS)]
    lru_a_params = [wa_b, wx_b, _chunks(row(w["rglru_ba"]), C_BLOCKS), _chunks(row(w["rglru_bx"]), C_BLOCKS),
                    _chunks(row(w["rglru_lambda"]), C_BLOCKS)]
    mconv_w, mconv_b = w["mlstm_conv_w"], row(w["mlstm_conv_b"])
    rconv_w, rconv_b = w["rglru_conv_w"], row(w["rglru_conv_b"])
    hg_piece = [(HEAD_W, SUB)] * 4
    ml_piece = [(HEAD_W, CHUNK)] * 3 + [(128, CHUNK)]

    (t,), _ = rowwise(pre_fn, [Rows(x)], [mrow(0, 0, 1), mrow(0, 0, 0)], [(D_MODEL, BF16)], [],
                      tile=tile, name="pre", after=tuple(first_after))
    saved = {}
    x_in = x
    for idx, (l, j) in enumerate(subs):
        sv = {"x": x_in, "t": t}
        if j != 1:
            w1, w3, w2 = get_weights(("ffn", l, j // 2), t)
            h1, h3, act = matmul([[(t, w1)], [(t, w3)]], "nn", [F32, F32, BF16], tm=512, tn=1408, tk=D_MODEL,
                                 epi=swiglu_epi, name="ffn_up")
            (y,) = matmul([[(act, w2)]], "nn", [F32], tm=512, tn=1024, tk=1408, name="ffn_down")
            sv.update(h1=h1, h3=h3, act=act, big=(w1, w3, w2))
        elif l == 0:
            ab_w_in, ab_w_out = get_weights(("ab",), t)
            sv["big"] = (ab_w_in, ab_w_out)
            (proj,) = matmul([[(t, ab_w_in)]], "nn", [F32], tm=256, tn=AB_ALL, tk=D_MODEL, name="ab_in")
            xconv = conv_fwd(Rows(proj, (4, MIX_W)), mconv_w, mconv_b, tile=tile, name="mconv")
            a_rows = [Rows(proj, (0, MIX_W)), Rows(proj, (1, MIX_W)), Rows(xconv), Rows(proj, (AB_MAIN // 128, 128))]
            (q_a, k_a, lf_a, xc, q_b, k_b, gates), _ = rowwise(
                mix_a_fn, a_rows, mix_a_params, [(MIX_W, F32)] * 6 + [(128, F32)], [], tile=tile, name="mix_a")
            hg_xs = [Rows(q_a), Rows(k_a), Rows(proj, (2, MIX_W)), Rows(lf_a)]
            (o_a,), hg_kept = chunk_scan_fwd(hgrn2_step, hg_xs, hg_piece, [(HEAD_W, HEAD_W)] * HEADS, [MIX_W],
                                             name="hgrn2_fwd")
            ml_xs = [Rows(q_b), Rows(k_b), Rows(proj, (5, MIX_W)), Rows(gates)]
            ml_states = [(HEAD_W, HEAD_W)] * HEADS + [(1, HEAD_W)] * (2 * HEADS)
            (h_b,), ml_kept = chunk_scan_fwd(mlstm_step, ml_xs, ml_piece, ml_states, [MIX_W], name="mlstm_fwd")
            b_rows = [Rows(o_a, split=HEAD_W), Rows(proj, (3, MIX_W), HEAD_W), Rows(h_b, split=HEAD_W),
                      Rows(xc, split=HEAD_W), Rows(proj, (6, MIX_W), HEAD_W)]
            (ycat,), _ = rowwise(mix_b_fn, b_rows, mix_b_params, [(2 * MIX_W, BF16)], [], tile=tile, name="mix_b")
            (y,) = matmul([[(ycat, ab_w_out)]], "nn", [F32], tm=512, tn=1024, tk=D_MODEL, name="ab_out")
            sv.update(proj=proj, xconv=xconv, a_rows=a_rows, hg_xs=hg_xs, hg_kept=hg_kept, ml_xs=ml_xs,
                      ml_kept=ml_kept, b_rows=b_rows, ycat=ycat)
        else:
            rg_w_in, rg_w_out = get_weights(("rg",), t)
            sv["big"] = (rg_w_in, rg_w_out)
            (proj,) = matmul([[(t, rg_w_in)]], "nn", [F32], tm=512, tn=1024, tk=D_MODEL, name="rg_in")
            xr = conv_fwd(Rows(proj, (1, D_MODEL)), rconv_w, rconv_b, tile=tile, name="rconv")
            (a_t, u_t), _ = rowwise(lru_a_fn, [Rows(xr, split=128)], lru_a_params, [(D_MODEL, F32)] * 2, [],
                                    tile=tile, name="lru_a")
            h = lru_fwd(a_t, u_t, tile=min(128, s_len), name="lru_fwd")
            b_rows = [Rows(h), Rows(proj, (0, D_MODEL))]
            (hgate,), _ = rowwise(lru_b_fn, b_rows, [], [(D_MODEL, BF16)], [], tile=tile, name="lru_b")
            (y,) = matmul([[(hgate, rg_w_out)]], "nn", [F32], tm=512, tn=1024, tk=D_MODEL, name="rg_out")
            sv.update(proj=proj, xr=xr, a_t=a_t, h=h, b_rows=b_rows, hgate=hgate)
        sv["y"] = y
        post_params = [mrow(l, j, 2), row(w["ln_g"][l, j]), row(w["ln_b"][l, j])]
        if idx + 1 < len(subs):
            nl, nj = subs[idx + 1]
            post_params += [mrow(nl, nj, 1), mrow(nl, nj, 0)]
            (x_out, t), _ = rowwise(make_post_fn(weight_of(j), True), [Rows(x_in), Rows(y)], post_params,
                                    [(D_MODEL, F32), (D_MODEL, BF16)], [], tile=tile, name="post")
        else:
            (d_xo,), (loss_row,) = rowwise(make_last_fn(weight_of(j)), [Rows(x_in), Rows(y), Rows(target)],
                                           post_params, [(D_MODEL, F32)], [(1, 128)], tile=tile, name="post_loss")
            x_out = None
        sv["post_params"] = post_params
        saved[(l, j)] = sv
        x_in = x_out
    loss = loss_row[0, 0]

    d_t_next = None
    sent = []

    def hand_over(key, grads):
        token = put_grads(key, grads)
        if token is not None:
            sent.append(token)

    for idx in range(len(subs) - 1, -1, -1):
        l, j = subs[idx]
        sv = saved[(l, j)]
        has_next = idx + 1 < len(subs)
        cots = [Rows(d_xo)] + ([Rows(d_t_next)] if has_next else [])
        want_p = [0, 1, 2] + ([3, 4] if has_next else [])
        (d_xres, d_y), d_par = rowwise_bwd(
            make_post_fn(weight_of(j), has_next), [Rows(sv["x"]), Rows(sv["y"])], cots, sv["post_params"],
            [0, 1], want_p, [F32, BF16], tile=tile, name="post_bwd", after=tuple(sent))
        sent.clear()
        d_mod[l][3 * j + 2], d_ln_g[l][j], d_ln_b[l][j] = d_par[:3]
        if has_next:
            nl, nj = subs[idx + 1]
            d_mod[nl][3 * nj + 1], d_mod[nl][3 * nj] = d_par[3:]
        t = sv["t"]
        if j != 1:
            w1, w3, w2 = sv["big"]
            d_h1, d_h3 = matmul([[(d_y, w2)]], "nt", [BF16, BF16], tm=512, tn=1408, tk=D_MODEL,
                                extras=[sv["h1"], sv["h3"]], epi=swiglu_bwd_epi, name="ffn_down_bwd")
            (g_w2,) = matmul([[(sv["act"], d_y)]], "tn", [GRAD_WIRE], tm=1408, tn=1024, tk=512,
                                                 name="ffn_dw2")
            (g_w1,) = matmul([[(t, d_h1)]], "tn", [GRAD_WIRE], tm=1024, tn=1408, tk=512,
                                                 name="ffn_dw1")
            (g_w3,) = matmul([[(t, d_h3)]], "tn", [GRAD_WIRE], tm=1024, tn=1408, tk=512,
                                                 name="ffn_dw3")
            hand_over(("ffn", l, j // 2), [g_w1, g_w3, g_w2])
            (d_t,) = matmul([[(d_h1, w1), (d_h3, w3)]], "nt", [F32], tm=512, tn=1024, tk=1408, name="ffn_up_bwd")
        elif l == 0:
            ab_w_in, ab_w_out = sv["big"]
            (d_ycat,) = matmul([[(d_y, ab_w_out)]], "nt", [F32], tm=512, tn=1024, tk=D_MODEL, name="ab_out_bwd")
            (g_out,) = matmul([[(sv["ycat"], d_y)]], "tn", [GRAD_WIRE], tm=1024, tn=1024, tk=512, name="ab_dwout")
            (d_oa, d_ag, d_hb, d_xc, d_bz), (d_hg, d_mg, d_skip) = rowwise_bwd(
                mix_b_fn, sv["b_rows"], [Rows(d_ycat, split=HEAD_W)], mix_b_params, [0, 1, 2, 3, 4], [0, 1, 2],
                [F32, BF16, F32, F32, BF16], tile=tile, name="mix_b_bwd")
            g["hgrn_norm_g"], g["mlstm_norm_g"], g["mlstm_skip"] = (v.reshape(1, MIX_W) for v in (d_hg, d_mg, d_skip))
            d_qb, d_kb, d_bv, d_gates = chunk_scan_bwd(mlstm_step, sv["ml_xs"], ml_piece, sv["ml_kept"], [d_hb],
                                                       name="mlstm_bwd")
            d_qa, d_ka, d_ai, d_lf = chunk_scan_bwd(hgrn2_step, sv["hg_xs"], hg_piece, sv["hg_kept"], [d_oa],
                                                    name="hgrn2_bwd")
            a_cots = [Rows(v) for v in (d_qa, d_ka, d_lf, d_xc, d_qb, d_kb, d_gates)]
            (d_aq, d_af, d_xconv, d_graw), (d_l0, d_l1, d_l2, d_wq, d_wk, d_gb) = rowwise_bwd(
                mix_a_fn, sv["a_rows"], a_cots, mix_a_params, [0, 1, 2, 3], [0, 1, 2, 3, 4, 5],
                [BF16, BF16, F32, BF16], tile=tile, name="mix_a_bwd")
            g["hgrn_lb_logits"] = jnp.concatenate([d_l0, d_l1, d_l2], axis=0)
            g["mlstm_wq"] = _block_diag_of(d_wq, MIX_W // 4, 4)
            g["mlstm_wk"] = _block_diag_of(d_wk, MIX_W // 4, 4)
            g["mlstm_gate_b"] = d_gb[:, :8]
            d_bx, g["mlstm_conv_w"], g["mlstm_conv_b"] = conv_bwd(Rows(sv["proj"], (4, MIX_W)), d_xconv, mconv_w,
                                                                  tile=tile, name="mconv_bwd")
            d_proj = jnp.concatenate([d_aq, d_af, d_ai.astype(BF16), d_ag, d_bx, d_bv.astype(BF16), d_bz, d_graw],
                                     axis=1)
            (g_in,) = matmul([[(t, d_proj)]], "tn", [GRAD_WIRE], tm=256, tn=AB_ALL, tk=512, name="ab_dwin")
            hand_over(("ab",), [g_in, g_out])
            (d_t,) = matmul([[(d_proj, ab_w_in)]], "nt", [F32], tm=512, tn=1024, tk=AB_ALL, name="ab_in_bwd")
        else:
            rg_w_in, rg_w_out = sv["big"]
            (d_hgate,) = matmul([[(d_y, rg_w_out)]], "nt", [F32], tm=512, tn=1024, tk=D_MODEL,
                                name="rg_out_bwd")
            (g_out,) = matmul([[(sv["hgate"], d_y)]], "tn", [GRAD_WIRE], tm=1024, tn=1024, tk=512,
                                         name="rg_dwout")
            (d_h, d_ybr), _ = rowwise_bwd(lru_b_fn, sv["b_rows"], [Rows(d_hgate)], [], [0, 1], [], [F32, BF16],
                                          tile=tile, name="lru_b_bwd")
            d_a, d_u = lru_bwd(sv["a_t"], sv["h"], d_h, tile=min(128, s_len), name="lru_bwd")
            (d_xr,), (d_wa, d_wx, d_ba, d_bx_, d_lam) = rowwise_bwd(
                lru_a_fn, [Rows(sv["xr"], split=128)], [Rows(d_a, split=128), Rows(d_u, split=128)], lru_a_params,
                [0], [0, 1, 2, 3, 4], [F32], tile=tile, name="lru_a_bwd")
            g["rglru_wa"], g["rglru_wx"] = d_wa, d_wx
            g["rglru_ba"], g["rglru_bx"], g["rglru_lambda"] = (v.reshape(1, D_MODEL) for v in (d_ba, d_bx_, d_lam))
            d_xbr, g["rglru_conv_w"], g["rglru_conv_b"] = conv_bwd(Rows(sv["proj"], (1, D_MODEL)), d_xr, rconv_w,
                                                                   tile=tile, name="rconv_bwd")
            d_proj = jnp.concatenate([d_ybr, d_xbr], axis=1)
            (g_in,) = matmul([[(t, d_proj)]], "tn", [GRAD_WIRE], tm=1024, tn=1024, tk=512, name="rg_dwin")
            hand_over(("rg",), [g_in, g_out])
            (d_t,) = matmul([[(d_proj, rg_w_in)]], "nt", [F32], tm=512, tn=1024, tk=1024, name="rg_in_bwd")
        d_xo, d_t_next = d_xres, d_t

    def first_bwd(rows, params):
        x0, d_res, d_t0 = rows
        _, vjp = jax.vjp(lambda r, p: pre_fn(r, p)[0], [x0], params)
        (d_x0,), d_p = vjp([d_t0])
        return [d_res + d_x0], d_p

    (grad_x,), (d_mod[0][1], d_mod[0][0]) = rowwise(
        first_bwd, [Rows(x), Rows(d_xo), Rows(d_t_next)], [mrow(0, 0, 1), mrow(0, 0, 0)], [(D_MODEL, F32)],
        [(1, D_MODEL)] * 2, tile=tile, name="pre_bwd", after=tuple(sent))
    g["ln_g"] = jnp.stack([jnp.concatenate(r, axis=0) for r in d_ln_g])
    g["ln_b"] = jnp.stack([jnp.concatenate(r, axis=0) for r in d_ln_b])
    d_mod = jnp.stack([jnp.concatenate(r, axis=0) for r in d_mod])
    return loss, grad_x, d_mod, g


MESH_ID = pl.DeviceIdType.MESH
ANY_SPEC = pl.BlockSpec(memory_space=pl.ANY)


def _my_position():
    return lax.axis_index("x"), lax.axis_index("y"), lax.axis_index("c")


def _flat_index(pos):
    return 4 * pos[0] + 2 * pos[1] + pos[2]


def _peer_position(pos, k):
    return tuple(lax.rem(p + ((k >> s) & 1), 2) for p, s in zip(pos, (2, 1, 0)))


def _exchange(x, gather, name):
    out_shape = (NDEV,) + x.shape if gather else x.shape

    def body(x_ref, o_ref, send_sems, recv_sems, local_sem):
        pos = _my_position()
        me = _flat_index(pos)
        local = pltpu.make_async_copy(x_ref if gather else x_ref.at[me], o_ref.at[me], local_sem)
        local.start()
        copies = []
        for k in range(1, NDEV):
            peer = _peer_position(pos, k)
            src = x_ref if gather else x_ref.at[_flat_index(peer)]
            copies.append(pltpu.make_async_remote_copy(
                src_ref=src, dst_ref=o_ref.at[me], send_sem=send_sems.at[k - 1], recv_sem=recv_sems.at[k - 1],
                device_id=peer, device_id_type=MESH_ID))
            copies[-1].start()
        for cp in copies:
            cp.wait()
        local.wait()

    return pl.pallas_call(
        body, name=name, in_specs=[ANY_SPEC], out_specs=ANY_SPEC,
        out_shape=jax.ShapeDtypeStruct(out_shape, x.dtype),
        scratch_shapes=[pltpu.SemaphoreType.DMA((NDEV - 1,)), pltpu.SemaphoreType.DMA((NDEV - 1,)),
                        pltpu.SemaphoreType.DMA],
    )(x)


HBM_SPEC = pl.BlockSpec(memory_space=pltpu.HBM)
SEM_SPEC = pl.BlockSpec(memory_space=pltpu.SEMAPHORE)
SIDE_EFFECT = pltpu.SideEffectType.DATAFLOW_SIDE_EFFECTING


def _exchange_copies(x_refs, land_refs, send_sems, recv_sems, gather):
    pos = _my_position()
    me = _flat_index(pos)
    copies = []
    for k in range(1, NDEV):
        peer = _peer_position(pos, k)
        for x_ref, land_ref, s_sem, r_sem in zip(x_refs, land_refs, send_sems, recv_sems):
            src = x_ref if gather else x_ref.at[_flat_index(peer)]
            copies.append(pltpu.make_async_remote_copy(src_ref=src, dst_ref=land_ref.at[me], send_sem=s_sem,
                                                       recv_sem=r_sem, device_id=peer, device_id_type=MESH_ID))
    return copies


def exchange_start(xs, gather, name):
    n = len(xs)
    land_shapes = [(NDEV,) + x.shape if gather else x.shape for x in xs]

    def body(*refs):
        x_refs, land_refs = refs[:n], refs[n:2 * n]
        send_sems, recv_sems = refs[2 * n:3 * n], refs[3 * n:4 * n]
        token = refs[-1]
        for cp in _exchange_copies(x_refs, land_refs, send_sems, recv_sems, gather):
            cp.start()
        token[...] = jnp.zeros(token.shape, token.dtype)

    sem = pltpu.SemaphoreType.DMA(())
    res = pl.pallas_call(
        body, name=name,
        out_shape=[sem] * (2 * n) + [pltpu.HBM(x.shape, x.dtype) for x in xs]
        + [pltpu.HBM(s, x.dtype) for s, x in zip(land_shapes, xs)] + [jax.ShapeDtypeStruct((8, 128), F32)],
        in_specs=[HBM_SPEC] * (2 * n),
        out_specs=[SEM_SPEC] * (2 * n) + [HBM_SPEC] * (2 * n) + [pl.BlockSpec(memory_space=pltpu.VMEM)],
        input_output_aliases={i: 2 * n + i for i in range(2 * n)},
        compiler_params=pltpu.CompilerParams(has_side_effects=SIDE_EFFECT),
    )(*[pltpu.with_memory_space_constraint(x, pltpu.HBM) for x in xs],
      *[pltpu.with_memory_space_constraint(lax.empty(s, x.dtype), pltpu.HBM) for s, x in zip(land_shapes, xs)])
    return (res[:n], res[n:2 * n], res[2 * n:3 * n], res[3 * n:4 * n]), res[-1]


def exchange_wait(handles, after, name):
    send_sems, recv_sems, x_thru, land_thru = handles
    n = len(x_thru)

    def body(*refs):
        land_refs = refs[n:2 * n]
        s_sems, r_sems = refs[2 * n:3 * n], refs[3 * n:4 * n]
        pos = _my_position()
        for land_ref, s_sem, r_sem in zip(land_refs, s_sems, r_sems):
            seven = land_ref.at[pl.ds(0, NDEV - 1)]
            all_seven = pltpu.make_async_remote_copy(src_ref=seven, dst_ref=seven, send_sem=s_sem, recv_sem=r_sem,
                                                     device_id=pos, device_id_type=MESH_ID)
            all_seven.wait_send()
            all_seven.wait_recv()

    res = pl.pallas_call(
        body, name=name,
        out_shape=[pltpu.HBM(x.shape, x.dtype) for x in x_thru] + [pltpu.HBM(x.shape, x.dtype) for x in land_thru],
        in_specs=[HBM_SPEC] * (2 * n) + [SEM_SPEC] * (2 * n) + [ANY_SPEC],
        out_specs=[HBM_SPEC] * (2 * n),
        input_output_aliases={i: i for i in range(2 * n)},
        compiler_params=pltpu.CompilerParams(has_side_effects=SIDE_EFFECT),
    )(*x_thru, *land_thru, *send_sems, *recv_sems, after)
    return res[:n], res[n:]


def all_gather(x, name):
    return _exchange(x, True, name)


def all_to_all(x, name):
    return _exchange(x, False, name)


def _row_tile(n_rows, cap):
    best = None
    for t in range(8, min(n_rows, cap) + 1, 8):
        if n_rows % t == 0:
            best = t
    return best if best else n_rows


def adamw(w, m, v, slots, *, name):
    n_rows, width = w.shape
    n_slots = slots.shape[0]
    lanes = -(-width // 128) * 128
    tile = _row_tile(n_rows, max(8, (1 << 20) // (4 * lanes) // 8 * 8))
    bc1 = 1.0 - ADAM_B1 ** ADAM_STEP
    bc2 = 1.0 - ADAM_B2 ** ADAM_STEP

    def body(w_ref, m_ref, v_ref, s_ref, g_ref, d_ref, nm_ref, nv_ref):
        g = s_ref[0].astype(F32)
        for k in range(1, n_slots):
            g = g + s_ref[k].astype(F32)
        wv = w_ref[...]
        nm = ADAM_B1 * m_ref[...] + (1.0 - ADAM_B1) * g
        nv = ADAM_B2 * v_ref[...] + (1.0 - ADAM_B2) * (g * g)
        g_ref[...] = g
        nm_ref[...] = nm
        nv_ref[...] = nv
        d_ref[...] = -ADAM_LR * ((nm / bc1) / (jnp.sqrt(nv / bc2) + ADAM_EPS) + ADAM_WD * wv)

    spec = pl.BlockSpec((tile, width), lambda i: (i, 0))
    return pl.pallas_call(
        body, name=name, grid=(n_rows // tile,),
        in_specs=[spec, spec, spec, pl.BlockSpec((n_slots, tile, width), lambda i: (0, i, 0))],
        out_specs=[spec] * 4, out_shape=[jax.ShapeDtypeStruct((n_rows, width), F32)] * 4,
        compiler_params=_params(dimension_semantics=("parallel",)),
    )(w, m, v, slots)


def sum_slots(slots, *, name):
    n_slots, n_rows, width = slots.shape

    def body(s_ref, o_ref):
        @pl.when(pl.program_id(0) == 0)
        def _():
            o_ref[...] = s_ref[...]

        @pl.when(pl.program_id(0) > 0)
        def _():
            o_ref[...] += s_ref[...]

    return pl.pallas_call(
        body, name=name, grid=(n_slots,),
        in_specs=[pl.BlockSpec((None, n_rows, width), lambda k: (k, 0, 0))],
        out_specs=pl.BlockSpec((n_rows, width), lambda k: (0, 0)),
        out_shape=jax.ShapeDtypeStruct((n_rows, width), F32),
        compiler_params=_params(dimension_semantics=("arbitrary",)),
    )(slots)


def adamw_nd(w, m, v, slots, *, name):
    shp = w.shape
    two = (-1, shp[-1])
    res = adamw(w.reshape(two), m.reshape(two), v.reshape(two), slots.reshape((slots.shape[0],) + (w.size // shp[-1], shp[-1])),
                name=name)
    return [r.reshape(shp) for r in res]


def _pack(arrs):
    parts = []
    for a in arrs:
        flat = a.reshape(-1).astype(F32)
        parts.append(jnp.pad(flat, (0, (-flat.shape[0]) % 1024)))
    return jnp.concatenate(parts).reshape(-1, 128)


def _unpack(buf, shapes):
    outs, at = [], 0
    flat = buf.reshape(-1)
    for shp in shapes:
        n = int(np.prod(shp))
        outs.append(flat[at:at + n].reshape(shp))
        at += n + (-n) % 1024
    return outs


ARG_NAMES = ["x", "c", "ada_w", "ada_b", "ln_g", "ln_b", "ffn_w1", "ffn_w3", "ffn_w2", "hgrn_lb_logits", "ab_w_in",
             "ab_w_out", "hgrn_norm_g", "mlstm_conv_w", "mlstm_conv_b", "mlstm_wq", "mlstm_wk", "mlstm_gate_b",
             "mlstm_skip", "mlstm_norm_g", "rglru_w_in", "rglru_conv_w", "rglru_conv_b", "rglru_wa", "rglru_ba",
             "rglru_wx", "rglru_bx", "rglru_lambda", "rglru_w_out", "loss_target"]
WEIGHTS = ARG_NAMES[2:-1]
BIG = ["ffn_w1", "ffn_w3", "ffn_w2", "ab_w_in", "ab_w_out", "rglru_w_in", "rglru_w_out"]
REPLICATED = ["ada_b", "hgrn_lb_logits", "hgrn_norm_g", "mlstm_conv_b", "mlstm_wq", "mlstm_wk", "mlstm_gate_b",
              "mlstm_skip", "mlstm_norm_g", "rglru_wa", "rglru_wx"]
SHARDED_SMALL = ["ln_g", "ln_b", "mlstm_conv_w", "rglru_conv_w", "rglru_conv_b", "rglru_ba", "rglru_bx", "rglru_lambda"]


def _unshard_last(gathered):
    moved = jnp.moveaxis(gathered, 0, -2)
    return moved.reshape(moved.shape[:-2] + (NDEV * moved.shape[-1],))


def _shard_last(full):
    split = full.reshape(full.shape[:-1] + (NDEV, full.shape[-1] // NDEV))
    return jnp.moveaxis(split, -2, 0)


def kernel(x, c, ada_w, ada_b, ln_g, ln_b, ffn_w1, ffn_w3, ffn_w2, hgrn_lb_logits, ab_w_in, ab_w_out, hgrn_norm_g, mlstm_conv_w, mlstm_conv_b, mlstm_wq, mlstm_wk, mlstm_gate_b, mlstm_skip, mlstm_norm_g, rglru_w_in, rglru_conv_w, rglru_conv_b, rglru_wa, rglru_ba, rglru_wx, rglru_bx, rglru_lambda, rglru_w_out, loss_target, m_ada_w, m_ada_b, m_ln_g, m_ln_b, m_ffn_w1, m_ffn_w3, m_ffn_w2, m_hgrn_lb_logits, m_ab_w_in, m_ab_w_out, m_hgrn_norm_g, m_mlstm_conv_w, m_mlstm_conv_b, m_mlstm_wq, m_mlstm_wk, m_mlstm_gate_b, m_mlstm_skip, m_mlstm_norm_g, m_rglru_w_in, m_rglru_conv_w, m_rglru_conv_b, m_rglru_wa, m_rglru_ba, m_rglru_wx, m_rglru_bx, m_rglru_lambda, m_rglru_w_out, v_ada_w, v_ada_b, v_ln_g, v_ln_b, v_ffn_w1, v_ffn_w3, v_ffn_w2, v_hgrn_lb_logits, v_ab_w_in, v_ab_w_out, v_hgrn_norm_g, v_mlstm_conv_w, v_mlstm_conv_b, v_mlstm_wq, v_mlstm_wk, v_mlstm_gate_b, v_mlstm_skip, v_mlstm_norm_g, v_rglru_w_in, v_rglru_conv_w, v_rglru_conv_b, v_rglru_wa, v_rglru_ba, v_rglru_wx, v_rglru_bx, v_rglru_lambda, v_rglru_w_out):
    args = locals()
    p = {n: args[n] for n in ARG_NAMES}
    mom = {n: (args["m_" + n], args["v_" + n]) for n in WEIGHTS}
    me = _flat_index(_my_position())

    keys = [("ffn", 0, 0), ("ab",), ("ffn", 0, 1), ("ffn", 1, 0), ("rg",), ("ffn", 1, 1)]
    names = {("ab",): ("ab_w_in", "ab_w_out"), ("rg",): ("rglru_w_in", "rglru_w_out")}
    for l in range(DEPTH):
        for i in range(2):
            names[("ffn", l, i)] = ("ffn_w1", "ffn_w3", "ffn_w2")

    def part(key):
        return (lambda a: a[key[1], key[2]]) if key[0] == "ffn" else (lambda a: a[0])

    gather_handles = {}

    def landed(handles, own_of, after, name):
        sources, lands = exchange_wait(handles, after, name)
        return [lax.dynamic_update_index_in_dim(ld, own_of(src), me, 0) for src, ld in zip(sources, lands)]

    starts_next = {("ffn", 0, 0): [("ab",)], ("ab",): [("ffn", 0, 1), ("ffn", 1, 0)],
                   ("ffn", 0, 1): [("rg",), ("ffn", 1, 1)]}

    def start_gather(key, after):
        shards = [part(key)(big[n]).astype(BF16) for n in names[key]]
        shards, _ = lax.optimization_barrier((shards, after))
        gather_handles[key], token = exchange_start(shards, True, "gather_start_" + "_".join(map(str, key)))
        return token

    def get_weights(key, after):
        got = landed(gather_handles[key], lambda src: src, after, "gather_wait_" + "_".join(map(str, key)))
        tokens = [start_gather(nxt, got) for nxt in starts_next.get(key, [])]
        if tokens:
            got, _ = lax.optimization_barrier((got, tokens))
        if key[0] == "ffn":
            return _unshard_last(got[0]), _unshard_last(got[1]), got[2].reshape(D_FF, D_MODEL)
        w_in = _unshard_last(got[0])
        if key[0] == "ab":
            w_in = jnp.concatenate([w_in[:, :AB_MAIN], jnp.pad(w_in[:, AB_MAIN:], ((0, 0), (0, 120)))], axis=1)
        return w_in, got[1].reshape(D_MODEL, D_MODEL)

    scatter_handles = {}

    def put_grads(key, grads):
        if key[0] == "ffn":
            slots = [_shard_last(grads[0]), _shard_last(grads[1]), grads[2].reshape(NDEV, D_FF // NDEV, D_MODEL)]
        else:
            g_in = grads[0][:, :AB_MAIN + 8] if key[0] == "ab" else grads[0]
            slots = [_shard_last(g_in), grads[1].reshape(NDEV, D_MODEL // NDEV, D_MODEL)]
        scatter_handles[key], token = exchange_start(slots, False, "scatter_start_" + "_".join(map(str, key)))
        return token

    sharded_shapes = [p[n].shape for n in SHARDED_SMALL]
    small = all_gather(_pack([p[n] for n in SHARDED_SMALL] + [c]), "gather_small")
    per_dev = [_unpack(small[d], sharded_shapes + [c.shape]) for d in range(NDEV)]
    full_small = {n: _unshard_last(jnp.stack([per_dev[d][i] for d in range(NDEV)]))
                  for i, n in enumerate(SHARDED_SMALL)}
    c_all = jnp.concatenate([per_dev[d][-1] for d in range(NDEV)], axis=0)

    c16 = jnp.pad(c_all, ((0, 8), (0, 0)))
    (c_act,), _ = rowwise(lambda r, q: ([_silu(r[0])], []), [Rows(c16)], [], [(D_MODEL, BF16)], [], tile=16,
                          name="cond_act")
    n_ada = ada_w.shape[-1]
    ada_b_mine = lax.dynamic_slice_in_dim(ada_b, me * n_ada, n_ada, axis=1)
    ada_cols = []
    for l in range(DEPTH):
        bias = jnp.broadcast_to(ada_b_mine[l][None, :], (16, n_ada))
        (cols,) = matmul([[(c_act, ada_w[l])]], "nn", [F32], tm=16, tn=n_ada, tk=D_MODEL, extras=[bias],
                         epi=lambda accs, ex: [accs[0] + ex[0]], name="ada_fwd")
        ada_cols.append(cols[:8])
    ada_mine = all_to_all(jnp.stack(ada_cols, axis=1), "ada_to_owner")
    mod = jnp.moveaxis(ada_mine, 0, 1).reshape(DEPTH, 9, D_MODEL)

    big = {n: p[n] for n in BIG}
    started = [start_gather(keys[0], mod)]

    w = {"ln_g": full_small["ln_g"], "ln_b": full_small["ln_b"], "hgrn_lb_logits": hgrn_lb_logits}
    for n in ("hgrn_norm_g", "mlstm_conv_b", "mlstm_wq", "mlstm_wk", "mlstm_gate_b", "mlstm_skip", "mlstm_norm_g",
              "rglru_wa", "rglru_wx"):
        w[n] = p[n][0]
    for n in ("mlstm_conv_w", "rglru_conv_w", "rglru_conv_b", "rglru_ba", "rglru_bx", "rglru_lambda"):
        w[n] = full_small[n][0]

    loss, grad_x, d_mod, g = local_step(x[0], loss_target[0], mod, w, get_weights, put_grads, started)
    loss = lax.psum(loss, ("x", "y", "c"))

    outs = {}
    parts = {n: {} for n in BIG}
    arrived = []

    def update_group(key, after):
        slots = landed(scatter_handles[key], lambda src: lax.dynamic_index_in_dim(src, me, 0, keepdims=False), after,
                       "scatter_wait_" + "_".join(map(str, key)))
        arrived.append(slots)
        for n, sl in zip(names[key], slots):
            sel = part(key)
            parts[n][key] = adamw_nd(sel(p[n]), sel(mom[n][0]), sel(mom[n][1]), sl, name="adamw_" + n)

    for key in keys[:0:-1]:
        update_group(key, grad_x)

    g["ada_b"] = d_mod.reshape(DEPTH, 9 * D_MODEL)
    small_names = REPLICATED + SHARDED_SMALL
    full_shapes = [p[n].shape for n in REPLICATED] + [full_small[n].shape for n in SHARDED_SMALL]
    packed, _ = lax.optimization_barrier((_pack([g[n] for n in small_names]), arrived))
    all_small = all_gather(packed, "gather_small_grads")
    summed = sum_slots(all_small, name="sum_small_grads")
    g_small = dict(zip(small_names, _unpack(summed, full_shapes)))
    rep = adamw(*[_pack([t[n] for n in REPLICATED]) for t in (p, {n: mom[n][0] for n in WEIGHTS},
                                                               {n: mom[n][1] for n in WEIGHTS})],
                _pack([g_small[n] for n in REPLICATED])[None], name="adamw_replicated")
    rep = [_unpack(r, [p[n].shape for n in REPLICATED]) for r in rep]
    for i, n in enumerate(REPLICATED):
        outs[n] = [r[i] for r in rep]
    g_mine = {n: lax.dynamic_slice_in_dim(g_small[n], me * p[n].shape[-1], p[n].shape[-1], axis=-1)
              for n in SHARDED_SMALL}
    shd = adamw(*[_pack([t[n] for n in SHARDED_SMALL]) for t in (p, {n: mom[n][0] for n in WEIGHTS},
                                                                  {n: mom[n][1] for n in WEIGHTS})],
                _pack([g_mine[n] for n in SHARDED_SMALL])[None], name="adamw_sharded_small")
    update_group(keys[0], shd[0])
    for n in BIG:
        if n.startswith("ffn"):
            outs[n] = [jnp.stack([jnp.stack([parts[n][("ffn", l, i)][k] for i in range(2)]) for l in range(DEPTH)])
                       for k in range(4)]
        else:
            (only,) = parts[n].values()
            outs[n] = [r[None] for r in only]
    shd = [_unpack(r, sharded_shapes) for r in shd]
    for i, n in enumerate(SHARDED_SMALL):
        outs[n] = [r[i] for r in shd]

    d_ada = all_small[:, :DEPTH * 9 * D_MODEL // 128].reshape(NDEV, DEPTH, 9 * D_MODEL)
    d_mine = lax.dynamic_slice_in_dim(d_ada, me * n_ada, n_ada, axis=2)
    g_ada = []
    for l in range(DEPTH):
        d16 = jnp.pad(d_mine[:, l], ((0, 8), (0, 0)))
        (gl,) = matmul([[(c_act, d16)]], "tn", [F32], tm=D_MODEL, tn=n_ada, tk=16, name="ada_bwd")
        g_ada.append(gl)
    outs["ada_w"] = adamw_nd(ada_w, *mom["ada_w"], jnp.stack(g_ada)[None], name="adamw_ada_w")

    result = [loss, grad_x[None]]
    for k in range(4):
        result += [outs[n][k].reshape(p[n].shape) for n in WEIGHTS]
    return tuple(result)
```

```python
import functools

import jax
import jax.numpy as jnp
import numpy as np
from jax import lax
from jax.experimental import pallas as pl
from jax.experimental.pallas import tpu as pltpu

F32 = jnp.float32
BF16 = jnp.bfloat16
HIGHEST = lax.Precision.HIGHEST

NDEV = 8
D_MODEL = 1024
D_FF = 2816
DEPTH = 2
CHUNK = 64
SUB = 16
HEADS = 4
HEAD_W = 128
MIX_W = HEADS * HEAD_W
AB_MAIN = 7 * MIX_W
AB_ALL = AB_MAIN + 128
CONV_W = 4
C_BLOCKS = 8
RG_C = 8.0
ALPHA = (2 * DEPTH) ** 0.25
FFN_RES_W = 0.5
NEG = -1e30

ADAM_LR = 0.001
ADAM_B1 = 0.9
ADAM_B2 = 0.999
ADAM_EPS = 1e-08
ADAM_WD = 0.01
ADAM_STEP = 10

VMEM_LIMIT = 56 * 1024 * 1024
GRAD_WIRE = jnp.bfloat16

NN = ((1,), (0,))
NT = ((1,), (1,))
TN = ((0,), (0,))


def _dot(a, b, dims, precision=None):
    return lax.dot_general(a, b, (dims, ((), ())), precision=precision, preferred_element_type=F32)


def _make_mm(dims, d_lhs, d_rhs, swap_lhs, swap_rhs, prec):
    def cast(v):
        return v.astype(BF16) if prec is None else v.astype(F32)

    @jax.custom_vjp
    def mm(a, b):
        return _dot(cast(a), cast(b), dims, prec)

    def fwd(a, b):
        return mm(a, b), (a, b)

    def bwd(res, g):
        a, b = res
        g = cast(g)
        da = _dot(cast(b), g, d_lhs, prec) if swap_lhs else _dot(g, cast(b), d_lhs, prec)
        db = _dot(g, cast(a), d_rhs, prec) if swap_rhs else _dot(cast(a), g, d_rhs, prec)
        return da.astype(a.dtype), db.astype(b.dtype)

    mm.defvjp(fwd, bwd)
    return mm


def _mm_family(prec):
    return (_make_mm(NN, NT, TN, False, False, prec), _make_mm(NT, NN, TN, False, True, prec),
            _make_mm(TN, NT, NN, True, False, prec))


def _round(v):
    return v.astype(BF16).astype(F32)


@jax.custom_vjp
def row_dot(a, n):
    return jnp.sum(_round(a) * _round(n), axis=1, keepdims=True)


def _row_dot_fwd(a, n):
    return row_dot(a, n), (a, n)


def _row_dot_bwd(res, g):
    a, n = res
    g = _round(g)
    return g * _round(n), jnp.sum(g * _round(a), axis=0, keepdims=True)


row_dot.defvjp(_row_dot_fwd, _row_dot_bwd)


@jax.custom_vjp
def col_dot(s, a):
    return jnp.sum(_round(s) * _round(a), axis=0, keepdims=True)


def _col_dot_fwd(s, a):
    return col_dot(s, a), (s, a)


def _col_dot_bwd(res, g):
    s, a = res
    g = _round(g)
    return jnp.sum(g * _round(a), axis=1, keepdims=True), _round(s) * g


col_dot.defvjp(_col_dot_fwd, _col_dot_bwd)

mm_nn, mm_nt, mm_tn = _mm_family(None)
mid_nn, mid_nt, mid_tn = _mm_family(lax.Precision.HIGH)
hi_nn, hi_nt, hi_tn = _mm_family(HIGHEST)


def _silu(v):
    return v * jax.nn.sigmoid(v)


def _log_sigmoid(v):
    return jnp.minimum(v, 0.0) - jnp.log1p(jnp.exp(-jnp.abs(v)))


def _softplus(v):
    return jnp.maximum(v, 0.0) + jnp.log1p(jnp.exp(-jnp.abs(v)))


def _neg_expm1(v):
    series = -v * (1.0 + v * (0.5 + v * (1.0 / 6.0 + v * (1.0 / 24.0 + v * (1.0 / 120.0)))))
    return jnp.where(v > -0.05, series, 1.0 - jnp.exp(v))


def _params(**kw):
    return pltpu.CompilerParams(vmem_limit_bytes=VMEM_LIMIT, **kw)


class Rows:
    def __init__(self, arr, block=None, split=None):
        self.arr = arr
        self.block = block
        self.split = split

    @property
    def width(self):
        return self.block[1] if self.block else self.arr.shape[1]


def _load(ref, split):
    if ref.ndim == 3:
        return [ref[k].astype(F32) for k in range(ref.shape[0])]
    if split is None:
        return ref[...].astype(F32)
    return [ref[:, k * split:(k + 1) * split].astype(F32) for k in range(ref.shape[1] // split)]


def _store(ref, val, accumulate=False):
    if isinstance(val, (list, tuple)):
        if ref.ndim == 3:
            for k, v in enumerate(val):
                ref[k] = (ref[k] + v if accumulate else v).astype(ref.dtype)
            return
        w = ref.shape[1] // len(val)
        for k, v in enumerate(val):
            sl = slice(k * w, (k + 1) * w)
            ref[:, sl] = (ref[:, sl] + v if accumulate else v).astype(ref.dtype)
    else:
        ref[...] = (ref[...] + val if accumulate else val).astype(ref.dtype)


def rowwise(fn, rows, params, out_rows, out_sums, *, tile, name, after=()):
    n_rows = rows[0].arr.shape[0]
    n_r, n_p, n_o = len(rows), len(params), len(out_rows)
    n_in = n_r + n_p + len(after)
    splits = [r.split for r in rows]

    def body(*refs):
        r_refs, p_refs = refs[:n_r], refs[n_r:n_r + n_p]
        o_refs, s_refs = refs[n_in:n_in + n_o], refs[n_in + n_o:]
        row_out, sum_out = fn([_load(r, s) for r, s in zip(r_refs, splits)], [_load(p, None) for p in p_refs])
        for ref, val in zip(o_refs, row_out):
            _store(ref, val)
        if s_refs:
            @pl.when(pl.program_id(0) == 0)
            def _():
                for ref in s_refs:
                    ref[...] = jnp.zeros(ref.shape, ref.dtype)

            for ref, val in zip(s_refs, sum_out):
                _store(ref, val, accumulate=True)

    in_specs = []
    for r in rows:
        blk = r.block[0] if r.block else 0
        in_specs.append(pl.BlockSpec((tile, r.width), functools.partial(lambda i, b: (i, b), b=blk)))
    for p in params:
        in_specs.append(pl.BlockSpec(p.shape, functools.partial(lambda i, n: (0,) * n, n=p.ndim)))
    in_specs += [pl.BlockSpec(memory_space=pl.ANY)] * len(after)
    out_shape = [jax.ShapeDtypeStruct((n_rows, w), dt) for w, dt in out_rows]
    out_specs = [pl.BlockSpec((tile, w), lambda i: (i, 0)) for w, _ in out_rows]
    for shp in out_sums:
        out_shape.append(jax.ShapeDtypeStruct(shp, F32))
        out_specs.append(pl.BlockSpec(shp, functools.partial(lambda i, n: (0,) * n, n=len(shp))))
    res = pl.pallas_call(
        body, name=name, grid=(n_rows // tile,), in_specs=in_specs, out_specs=out_specs, out_shape=out_shape,
        compiler_params=_params(dimension_semantics=("arbitrary",)),
    )(*[r.arr for r in rows], *params, *after)
    return res[:n_o], res[n_o:]


def rowwise_bwd(fn, rows, cots, params, want_rows, want_params, out_dtypes, *, tile, name, after=()):
    n = len(rows)

    def bwd(row_vals, param_vals):
        prim, cot = row_vals[:n], row_vals[n:]
        _, vjp = jax.vjp(lambda r, p: fn(r, p)[0], prim, param_vals)
        d_rows, d_params = vjp(cot)
        return [d_rows[i] for i in want_rows], [d_params[j] for j in want_params]

    out_rows = [(rows[i].width, dt) for i, dt in zip(want_rows, out_dtypes)]
    out_sums = [params[j].shape for j in want_params]
    return rowwise(bwd, list(rows) + list(cots), params, out_rows, out_sums, tile=tile, name=name, after=after)


def matmul(groups, mode, outs, *, tm, tn, tk, extras=(), epi=None, name):
    a0, b0 = groups[0][0]
    if mode == "tn":
        k_dim, m_dim = a0.shape
    else:
        m_dim, k_dim = a0.shape
    n_dim = b0.shape[0] if mode == "nt" else b0.shape[1]
    tm, tn, tk = min(tm, m_dim), min(tn, n_dim), min(tk, k_dim)
    assert m_dim % tm == 0 and n_dim % tn == 0 and k_dim % tk == 0, (name, m_dim, n_dim, k_dim)
    nk = k_dim // tk
    pairs = [p for g in groups for p in g]
    n_pairs, n_groups, n_ex, n_out = len(pairs), len(groups), len(extras), len(outs)
    dims = {"nn": NN, "nt": NT, "tn": TN}[mode]

    def body(*refs):
        ab = refs[:2 * n_pairs]
        ex = refs[2 * n_pairs:2 * n_pairs + n_ex]
        o_refs = refs[2 * n_pairs + n_ex:2 * n_pairs + n_ex + n_out]
        accs = refs[2 * n_pairs + n_ex + n_out:]

        def partial_sums():
            sums, p = [], 0
            for g in groups:
                tot = None
                for _ in g:
                    d = _dot(ab[2 * p][...].astype(BF16), ab[2 * p + 1][...].astype(BF16), dims)
                    tot = d if tot is None else tot + d
                    p += 1
                sums.append(tot)
            return sums

        def finish(vals):
            res = epi(vals, [e[...] for e in ex]) if epi else vals
            for ref, v in zip(o_refs, res):
                ref[...] = v.astype(ref.dtype)

        if nk == 1:
            finish(partial_sums())
        else:
            k = pl.program_id(2)

            @pl.when(k == 0)
            def _():
                for acc in accs:
                    acc[...] = jnp.zeros(acc.shape, F32)

            for acc, s in zip(accs, partial_sums()):
                acc[...] += s

            @pl.when(k == nk - 1)
            def _():
                finish([acc[...] for acc in accs])

    if mode == "nn":
        a_spec = pl.BlockSpec((tm, tk), lambda i, j, k: (i, k))
        b_spec = pl.BlockSpec((tk, tn), lambda i, j, k: (k, j))
    elif mode == "nt":
        a_spec = pl.BlockSpec((tm, tk), lambda i, j, k: (i, k))
        b_spec = pl.BlockSpec((tn, tk), lambda i, j, k: (j, k))
    else:
        a_spec = pl.BlockSpec((tk, tm), lambda i, j, k: (k, i))
        b_spec = pl.BlockSpec((tk, tn), lambda i, j, k: (k, j))
    mn_spec = pl.BlockSpec((tm, tn), lambda i, j, k: (i, j))
    return pl.pallas_call(
        body, name=name, grid=(m_dim // tm, n_dim // tn, nk),
        in_specs=[a_spec, b_spec] * n_pairs + [mn_spec] * n_ex,
        out_specs=[mn_spec] * n_out,
        out_shape=[jax.ShapeDtypeStruct((m_dim, n_dim), dt) for dt in outs],
        scratch_shapes=[pltpu.VMEM((tm, tn), F32)] * (n_groups if nk > 1 else 0),
        compiler_params=_params(dimension_semantics=("parallel", "parallel", "arbitrary")),
    )(*[x for p in pairs for x in p], *extras)


def conv_fwd(x, w, b, *, tile, name):
    n_rows, width = x.arr.shape[0], x.width
    blk = x.block[0] if x.block else 0

    def body(x_ref, halo_ref, w_ref, b_ref, y_ref, buf):
        i = pl.program_id(0)
        halo = halo_ref[...]
        buf[0:8, :] = jnp.where(i == 0, jnp.zeros_like(halo), _round(halo))
        buf[8:, :] = _round(x_ref[...])
        acc = jnp.zeros((tile, width), F32)
        for j in range(CONV_W):
            s = CONV_W - 1 - j
            acc = acc + _round(w_ref[j:j + 1, :]) * buf[8 - s:8 - s + tile, :]
        y_ref[...] = acc + b_ref[...]

    hb = tile // 8
    return pl.pallas_call(
        body, name=name, grid=(n_rows // tile,),
        in_specs=[pl.BlockSpec((tile, width), lambda i: (i, blk)),
                  pl.BlockSpec((8, width), lambda i: (jnp.maximum(i * hb - 1, 0), blk)),
                  pl.BlockSpec((CONV_W, width), lambda i: (0, 0)),
                  pl.BlockSpec((1, width), lambda i: (0, 0))],
        out_specs=pl.BlockSpec((tile, width), lambda i: (i, 0)),
        out_shape=jax.ShapeDtypeStruct((n_rows, width), F32),
        scratch_shapes=[pltpu.VMEM((tile + 8, width), F32)],
        compiler_params=_params(dimension_semantics=("arbitrary",)),
    )(x.arr, x.arr, w, b)


def conv_bwd(x, dy, w, *, tile, name):
    n_rows, width = x.arr.shape[0], x.width
    blk = x.block[0] if x.block else 0
    n_tiles = n_rows // tile

    def body(x_ref, xh_ref, dy_ref, dyh_ref, w_ref, dx_ref, dw_ref, db_ref, xbuf, dbuf):
        i = pl.program_id(0)
        xh, dyh = xh_ref[...], dyh_ref[...]
        xbuf[0:8, :] = jnp.where(i == 0, jnp.zeros_like(xh), _round(xh))
        xbuf[8:, :] = _round(x_ref[...])
        dy_t = dy_ref[...]
        dy_r = _round(dy_t)
        dbuf[0:tile, :] = dy_r
        dbuf[tile:, :] = jnp.where(i == n_tiles - 1, jnp.zeros_like(dyh), _round(dyh))

        @pl.when(i == 0)
        def _():
            dw_ref[...] = jnp.zeros(dw_ref.shape, F32)
            db_ref[...] = jnp.zeros(db_ref.shape, F32)

        acc = jnp.zeros((tile, width), F32)
        for j in range(CONV_W):
            s = CONV_W - 1 - j
            acc = acc + _round(w_ref[j:j + 1, :]) * dbuf[s:s + tile, :]
            dw_ref[j:j + 1, :] += jnp.sum(dy_r * xbuf[8 - s:8 - s + tile, :], axis=0, keepdims=True)
        dx_ref[...] = acc.astype(dx_ref.dtype)
        db_ref[...] += jnp.sum(dy_t, axis=0, keepdims=True)

    hb = tile // 8
    return pl.pallas_call(
        body, name=name, grid=(n_tiles,),
        in_specs=[pl.BlockSpec((tile, width), lambda i: (i, blk)),
                  pl.BlockSpec((8, width), lambda i: (jnp.maximum(i * hb - 1, 0), blk)),
                  pl.BlockSpec((tile, width), lambda i: (i, 0)),
                  pl.BlockSpec((8, width), lambda i: (jnp.minimum((i + 1) * hb, n_tiles * hb - 1), 0)),
                  pl.BlockSpec((CONV_W, width), lambda i: (0, 0))],
        out_specs=[pl.BlockSpec((tile, width), lambda i: (i, 0)),
                   pl.BlockSpec((CONV_W, width), lambda i: (0, 0)),
                   pl.BlockSpec((1, width), lambda i: (0, 0))],
        out_shape=[jax.ShapeDtypeStruct((n_rows, width), BF16),
                   jax.ShapeDtypeStruct((CONV_W, width), F32),
                   jax.ShapeDtypeStruct((1, width), F32)],
        scratch_shapes=[pltpu.VMEM((tile + 8, width), F32), pltpu.VMEM((tile + 8, width), F32)],
        compiler_params=_params(dimension_semantics=("arbitrary",)),
    )(x.arr, x.arr, dy, dy, w)


def _pieces(ref, col_w, row_h):
    n_c, n_r = ref.shape[1] // col_w, ref.shape[0] // row_h
    return [[ref[r * row_h:(r + 1) * row_h, c * col_w:(c + 1) * col_w].astype(F32) for r in range(n_r)]
            for c in range(n_c)]


def _store_pieces(ref, vals, col_w, row_h):
    for c, col in enumerate(vals):
        for r, v in enumerate(col):
            ref[r * row_h:(r + 1) * row_h, c * col_w:(c + 1) * col_w] = v.astype(ref.dtype)


def _x_spec(x, n_chunks, reverse):
    blk = x.block[0] if x.block else 0
    if reverse:
        return pl.BlockSpec((CHUNK, x.width), functools.partial(lambda n, b: (n_chunks - 1 - n, b), b=blk))
    return pl.BlockSpec((CHUNK, x.width), functools.partial(lambda n, b: (n, b), b=blk))


def chunk_scan_fwd(step, xs, piece, state_shapes, out_widths, *, name):
    n_rows = xs[0].arr.shape[0]
    n_chunks = n_rows // CHUNK
    n_x, n_s, n_o = len(xs), len(state_shapes), len(out_widths)

    def body(*refs):
        x_refs, o_refs = refs[:n_x], refs[n_x:n_x + n_o]
        keep_refs, st_refs = refs[n_x + n_o:n_x + n_o + n_s], refs[n_x + n_o + n_s:]

        @pl.when(pl.program_id(0) == 0)
        def _():
            for st in st_refs:
                st[...] = jnp.zeros(st.shape, F32)

        states = [st[...] for st in st_refs]
        for keep, s in zip(keep_refs, states):
            keep[...] = s
        new_states, outs = step(states, [_pieces(x, *p) for x, p in zip(x_refs, piece)])
        for st, s in zip(st_refs, new_states):
            st[...] = s
        for o, v in zip(o_refs, outs):
            o[...] = v

    out_shape = [jax.ShapeDtypeStruct((n_rows, w), F32) for w in out_widths]
    out_specs = [pl.BlockSpec((CHUNK, w), lambda n: (n, 0)) for w in out_widths]
    for shp in state_shapes:
        out_shape.append(jax.ShapeDtypeStruct((n_chunks,) + shp, F32))
        out_specs.append(pl.BlockSpec((None,) + shp, lambda n: (n, 0, 0)))
    res = pl.pallas_call(
        body, name=name, grid=(n_chunks,),
        in_specs=[_x_spec(x, n_chunks, False) for x in xs],
        out_specs=out_specs, out_shape=out_shape,
        scratch_shapes=[pltpu.VMEM(shp, F32) for shp in state_shapes],
        compiler_params=_params(dimension_semantics=("arbitrary",)),
    )(*[x.arr for x in xs])
    return res[:n_o], res[n_o:]


def chunk_scan_bwd(step, xs, piece, kept, d_outs, *, name):
    n_rows = xs[0].arr.shape[0]
    n_chunks = n_rows // CHUNK
    n_x, n_s, n_o = len(xs), len(kept), len(d_outs)
    state_shapes = [k.shape[1:] for k in kept]

    def body(*refs):
        x_refs, k_refs = refs[:n_x], refs[n_x:n_x + n_s]
        do_refs = refs[n_x + n_s:n_x + n_s + n_o]
        dx_refs = refs[n_x + n_s + n_o:2 * n_x + n_s + n_o]
        ds_refs = refs[2 * n_x + n_s + n_o:]

        @pl.when(pl.program_id(0) == 0)
        def _():
            for ds in ds_refs:
                ds[...] = jnp.zeros(ds.shape, F32)

        states = [k[...] for k in k_refs]
        inputs = [_pieces(x, *p) for x, p in zip(x_refs, piece)]
        _, vjp = jax.vjp(step, states, inputs)
        d_states, d_inputs = vjp(([ds[...] for ds in ds_refs], [do[...] for do in do_refs]))
        for ds, v in zip(ds_refs, d_states):
            ds[...] = v
        for dx, v, p in zip(dx_refs, d_inputs, piece):
            _store_pieces(dx, v, *p)

    rev3 = lambda n: (n_chunks - 1 - n, 0, 0)
    rev2 = lambda n: (n_chunks - 1 - n, 0)
    return pl.pallas_call(
        body, name=name, grid=(n_chunks,),
        in_specs=[_x_spec(x, n_chunks, True) for x in xs]
        + [pl.BlockSpec((None,) + shp, rev3) for shp in state_shapes]
        + [pl.BlockSpec((CHUNK, d.shape[1]), rev2) for d in d_outs],
        out_specs=[pl.BlockSpec((CHUNK, x.width), rev2) for x in xs],
        out_shape=[jax.ShapeDtypeStruct((n_rows, x.width), F32) for x in xs],
        scratch_shapes=[pltpu.VMEM(shp, F32) for shp in state_shapes],
        compiler_params=_params(dimension_semantics=("arbitrary",)),
    )(*[x.arr for x in xs], *kept, *d_outs)


def _tri(n, strict=False):
    r = lax.broadcasted_iota(jnp.int32, (n, n), 0)
    c = lax.broadcasted_iota(jnp.int32, (n, n), 1)
    return (r > c) if strict else (r >= c)


def hgrn2_step(states, inputs):
    q_all, k_all, v_all, lf_all = inputs
    n_sub = CHUNK // SUB
    low = _tri(SUB).astype(F32)
    ones_sub = jnp.ones((SUB, SUB), F32)
    ones_chunk = jnp.ones((CHUNK, HEAD_W), F32)
    new_states, outs = [], []
    for h in range(HEADS):
        state = states[h]
        q, k, v, lf = q_all[h], k_all[h], v_all[h], lf_all[h]
        cum = [hi_nn(low, lf[i]) for i in range(n_sub)]
        tot = [hi_nn(ones_sub, lf[i]) for i in range(n_sub)]
        start = [jnp.zeros((SUB, HEAD_W), F32)]
        for i in range(n_sub):
            start.append(start[-1] + tot[i])
        q_in = [q[i] * jnp.exp(cum[i]) for i in range(n_sub)]
        intra = []
        for i in range(n_sub):
            keys = [k[j] * jnp.exp(start[i] - start[j] - cum[j]) for j in range(i)]
            keys.append(k[i] * jnp.exp(jnp.minimum(-cum[i], 80.0)))
            att = mid_nt(q_in[i], jnp.concatenate(keys, axis=0))
            r_id = lax.broadcasted_iota(jnp.int32, att.shape, 0)
            c_id = lax.broadcasted_iota(jnp.int32, att.shape, 1)
            att = jnp.where(c_id - SUB * i <= r_id, att, 0.0)
            intra.append(mm_nn(att, jnp.concatenate(v[:i + 1], axis=0)))
        q_state = jnp.concatenate([q_in[i] * jnp.exp(start[i]) for i in range(n_sub)], axis=0)
        out = mm_nn(q_state, state) + jnp.concatenate(intra, axis=0)
        k_end = jnp.concatenate([k[j] * jnp.exp(start[n_sub] - start[j] - cum[j]) for j in range(n_sub)], axis=0)
        decay = jnp.exp(hi_tn(jnp.concatenate(lf, axis=0), ones_chunk))
        new_states.append(decay * state + mm_tn(k_end, jnp.concatenate(v, axis=0)))
        outs.append(out)
    return new_states, [jnp.concatenate(outs, axis=1)]


def mlstm_step(states, inputs):
    q_all, k_all, v_all, gates = inputs
    gates = gates[0][0]
    c_st, n_st, m_st = states[:HEADS], states[HEADS:2 * HEADS], states[2 * HEADS:]
    lane = lax.broadcasted_iota(jnp.int32, (CHUNK, 128), 1)
    low = _tri(CHUNK).astype(F32)
    causal = _tri(CHUNK)
    gates_cum = hi_nn(low, gates)
    new_c, new_n, new_m, outs = [], [], [], []
    for h in range(HEADS):
        q, k, v = q_all[h][0], k_all[h][0], v_all[h][0]
        pick_i = (lane == h).astype(F32)
        pick_f = (lane == HEADS + h).astype(F32)
        li_col = jnp.sum(gates * pick_i, axis=1, keepdims=True)
        lf_col = jnp.sum(gates * pick_f, axis=1, keepdims=True)
        b_col = jnp.sum(gates_cum * pick_f, axis=1, keepdims=True)
        by_key = hi_nt(pick_i, gates) - hi_nt(pick_f, gates_cum)
        d_mat = jnp.where(causal, b_col + by_key, NEG)
        m_prev = lax.stop_gradient(jnp.max(m_st[h], axis=1, keepdims=True))
        g_inter = b_col + m_prev
        m_t = lax.stop_gradient(jnp.maximum(g_inter, jnp.max(d_mat, axis=1, keepdims=True)))
        w_inter = jnp.exp(g_inter - m_t)
        aw = jnp.exp(d_mat - m_t) * mm_nt(q, k)
        num = w_inter * mm_nn(q, c_st[h]) + mm_nn(aw, v)
        den = w_inter * row_dot(q, n_st[h]) + jnp.sum(aw, axis=1, keepdims=True)
        outs.append(num / jnp.maximum(jnp.abs(den), jnp.exp(-m_t)))
        b_end = jnp.sum(lf_col, axis=0, keepdims=True)
        g_state = b_end + m_prev
        s_w = b_end - b_col + li_col
        m_next = lax.stop_gradient(jnp.maximum(g_state, jnp.max(s_w, axis=0, keepdims=True)))
        dec = jnp.exp(g_state - m_next)
        w_s = jnp.exp(s_w - m_next)
        kw = k * w_s
        new_c.append(dec * c_st[h] + mm_tn(kw, v))
        new_n.append(dec * n_st[h] + col_dot(w_s, k))
        new_m.append(jnp.broadcast_to(m_next, (1, HEAD_W)))
    return new_c + new_n + new_m, [jnp.concatenate(outs, axis=1)]


def lru_fwd(a, u, *, tile, name):
    n_rows, width = a.shape

    def body(a_ref, u_ref, h_ref, carry):
        @pl.when(pl.program_id(0) == 0)
        def _():
            carry[...] = jnp.zeros(carry.shape, F32)

        h = carry[...]
        for t in range(tile):
            h = a_ref[t:t + 1, :] * h + u_ref[t:t + 1, :]
            h_ref[t:t + 1, :] = h
        carry[...] = h

    spec = pl.BlockSpec((tile, width), lambda i: (i, 0))
    return pl.pallas_call(
        body, name=name, grid=(n_rows // tile,), in_specs=[spec, spec], out_specs=spec,
        out_shape=jax.ShapeDtypeStruct((n_rows, width), F32),
        scratch_shapes=[pltpu.VMEM((1, width), F32)],
        compiler_params=_params(dimension_semantics=("arbitrary",)),
    )(a, u)


def lru_bwd(a, h, dh, *, tile, name):
    n_rows, width = a.shape
    n_tiles = n_rows // tile
    hb = tile // 8

    def body(a_ref, h_ref, hh_ref, dh_ref, da_ref, du_ref, carry):
        i = pl.program_id(0)

        @pl.when(i == 0)
        def _():
            carry[...] = jnp.zeros(carry.shape, F32)

        c = carry[...]
        for t in range(tile - 1, -1, -1):
            g = dh_ref[t:t + 1, :] + c
            du_ref[t:t + 1, :] = g
            if t:
                h_prev = h_ref[t - 1:t, :]
            else:
                h_prev = jnp.where(i == n_tiles - 1, 0.0, hh_ref[7:8, :])
            da_ref[t:t + 1, :] = g * h_prev
            c = a_ref[t:t + 1, :] * g
        carry[...] = c

    rev = lambda i: (n_tiles - 1 - i, 0)
    spec = pl.BlockSpec((tile, width), rev)
    halo = pl.BlockSpec((8, width), lambda i: (jnp.maximum((n_tiles - 1 - i) * hb - 1, 0), 0))
    return pl.pallas_call(
        body, name=name, grid=(n_tiles,), in_specs=[spec, spec, halo, spec], out_specs=[spec, spec],
        out_shape=[jax.ShapeDtypeStruct((n_rows, width), F32)] * 2,
        scratch_shapes=[pltpu.VMEM((1, width), F32)],
        compiler_params=_params(dimension_semantics=("arbitrary",)),
    )(a, h, h, dh)


def _layer_norm(z, g, b):
    mu = jnp.mean(z, axis=-1, keepdims=True)
    zc = z - mu
    var = jnp.mean(zc * zc, axis=-1, keepdims=True)
    return zc * lax.rsqrt(var + 1e-5) * g + b


def pre_fn(rows, params):
    (x,), (scale, shift) = rows, params
    return [x * (1.0 + scale) + shift], []


def make_post_fn(weight, with_next):
    def fn(rows, params):
        x, y = rows
        gate, g, b = params[:3]
        xo = _layer_norm(ALPHA * x + weight * (1.0 + gate) * y, g, b)
        if with_next:
            return [xo, xo * (1.0 + params[3]) + params[4]], []
        return [xo], []

    return fn


def make_last_fn(weight):
    post = make_post_fn(weight, False)

    def fn(rows, params):
        x, y, target = rows
        err = post([x, y], params)[0][0] - target
        loss = 0.5 * jnp.sum(jnp.mean(err * err, axis=-1, keepdims=True), axis=0, keepdims=True)
        return [err * (1.0 / D_MODEL)], [jnp.broadcast_to(loss, (1, 128))]

    return fn


def mix_a_fn(rows, params):
    a_q, a_f, xconv, graw = rows
    l0, l1, l2, wq, wk, gate_b = params
    mx = jnp.maximum(jnp.maximum(l0, l1), l2)
    e0, e1, e2 = jnp.exp(l0 - mx), jnp.exp(l1 - mx), jnp.exp(l2 - mx)
    lb = e0 / (e0 + e1 + e2)
    f = lb + (1.0 - lb) * jax.nn.sigmoid(a_f)
    xc = _silu(xconv)
    q_b = mm_nn(xc, wq)
    k_b = mm_nn(xc, wk) * (HEAD_W ** -0.5)
    g = graw + gate_b
    lane = lax.broadcasted_iota(jnp.int32, g.shape, 1)
    gates = jnp.where(lane < HEADS, g, _log_sigmoid(g))
    return [_silu(a_q), 1.0 - f, jnp.log(f), xc, q_b, k_b, gates], []


def _head_norm(v, g, center):
    if center:
        v = v - jnp.mean(v, axis=-1, keepdims=True)
    return v * lax.rsqrt(jnp.mean(v * v, axis=-1, keepdims=True) + 1e-6) * g


def mix_b_fn(rows, params):
    o_a, a_g, h_b, xc, b_z = rows
    hg, mg, skip = params
    y_a = [_head_norm(o_a[h], hg[h], False) * _silu(a_g[h]) for h in range(HEADS)]
    y_b = [(_head_norm(h_b[h], mg[h], True) + skip[h] * xc[h]) * _silu(b_z[h]) for h in range(HEADS)]
    return [y_a + y_b], []


def lru_a_fn(rows, params):
    (xr,) = rows
    wa, wx, ba, bx, lam = params
    a_out, u_out = [], []
    for n in range(C_BLOCKS):
        r = jax.nn.sigmoid(mm_nt(xr[n], wa[n]) + ba[n])
        i = jax.nn.sigmoid(mm_nt(xr[n], wx[n]) + bx[n])
        log_a = -RG_C * r * _softplus(-lam[n])
        a_out.append(jnp.exp(log_a))
        u_out.append(jnp.sqrt(_neg_expm1(2.0 * log_a)) * i * xr[n])
    return [a_out, u_out], []


def lru_b_fn(rows, params):
    h, y_br = rows
    return [h * jax.nn.gelu(y_br)], []


def swiglu_epi(accs, extras):
    h1, h3 = accs
    return [h1, h3, _silu(h1) * h3]


def swiglu_bwd_epi(accs, extras):
    (da,), (h1, h3) = accs, extras
    h1, h3 = h1.astype(F32), h3.astype(F32)
    sig = jax.nn.sigmoid(h1)
    return [da * h3 * sig * (1.0 + h1 * (1.0 - sig)), da * h1 * sig]


ROW_TILE = 256


def _chunks(v, n):
    return v.reshape(n, 1, v.shape[-1] // n)


def _dense_blocks(w):
    n, b, _ = w.shape
    by_row = jnp.swapaxes(w, 1, 2).reshape(n * b, b)
    spread = jnp.dot(by_row, _column_picker(n, b).T, precision=HIGHEST)
    return spread * _block_mask(n, b)


def _column_picker(n, b):
    return jnp.asarray(np.tile(np.eye(b, dtype=np.float32), (n, 1)))


def _block_mask(n, b):
    return jnp.asarray(np.kron(np.eye(n, dtype=np.float32), np.ones((b, b), np.float32)))


def _block_diag_of(m, n, b):
    by_row = jnp.dot(m * _block_mask(n, b), _column_picker(n, b), precision=HIGHEST)
    return jnp.swapaxes(by_row.reshape(n, b, b), 1, 2)


def local_step(x, target, mod, w, get_weights, put_grads, put_small, first_after=()):
    s_len = x.shape[0]
    tile = min(ROW_TILE, s_len)
    row = lambda v: v.reshape(1, -1)
    mrow = lambda l, j, k: mod[l, 3 * j + k].reshape(1, D_MODEL)
    g = {}
    d_mod = [[None] * 9 for _ in range(DEPTH)]
    d_ln_g = [[None] * 3 for _ in range(DEPTH)]
    d_ln_b = [[None] * 3 for _ in range(DEPTH)]
    subs = [(l, j) for l in range(DEPTH) for j in range(3)]
    weight_of = lambda j: 1.0 if j == 1 else FFN_RES_W

    wq_d = _dense_blocks(w["mlstm_wq"])
    wk_d = _dense_blocks(w["mlstm_wk"])
    wa_b, wx_b = w["rglru_wa"].astype(BF16), w["rglru_wx"].astype(BF16)
    gate_b = jnp.pad(w["mlstm_gate_b"].reshape(1, 8), ((0, 0), (0, 120)))
    lb_rows = [row(w["hgrn_lb_logits"][k]) for k in range(3)]
    mix_a_params = lb_rows + [wq_d, wk_d, gate_b]
    mix_b_params = [_chunks(row(w["hgrn_norm_g"]), HEADS), _chunks(row(w["mlstm_norm_g"]), HEADS),
                    _chunks(row(w["mlstm_skip"]), HEADS)]
    lru_a_params = [wa_b, wx_b, _chunks(row(w["rglru_ba"]), C_BLOCKS), _chunks(row(w["rglru_bx"]), C_BLOCKS),
                    _chunks(row(w["rglru_lambda"]), C_BLOCKS)]
    mconv_w, mconv_b = w["mlstm_conv_w"], row(w["mlstm_conv_b"])
    rconv_w, rconv_b = w["rglru_conv_w"], row(w["rglru_conv_b"])
    hg_piece = [(HEAD_W, SUB)] * 4
    ml_piece = [(HEAD_W, CHUNK)] * 3 + [(128, CHUNK)]

    (t,), _ = rowwise(pre_fn, [Rows(x)], [mrow(0, 0, 1), mrow(0, 0, 0)], [(D_MODEL, BF16)], [],
                      tile=tile, name="pre", after=tuple(first_after))
    saved = {}
    x_in = x
    for idx, (l, j) in enumerate(subs):
        sv = {"x": x_in, "t": t}
        if j != 1:
            w1, w3, w2 = get_weights(("ffn", l, j // 2), t)
            h1, h3, act = matmul([[(t, w1)], [(t, w3)]], "nn", [BF16, BF16, BF16], tm=512, tn=1408, tk=D_MODEL,
                                 epi=swiglu_epi, name="ffn_up")
            (y,) = matmul([[(act, w2)]], "nn", [F32], tm=512, tn=1024, tk=1408, name="ffn_down")
            sv.update(h1=h1, h3=h3, act=act, big=(w1, w3, w2))
        elif l == 0:
            ab_w_in, ab_w_out = get_weights(("ab",), t)
            sv["big"] = (ab_w_in, ab_w_out)
            (proj,) = matmul([[(t, ab_w_in)]], "nn", [F32], tm=256, tn=AB_ALL, tk=D_MODEL, name="ab_in")
            xconv = conv_fwd(Rows(proj, (4, MIX_W)), mconv_w, mconv_b, tile=tile, name="mconv")
            a_rows = [Rows(proj, (0, MIX_W)), Rows(proj, (1, MIX_W)), Rows(xconv), Rows(proj, (AB_MAIN // 128, 128))]
            (q_a, k_a, lf_a, xc, q_b, k_b, gates), _ = rowwise(
                mix_a_fn, a_rows, mix_a_params, [(MIX_W, F32)] * 6 + [(128, F32)], [], tile=tile, name="mix_a")
            hg_xs = [Rows(q_a), Rows(k_a), Rows(proj, (2, MIX_W)), Rows(lf_a)]
            (o_a,), hg_kept = chunk_scan_fwd(hgrn2_step, hg_xs, hg_piece, [(HEAD_W, HEAD_W)] * HEADS, [MIX_W],
                                             name="hgrn2_fwd")
            ml_xs = [Rows(q_b), Rows(k_b), Rows(proj, (5, MIX_W)), Rows(gates)]
            ml_states = [(HEAD_W, HEAD_W)] * HEADS + [(1, HEAD_W)] * (2 * HEADS)
            (h_b,), ml_kept = chunk_scan_fwd(mlstm_step, ml_xs, ml_piece, ml_states, [MIX_W], name="mlstm_fwd")
            b_rows = [Rows(o_a, split=HEAD_W), Rows(proj, (3, MIX_W), HEAD_W), Rows(h_b, split=HEAD_W),
                      Rows(xc, split=HEAD_W), Rows(proj, (6, MIX_W), HEAD_W)]
            (ycat,), _ = rowwise(mix_b_fn, b_rows, mix_b_params, [(2 * MIX_W, BF16)], [], tile=tile, name="mix_b")
            (y,) = matmul([[(ycat, ab_w_out)]], "nn", [F32], tm=512, tn=1024, tk=D_MODEL, name="ab_out")
            sv.update(proj=proj, xconv=xconv, a_rows=a_rows, hg_xs=hg_xs, hg_kept=hg_kept, ml_xs=ml_xs,
                      ml_kept=ml_kept, b_rows=b_rows, ycat=ycat)
        else:
            rg_w_in, rg_w_out = get_weights(("rg",), t)
            sv["big"] = (rg_w_in, rg_w_out)
            (proj,) = matmul([[(t, rg_w_in)]], "nn", [F32], tm=512, tn=1024, tk=D_MODEL, name="rg_in")
            xr = conv_fwd(Rows(proj, (1, D_MODEL)), rconv_w, rconv_b, tile=tile, name="rconv")
            (a_t, u_t), _ = rowwise(lru_a_fn, [Rows(xr, split=128)], lru_a_params, [(D_MODEL, F32)] * 2, [],
                                    tile=tile, name="lru_a")
            h = lru_fwd(a_t, u_t, tile=min(128, s_len), name="lru_fwd")
            b_rows = [Rows(h), Rows(proj, (0, D_MODEL))]
            (hgate,), _ = rowwise(lru_b_fn, b_rows, [], [(D_MODEL, BF16)], [], tile=tile, name="lru_b")
            (y,) = matmul([[(hgate, rg_w_out)]], "nn", [F32], tm=512, tn=1024, tk=D_MODEL, name="rg_out")
            sv.update(proj=proj, xr=xr, a_t=a_t, h=h, b_rows=b_rows, hgate=hgate)
        sv["y"] = y
        post_params = [mrow(l, j, 2), row(w["ln_g"][l, j]), row(w["ln_b"][l, j])]
        if idx + 1 < len(subs):
            nl, nj = subs[idx + 1]
            post_params += [mrow(nl, nj, 1), mrow(nl, nj, 0)]
            (x_out, t), _ = rowwise(make_post_fn(weight_of(j), True), [Rows(x_in), Rows(y)], post_params,
                                    [(D_MODEL, F32), (D_MODEL, BF16)], [], tile=tile, name="post")
        else:
            (d_xo,), (loss_row,) = rowwise(make_last_fn(weight_of(j)), [Rows(x_in), Rows(y), Rows(target)],
                                           post_params, [(D_MODEL, F32)], [(1, 128)], tile=tile, name="post_loss")
            x_out = None
        sv["post_params"] = post_params
        saved[(l, j)] = sv
        x_in = x_out
    loss = loss_row[0, 0]

    d_t_next = None
    sent = []
    deferred = []

    def hand_over(key, grads):
        token = put_grads(key, grads)
        if token is not None:
            sent.append(token)

    for idx in range(len(subs) - 1, -1, -1):
        l, j = subs[idx]
        sv = saved[(l, j)]
        has_next = idx + 1 < len(subs)
        cots = [Rows(d_xo)] + ([Rows(d_t_next)] if has_next else [])
        want_p = [0, 1, 2] + ([3, 4] if has_next else [])
        (d_xres, d_y), d_par = rowwise_bwd(
            make_post_fn(weight_of(j), has_next), [Rows(sv["x"]), Rows(sv["y"])], cots, sv["post_params"],
            [0, 1], want_p, [F32, BF16], tile=tile, name="post_bwd", after=tuple(sent))
        sent.clear()
        d_mod[l][3 * j + 2], d_ln_g[l][j], d_ln_b[l][j] = d_par[:3]
        if has_next:
            nl, nj = subs[idx + 1]
            d_mod[nl][3 * nj + 1], d_mod[nl][3 * nj] = d_par[3:]
        t = sv["t"]
        if j != 1:
            w1, w3, w2 = sv["big"]
            d_h1, d_h3 = matmul([[(d_y, w2)]], "nt", [BF16, BF16], tm=512, tn=1408, tk=D_MODEL,
                                extras=[sv["h1"], sv["h3"]], epi=swiglu_bwd_epi, name="ffn_down_bwd")

            def ffn_weight_grads(after, ops=(sv["act"], t, d_y, d_h1, d_h3), key=("ffn", l, j // 2)):
                if after is not None:
                    ops, _ = lax.optimization_barrier((ops, after))
                act, t_in, d_out, d_1, d_3 = ops
                (g_w2,) = matmul([[(act, d_out)]], "tn", [GRAD_WIRE], tm=1408, tn=1024, tk=512, name="ffn_dw2")
                (g_w1,) = matmul([[(t_in, d_1)]], "tn", [GRAD_WIRE], tm=1024, tn=1408, tk=512, name="ffn_dw1")
                (g_w3,) = matmul([[(t_in, d_3)]], "tn", [GRAD_WIRE], tm=1024, tn=1408, tk=512, name="ffn_dw3")
                hand_over(key, [g_w1, g_w3, g_w2])

            if idx:
                ffn_weight_grads(None)
            else:
                deferred.append(ffn_weight_grads)
            (d_t,) = matmul([[(d_h1, w1), (d_h3, w3)]], "nt", [F32], tm=512, tn=1024, tk=1408, name="ffn_up_bwd")
        elif l == 0:
            ab_w_in, ab_w_out = sv["big"]
            (d_ycat,) = matmul([[(d_y, ab_w_out)]], "nt", [F32], tm=512, tn=1024, tk=D_MODEL, name="ab_out_bwd")
            (g_out,) = matmul([[(sv["ycat"], d_y)]], "tn", [GRAD_WIRE], tm=1024, tn=1024, tk=512, name="ab_dwout")
            (d_oa, d_ag, d_hb, d_xc, d_bz), (d_hg, d_mg, d_skip) = rowwise_bwd(
                mix_b_fn, sv["b_rows"], [Rows(d_ycat, split=HEAD_W)], mix_b_params, [0, 1, 2, 3, 4], [0, 1, 2],
                [F32, BF16, F32, F32, BF16], tile=tile, name="mix_b_bwd")
            g["hgrn_norm_g"], g["mlstm_norm_g"], g["mlstm_skip"] = (v.reshape(1, MIX_W) for v in (d_hg, d_mg, d_skip))
            d_qb, d_kb, d_bv, d_gates = chunk_scan_bwd(mlstm_step, sv["ml_xs"], ml_piece, sv["ml_kept"], [d_hb],
                                                       name="mlstm_bwd")
            d_qa, d_ka, d_ai, d_lf = chunk_scan_bwd(hgrn2_step, sv["hg_xs"], hg_piece, sv["hg_kept"], [d_oa],
                                                    name="hgrn2_bwd")
            a_cots = [Rows(v) for v in (d_qa, d_ka, d_lf, d_xc, d_qb, d_kb, d_gates)]
            (d_aq, d_af, d_xconv, d_graw), (d_l0, d_l1, d_l2, d_wq, d_wk, d_gb) = rowwise_bwd(
                mix_a_fn, sv["a_rows"], a_cots, mix_a_params, [0, 1, 2, 3], [0, 1, 2, 3, 4, 5],
                [BF16, BF16, F32, BF16], tile=tile, name="mix_a_bwd")
            g["hgrn_lb_logits"] = jnp.concatenate([d_l0, d_l1, d_l2], axis=0)
            g["mlstm_wq"] = _block_diag_of(d_wq, MIX_W // 4, 4)
            g["mlstm_wk"] = _block_diag_of(d_wk, MIX_W // 4, 4)
            g["mlstm_gate_b"] = d_gb[:, :8]
            d_bx, g["mlstm_conv_w"], g["mlstm_conv_b"] = conv_bwd(Rows(sv["proj"], (4, MIX_W)), d_xconv, mconv_w,
                                                                  tile=tile, name="mconv_bwd")
            d_proj = jnp.concatenate([d_aq, d_af, d_ai.astype(BF16), d_ag, d_bx, d_bv.astype(BF16), d_bz, d_graw],
                                     axis=1)
            (g_in,) = matmul([[(t, d_proj)]], "tn", [GRAD_WIRE], tm=256, tn=AB_ALL, tk=512, name="ab_dwin")
            hand_over(("ab",), [g_in, g_out])
            (d_t,) = matmul([[(d_proj, ab_w_in)]], "nt", [F32], tm=512, tn=1024, tk=AB_ALL, name="ab_in_bwd")
        else:
            rg_w_in, rg_w_out = sv["big"]
            (d_hgate,) = matmul([[(d_y, rg_w_out)]], "nt", [F32], tm=512, tn=1024, tk=D_MODEL,
                                name="rg_out_bwd")
            (g_out,) = matmul([[(sv["hgate"], d_y)]], "tn", [GRAD_WIRE], tm=1024, tn=1024, tk=512,
                                         name="rg_dwout")
            (d_h, d_ybr), _ = rowwise_bwd(lru_b_fn, sv["b_rows"], [Rows(d_hgate)], [], [0, 1], [], [F32, BF16],
                                          tile=tile, name="lru_b_bwd")
            d_a, d_u = lru_bwd(sv["a_t"], sv["h"], d_h, tile=min(128, s_len), name="lru_bwd")
            (d_xr,), (d_wa, d_wx, d_ba, d_bx_, d_lam) = rowwise_bwd(
                lru_a_fn, [Rows(sv["xr"], split=128)], [Rows(d_a, split=128), Rows(d_u, split=128)], lru_a_params,
                [0], [0, 1, 2, 3, 4], [F32], tile=tile, name="lru_a_bwd")
            g["rglru_wa"], g["rglru_wx"] = d_wa, d_wx
            g["rglru_ba"], g["rglru_bx"], g["rglru_lambda"] = (v.reshape(1, D_MODEL) for v in (d_ba, d_bx_, d_lam))
            d_xbr, g["rglru_conv_w"], g["rglru_conv_b"] = conv_bwd(Rows(sv["proj"], (1, D_MODEL)), d_xr, rconv_w,
                                                                   tile=tile, name="rconv_bwd")
            d_proj = jnp.concatenate([d_ybr, d_xbr], axis=1)
            (g_in,) = matmul([[(t, d_proj)]], "tn", [GRAD_WIRE], tm=1024, tn=1024, tk=512, name="rg_dwin")
            hand_over(("rg",), [g_in, g_out])
            (d_t,) = matmul([[(d_proj, rg_w_in)]], "nt", [F32], tm=512, tn=1024, tk=1024, name="rg_in_bwd")
        d_xo, d_t_next = d_xres, d_t

    def first_bwd(rows, params):
        x0, d_res, d_t0 = rows
        _, vjp = jax.vjp(lambda r, p: pre_fn(r, p)[0], [x0], params)
        (d_x0,), d_p = vjp([d_t0])
        return [d_res + d_x0], d_p

    (grad_x,), (d_mod[0][1], d_mod[0][0]) = rowwise(
        first_bwd, [Rows(x), Rows(d_xo), Rows(d_t_next)], [mrow(0, 0, 1), mrow(0, 0, 0)], [(D_MODEL, F32)],
        [(1, D_MODEL)] * 2, tile=tile, name="pre_bwd", after=tuple(sent))
    sent.clear()
    g["ln_g"] = jnp.stack([jnp.concatenate(r, axis=0) for r in d_ln_g])
    g["ln_b"] = jnp.stack([jnp.concatenate(r, axis=0) for r in d_ln_b])
    d_mod = jnp.stack([jnp.concatenate(r, axis=0) for r in d_mod])
    small_sent = put_small(g, d_mod)
    for weight_grads in deferred:
        weight_grads(small_sent)
    return loss, grad_x, d_mod, g, list(sent)


MESH_ID = pl.DeviceIdType.MESH
ANY_SPEC = pl.BlockSpec(memory_space=pl.ANY)


def _my_position():
    return lax.axis_index("x"), lax.axis_index("y"), lax.axis_index("c")


def _flat_index(pos):
    return 4 * pos[0] + 2 * pos[1] + pos[2]


def _peer_position(pos, k):
    return tuple(lax.rem(p + ((k >> s) & 1), 2) for p, s in zip(pos, (2, 1, 0)))


def _exchange(x, gather, name):
    out_shape = (NDEV,) + x.shape if gather else x.shape

    def body(x_ref, o_ref, send_sems, recv_sems, local_sem):
        pos = _my_position()
        me = _flat_index(pos)
        local = pltpu.make_async_copy(x_ref if gather else x_ref.at[me], o_ref.at[me], local_sem)
        local.start()
        copies = []
        for k in range(1, NDEV):
            peer = _peer_position(pos, k)
            src = x_ref if gather else x_ref.at[_flat_index(peer)]
            copies.append(pltpu.make_async_remote_copy(
                src_ref=src, dst_ref=o_ref.at[me], send_sem=send_sems.at[k - 1], recv_sem=recv_sems.at[k - 1],
                device_id=peer, device_id_type=MESH_ID))
            copies[-1].start()
        for cp in copies:
            cp.wait()
        local.wait()

    return pl.pallas_call(
        body, name=name, in_specs=[ANY_SPEC], out_specs=ANY_SPEC,
        out_shape=jax.ShapeDtypeStruct(out_shape, x.dtype),
        scratch_shapes=[pltpu.SemaphoreType.DMA((NDEV - 1,)), pltpu.SemaphoreType.DMA((NDEV - 1,)),
                        pltpu.SemaphoreType.DMA],
    )(x)


HBM_SPEC = pl.BlockSpec(memory_space=pltpu.HBM)
SEM_SPEC = pl.BlockSpec(memory_space=pltpu.SEMAPHORE)
SIDE_EFFECT = pltpu.SideEffectType.DATAFLOW_SIDE_EFFECTING


def _exchange_copies(x_refs, land_refs, send_sems, recv_sems, gather):
    pos = _my_position()
    me = _flat_index(pos)
    copies = []
    for k in range(1, NDEV):
        peer = _peer_position(pos, k)
        for x_ref, land_ref, s_sem, r_sem in zip(x_refs, land_refs, send_sems, recv_sems):
            src = x_ref if gather else x_ref.at[_flat_index(peer)]
            copies.append(pltpu.make_async_remote_copy(src_ref=src, dst_ref=land_ref.at[me], send_sem=s_sem,
                                                       recv_sem=r_sem, device_id=peer, device_id_type=MESH_ID))
    return copies


def exchange_start(xs, gather, name):
    n = len(xs)
    land_shapes = [(NDEV,) + x.shape if gather else x.shape for x in xs]

    def body(*refs):
        x_refs, land_refs = refs[:n], refs[n:2 * n]
        send_sems, recv_sems = refs[2 * n:3 * n], refs[3 * n:4 * n]
        token = refs[-1]
        for cp in _exchange_copies(x_refs, land_refs, send_sems, recv_sems, gather):
            cp.start()
        token[...] = jnp.zeros(token.shape, token.dtype)

    sem = pltpu.SemaphoreType.DMA(())
    res = pl.pallas_call(
        body, name=name,
        out_shape=[sem] * (2 * n) + [pltpu.HBM(x.shape, x.dtype) for x in xs]
        + [pltpu.HBM(s, x.dtype) for s, x in zip(land_shapes, xs)] + [jax.ShapeDtypeStruct((8, 128), F32)],
        in_specs=[HBM_SPEC] * (2 * n),
        out_specs=[SEM_SPEC] * (2 * n) + [HBM_SPEC] * (2 * n) + [pl.BlockSpec(memory_space=pltpu.VMEM)],
        input_output_aliases={i: 2 * n + i for i in range(2 * n)},
        compiler_params=pltpu.CompilerParams(has_side_effects=SIDE_EFFECT),
    )(*[pltpu.with_memory_space_constraint(x, pltpu.HBM) for x in xs],
      *[pltpu.with_memory_space_constraint(lax.empty(s, x.dtype), pltpu.HBM) for s, x in zip(land_shapes, xs)])
    return (res[:n], res[n:2 * n], res[2 * n:3 * n], res[3 * n:4 * n]), res[-1]


def exchange_wait(handles, after, name):
    send_sems, recv_sems, x_thru, land_thru = handles
    n = len(x_thru)
    after = jax.tree_util.tree_leaves(after)

    def body(*refs):
        land_refs = refs[n:2 * n]
        s_sems, r_sems = refs[2 * n:3 * n], refs[3 * n:4 * n]
        pos = _my_position()
        for land_ref, s_sem, r_sem in zip(land_refs, s_sems, r_sems):
            seven = land_ref.at[pl.ds(0, NDEV - 1)]
            all_seven = pltpu.make_async_remote_copy(src_ref=seven, dst_ref=seven, send_sem=s_sem, recv_sem=r_sem,
                                                     device_id=pos, device_id_type=MESH_ID)
            all_seven.wait_send()
            all_seven.wait_recv()

    res = pl.pallas_call(
        body, name=name,
        out_shape=[pltpu.HBM(x.shape, x.dtype) for x in x_thru] + [pltpu.HBM(x.shape, x.dtype) for x in land_thru],
        in_specs=[HBM_SPEC] * (2 * n) + [SEM_SPEC] * (2 * n) + [ANY_SPEC] * len(after),
        out_specs=[HBM_SPEC] * (2 * n),
        input_output_aliases={i: i for i in range(2 * n)},
        compiler_params=pltpu.CompilerParams(has_side_effects=SIDE_EFFECT),
    )(*x_thru, *land_thru, *send_sems, *recv_sems, *after)
    return res[:n], res[n:]


def all_gather(x, name):
    return _exchange(x, True, name)


def all_to_all(x, name):
    return _exchange(x, False, name)


def _row_tile(n_rows, cap):
    best = None
    for t in range(8, min(n_rows, cap) + 1, 8):
        if n_rows % t == 0:
            best = t
    return best if best else n_rows


def adamw(w, m, v, slots, *, name, index=(), prev=None):
    n_rows, width = w.shape[-2:]
    n_lead = w.ndim - 2
    assert len(index) == n_lead
    n_slots = slots.shape[0]
    lanes = -(-width // 128) * 128
    tile = _row_tile(n_rows, max(8, (1 << 20) // (4 * lanes) // 8 * 8))
    bc1 = 1.0 - ADAM_B1 ** ADAM_STEP
    bc2 = 1.0 - ADAM_B2 ** ADAM_STEP

    def body(w_ref, m_ref, v_ref, s_ref, *rest):
        g_ref, d_ref, nm_ref, nv_ref = rest[-4:]
        g = s_ref[0].astype(F32)
        for k in range(1, n_slots):
            g = g + s_ref[k].astype(F32)
        wv = w_ref[...]
        nm = ADAM_B1 * m_ref[...] + (1.0 - ADAM_B1) * g
        nv = ADAM_B2 * v_ref[...] + (1.0 - ADAM_B2) * (g * g)
        g_ref[...] = g
        nm_ref[...] = nm
        nv_ref[...] = nv
        d_ref[...] = -ADAM_LR * ((nm / bc1) / (jnp.sqrt(nv / bc2) + ADAM_EPS) + ADAM_WD * wv)

    spec = pl.BlockSpec((None,) * n_lead + (tile, width), lambda i: tuple(index) + (i, 0))
    prev = list(prev) if prev is not None else []
    return pl.pallas_call(
        body, name=name, grid=(n_rows // tile,),
        in_specs=[spec, spec, spec, pl.BlockSpec((n_slots, tile, width), lambda i: (0, i, 0))]
        + [ANY_SPEC] * len(prev),
        out_specs=[spec] * 4, out_shape=[jax.ShapeDtypeStruct(w.shape, F32)] * 4,
        input_output_aliases={4 + k: k for k in range(len(prev))},
        compiler_params=_params(dimension_semantics=("parallel",)),
    )(w, m, v, slots, *prev)


def sum_slots(slots, *, name):
    n_slots, n_rows, width = slots.shape

    def body(s_ref, o_ref):
        @pl.when(pl.program_id(0) == 0)
        def _():
            o_ref[...] = s_ref[...]

        @pl.when(pl.program_id(0) > 0)
        def _():
            o_ref[...] += s_ref[...]

    return pl.pallas_call(
        body, name=name, grid=(n_slots,),
        in_specs=[pl.BlockSpec((None, n_rows, width), lambda k: (k, 0, 0))],
        out_specs=pl.BlockSpec((n_rows, width), lambda k: (0, 0)),
        out_shape=jax.ShapeDtypeStruct((n_rows, width), F32),
        compiler_params=_params(dimension_semantics=("arbitrary",)),
    )(slots)


def adamw_nd(w, m, v, slots, *, name):
    shp = w.shape
    two = (-1, shp[-1])
    res = adamw(w.reshape(two), m.reshape(two), v.reshape(two), slots.reshape((slots.shape[0],) + (w.size // shp[-1], shp[-1])),
                name=name)
    return [r.reshape(shp) for r in res]


def _pack(arrs):
    parts = []
    for a in arrs:
        flat = a.reshape(-1).astype(F32)
        parts.append(jnp.pad(flat, (0, (-flat.shape[0]) % 1024)))
    return jnp.concatenate(parts).reshape(-1, 128)


def _unpack(buf, shapes):
    outs, at = [], 0
    flat = buf.reshape(-1)
    for shp in shapes:
        n = int(np.prod(shp))
        outs.append(flat[at:at + n].reshape(shp))
        at += n + (-n) % 1024
    return outs


ARG_NAMES = ["x", "c", "ada_w", "ada_b", "ln_g", "ln_b", "ffn_w1", "ffn_w3", "ffn_w2", "hgrn_lb_logits", "ab_w_in",
             "ab_w_out", "hgrn_norm_g", "mlstm_conv_w", "mlstm_conv_b", "mlstm_wq", "mlstm_wk", "mlstm_gate_b",
             "mlstm_skip", "mlstm_norm_g", "rglru_w_in", "rglru_conv_w", "rglru_conv_b", "rglru_wa", "rglru_ba",
             "rglru_wx", "rglru_bx", "rglru_lambda", "rglru_w_out", "loss_target"]
WEIGHTS = ARG_NAMES[2:-1]
BIG = ["ffn_w1", "ffn_w3", "ffn_w2", "ab_w_in", "ab_w_out", "rglru_w_in", "rglru_w_out"]
REPLICATED = ["ada_b", "hgrn_lb_logits", "hgrn_norm_g", "mlstm_conv_b", "mlstm_wq", "mlstm_wk", "mlstm_gate_b",
              "mlstm_skip", "mlstm_norm_g", "rglru_wa", "rglru_wx"]
SHARDED_SMALL = ["ln_g", "ln_b", "mlstm_conv_w", "rglru_conv_w", "rglru_conv_b", "rglru_ba", "rglru_bx", "rglru_lambda"]


def _unshard_last(gathered):
    moved = jnp.moveaxis(gathered, 0, -2)
    return moved.reshape(moved.shape[:-2] + (NDEV * moved.shape[-1],))


def _shard_last(full):
    split = full.reshape(full.shape[:-1] + (NDEV, full.shape[-1] // NDEV))
    return jnp.moveaxis(split, -2, 0)


def kernel(x, c, ada_w, ada_b, ln_g, ln_b, ffn_w1, ffn_w3, ffn_w2, hgrn_lb_logits, ab_w_in, ab_w_out, hgrn_norm_g, mlstm_conv_w, mlstm_conv_b, mlstm_wq, mlstm_wk, mlstm_gate_b, mlstm_skip, mlstm_norm_g, rglru_w_in, rglru_conv_w, rglru_conv_b, rglru_wa, rglru_ba, rglru_wx, rglru_bx, rglru_lambda, rglru_w_out, loss_target, m_ada_w, m_ada_b, m_ln_g, m_ln_b, m_ffn_w1, m_ffn_w3, m_ffn_w2, m_hgrn_lb_logits, m_ab_w_in, m_ab_w_out, m_hgrn_norm_g, m_mlstm_conv_w, m_mlstm_conv_b, m_mlstm_wq, m_mlstm_wk, m_mlstm_gate_b, m_mlstm_skip, m_mlstm_norm_g, m_rglru_w_in, m_rglru_conv_w, m_rglru_conv_b, m_rglru_wa, m_rglru_ba, m_rglru_wx, m_rglru_bx, m_rglru_lambda, m_rglru_w_out, v_ada_w, v_ada_b, v_ln_g, v_ln_b, v_ffn_w1, v_ffn_w3, v_ffn_w2, v_hgrn_lb_logits, v_ab_w_in, v_ab_w_out, v_hgrn_norm_g, v_mlstm_conv_w, v_mlstm_conv_b, v_mlstm_wq, v_mlstm_wk, v_mlstm_gate_b, v_mlstm_skip, v_mlstm_norm_g, v_rglru_w_in, v_rglru_conv_w, v_rglru_conv_b, v_rglru_wa, v_rglru_ba, v_rglru_wx, v_rglru_bx, v_rglru_lambda, v_rglru_w_out):
    args = locals()
    p = {n: args[n] for n in ARG_NAMES}
    mom = {n: (args["m_" + n], args["v_" + n]) for n in WEIGHTS}
    me = _flat_index(_my_position())

    keys = [("ffn", 0, 0), ("ab",), ("ffn", 0, 1), ("ffn", 1, 0), ("rg",), ("ffn", 1, 1)]
    names = {("ab",): ("ab_w_in", "ab_w_out"), ("rg",): ("rglru_w_in", "rglru_w_out")}
    for l in range(DEPTH):
        for i in range(2):
            names[("ffn", l, i)] = ("ffn_w1", "ffn_w3", "ffn_w2")

    def part(key):
        return (lambda a: a[key[1], key[2]]) if key[0] == "ffn" else (lambda a: a[0])

    gather_handles = {}

    def landed(handles, own_of, after, name):
        sources, lands = exchange_wait(handles, after, name)
        return [lax.dynamic_update_index_in_dim(ld, own_of(src), me, 0) for src, ld in zip(sources, lands)]

    starts_next = {("ffn", 0, 0): [("ab",)], ("ab",): [("ffn", 0, 1), ("ffn", 1, 0)],
                   ("ffn", 0, 1): [("rg",), ("ffn", 1, 1)]}

    def start_gather(key, after):
        shards = [part(key)(big[n]).astype(BF16) for n in names[key]]
        shards, _ = lax.optimization_barrier((shards, after))
        gather_handles[key], token = exchange_start(shards, True, "gather_start_" + "_".join(map(str, key)))
        return token

    def get_weights(key, after):
        got = landed(gather_handles[key], lambda src: src, after, "gather_wait_" + "_".join(map(str, key)))
        tokens = [start_gather(nxt, got) for nxt in starts_next.get(key, [])]
        if tokens:
            got, _ = lax.optimization_barrier((got, tokens))
        if key[0] == "ffn":
            return _unshard_last(got[0]), _unshard_last(got[1]), got[2].reshape(D_FF, D_MODEL)
        w_in = _unshard_last(got[0])
        if key[0] == "ab":
            w_in = jnp.concatenate([w_in[:, :AB_MAIN], jnp.pad(w_in[:, AB_MAIN:], ((0, 0), (0, 120)))], axis=1)
        return w_in, got[1].reshape(D_MODEL, D_MODEL)

    scatter_handles = {}

    def put_grads(key, grads):
        if key[0] == "ffn":
            slots = [_shard_last(grads[0]), _shard_last(grads[1]), grads[2].reshape(NDEV, D_FF // NDEV, D_MODEL)]
        else:
            g_in = grads[0][:, :AB_MAIN + 8] if key[0] == "ab" else grads[0]
            slots = [_shard_last(g_in), grads[1].reshape(NDEV, D_MODEL // NDEV, D_MODEL)]
        scatter_handles[key], token = exchange_start(slots, False, "scatter_start_" + "_".join(map(str, key)))
        return token

    sharded_shapes = [p[n].shape for n in SHARDED_SMALL]
    small = all_gather(_pack([p[n] for n in SHARDED_SMALL] + [c]), "gather_small")
    per_dev = [_unpack(small[d], sharded_shapes + [c.shape]) for d in range(NDEV)]
    full_small = {n: _unshard_last(jnp.stack([per_dev[d][i] for d in range(NDEV)]))
                  for i, n in enumerate(SHARDED_SMALL)}
    c_all = jnp.concatenate([per_dev[d][-1] for d in range(NDEV)], axis=0)

    c16 = jnp.pad(c_all, ((0, 8), (0, 0)))
    (c_act,), _ = rowwise(lambda r, q: ([_silu(r[0])], []), [Rows(c16)], [], [(D_MODEL, BF16)], [], tile=16,
                          name="cond_act")
    n_ada = ada_w.shape[-1]
    ada_b_mine = lax.dynamic_slice_in_dim(ada_b, me * n_ada, n_ada, axis=1)
    ada_cols = []
    for l in range(DEPTH):
        bias = jnp.broadcast_to(ada_b_mine[l][None, :], (16, n_ada))
        (cols,) = matmul([[(c_act, ada_w[l])]], "nn", [F32], tm=16, tn=n_ada, tk=D_MODEL, extras=[bias],
                         epi=lambda accs, ex: [accs[0] + ex[0]], name="ada_fwd")
        ada_cols.append(cols[:8])
    ada_mine = all_to_all(jnp.stack(ada_cols, axis=1), "ada_to_owner")
    mod = jnp.moveaxis(ada_mine, 0, 1).reshape(DEPTH, 9, D_MODEL)

    big = {n: p[n] for n in BIG}
    started = [start_gather(keys[0], mod)]

    w = {"ln_g": full_small["ln_g"], "ln_b": full_small["ln_b"], "hgrn_lb_logits": hgrn_lb_logits}
    for n in ("hgrn_norm_g", "mlstm_conv_b", "mlstm_wq", "mlstm_wk", "mlstm_gate_b", "mlstm_skip", "mlstm_norm_g",
              "rglru_wa", "rglru_wx"):
        w[n] = p[n][0]
    for n in ("mlstm_conv_w", "rglru_conv_w", "rglru_conv_b", "rglru_ba", "rglru_bx", "rglru_lambda"):
        w[n] = full_small[n][0]

    small_names = REPLICATED + SHARDED_SMALL
    small_handles = []

    def put_small(g_small_parts, d_modulation):
        parts_ = dict(g_small_parts, ada_b=d_modulation.reshape(DEPTH, 9 * D_MODEL))
        handles, token = exchange_start([_pack([parts_[n] for n in small_names])], True, "gather_start_small_grads")
        small_handles.append(handles)
        return token

    loss, grad_x, d_mod, g, last_sent = local_step(x[0], loss_target[0], mod, w, get_weights, put_grads, put_small,
                                                   started)
    loss = lax.psum(loss, ("x", "y", "c"))

    outs = {}
    def update_group(key, after):
        slots = landed(scatter_handles[key], lambda src: lax.dynamic_index_in_dim(src, me, 0, keepdims=False), after,
                       "scatter_wait_" + "_".join(map(str, key)))
        for n, sl in zip(names[key], slots):
            index = key[1:] if key[0] == "ffn" else (0,)
            outs[n] = adamw(p[n], mom[n][0], mom[n][1], sl, name="adamw_" + n, index=index, prev=outs.get(n))
        return outs[names[key][-1]][0]

    full_shapes = [p[n].shape for n in REPLICATED] + [full_small[n].shape for n in SHARDED_SMALL]
    (all_small,) = landed(small_handles[0], lambda src: src, (last_sent, grad_x), "gather_wait_small_grads")
    summed = sum_slots(all_small, name="sum_small_grads")
    g_small = dict(zip(small_names, _unpack(summed, full_shapes)))
    rep = adamw(*[_pack([t[n] for n in REPLICATED]) for t in (p, {n: mom[n][0] for n in WEIGHTS},
                                                               {n: mom[n][1] for n in WEIGHTS})],
                _pack([g_small[n] for n in REPLICATED])[None], name="adamw_replicated")
    rep = [_unpack(r, [p[n].shape for n in REPLICATED]) for r in rep]
    for i, n in enumerate(REPLICATED):
        outs[n] = [r[i] for r in rep]
    g_mine = {n: lax.dynamic_slice_in_dim(g_small[n], me * p[n].shape[-1], p[n].shape[-1], axis=-1)
              for n in SHARDED_SMALL}
    shd = adamw(*[_pack([t[n] for n in SHARDED_SMALL]) for t in (p, {n: mom[n][0] for n in WEIGHTS},
                                                                  {n: mom[n][1] for n in WEIGHTS})],
                _pack([g_mine[n] for n in SHARDED_SMALL])[None], name="adamw_sharded_small")
    done = shd[0]
    for key in keys[:0:-1] + keys[:1]:
        done = update_group(key, done)
    shd = [_unpack(r, sharded_shapes) for r in shd]
    for i, n in enumerate(SHARDED_SMALL):
        outs[n] = [r[i] for r in shd]

    d_ada = all_small[:, :DEPTH * 9 * D_MODEL // 128].reshape(NDEV, DEPTH, 9 * D_MODEL)
    d_mine = lax.dynamic_slice_in_dim(d_ada, me * n_ada, n_ada, axis=2)
    g_ada = []
    for l in range(DEPTH):
        d16 = jnp.pad(d_mine[:, l], ((0, 8), (0, 0)))
        (gl,) = matmul([[(c_act, d16)]], "tn", [F32], tm=D_MODEL, tn=n_ada, tk=16, name="ada_bwd")
        g_ada.append(gl)
    outs["ada_w"] = adamw_nd(ada_w, *mom["ada_w"], jnp.stack(g_ada)[None], name="adamw_ada_w")

    result = [loss, grad_x[None]]
    for k in range(4):
        result += [outs[n][k].reshape(p[n].shape) for n in WEIGHTS]
    return tuple(result)
```

```python
import functools

import jax
import jax.numpy as jnp
import numpy as np
from jax import lax
from jax.experimental import pallas as pl
from jax.experimental.pallas import tpu as pltpu

F32 = jnp.float32
BF16 = jnp.bfloat16
HIGHEST = lax.Precision.HIGHEST

NDEV = 8
D_MODEL = 1024
D_FF = 2816
DEPTH = 2
CHUNK = 64
SUB = 16
HEADS = 4
HEAD_W = 128
MIX_W = HEADS * HEAD_W
AB_MAIN = 7 * MIX_W
AB_ALL = AB_MAIN + 128
CONV_W = 4
C_BLOCKS = 8
RG_C = 8.0
ALPHA = (2 * DEPTH) ** 0.25
FFN_RES_W = 0.5
NEG = -1e30

ADAM_LR = 0.001
ADAM_B1 = 0.9
ADAM_B2 = 0.999
ADAM_EPS = 1e-08
ADAM_WD = 0.01
ADAM_STEP = 10

VMEM_LIMIT = 56 * 1024 * 1024
GRAD_WIRE = jnp.bfloat16

NN = ((1,), (0,))
NT = ((1,), (1,))
TN = ((0,), (0,))


def _dot(a, b, dims, precision=None):
    return lax.dot_general(a, b, (dims, ((), ())), precision=precision, preferred_element_type=F32)


def _make_mm(dims, d_lhs, d_rhs, swap_lhs, swap_rhs, prec):
    def cast(v):
        return v.astype(BF16) if prec is None else v.astype(F32)

    @jax.custom_vjp
    def mm(a, b):
        return _dot(cast(a), cast(b), dims, prec)

    def fwd(a, b):
        return mm(a, b), (a, b)

    def bwd(res, g):
        a, b = res
        g = cast(g)
        da = _dot(cast(b), g, d_lhs, prec) if swap_lhs else _dot(g, cast(b), d_lhs, prec)
        db = _dot(g, cast(a), d_rhs, prec) if swap_rhs else _dot(cast(a), g, d_rhs, prec)
        return da.astype(a.dtype), db.astype(b.dtype)

    mm.defvjp(fwd, bwd)
    return mm


def _mm_family(prec):
    return (_make_mm(NN, NT, TN, False, False, prec), _make_mm(NT, NN, TN, False, True, prec),
            _make_mm(TN, NT, NN, True, False, prec))


def _round(v):
    return v.astype(BF16).astype(F32)


@jax.custom_vjp
def row_dot(a, n):
    return jnp.sum(_round(a) * _round(n), axis=1, keepdims=True)


def _row_dot_fwd(a, n):
    return row_dot(a, n), (a, n)


def _row_dot_bwd(res, g):
    a, n = res
    g = _round(g)
    return g * _round(n), jnp.sum(g * _round(a), axis=0, keepdims=True)


row_dot.defvjp(_row_dot_fwd, _row_dot_bwd)


@jax.custom_vjp
def col_dot(s, a):
    return jnp.sum(_round(s) * _round(a), axis=0, keepdims=True)


def _col_dot_fwd(s, a):
    return col_dot(s, a), (s, a)


def _col_dot_bwd(res, g):
    s, a = res
    g = _round(g)
    return jnp.sum(g * _round(a), axis=1, keepdims=True), _round(s) * g


col_dot.defvjp(_col_dot_fwd, _col_dot_bwd)

mm_nn, mm_nt, mm_tn = _mm_family(None)
mid_nn, mid_nt, mid_tn = _mm_family(lax.Precision.HIGH)
hi_nn, hi_nt, hi_tn = _mm_family(HIGHEST)


def _silu(v):
    return v * jax.nn.sigmoid(v)


def _log_sigmoid(v):
    return jnp.minimum(v, 0.0) - jnp.log1p(jnp.exp(-jnp.abs(v)))


def _softplus(v):
    return jnp.maximum(v, 0.0) + jnp.log1p(jnp.exp(-jnp.abs(v)))


def _neg_expm1(v):
    series = -v * (1.0 + v * (0.5 + v * (1.0 / 6.0 + v * (1.0 / 24.0 + v * (1.0 / 120.0)))))
    return jnp.where(v > -0.05, series, 1.0 - jnp.exp(v))


def _params(**kw):
    return pltpu.CompilerParams(vmem_limit_bytes=VMEM_LIMIT, **kw)


class Rows:
    def __init__(self, arr, block=None, split=None):
        self.arr = arr
        self.block = block
        self.split = split

    @property
    def width(self):
        return self.block[1] if self.block else self.arr.shape[1]


def _load(ref, split):
    if ref.ndim == 3:
        return [ref[k].astype(F32) for k in range(ref.shape[0])]
    if split is None:
        return ref[...].astype(F32)
    return [ref[:, k * split:(k + 1) * split].astype(F32) for k in range(ref.shape[1] // split)]


def _store(ref, val, accumulate=False):
    if isinstance(val, (list, tuple)):
        if ref.ndim == 3:
            for k, v in enumerate(val):
                ref[k] = (ref[k] + v if accumulate else v).astype(ref.dtype)
            return
        w = ref.shape[1] // len(val)
        for k, v in enumerate(val):
            sl = slice(k * w, (k + 1) * w)
            ref[:, sl] = (ref[:, sl] + v if accumulate else v).astype(ref.dtype)
    else:
        ref[...] = (ref[...] + val if accumulate else val).astype(ref.dtype)


def rowwise(fn, rows, params, out_rows, out_sums, *, tile, name, after=()):
    n_rows = rows[0].arr.shape[0]
    n_r, n_p, n_o = len(rows), len(params), len(out_rows)
    n_in = n_r + n_p + len(after)
    splits = [r.split for r in rows]

    def body(*refs):
        r_refs, p_refs = refs[:n_r], refs[n_r:n_r + n_p]
        o_refs, s_refs = refs[n_in:n_in + n_o], refs[n_in + n_o:]
        row_out, sum_out = fn([_load(r, s) for r, s in zip(r_refs, splits)], [_load(p, None) for p in p_refs])
        for ref, val in zip(o_refs, row_out):
            _store(ref, val)
        if s_refs:
            @pl.when(pl.program_id(0) == 0)
            def _():
                for ref in s_refs:
                    ref[...] = jnp.zeros(ref.shape, ref.dtype)

            for ref, val in zip(s_refs, sum_out):
                _store(ref, val, accumulate=True)

    in_specs = []
    for r in rows:
        blk = r.block[0] if r.block else 0
        in_specs.append(pl.BlockSpec((tile, r.width), functools.partial(lambda i, b: (i, b), b=blk)))
    for p in params:
        in_specs.append(pl.BlockSpec(p.shape, functools.partial(lambda i, n: (0,) * n, n=p.ndim)))
    in_specs += [pl.BlockSpec(memory_space=pl.ANY)] * len(after)
    out_shape = [jax.ShapeDtypeStruct((n_rows, w), dt) for w, dt in out_rows]
    out_specs = [pl.BlockSpec((tile, w), lambda i: (i, 0)) for w, _ in out_rows]
    for shp in out_sums:
        out_shape.append(jax.ShapeDtypeStruct(shp, F32))
        out_specs.append(pl.BlockSpec(shp, functools.partial(lambda i, n: (0,) * n, n=len(shp))))
    res = pl.pallas_call(
        body, name=name, grid=(n_rows // tile,), in_specs=in_specs, out_specs=out_specs, out_shape=out_shape,
        compiler_params=_params(dimension_semantics=("arbitrary",)),
    )(*[r.arr for r in rows], *params, *after)
    return res[:n_o], res[n_o:]


def rowwise_bwd(fn, rows, cots, params, want_rows, want_params, out_dtypes, *, tile, name, after=()):
    n = len(rows)

    def bwd(row_vals, param_vals):
        prim, cot = row_vals[:n], row_vals[n:]
        _, vjp = jax.vjp(lambda r, p: fn(r, p)[0], prim, param_vals)
        d_rows, d_params = vjp(cot)
        return [d_rows[i] for i in want_rows], [d_params[j] for j in want_params]

    out_rows = [(rows[i].width, dt) for i, dt in zip(want_rows, out_dtypes)]
    out_sums = [params[j].shape for j in want_params]
    return rowwise(bwd, list(rows) + list(cots), params, out_rows, out_sums, tile=tile, name=name, after=after)


def matmul(groups, mode, outs, *, tm, tn, tk, extras=(), epi=None, name, n_outer=False):
    a0, b0 = groups[0][0]
    if mode == "tn":
        k_dim, m_dim = a0.shape
    else:
        m_dim, k_dim = a0.shape
    n_dim = b0.shape[0] if mode == "nt" else b0.shape[1]
    tm, tn, tk = min(tm, m_dim), min(tn, n_dim), min(tk, k_dim)
    assert m_dim % tm == 0 and n_dim % tn == 0 and k_dim % tk == 0, (name, m_dim, n_dim, k_dim)
    nk = k_dim // tk
    pairs = [p for g in groups for p in g]
    n_pairs, n_groups, n_ex, n_out = len(pairs), len(groups), len(extras), len(outs)
    dims = {"nn": NN, "nt": NT, "tn": TN}[mode]

    def body(*refs):
        ab = refs[:2 * n_pairs]
        ex = refs[2 * n_pairs:2 * n_pairs + n_ex]
        o_refs = refs[2 * n_pairs + n_ex:2 * n_pairs + n_ex + n_out]
        accs = refs[2 * n_pairs + n_ex + n_out:]

        def partial_sums():
            sums, p = [], 0
            for g in groups:
                tot = None
                for _ in g:
                    d = _dot(ab[2 * p][...].astype(BF16), ab[2 * p + 1][...].astype(BF16), dims)
                    tot = d if tot is None else tot + d
                    p += 1
                sums.append(tot)
            return sums

        def finish(vals):
            res = epi(vals, [e[...] for e in ex]) if epi else vals
            for ref, v in zip(o_refs, res):
                ref[...] = v.astype(ref.dtype)

        if nk == 1:
            finish(partial_sums())
        else:
            k = pl.program_id(2)

            @pl.when(k == 0)
            def _():
                for acc in accs:
                    acc[...] = jnp.zeros(acc.shape, F32)

            for acc, s in zip(accs, partial_sums()):
                acc[...] += s

            @pl.when(k == nk - 1)
            def _():
                finish([acc[...] for acc in accs])

    def at(pick):
        return (lambda j, i, k: pick(i, j, k)) if n_outer else pick

    if mode == "nn":
        a_spec = pl.BlockSpec((tm, tk), at(lambda i, j, k: (i, k)))
        b_spec = pl.BlockSpec((tk, tn), at(lambda i, j, k: (k, j)))
    elif mode == "nt":
        a_spec = pl.BlockSpec((tm, tk), at(lambda i, j, k: (i, k)))
        b_spec = pl.BlockSpec((tn, tk), at(lambda i, j, k: (j, k)))
    else:
        a_spec = pl.BlockSpec((tk, tm), at(lambda i, j, k: (k, i)))
        b_spec = pl.BlockSpec((tk, tn), at(lambda i, j, k: (k, j)))
    mn_spec = pl.BlockSpec((tm, tn), at(lambda i, j, k: (i, j)))
    grid = (n_dim // tn, m_dim // tm, nk) if n_outer else (m_dim // tm, n_dim // tn, nk)
    return pl.pallas_call(
        body, name=name, grid=grid,
        in_specs=[a_spec, b_spec] * n_pairs + [mn_spec] * n_ex,
        out_specs=[mn_spec] * n_out,
        out_shape=[jax.ShapeDtypeStruct((m_dim, n_dim), dt) for dt in outs],
        scratch_shapes=[pltpu.VMEM((tm, tn), F32)] * (n_groups if nk > 1 else 0),
        compiler_params=_params(dimension_semantics=("parallel", "parallel", "arbitrary")),
    )(*[x for p in pairs for x in p], *extras)


def conv_fwd(x, w, b, *, tile, name):
    n_rows, width = x.arr.shape[0], x.width
    blk = x.block[0] if x.block else 0

    def body(x_ref, halo_ref, w_ref, b_ref, y_ref, buf):
        i = pl.program_id(0)
        halo = halo_ref[...]
        buf[0:8, :] = jnp.where(i == 0, jnp.zeros_like(halo), _round(halo))
        buf[8:, :] = _round(x_ref[...])
        acc = jnp.zeros((tile, width), F32)
        for j in range(CONV_W):
            s = CONV_W - 1 - j
            acc = acc + _round(w_ref[j:j + 1, :]) * buf[8 - s:8 - s + tile, :]
        y_ref[...] = acc + b_ref[...]

    hb = tile // 8
    return pl.pallas_call(
        body, name=name, grid=(n_rows // tile,),
        in_specs=[pl.BlockSpec((tile, width), lambda i: (i, blk)),
                  pl.BlockSpec((8, width), lambda i: (jnp.maximum(i * hb - 1, 0), blk)),
                  pl.BlockSpec((CONV_W, width), lambda i: (0, 0)),
                  pl.BlockSpec((1, width), lambda i: (0, 0))],
        out_specs=pl.BlockSpec((tile, width), lambda i: (i, 0)),
        out_shape=jax.ShapeDtypeStruct((n_rows, width), F32),
        scratch_shapes=[pltpu.VMEM((tile + 8, width), F32)],
        compiler_params=_params(dimension_semantics=("arbitrary",)),
    )(x.arr, x.arr, w, b)


def conv_bwd(x, dy, w, *, tile, name):
    n_rows, width = x.arr.shape[0], x.width
    blk = x.block[0] if x.block else 0
    n_tiles = n_rows // tile

    def body(x_ref, xh_ref, dy_ref, dyh_ref, w_ref, dx_ref, dw_ref, db_ref, xbuf, dbuf):
        i = pl.program_id(0)
        xh, dyh = xh_ref[...], dyh_ref[...]
        xbuf[0:8, :] = jnp.where(i == 0, jnp.zeros_like(xh), _round(xh))
        xbuf[8:, :] = _round(x_ref[...])
        dy_t = dy_ref[...]
        dy_r = _round(dy_t)
        dbuf[0:tile, :] = dy_r
        dbuf[tile:, :] = jnp.where(i == n_tiles - 1, jnp.zeros_like(dyh), _round(dyh))

        @pl.when(i == 0)
        def _():
            dw_ref[...] = jnp.zeros(dw_ref.shape, F32)
            db_ref[...] = jnp.zeros(db_ref.shape, F32)

        acc = jnp.zeros((tile, width), F32)
        for j in range(CONV_W):
            s = CONV_W - 1 - j
            acc = acc + _round(w_ref[j:j + 1, :]) * dbuf[s:s + tile, :]
            dw_ref[j:j + 1, :] += jnp.sum(dy_r * xbuf[8 - s:8 - s + tile, :], axis=0, keepdims=True)
        dx_ref[...] = acc.astype(dx_ref.dtype)
        db_ref[...] += jnp.sum(dy_t, axis=0, keepdims=True)

    hb = tile // 8
    return pl.pallas_call(
        body, name=name, grid=(n_tiles,),
        in_specs=[pl.BlockSpec((tile, width), lambda i: (i, blk)),
                  pl.BlockSpec((8, width), lambda i: (jnp.maximum(i * hb - 1, 0), blk)),
                  pl.BlockSpec((tile, width), lambda i: (i, 0)),
                  pl.BlockSpec((8, width), lambda i: (jnp.minimum((i + 1) * hb, n_tiles * hb - 1), 0)),
                  pl.BlockSpec((CONV_W, width), lambda i: (0, 0))],
        out_specs=[pl.BlockSpec((tile, width), lambda i: (i, 0)),
                   pl.BlockSpec((CONV_W, width), lambda i: (0, 0)),
                   pl.BlockSpec((1, width), lambda i: (0, 0))],
        out_shape=[jax.ShapeDtypeStruct((n_rows, width), BF16),
                   jax.ShapeDtypeStruct((CONV_W, width), F32),
                   jax.ShapeDtypeStruct((1, width), F32)],
        scratch_shapes=[pltpu.VMEM((tile + 8, width), F32), pltpu.VMEM((tile + 8, width), F32)],
        compiler_params=_params(dimension_semantics=("arbitrary",)),
    )(x.arr, x.arr, dy, dy, w)


def _pieces(ref, col_w, row_h):
    n_c, n_r = ref.shape[1] // col_w, ref.shape[0] // row_h
    return [[ref[r * row_h:(r + 1) * row_h, c * col_w:(c + 1) * col_w].astype(F32) for r in range(n_r)]
            for c in range(n_c)]


def _store_pieces(ref, vals, col_w, row_h):
    for c, col in enumerate(vals):
        for r, v in enumerate(col):
            ref[r * row_h:(r + 1) * row_h, c * col_w:(c + 1) * col_w] = v.astype(ref.dtype)


def _x_spec(x, n_chunks, reverse):
    blk = x.block[0] if x.block else 0
    if reverse:
        return pl.BlockSpec((CHUNK, x.width), functools.partial(lambda n, b: (n_chunks - 1 - n, b), b=blk))
    return pl.BlockSpec((CHUNK, x.width), functools.partial(lambda n, b: (n, b), b=blk))


def chunk_scan_fwd(step, xs, piece, state_shapes, out_widths, *, name):
    n_rows = xs[0].arr.shape[0]
    n_chunks = n_rows // CHUNK
    n_x, n_s, n_o = len(xs), len(state_shapes), len(out_widths)

    def body(*refs):
        x_refs, o_refs = refs[:n_x], refs[n_x:n_x + n_o]
        keep_refs, st_refs = refs[n_x + n_o:n_x + n_o + n_s], refs[n_x + n_o + n_s:]

        @pl.when(pl.program_id(0) == 0)
        def _():
            for st in st_refs:
                st[...] = jnp.zeros(st.shape, F32)

        states = [st[...] for st in st_refs]
        for keep, s in zip(keep_refs, states):
            keep[...] = s
        new_states, outs = step(states, [_pieces(x, *p) for x, p in zip(x_refs, piece)])
        for st, s in zip(st_refs, new_states):
            st[...] = s
        for o, v in zip(o_refs, outs):
            o[...] = v

    out_shape = [jax.ShapeDtypeStruct((n_rows, w), F32) for w in out_widths]
    out_specs = [pl.BlockSpec((CHUNK, w), lambda n: (n, 0)) for w in out_widths]
    for shp in state_shapes:
        out_shape.append(jax.ShapeDtypeStruct((n_chunks,) + shp, F32))
        out_specs.append(pl.BlockSpec((None,) + shp, lambda n: (n, 0, 0)))
    res = pl.pallas_call(
        body, name=name, grid=(n_chunks,),
        in_specs=[_x_spec(x, n_chunks, False) for x in xs],
        out_specs=out_specs, out_shape=out_shape,
        scratch_shapes=[pltpu.VMEM(shp, F32) for shp in state_shapes],
        compiler_params=_params(dimension_semantics=("arbitrary",)),
    )(*[x.arr for x in xs])
    return res[:n_o], res[n_o:]


def chunk_scan_bwd(step, xs, piece, kept, d_outs, *, name):
    n_rows = xs[0].arr.shape[0]
    n_chunks = n_rows // CHUNK
    n_x, n_s, n_o = len(xs), len(kept), len(d_outs)
    state_shapes = [k.shape[1:] for k in kept]

    def body(*refs):
        x_refs, k_refs = refs[:n_x], refs[n_x:n_x + n_s]
        do_refs = refs[n_x + n_s:n_x + n_s + n_o]
        dx_refs = refs[n_x + n_s + n_o:2 * n_x + n_s + n_o]
        ds_refs = refs[2 * n_x + n_s + n_o:]

        @pl.when(pl.program_id(0) == 0)
        def _():
            for ds in ds_refs:
                ds[...] = jnp.zeros(ds.shape, F32)

        states = [k[...] for k in k_refs]
        inputs = [_pieces(x, *p) for x, p in zip(x_refs, piece)]
        _, vjp = jax.vjp(step, states, inputs)
        d_states, d_inputs = vjp(([ds[...] for ds in ds_refs], [do[...] for do in do_refs]))
        for ds, v in zip(ds_refs, d_states):
            ds[...] = v
        for dx, v, p in zip(dx_refs, d_inputs, piece):
            _store_pieces(dx, v, *p)

    rev3 = lambda n: (n_chunks - 1 - n, 0, 0)
    rev2 = lambda n: (n_chunks - 1 - n, 0)
    return pl.pallas_call(
        body, name=name, grid=(n_chunks,),
        in_specs=[_x_spec(x, n_chunks, True) for x in xs]
        + [pl.BlockSpec((None,) + shp, rev3) for shp in state_shapes]
        + [pl.BlockSpec((CHUNK, d.shape[1]), rev2) for d in d_outs],
        out_specs=[pl.BlockSpec((CHUNK, x.width), rev2) for x in xs],
        out_shape=[jax.ShapeDtypeStruct((n_rows, x.width), F32) for x in xs],
        scratch_shapes=[pltpu.VMEM(shp, F32) for shp in state_shapes],
        compiler_params=_params(dimension_semantics=("arbitrary",)),
    )(*[x.arr for x in xs], *kept, *d_outs)


def _tri(n, strict=False):
    r = lax.broadcasted_iota(jnp.int32, (n, n), 0)
    c = lax.broadcasted_iota(jnp.int32, (n, n), 1)
    return (r > c) if strict else (r >= c)


def hgrn2_step(states, inputs):
    q_all, k_all, v_all, lf_all = inputs
    n_sub = CHUNK // SUB
    low = _tri(SUB).astype(F32)
    ones_sub = jnp.ones((SUB, SUB), F32)
    ones_chunk = jnp.ones((CHUNK, HEAD_W), F32)
    new_states, outs = [], []
    for h in range(HEADS):
        state = states[h]
        q, k, v, lf = q_all[h], k_all[h], v_all[h], lf_all[h]
        cum = [hi_nn(low, lf[i]) for i in range(n_sub)]
        tot = [hi_nn(ones_sub, lf[i]) for i in range(n_sub)]
        start = [jnp.zeros((SUB, HEAD_W), F32)]
        for i in range(n_sub):
            start.append(start[-1] + tot[i])
        q_in = [q[i] * jnp.exp(cum[i]) for i in range(n_sub)]
        intra = []
        for i in range(n_sub):
            keys = [k[j] * jnp.exp(start[i] - start[j] - cum[j]) for j in range(i)]
            keys.append(k[i] * jnp.exp(jnp.minimum(-cum[i], 80.0)))
            att = mid_nt(q_in[i], jnp.concatenate(keys, axis=0))
            r_id = lax.broadcasted_iota(jnp.int32, att.shape, 0)
            c_id = lax.broadcasted_iota(jnp.int32, att.shape, 1)
            att = jnp.where(c_id - SUB * i <= r_id, att, 0.0)
            intra.append(mm_nn(att, jnp.concatenate(v[:i + 1], axis=0)))
        q_state = jnp.concatenate([q_in[i] * jnp.exp(start[i]) for i in range(n_sub)], axis=0)
        out = mm_nn(q_state, state) + jnp.concatenate(intra, axis=0)
        k_end = jnp.concatenate([k[j] * jnp.exp(start[n_sub] - start[j] - cum[j]) for j in range(n_sub)], axis=0)
        decay = jnp.exp(hi_tn(jnp.concatenate(lf, axis=0), ones_chunk))
        new_states.append(decay * state + mm_tn(k_end, jnp.concatenate(v, axis=0)))
        outs.append(out)
    return new_states, [jnp.concatenate(outs, axis=1)]


def mlstm_step(states, inputs):
    q_all, k_all, v_all, gates = inputs
    gates = gates[0][0]
    c_st, n_st, m_st = states[:HEADS], states[HEADS:2 * HEADS], states[2 * HEADS:]
    lane = lax.broadcasted_iota(jnp.int32, (CHUNK, 128), 1)
    low = _tri(CHUNK).astype(F32)
    causal = _tri(CHUNK)
    gates_cum = hi_nn(low, gates)
    new_c, new_n, new_m, outs = [], [], [], []
    for h in range(HEADS):
        q, k, v = q_all[h][0], k_all[h][0], v_all[h][0]
        pick_i = (lane == h).astype(F32)
        pick_f = (lane == HEADS + h).astype(F32)
        li_col = jnp.sum(gates * pick_i, axis=1, keepdims=True)
        lf_col = jnp.sum(gates * pick_f, axis=1, keepdims=True)
        b_col = jnp.sum(gates_cum * pick_f, axis=1, keepdims=True)
        by_key = hi_nt(pick_i, gates) - hi_nt(pick_f, gates_cum)
        d_mat = jnp.where(causal, b_col + by_key, NEG)
        m_prev = lax.stop_gradient(jnp.max(m_st[h], axis=1, keepdims=True))
        g_inter = b_col + m_prev
        m_t = lax.stop_gradient(jnp.maximum(g_inter, jnp.max(d_mat, axis=1, keepdims=True)))
        w_inter = jnp.exp(g_inter - m_t)
        aw = jnp.exp(d_mat - m_t) * mm_nt(q, k)
        num = w_inter * mm_nn(q, c_st[h]) + mm_nn(aw, v)
        den = w_inter * row_dot(q, n_st[h]) + jnp.sum(aw, axis=1, keepdims=True)
        outs.append(num / jnp.maximum(jnp.abs(den), jnp.exp(-m_t)))
        b_end = jnp.sum(lf_col, axis=0, keepdims=True)
        g_state = b_end + m_prev
        s_w = b_end - b_col + li_col
        m_next = lax.stop_gradient(jnp.maximum(g_state, jnp.max(s_w, axis=0, keepdims=True)))
        dec = jnp.exp(g_state - m_next)
        w_s = jnp.exp(s_w - m_next)
        kw = k * w_s
        new_c.append(dec * c_st[h] + mm_tn(kw, v))
        new_n.append(dec * n_st[h] + col_dot(w_s, k))
        new_m.append(jnp.broadcast_to(m_next, (1, HEAD_W)))
    return new_c + new_n + new_m, [jnp.concatenate(outs, axis=1)]


def lru_fwd(a, u, *, tile, name):
    n_rows, width = a.shape

    def body(a_ref, u_ref, h_ref, carry):
        @pl.when(pl.program_id(0) == 0)
        def _():
            carry[...] = jnp.zeros(carry.shape, F32)

        h = carry[...]
        for t in range(tile):
            h = a_ref[t:t + 1, :] * h + u_ref[t:t + 1, :]
            h_ref[t:t + 1, :] = h
        carry[...] = h

    spec = pl.BlockSpec((tile, width), lambda i: (i, 0))
    return pl.pallas_call(
        body, name=name, grid=(n_rows // tile,), in_specs=[spec, spec], out_specs=spec,
        out_shape=jax.ShapeDtypeStruct((n_rows, width), F32),
        scratch_shapes=[pltpu.VMEM((1, width), F32)],
        compiler_params=_params(dimension_semantics=("arbitrary",)),
    )(a, u)


def lru_bwd(a, h, dh, *, tile, name):
    n_rows, width = a.shape
    n_tiles = n_rows // tile
    hb = tile // 8

    def body(a_ref, h_ref, hh_ref, dh_ref, da_ref, du_ref, carry):
        i = pl.program_id(0)

        @pl.when(i == 0)
        def _():
            carry[...] = jnp.zeros(carry.shape, F32)

        c = carry[...]
        for t in range(tile - 1, -1, -1):
            g = dh_ref[t:t + 1, :] + c
            du_ref[t:t + 1, :] = g
            if t:
                h_prev = h_ref[t - 1:t, :]
            else:
                h_prev = jnp.where(i == n_tiles - 1, 0.0, hh_ref[7:8, :])
            da_ref[t:t + 1, :] = g * h_prev
            c = a_ref[t:t + 1, :] * g
        carry[...] = c

    rev = lambda i: (n_tiles - 1 - i, 0)
    spec = pl.BlockSpec((tile, width), rev)
    halo = pl.BlockSpec((8, width), lambda i: (jnp.maximum((n_tiles - 1 - i) * hb - 1, 0), 0))
    return pl.pallas_call(
        body, name=name, grid=(n_tiles,), in_specs=[spec, spec, halo, spec], out_specs=[spec, spec],
        out_shape=[jax.ShapeDtypeStruct((n_rows, width), F32)] * 2,
        scratch_shapes=[pltpu.VMEM((1, width), F32)],
        compiler_params=_params(dimension_semantics=("arbitrary",)),
    )(a, h, h, dh)


def _layer_norm(z, g, b):
    mu = jnp.mean(z, axis=-1, keepdims=True)
    zc = z - mu
    var = jnp.mean(zc * zc, axis=-1, keepdims=True)
    return zc * lax.rsqrt(var + 1e-5) * g + b


def pre_fn(rows, params):
    (x,), (scale, shift) = rows, params
    return [x * (1.0 + scale) + shift], []


def make_post_fn(weight, with_next):
    def fn(rows, params):
        x, y = rows
        gate, g, b = params[:3]
        xo = _layer_norm(ALPHA * x + weight * (1.0 + gate) * y, g, b)
        if with_next:
            return [xo, xo * (1.0 + params[3]) + params[4]], []
        return [xo], []

    return fn


def make_last_fn(weight):
    post = make_post_fn(weight, False)

    def fn(rows, params):
        x, y, target = rows
        err = post([x, y], params)[0][0] - target
        loss = 0.5 * jnp.sum(jnp.mean(err * err, axis=-1, keepdims=True), axis=0, keepdims=True)
        return [err * (1.0 / D_MODEL)], [jnp.broadcast_to(loss, (1, 128))]

    return fn


def mix_a_fn(rows, params):
    a_q, a_f, xconv, graw = rows
    l0, l1, l2, wq, wk, gate_b = params
    mx = jnp.maximum(jnp.maximum(l0, l1), l2)
    e0, e1, e2 = jnp.exp(l0 - mx), jnp.exp(l1 - mx), jnp.exp(l2 - mx)
    lb = e0 / (e0 + e1 + e2)
    f = lb + (1.0 - lb) * jax.nn.sigmoid(a_f)
    xc = _silu(xconv)
    q_b = mm_nn(xc, wq)
    k_b = mm_nn(xc, wk) * (HEAD_W ** -0.5)
    g = graw + gate_b
    lane = lax.broadcasted_iota(jnp.int32, g.shape, 1)
    gates = jnp.where(lane < HEADS, g, _log_sigmoid(g))
    return [_silu(a_q), 1.0 - f, jnp.log(f), xc, q_b, k_b, gates], []


def _head_norm(v, g, center):
    if center:
        v = v - jnp.mean(v, axis=-1, keepdims=True)
    return v * lax.rsqrt(jnp.mean(v * v, axis=-1, keepdims=True) + 1e-6) * g


def mix_b_fn(rows, params):
    o_a, a_g, h_b, xc, b_z = rows
    hg, mg, skip = params
    y_a = [_head_norm(o_a[h], hg[h], False) * _silu(a_g[h]) for h in range(HEADS)]
    y_b = [(_head_norm(h_b[h], mg[h], True) + skip[h] * xc[h]) * _silu(b_z[h]) for h in range(HEADS)]
    return [y_a + y_b], []


def lru_a_fn(rows, params):
    (xr,) = rows
    wa, wx, ba, bx, lam = params
    a_out, u_out = [], []
    for n in range(C_BLOCKS):
        r = jax.nn.sigmoid(mm_nt(xr[n], wa[n]) + ba[n])
        i = jax.nn.sigmoid(mm_nt(xr[n], wx[n]) + bx[n])
        log_a = -RG_C * r * _softplus(-lam[n])
        a_out.append(jnp.exp(log_a))
        u_out.append(jnp.sqrt(_neg_expm1(2.0 * log_a)) * i * xr[n])
    return [a_out, u_out], []


def lru_b_fn(rows, params):
    h, y_br = rows
    return [h * jax.nn.gelu(y_br)], []


def swiglu_epi(accs, extras):
    h1, h3 = accs
    return [h1, h3, _silu(h1) * h3]


def swiglu_bwd_epi(accs, extras):
    (da,), (h1, h3) = accs, extras
    h1, h3 = h1.astype(F32), h3.astype(F32)
    sig = jax.nn.sigmoid(h1)
    return [da * h3 * sig * (1.0 + h1 * (1.0 - sig)), da * h1 * sig]


ROW_TILE = 256


def _chunks(v, n):
    return v.reshape(n, 1, v.shape[-1] // n)


def _dense_blocks(w):
    n, b, _ = w.shape
    by_row = jnp.swapaxes(w, 1, 2).reshape(n * b, b)
    spread = jnp.dot(by_row, _column_picker(n, b).T, precision=HIGHEST)
    return spread * _block_mask(n, b)


def _column_picker(n, b):
    return jnp.asarray(np.tile(np.eye(b, dtype=np.float32), (n, 1)))


def _block_mask(n, b):
    return jnp.asarray(np.kron(np.eye(n, dtype=np.float32), np.ones((b, b), np.float32)))


def _block_diag_of(m, n, b):
    by_row = jnp.dot(m * _block_mask(n, b), _column_picker(n, b), precision=HIGHEST)
    return jnp.swapaxes(by_row.reshape(n, b, b), 1, 2)


def local_step(x, target, mod, w, get_weights, put_grads, put_small, first_after=()):
    s_len = x.shape[0]
    tile = min(ROW_TILE, s_len)
    row = lambda v: v.reshape(1, -1)
    mrow = lambda l, j, k: mod[l, 3 * j + k].reshape(1, D_MODEL)
    g = {}
    d_mod = [[None] * 9 for _ in range(DEPTH)]
    d_ln_g = [[None] * 3 for _ in range(DEPTH)]
    d_ln_b = [[None] * 3 for _ in range(DEPTH)]
    subs = [(l, j) for l in range(DEPTH) for j in range(3)]
    weight_of = lambda j: 1.0 if j == 1 else FFN_RES_W

    wq_d = _dense_blocks(w["mlstm_wq"])
    wk_d = _dense_blocks(w["mlstm_wk"])
    wa_b, wx_b = w["rglru_wa"].astype(BF16), w["rglru_wx"].astype(BF16)
    gate_b = jnp.pad(w["mlstm_gate_b"].reshape(1, 8), ((0, 0), (0, 120)))
    lb_rows = [row(w["hgrn_lb_logits"][k]) for k in range(3)]
    mix_a_params = lb_rows + [wq_d, wk_d, gate_b]
    mix_b_params = [_chunks(row(w["hgrn_norm_g"]), HEADS), _chunks(row(w["mlstm_norm_g"]), HEADS),
                    _chunks(row(w["mlstm_skip"]), HEADS)]
    lru_a_params = [wa_b, wx_b, _chunks(row(w["rglru_ba"]), C_BLOCKS), _chunks(row(w["rglru_bx"]), C_BLOCKS),
                    _chunks(row(w["rglru_lambda"]), C_BLOCKS)]
    mconv_w, mconv_b = w["mlstm_conv_w"], row(w["mlstm_conv_b"])
    rconv_w, rconv_b = w["rglru_conv_w"], row(w["rglru_conv_b"])
    hg_piece = [(HEAD_W, SUB)] * 4
    ml_piece = [(HEAD_W, CHUNK)] * 3 + [(128, CHUNK)]

    (t,), _ = rowwise(pre_fn, [Rows(x)], [mrow(0, 0, 1), mrow(0, 0, 0)], [(D_MODEL, BF16)], [],
                      tile=tile, name="pre", after=tuple(first_after))
    saved = {}
    x_in = x
    for idx, (l, j) in enumerate(subs):
        sv = {"x": x_in, "t": t}
        if j != 1:
            w1, w3, w2 = get_weights(("ffn", l, j // 2), t)
            h1, h3, act = matmul([[(t, w1)], [(t, w3)]], "nn", [BF16, BF16, BF16], tm=512, tn=1408, tk=D_MODEL,
                                 epi=swiglu_epi, name="ffn_up", n_outer=True)
            (y,) = matmul([[(act, w2)]], "nn", [F32], tm=1024, tn=1024, tk=1408, name="ffn_down")
            sv.update(h1=h1, h3=h3, act=act, big=(w1, w3, w2))
        elif l == 0:
            ab_w_in, ab_w_out = get_weights(("ab",), t)
            sv["big"] = (ab_w_in, ab_w_out)
            (proj,) = matmul([[(t, ab_w_in)]], "nn", [F32], tm=256, tn=AB_ALL, tk=D_MODEL, name="ab_in")
            xconv = conv_fwd(Rows(proj, (4, MIX_W)), mconv_w, mconv_b, tile=tile, name="mconv")
            a_rows = [Rows(proj, (0, MIX_W)), Rows(proj, (1, MIX_W)), Rows(xconv), Rows(proj, (AB_MAIN // 128, 128))]
            (q_a, k_a, lf_a, xc, q_b, k_b, gates), _ = rowwise(
                mix_a_fn, a_rows, mix_a_params, [(MIX_W, F32)] * 6 + [(128, F32)], [], tile=tile, name="mix_a")
            hg_xs = [Rows(q_a), Rows(k_a), Rows(proj, (2, MIX_W)), Rows(lf_a)]
            (o_a,), hg_kept = chunk_scan_fwd(hgrn2_step, hg_xs, hg_piece, [(HEAD_W, HEAD_W)] * HEADS, [MIX_W],
                                             name="hgrn2_fwd")
            ml_xs = [Rows(q_b), Rows(k_b), Rows(proj, (5, MIX_W)), Rows(gates)]
            ml_states = [(HEAD_W, HEAD_W)] * HEADS + [(1, HEAD_W)] * (2 * HEADS)
            (h_b,), ml_kept = chunk_scan_fwd(mlstm_step, ml_xs, ml_piece, ml_states, [MIX_W], name="mlstm_fwd")
            b_rows = [Rows(o_a, split=HEAD_W), Rows(proj, (3, MIX_W), HEAD_W), Rows(h_b, split=HEAD_W),
                      Rows(xc, split=HEAD_W), Rows(proj, (6, MIX_W), HEAD_W)]
            (ycat,), _ = rowwise(mix_b_fn, b_rows, mix_b_params, [(2 * MIX_W, BF16)], [], tile=tile, name="mix_b")
            (y,) = matmul([[(ycat, ab_w_out)]], "nn", [F32], tm=512, tn=1024, tk=D_MODEL, name="ab_out")
            sv.update(proj=proj, xconv=xconv, a_rows=a_rows, hg_xs=hg_xs, hg_kept=hg_kept, ml_xs=ml_xs,
                      ml_kept=ml_kept, b_rows=b_rows, ycat=ycat)
        else:
            rg_w_in, rg_w_out = get_weights(("rg",), t)
            sv["big"] = (rg_w_in, rg_w_out)
            (proj,) = matmul([[(t, rg_w_in)]], "nn", [F32], tm=512, tn=1024, tk=D_MODEL, name="rg_in")
            xr = conv_fwd(Rows(proj, (1, D_MODEL)), rconv_w, rconv_b, tile=tile, name="rconv")
            (a_t, u_t), _ = rowwise(lru_a_fn, [Rows(xr, split=128)], lru_a_params, [(D_MODEL, F32)] * 2, [],
                                    tile=tile, name="lru_a")
            h = lru_fwd(a_t, u_t, tile=min(128, s_len), name="lru_fwd")
            b_rows = [Rows(h), Rows(proj, (0, D_MODEL))]
            (hgate,), _ = rowwise(lru_b_fn, b_rows, [], [(D_MODEL, BF16)], [], tile=tile, name="lru_b")
            (y,) = matmul([[(hgate, rg_w_out)]], "nn", [F32], tm=512, tn=1024, tk=D_MODEL, name="rg_out")
            sv.update(proj=proj, xr=xr, a_t=a_t, h=h, b_rows=b_rows, hgate=hgate)
        sv["y"] = y
        post_params = [mrow(l, j, 2), row(w["ln_g"][l, j]), row(w["ln_b"][l, j])]
        if idx + 1 < len(subs):
            nl, nj = subs[idx + 1]
            post_params += [mrow(nl, nj, 1), mrow(nl, nj, 0)]
            (x_out, t), _ = rowwise(make_post_fn(weight_of(j), True), [Rows(x_in), Rows(y)], post_params,
                                    [(D_MODEL, F32), (D_MODEL, BF16)], [], tile=tile, name="post")
        else:
            (d_xo,), (loss_row,) = rowwise(make_last_fn(weight_of(j)), [Rows(x_in), Rows(y), Rows(target)],
                                           post_params, [(D_MODEL, F32)], [(1, 128)], tile=tile, name="post_loss")
            x_out = None
        sv["post_params"] = post_params
        saved[(l, j)] = sv
        x_in = x_out
    loss = loss_row[0, 0]

    d_t_next = None
    sent = []
    deferred = []

    def hand_over(key, grads):
        token = put_grads(key, grads)
        if token is not None:
            sent.append(token)

    for idx in range(len(subs) - 1, -1, -1):
        l, j = subs[idx]
        sv = saved[(l, j)]
        has_next = idx + 1 < len(subs)
        cots = [Rows(d_xo)] + ([Rows(d_t_next)] if has_next else [])
        want_p = [0, 1, 2] + ([3, 4] if has_next else [])
        (d_xres, d_y), d_par = rowwise_bwd(
            make_post_fn(weight_of(j), has_next), [Rows(sv["x"]), Rows(sv["y"])], cots, sv["post_params"],
            [0, 1], want_p, [F32, BF16], tile=tile, name="post_bwd", after=tuple(sent))
        sent.clear()
        d_mod[l][3 * j + 2], d_ln_g[l][j], d_ln_b[l][j] = d_par[:3]
        if has_next:
            nl, nj = subs[idx + 1]
            d_mod[nl][3 * nj + 1], d_mod[nl][3 * nj] = d_par[3:]
        t = sv["t"]
        if j != 1:
            w1, w3, w2 = sv["big"]
            d_h1, d_h3 = matmul([[(d_y, w2)]], "nt", [BF16, BF16], tm=512, tn=1408, tk=D_MODEL,
                                extras=[sv["h1"], sv["h3"]], epi=swiglu_bwd_epi, name="ffn_down_bwd", n_outer=True)

            def ffn_weight_grads(after, ops=(sv["act"], t, d_y, d_h1, d_h3), key=("ffn", l, j // 2)):
                if after is not None:
                    ops, _ = lax.optimization_barrier((ops, after))
                act, t_in, d_out, d_1, d_3 = ops
                (g_w2,) = matmul([[(act, d_out)]], "tn", [GRAD_WIRE], tm=1408, tn=1024, tk=512, name="ffn_dw2")
                (g_w1,) = matmul([[(t_in, d_1)]], "tn", [GRAD_WIRE], tm=1024, tn=1408, tk=512, name="ffn_dw1")
                (g_w3,) = matmul([[(t_in, d_3)]], "tn", [GRAD_WIRE], tm=1024, tn=1408, tk=512, name="ffn_dw3")
                hand_over(key, [g_w1, g_w3, g_w2])

            if idx:
                ffn_weight_grads(None)
            else:
                deferred.append(ffn_weight_grads)
            (d_t,) = matmul([[(d_h1, w1), (d_h3, w3)]], "nt", [F32], tm=1024, tn=1024, tk=1408, name="ffn_up_bwd")
        elif l == 0:
            ab_w_in, ab_w_out = sv["big"]
            (d_ycat,) = matmul([[(d_y, ab_w_out)]], "nt", [F32], tm=512, tn=1024, tk=D_MODEL, name="ab_out_bwd")
            (g_out,) = matmul([[(sv["ycat"], d_y)]], "tn", [GRAD_WIRE], tm=1024, tn=1024, tk=512, name="ab_dwout")
            (d_oa, d_ag, d_hb, d_xc, d_bz), (d_hg, d_mg, d_skip) = rowwise_bwd(
                mix_b_fn, sv["b_rows"], [Rows(d_ycat, split=HEAD_W)], mix_b_params, [0, 1, 2, 3, 4], [0, 1, 2],
                [F32, BF16, F32, F32, BF16], tile=tile, name="mix_b_bwd")
            g["hgrn_norm_g"], g["mlstm_norm_g"], g["mlstm_skip"] = (v.reshape(1, MIX_W) for v in (d_hg, d_mg, d_skip))
            d_qb, d_kb, d_bv, d_gates = chunk_scan_bwd(mlstm_step, sv["ml_xs"], ml_piece, sv["ml_kept"], [d_hb],
                                                       name="mlstm_bwd")
            d_qa, d_ka, d_ai, d_lf = chunk_scan_bwd(hgrn2_step, sv["hg_xs"], hg_piece, sv["hg_kept"], [d_oa],
                                                    name="hgrn2_bwd")
            a_cots = [Rows(v) for v in (d_qa, d_ka, d_lf, d_xc, d_qb, d_kb, d_gates)]
            (d_aq, d_af, d_xconv, d_graw), (d_l0, d_l1, d_l2, d_wq, d_wk, d_gb) = rowwise_bwd(
                mix_a_fn, sv["a_rows"], a_cots, mix_a_params, [0, 1, 2, 3], [0, 1, 2, 3, 4, 5],
                [BF16, BF16, F32, BF16], tile=tile, name="mix_a_bwd")
            g["hgrn_lb_logits"] = jnp.concatenate([d_l0, d_l1, d_l2], axis=0)
            g["mlstm_wq"] = _block_diag_of(d_wq, MIX_W // 4, 4)
            g["mlstm_wk"] = _block_diag_of(d_wk, MIX_W // 4, 4)
            g["mlstm_gate_b"] = d_gb[:, :8]
            d_bx, g["mlstm_conv_w"], g["mlstm_conv_b"] = conv_bwd(Rows(sv["proj"], (4, MIX_W)), d_xconv, mconv_w,
                                                                  tile=tile, name="mconv_bwd")
            d_proj = jnp.concatenate([d_aq, d_af, d_ai.astype(BF16), d_ag, d_bx, d_bv.astype(BF16), d_bz, d_graw],
                                     axis=1)
            (g_in,) = matmul([[(t, d_proj)]], "tn", [GRAD_WIRE], tm=256, tn=AB_ALL, tk=512, name="ab_dwin")
            hand_over(("ab",), [g_in, g_out])
            (d_t,) = matmul([[(d_proj, ab_w_in)]], "nt", [F32], tm=512, tn=1024, tk=AB_ALL, name="ab_in_bwd")
        else:
            rg_w_in, rg_w_out = sv["big"]
            (d_hgate,) = matmul([[(d_y, rg_w_out)]], "nt", [F32], tm=512, tn=1024, tk=D_MODEL,
                                name="rg_out_bwd")
            (g_out,) = matmul([[(sv["hgate"], d_y)]], "tn", [GRAD_WIRE], tm=1024, tn=1024, tk=512,
                                         name="rg_dwout")
            (d_h, d_ybr), _ = rowwise_bwd(lru_b_fn, sv["b_rows"], [Rows(d_hgate)], [], [0, 1], [], [F32, BF16],
                                          tile=tile, name="lru_b_bwd")
            d_a, d_u = lru_bwd(sv["a_t"], sv["h"], d_h, tile=min(128, s_len), name="lru_bwd")
            (d_xr,), (d_wa, d_wx, d_ba, d_bx_, d_lam) = rowwise_bwd(
                lru_a_fn, [Rows(sv["xr"], split=128)], [Rows(d_a, split=128), Rows(d_u, split=128)], lru_a_params,
                [0], [0, 1, 2, 3, 4], [F32], tile=tile, name="lru_a_bwd")
            g["rglru_wa"], g["rglru_wx"] = d_wa, d_wx
            g["rglru_ba"], g["rglru_bx"], g["rglru_lambda"] = (v.reshape(1, D_MODEL) for v in (d_ba, d_bx_, d_lam))
            d_xbr, g["rglru_conv_w"], g["rglru_conv_b"] = conv_bwd(Rows(sv["proj"], (1, D_MODEL)), d_xr, rconv_w,
                                                                   tile=tile, name="rconv_bwd")
            d_proj = jnp.concatenate([d_ybr, d_xbr], axis=1)
            (g_in,) = matmul([[(t, d_proj)]], "tn", [GRAD_WIRE], tm=1024, tn=1024, tk=512, name="rg_dwin")
            hand_over(("rg",), [g_in, g_out])
            (d_t,) = matmul([[(d_proj, rg_w_in)]], "nt", [F32], tm=512, tn=1024, tk=1024, name="rg_in_bwd")
        d_xo, d_t_next = d_xres, d_t

    def first_bwd(rows, params):
        x0, d_res, d_t0 = rows
        _, vjp = jax.vjp(lambda r, p: pre_fn(r, p)[0], [x0], params)
        (d_x0,), d_p = vjp([d_t0])
        return [d_res + d_x0], d_p

    (grad_x,), (d_mod[0][1], d_mod[0][0]) = rowwise(
        first_bwd, [Rows(x), Rows(d_xo), Rows(d_t_next)], [mrow(0, 0, 1), mrow(0, 0, 0)], [(D_MODEL, F32)],
        [(1, D_MODEL)] * 2, tile=tile, name="pre_bwd", after=tuple(sent))
    sent.clear()
    g["ln_g"] = jnp.stack([jnp.concatenate(r, axis=0) for r in d_ln_g])
    g["ln_b"] = jnp.stack([jnp.concatenate(r, axis=0) for r in d_ln_b])
    d_mod = jnp.stack([jnp.concatenate(r, axis=0) for r in d_mod])
    small_sent = put_small(g, d_mod)
    for weight_grads in deferred:
        weight_grads(small_sent)
    return loss, grad_x, d_mod, g, list(sent)


MESH_ID = pl.DeviceIdType.MESH
ANY_SPEC = pl.BlockSpec(memory_space=pl.ANY)


def _my_position():
    return lax.axis_index("x"), lax.axis_index("y"), lax.axis_index("c")


def _flat_index(pos):
    return 4 * pos[0] + 2 * pos[1] + pos[2]


def _peer_position(pos, k):
    return tuple(lax.rem(p + ((k >> s) & 1), 2) for p, s in zip(pos, (2, 1, 0)))


def _exchange(x, gather, name):
    out_shape = (NDEV,) + x.shape if gather else x.shape

    def body(x_ref, o_ref, send_sems, recv_sems, local_sem):
        pos = _my_position()
        me = _flat_index(pos)
        local = pltpu.make_async_copy(x_ref if gather else x_ref.at[me], o_ref.at[me], local_sem)
        local.start()
        copies = []
        for k in range(1, NDEV):
            peer = _peer_position(pos, k)
            src = x_ref if gather else x_ref.at[_flat_index(peer)]
            copies.append(pltpu.make_async_remote_copy(
                src_ref=src, dst_ref=o_ref.at[me], send_sem=send_sems.at[k - 1], recv_sem=recv_sems.at[k - 1],
                device_id=peer, device_id_type=MESH_ID))
            copies[-1].start()
        for cp in copies:
            cp.wait()
        local.wait()

    return pl.pallas_call(
        body, name=name, in_specs=[ANY_SPEC], out_specs=ANY_SPEC,
        out_shape=jax.ShapeDtypeStruct(out_shape, x.dtype),
        scratch_shapes=[pltpu.SemaphoreType.DMA((NDEV - 1,)), pltpu.SemaphoreType.DMA((NDEV - 1,)),
                        pltpu.SemaphoreType.DMA],
    )(x)


HBM_SPEC = pl.BlockSpec(memory_space=pltpu.HBM)
SEM_SPEC = pl.BlockSpec(memory_space=pltpu.SEMAPHORE)
SIDE_EFFECT = pltpu.SideEffectType.DATAFLOW_SIDE_EFFECTING


def _exchange_copies(x_refs, land_refs, send_sems, recv_sems, gather):
    pos = _my_position()
    me = _flat_index(pos)
    copies = []
    for k in range(1, NDEV):
        peer = _peer_position(pos, k)
        for x_ref, land_ref, s_sem, r_sem in zip(x_refs, land_refs, send_sems, recv_sems):
            src = x_ref if gather else x_ref.at[_flat_index(peer)]
            copies.append(pltpu.make_async_remote_copy(src_ref=src, dst_ref=land_ref.at[me], send_sem=s_sem,
                                                       recv_sem=r_sem, device_id=peer, device_id_type=MESH_ID))
    return copies


def exchange_start(xs, gather, name):
    n = len(xs)
    land_shapes = [(NDEV,) + x.shape if gather else x.shape for x in xs]

    def body(*refs):
        x_refs, land_refs = refs[:n], refs[n:2 * n]
        send_sems, recv_sems = refs[2 * n:3 * n], refs[3 * n:4 * n]
        token = refs[-1]
        for cp in _exchange_copies(x_refs, land_refs, send_sems, recv_sems, gather):
            cp.start()
        token[...] = jnp.zeros(token.shape, token.dtype)

    sem = pltpu.SemaphoreType.DMA(())
    res = pl.pallas_call(
        body, name=name,
        out_shape=[sem] * (2 * n) + [pltpu.HBM(x.shape, x.dtype) for x in xs]
        + [pltpu.HBM(s, x.dtype) for s, x in zip(land_shapes, xs)] + [jax.ShapeDtypeStruct((8, 128), F32)],
        in_specs=[HBM_SPEC] * (2 * n),
        out_specs=[SEM_SPEC] * (2 * n) + [HBM_SPEC] * (2 * n) + [pl.BlockSpec(memory_space=pltpu.VMEM)],
        input_output_aliases={i: 2 * n + i for i in range(2 * n)},
        compiler_params=pltpu.CompilerParams(has_side_effects=SIDE_EFFECT),
    )(*[pltpu.with_memory_space_constraint(x, pltpu.HBM) for x in xs],
      *[pltpu.with_memory_space_constraint(lax.empty(s, x.dtype), pltpu.HBM) for s, x in zip(land_shapes, xs)])
    return (res[:n], res[n:2 * n], res[2 * n:3 * n], res[3 * n:4 * n]), res[-1]


def exchange_wait(handles, after, name):
    send_sems, recv_sems, x_thru, land_thru = handles
    n = len(x_thru)
    after = jax.tree_util.tree_leaves(after)

    def body(*refs):
        land_refs = refs[n:2 * n]
        s_sems, r_sems = refs[2 * n:3 * n], refs[3 * n:4 * n]
        pos = _my_position()
        for land_ref, s_sem, r_sem in zip(land_refs, s_sems, r_sems):
            seven = land_ref.at[pl.ds(0, NDEV - 1)]
            all_seven = pltpu.make_async_remote_copy(src_ref=seven, dst_ref=seven, send_sem=s_sem, recv_sem=r_sem,
                                                     device_id=pos, device_id_type=MESH_ID)
            all_seven.wait_send()
            all_seven.wait_recv()

    res = pl.pallas_call(
        body, name=name,
        out_shape=[pltpu.HBM(x.shape, x.dtype) for x in x_thru] + [pltpu.HBM(x.shape, x.dtype) for x in land_thru],
        in_specs=[HBM_SPEC] * (2 * n) + [SEM_SPEC] * (2 * n) + [ANY_SPEC] * len(after),
        out_specs=[HBM_SPEC] * (2 * n),
        input_output_aliases={i: i for i in range(2 * n)},
        compiler_params=pltpu.CompilerParams(has_side_effects=SIDE_EFFECT),
    )(*x_thru, *land_thru, *send_sems, *recv_sems, *after)
    return res[:n], res[n:]


def all_gather(x, name):
    return _exchange(x, True, name)


def all_to_all(x, name):
    return _exchange(x, False, name)


def _row_tile(n_rows, cap):
    best = None
    for t in range(8, min(n_rows, cap) + 1, 8):
        if n_rows % t == 0:
            best = t
    return best if best else n_rows


def adamw(w, m, v, slots, *, name, index=(), prev=None):
    n_rows, width = w.shape[-2:]
    n_lead = w.ndim - 2
    assert len(index) == n_lead
    n_slots = slots.shape[0]
    lanes = -(-width // 128) * 128
    tile = _row_tile(n_rows, max(8, (1 << 20) // (4 * lanes) // 8 * 8))
    bc1 = 1.0 - ADAM_B1 ** ADAM_STEP
    bc2 = 1.0 - ADAM_B2 ** ADAM_STEP

    def body(w_ref, m_ref, v_ref, s_ref, *rest):
        g_ref, d_ref, nm_ref, nv_ref = rest[-4:]
        g = s_ref[0].astype(F32)
        for k in range(1, n_slots):
            g = g + s_ref[k].astype(F32)
        wv = w_ref[...]
        nm = ADAM_B1 * m_ref[...] + (1.0 - ADAM_B1) * g
        nv = ADAM_B2 * v_ref[...] + (1.0 - ADAM_B2) * (g * g)
        g_ref[...] = g
        nm_ref[...] = nm
        nv_ref[...] = nv
        d_ref[...] = -ADAM_LR * ((nm / bc1) / (jnp.sqrt(nv / bc2) + ADAM_EPS) + ADAM_WD * wv)

    spec = pl.BlockSpec((None,) * n_lead + (tile, width), lambda i: tuple(index) + (i, 0))
    prev = list(prev) if prev is not None else []
    return pl.pallas_call(
        body, name=name, grid=(n_rows // tile,),
        in_specs=[spec, spec, spec, pl.BlockSpec((n_slots, tile, width), lambda i: (0, i, 0))]
        + [ANY_SPEC] * len(prev),
        out_specs=[spec] * 4, out_shape=[jax.ShapeDtypeStruct(w.shape, F32)] * 4,
        input_output_aliases={4 + k: k for k in range(len(prev))},
        compiler_params=_params(dimension_semantics=("parallel",)),
    )(w, m, v, slots, *prev)


def sum_slots(slots, *, name):
    n_slots, n_rows, width = slots.shape

    def body(s_ref, o_ref):
        @pl.when(pl.program_id(0) == 0)
        def _():
            o_ref[...] = s_ref[...]

        @pl.when(pl.program_id(0) > 0)
        def _():
            o_ref[...] += s_ref[...]

    return pl.pallas_call(
        body, name=name, grid=(n_slots,),
        in_specs=[pl.BlockSpec((None, n_rows, width), lambda k: (k, 0, 0))],
        out_specs=pl.BlockSpec((n_rows, width), lambda k: (0, 0)),
        out_shape=jax.ShapeDtypeStruct((n_rows, width), F32),
        compiler_params=_params(dimension_semantics=("arbitrary",)),
    )(slots)


def adamw_nd(w, m, v, slots, *, name):
    shp = w.shape
    two = (-1, shp[-1])
    res = adamw(w.reshape(two), m.reshape(two), v.reshape(two), slots.reshape((slots.shape[0],) + (w.size // shp[-1], shp[-1])),
                name=name)
    return [r.reshape(shp) for r in res]


def _pack(arrs):
    parts = []
    for a in arrs:
        flat = a.reshape(-1).astype(F32)
        parts.append(jnp.pad(flat, (0, (-flat.shape[0]) % 1024)))
    return jnp.concatenate(parts).reshape(-1, 128)


def _unpack(buf, shapes):
    outs, at = [], 0
    lead = buf.shape[:-2]
    flat = buf.reshape(lead + (-1,))
    for shp in shapes:
        n = int(np.prod(shp))
        outs.append(flat[..., at:at + n].reshape(lead + tuple(shp)))
        at += n + (-n) % 1024
    return outs


ARG_NAMES = ["x", "c", "ada_w", "ada_b", "ln_g", "ln_b", "ffn_w1", "ffn_w3", "ffn_w2", "hgrn_lb_logits", "ab_w_in",
             "ab_w_out", "hgrn_norm_g", "mlstm_conv_w", "mlstm_conv_b", "mlstm_wq", "mlstm_wk", "mlstm_gate_b",
             "mlstm_skip", "mlstm_norm_g", "rglru_w_in", "rglru_conv_w", "rglru_conv_b", "rglru_wa", "rglru_ba",
             "rglru_wx", "rglru_bx", "rglru_lambda", "rglru_w_out", "loss_target"]
WEIGHTS = ARG_NAMES[2:-1]
BIG = ["ffn_w1", "ffn_w3", "ffn_w2", "ab_w_in", "ab_w_out", "rglru_w_in", "rglru_w_out"]
REPLICATED = ["ada_b", "hgrn_lb_logits", "hgrn_norm_g", "mlstm_conv_b", "mlstm_wq", "mlstm_wk", "mlstm_gate_b",
              "mlstm_skip", "mlstm_norm_g", "rglru_wa", "rglru_wx"]
SHARDED_SMALL = ["ln_g", "ln_b", "mlstm_conv_w", "rglru_conv_w", "rglru_conv_b", "rglru_ba", "rglru_bx", "rglru_lambda"]


def _unshard_last(gathered):
    moved = jnp.moveaxis(gathered, 0, -2)
    return moved.reshape(moved.shape[:-2] + (NDEV * moved.shape[-1],))


def _shard_last(full):
    split = full.reshape(full.shape[:-1] + (NDEV, full.shape[-1] // NDEV))
    return jnp.moveaxis(split, -2, 0)


def kernel(x, c, ada_w, ada_b, ln_g, ln_b, ffn_w1, ffn_w3, ffn_w2, hgrn_lb_logits, ab_w_in, ab_w_out, hgrn_norm_g, mlstm_conv_w, mlstm_conv_b, mlstm_wq, mlstm_wk, mlstm_gate_b, mlstm_skip, mlstm_norm_g, rglru_w_in, rglru_conv_w, rglru_conv_b, rglru_wa, rglru_ba, rglru_wx, rglru_bx, rglru_lambda, rglru_w_out, loss_target, m_ada_w, m_ada_b, m_ln_g, m_ln_b, m_ffn_w1, m_ffn_w3, m_ffn_w2, m_hgrn_lb_logits, m_ab_w_in, m_ab_w_out, m_hgrn_norm_g, m_mlstm_conv_w, m_mlstm_conv_b, m_mlstm_wq, m_mlstm_wk, m_mlstm_gate_b, m_mlstm_skip, m_mlstm_norm_g, m_rglru_w_in, m_rglru_conv_w, m_rglru_conv_b, m_rglru_wa, m_rglru_ba, m_rglru_wx, m_rglru_bx, m_rglru_lambda, m_rglru_w_out, v_ada_w, v_ada_b, v_ln_g, v_ln_b, v_ffn_w1, v_ffn_w3, v_ffn_w2, v_hgrn_lb_logits, v_ab_w_in, v_ab_w_out, v_hgrn_norm_g, v_mlstm_conv_w, v_mlstm_conv_b, v_mlstm_wq, v_mlstm_wk, v_mlstm_gate_b, v_mlstm_skip, v_mlstm_norm_g, v_rglru_w_in, v_rglru_conv_w, v_rglru_conv_b, v_rglru_wa, v_rglru_ba, v_rglru_wx, v_rglru_bx, v_rglru_lambda, v_rglru_w_out):
    args = locals()
    p = {n: args[n] for n in ARG_NAMES}
    mom = {n: (args["m_" + n], args["v_" + n]) for n in WEIGHTS}
    me = _flat_index(_my_position())

    keys = [("ffn", 0, 0), ("ab",), ("ffn", 0, 1), ("ffn", 1, 0), ("rg",), ("ffn", 1, 1)]
    names = {("ab",): ("ab_w_in", "ab_w_out"), ("rg",): ("rglru_w_in", "rglru_w_out")}
    for l in range(DEPTH):
        for i in range(2):
            names[("ffn", l, i)] = ("ffn_w1", "ffn_w3", "ffn_w2")

    def part(key):
        return (lambda a: a[key[1], key[2]]) if key[0] == "ffn" else (lambda a: a[0])

    gather_handles = {}

    def landed(handles, own_of, after, name):
        sources, lands = exchange_wait(handles, after, name)
        return [lax.dynamic_update_index_in_dim(ld, own_of(src), me, 0) for src, ld in zip(sources, lands)]

    starts_next = {("ffn", 0, 0): [("ab",)], ("ab",): [("ffn", 0, 1), ("ffn", 1, 0)],
                   ("ffn", 0, 1): [("rg",), ("ffn", 1, 1)]}

    def start_gather(key, after):
        shards = [part(key)(p[n]).astype(BF16) for n in names[key]]
        if after is not None:
            shards, _ = lax.optimization_barrier((shards, after))
        gather_handles[key], token = exchange_start(shards, True, "gather_start_" + "_".join(map(str, key)))
        return token

    def get_weights(key, after):
        got = landed(gather_handles[key], lambda src: src, after, "gather_wait_" + "_".join(map(str, key)))
        tokens = [start_gather(nxt, got) for nxt in starts_next.get(key, [])]
        if tokens:
            got, _ = lax.optimization_barrier((got, tokens))
        if key[0] == "ffn":
            return _unshard_last(got[0]), _unshard_last(got[1]), got[2].reshape(D_FF, D_MODEL)
        w_in = _unshard_last(got[0])
        if key[0] == "ab":
            w_in = jnp.concatenate([w_in[:, :AB_MAIN], jnp.pad(w_in[:, AB_MAIN:], ((0, 0), (0, 120)))], axis=1)
        return w_in, got[1].reshape(D_MODEL, D_MODEL)

    scatter_handles = {}

    def put_grads(key, grads):
        if key[0] == "ffn":
            slots = [_shard_last(grads[0]), _shard_last(grads[1]), grads[2].reshape(NDEV, D_FF // NDEV, D_MODEL)]
        else:
            g_in = grads[0][:, :AB_MAIN + 8] if key[0] == "ab" else grads[0]
            slots = [_shard_last(g_in), grads[1].reshape(NDEV, D_MODEL // NDEV, D_MODEL)]
        scatter_handles[key], token = exchange_start(slots, False, "scatter_start_" + "_".join(map(str, key)))
        return token

    started = [start_gather(keys[0], None)]

    sharded_shapes = [p[n].shape for n in SHARDED_SMALL]
    small = all_gather(_pack([p[n] for n in SHARDED_SMALL] + [c]) + started[0][0, 0], "gather_small")
    per_dev = _unpack(small, sharded_shapes + [c.shape])
    full_small = {n: _unshard_last(per_dev[i]) for i, n in enumerate(SHARDED_SMALL)}
    c_all = per_dev[-1].reshape(NDEV, D_MODEL)

    c16 = jnp.pad(c_all, ((0, 8), (0, 0)))
    (c_act,), _ = rowwise(lambda r, q: ([_silu(r[0])], []), [Rows(c16)], [], [(D_MODEL, BF16)], [], tile=16,
                          name="cond_act")
    n_ada = ada_w.shape[-1]
    ada_b_mine = lax.dynamic_slice_in_dim(ada_b, me * n_ada, n_ada, axis=1)
    ada_cols = []
    for l in range(DEPTH):
        bias = jnp.broadcast_to(ada_b_mine[l][None, :], (16, n_ada))
        (cols,) = matmul([[(c_act, ada_w[l])]], "nn", [F32], tm=16, tn=n_ada, tk=D_MODEL, extras=[bias],
                         epi=lambda accs, ex: [accs[0] + ex[0]], name="ada_fwd")
        ada_cols.append(cols[:8])
    ada_mine = all_to_all(jnp.stack(ada_cols, axis=1), "ada_to_owner")
    mod = jnp.moveaxis(ada_mine, 0, 1).reshape(DEPTH, 9, D_MODEL)


    w = {"ln_g": full_small["ln_g"], "ln_b": full_small["ln_b"], "hgrn_lb_logits": hgrn_lb_logits}
    for n in ("hgrn_norm_g", "mlstm_conv_b", "mlstm_wq", "mlstm_wk", "mlstm_gate_b", "mlstm_skip", "mlstm_norm_g",
              "rglru_wa", "rglru_wx"):
        w[n] = p[n][0]
    for n in ("mlstm_conv_w", "rglru_conv_w", "rglru_conv_b", "rglru_ba", "rglru_bx", "rglru_lambda"):
        w[n] = full_small[n][0]

    small_names = REPLICATED + SHARDED_SMALL
    small_handles = []

    def put_small(g_small_parts, d_modulation):
        parts_ = dict(g_small_parts, ada_b=d_modulation.reshape(DEPTH, 9 * D_MODEL))
        handles, token = exchange_start([_pack([parts_[n] for n in small_names])], True, "gather_start_small_grads")
        small_handles.append(handles)
        return token

    loss, grad_x, d_mod, g, last_sent = local_step(x[0], loss_target[0], mod, w, get_weights, put_grads, put_small,
                                                   started)
    loss = lax.psum(loss, ("x", "y", "c"))

    outs = {}
    def update_group(key, after):
        slots = landed(scatter_handles[key], lambda src: lax.dynamic_index_in_dim(src, me, 0, keepdims=False), after,
                       "scatter_wait_" + "_".join(map(str, key)))
        for n, sl in zip(names[key], slots):
            index = key[1:] if key[0] == "ffn" else (0,)
            outs[n] = adamw(p[n], mom[n][0], mom[n][1], sl, name="adamw_" + n, index=index, prev=outs.get(n))
        return outs[names[key][-1]][0]

    full_shapes = [p[n].shape for n in REPLICATED] + [full_small[n].shape for n in SHARDED_SMALL]
    (all_small,) = landed(small_handles[0], lambda src: src, (last_sent, grad_x), "gather_wait_small_grads")
    summed = sum_slots(all_small, name="sum_small_grads")
    g_small = dict(zip(small_names, _unpack(summed, full_shapes)))
    rep = adamw(*[_pack([t[n] for n in REPLICATED]) for t in (p, {n: mom[n][0] for n in WEIGHTS},
                                                               {n: mom[n][1] for n in WEIGHTS})],
                _pack([g_small[n] for n in REPLICATED])[None], name="adamw_replicated")
    rep = [_unpack(r, [p[n].shape for n in REPLICATED]) for r in rep]
    for i, n in enumerate(REPLICATED):
        outs[n] = [r[i] for r in rep]
    g_mine = {n: lax.dynamic_slice_in_dim(g_small[n], me * p[n].shape[-1], p[n].shape[-1], axis=-1)
              for n in SHARDED_SMALL}
    shd = adamw(*[_pack([t[n] for n in SHARDED_SMALL]) for t in (p, {n: mom[n][0] for n in WEIGHTS},
                                                                  {n: mom[n][1] for n in WEIGHTS})],
                _pack([g_mine[n] for n in SHARDED_SMALL])[None], name="adamw_sharded_small")
    done = shd[0]
    for key in keys[:0:-1] + keys[:1]:
        done = update_group(key, done)
    shd = [_unpack(r, sharded_shapes) for r in shd]
    for i, n in enumerate(SHARDED_SMALL):
        outs[n] = [r[i] for r in shd]

    d_ada = all_small[:, :DEPTH * 9 * D_MODEL // 128].reshape(NDEV, DEPTH, 9 * D_MODEL)
    d_mine = lax.dynamic_slice_in_dim(d_ada, me * n_ada, n_ada, axis=2)
    g_ada = []
    for l in range(DEPTH):
        d16 = jnp.pad(d_mine[:, l], ((0, 8), (0, 0)))
        (gl,) = matmul([[(c_act, d16)]], "tn", [F32], tm=D_MODEL, tn=n_ada, tk=16, name="ada_bwd")
        g_ada.append(gl)
    outs["ada_w"] = adamw_nd(ada_w, *mom["ada_w"], jnp.stack(g_ada)[None], name="adamw_ada_w")

    result = [loss, grad_x[None]]
    for k in range(4):
        result += [outs[n][k].reshape(p[n].shape) for n in WEIGHTS]
    return tuple(result)
```

```python
import functools

import jax
import jax.numpy as jnp
import numpy as np
from jax import lax
from jax.experimental import pallas as pl
from jax.experimental.pallas import tpu as pltpu

F32 = jnp.float32
BF16 = jnp.bfloat16
HIGHEST = lax.Precision.HIGHEST

NDEV = 8
D_MODEL = 1024
D_FF = 2816
DEPTH = 2
CHUNK = 64
SUB = 16
HEADS = 4
HEAD_W = 128
MIX_W = HEADS * HEAD_W
AB_MAIN = 7 * MIX_W
AB_ALL = AB_MAIN + 128
CONV_W = 4
C_BLOCKS = 8
RG_C = 8.0
ALPHA = (2 * DEPTH) ** 0.25
FFN_RES_W = 0.5
NEG = -1e30

ADAM_LR = 0.001
ADAM_B1 = 0.9
ADAM_B2 = 0.999
ADAM_EPS = 1e-08
ADAM_WD = 0.01
ADAM_STEP = 10

VMEM_LIMIT = 56 * 1024 * 1024
GRAD_WIRE = jnp.bfloat16

NN = ((1,), (0,))
NT = ((1,), (1,))
TN = ((0,), (0,))


def _dot(a, b, dims, precision=None):
    return lax.dot_general(a, b, (dims, ((), ())), precision=precision, preferred_element_type=F32)


def _make_mm(dims, d_lhs, d_rhs, swap_lhs, swap_rhs, prec):
    def cast(v):
        return v.astype(BF16) if prec is None else v.astype(F32)

    @jax.custom_vjp
    def mm(a, b):
        return _dot(cast(a), cast(b), dims, prec)

    def fwd(a, b):
        return mm(a, b), (a, b)

    def bwd(res, g):
        a, b = res
        g = cast(g)
        da = _dot(cast(b), g, d_lhs, prec) if swap_lhs else _dot(g, cast(b), d_lhs, prec)
        db = _dot(g, cast(a), d_rhs, prec) if swap_rhs else _dot(cast(a), g, d_rhs, prec)
        return da.astype(a.dtype), db.astype(b.dtype)

    mm.defvjp(fwd, bwd)
    return mm


def _mm_family(prec):
    return (_make_mm(NN, NT, TN, False, False, prec), _make_mm(NT, NN, TN, False, True, prec),
            _make_mm(TN, NT, NN, True, False, prec))


def _round(v):
    return v.astype(BF16).astype(F32)


@jax.custom_vjp
def row_dot(a, n):
    return jnp.sum(_round(a) * _round(n), axis=1, keepdims=True)


def _row_dot_fwd(a, n):
    return row_dot(a, n), (a, n)


def _row_dot_bwd(res, g):
    a, n = res
    g = _round(g)
    return g * _round(n), jnp.sum(g * _round(a), axis=0, keepdims=True)


row_dot.defvjp(_row_dot_fwd, _row_dot_bwd)


@jax.custom_vjp
def col_dot(s, a):
    return jnp.sum(_round(s) * _round(a), axis=0, keepdims=True)


def _col_dot_fwd(s, a):
    return col_dot(s, a), (s, a)


def _col_dot_bwd(res, g):
    s, a = res
    g = _round(g)
    return jnp.sum(g * _round(a), axis=1, keepdims=True), _round(s) * g


col_dot.defvjp(_col_dot_fwd, _col_dot_bwd)

mm_nn, mm_nt, mm_tn = _mm_family(None)
mid_nn, mid_nt, mid_tn = _mm_family(lax.Precision.HIGH)
hi_nn, hi_nt, hi_tn = _mm_family(HIGHEST)


def _silu(v):
    return v * jax.nn.sigmoid(v)


def _log_sigmoid(v):
    return jnp.minimum(v, 0.0) - jnp.log1p(jnp.exp(-jnp.abs(v)))


def _softplus(v):
    return jnp.maximum(v, 0.0) + jnp.log1p(jnp.exp(-jnp.abs(v)))


def _neg_expm1(v):
    series = -v * (1.0 + v * (0.5 + v * (1.0 / 6.0 + v * (1.0 / 24.0 + v * (1.0 / 120.0)))))
    return jnp.where(v > -0.05, series, 1.0 - jnp.exp(v))


def _params(**kw):
    return pltpu.CompilerParams(vmem_limit_bytes=VMEM_LIMIT, **kw)


class Rows:
    def __init__(self, arr, block=None, split=None):
        self.arr = arr
        self.block = block
        self.split = split

    @property
    def width(self):
        return self.block[1] if self.block else self.arr.shape[1]


def _load(ref, split):
    if ref.ndim == 3:
        return [ref[k].astype(F32) for k in range(ref.shape[0])]
    if split is None:
        return ref[...].astype(F32)
    return [ref[:, k * split:(k + 1) * split].astype(F32) for k in range(ref.shape[1] // split)]


def _store(ref, val, accumulate=False):
    if isinstance(val, (list, tuple)):
        if ref.ndim == 3:
            for k, v in enumerate(val):
                ref[k] = (ref[k] + v if accumulate else v).astype(ref.dtype)
            return
        w = ref.shape[1] // len(val)
        for k, v in enumerate(val):
            sl = slice(k * w, (k + 1) * w)
            ref[:, sl] = (ref[:, sl] + v if accumulate else v).astype(ref.dtype)
    else:
        ref[...] = (ref[...] + val if accumulate else val).astype(ref.dtype)


def rowwise(fn, rows, params, out_rows, out_sums, *, tile, name, after=()):
    n_rows = rows[0].arr.shape[0]
    n_r, n_p, n_o = len(rows), len(params), len(out_rows)
    n_in = n_r + n_p + len(after)
    splits = [r.split for r in rows]

    def body(*refs):
        r_refs, p_refs = refs[:n_r], refs[n_r:n_r + n_p]
        o_refs, s_refs = refs[n_in:n_in + n_o], refs[n_in + n_o:]
        row_out, sum_out = fn([_load(r, s) for r, s in zip(r_refs, splits)], [_load(p, None) for p in p_refs])
        for ref, val in zip(o_refs, row_out):
            _store(ref, val)
        if s_refs:
            @pl.when(pl.program_id(0) == 0)
            def _():
                for ref in s_refs:
                    ref[...] = jnp.zeros(ref.shape, ref.dtype)

            for ref, val in zip(s_refs, sum_out):
                _store(ref, val, accumulate=True)

    in_specs = []
    for r in rows:
        blk = r.block[0] if r.block else 0
        in_specs.append(pl.BlockSpec((tile, r.width), functools.partial(lambda i, b: (i, b), b=blk)))
    for p in params:
        in_specs.append(pl.BlockSpec(p.shape, functools.partial(lambda i, n: (0,) * n, n=p.ndim)))
    in_specs += [pl.BlockSpec(memory_space=pl.ANY)] * len(after)
    out_shape = [jax.ShapeDtypeStruct((n_rows, w), dt) for w, dt in out_rows]
    out_specs = [pl.BlockSpec((tile, w), lambda i: (i, 0)) for w, _ in out_rows]
    for shp in out_sums:
        out_shape.append(jax.ShapeDtypeStruct(shp, F32))
        out_specs.append(pl.BlockSpec(shp, functools.partial(lambda i, n: (0,) * n, n=len(shp))))
    res = pl.pallas_call(
        body, name=name, grid=(n_rows // tile,), in_specs=in_specs, out_specs=out_specs, out_shape=out_shape,
        compiler_params=_params(dimension_semantics=("arbitrary",)),
    )(*[r.arr for r in rows], *params, *after)
    return res[:n_o], res[n_o:]


def rowwise_bwd(fn, rows, cots, params, want_rows, want_params, out_dtypes, *, tile, name, after=()):
    n = len(rows)

    def bwd(row_vals, param_vals):
        prim, cot = row_vals[:n], row_vals[n:]
        _, vjp = jax.vjp(lambda r, p: fn(r, p)[0], prim, param_vals)
        d_rows, d_params = vjp(cot)
        return [d_rows[i] for i in want_rows], [d_params[j] for j in want_params]

    out_rows = [(rows[i].width, dt) for i, dt in zip(want_rows, out_dtypes)]
    out_sums = [params[j].shape for j in want_params]
    return rowwise(bwd, list(rows) + list(cots), params, out_rows, out_sums, tile=tile, name=name, after=after)


def matmul(groups, mode, outs, *, tm, tn, tk, extras=(), epi=None, name, n_outer=False):
    a0, b0 = groups[0][0]
    if mode == "tn":
        k_dim, m_dim = a0.shape
    else:
        m_dim, k_dim = a0.shape
    n_dim = b0.shape[0] if mode == "nt" else b0.shape[1]
    tm, tn, tk = min(tm, m_dim), min(tn, n_dim), min(tk, k_dim)
    assert m_dim % tm == 0 and n_dim % tn == 0 and k_dim % tk == 0, (name, m_dim, n_dim, k_dim)
    nk = k_dim // tk
    pairs = [p for g in groups for p in g]
    n_pairs, n_groups, n_ex, n_out = len(pairs), len(groups), len(extras), len(outs)
    dims = {"nn": NN, "nt": NT, "tn": TN}[mode]

    def body(*refs):
        ab = refs[:2 * n_pairs]
        ex = refs[2 * n_pairs:2 * n_pairs + n_ex]
        o_refs = refs[2 * n_pairs + n_ex:2 * n_pairs + n_ex + n_out]
        accs = refs[2 * n_pairs + n_ex + n_out:]

        def partial_sums():
            sums, p = [], 0
            for g in groups:
                tot = None
                for _ in g:
                    d = _dot(ab[2 * p][...].astype(BF16), ab[2 * p + 1][...].astype(BF16), dims)
                    tot = d if tot is None else tot + d
                    p += 1
                sums.append(tot)
            return sums

        def finish(vals):
            res = epi(vals, [e[...] for e in ex]) if epi else vals
            for ref, v in zip(o_refs, res):
                ref[...] = v.astype(ref.dtype)

        if nk == 1:
            finish(partial_sums())
        else:
            k = pl.program_id(2)

            @pl.when(k == 0)
            def _():
                for acc in accs:
                    acc[...] = jnp.zeros(acc.shape, F32)

            for acc, s in zip(accs, partial_sums()):
                acc[...] += s

            @pl.when(k == nk - 1)
            def _():
                finish([acc[...] for acc in accs])

    def at(pick):
        return (lambda j, i, k: pick(i, j, k)) if n_outer else pick

    if mode == "nn":
        a_spec = pl.BlockSpec((tm, tk), at(lambda i, j, k: (i, k)))
        b_spec = pl.BlockSpec((tk, tn), at(lambda i, j, k: (k, j)))
    elif mode == "nt":
        a_spec = pl.BlockSpec((tm, tk), at(lambda i, j, k: (i, k)))
        b_spec = pl.BlockSpec((tn, tk), at(lambda i, j, k: (j, k)))
    else:
        a_spec = pl.BlockSpec((tk, tm), at(lambda i, j, k: (k, i)))
        b_spec = pl.BlockSpec((tk, tn), at(lambda i, j, k: (k, j)))
    mn_spec = pl.BlockSpec((tm, tn), at(lambda i, j, k: (i, j)))
    grid = (n_dim // tn, m_dim // tm, nk) if n_outer else (m_dim // tm, n_dim // tn, nk)
    return pl.pallas_call(
        body, name=name, grid=grid,
        in_specs=[a_spec, b_spec] * n_pairs + [mn_spec] * n_ex,
        out_specs=[mn_spec] * n_out,
        out_shape=[jax.ShapeDtypeStruct((m_dim, n_dim), dt) for dt in outs],
        scratch_shapes=[pltpu.VMEM((tm, tn), F32)] * (n_groups if nk > 1 else 0),
        compiler_params=_params(dimension_semantics=("parallel", "parallel", "arbitrary")),
    )(*[x for p in pairs for x in p], *extras)


def conv_fwd(x, w, b, *, tile, name):
    n_rows, width = x.arr.shape[0], x.width
    blk = x.block[0] if x.block else 0

    def body(x_ref, halo_ref, w_ref, b_ref, y_ref, buf):
        i = pl.program_id(0)
        halo = halo_ref[...]
        buf[0:8, :] = jnp.where(i == 0, jnp.zeros_like(halo), _round(halo))
        buf[8:, :] = _round(x_ref[...])
        acc = jnp.zeros((tile, width), F32)
        for j in range(CONV_W):
            s = CONV_W - 1 - j
            acc = acc + _round(w_ref[j:j + 1, :]) * buf[8 - s:8 - s + tile, :]
        y_ref[...] = acc + b_ref[...]

    hb = tile // 8
    return pl.pallas_call(
        body, name=name, grid=(n_rows // tile,),
        in_specs=[pl.BlockSpec((tile, width), lambda i: (i, blk)),
                  pl.BlockSpec((8, width), lambda i: (jnp.maximum(i * hb - 1, 0), blk)),
                  pl.BlockSpec((CONV_W, width), lambda i: (0, 0)),
                  pl.BlockSpec((1, width), lambda i: (0, 0))],
        out_specs=pl.BlockSpec((tile, width), lambda i: (i, 0)),
        out_shape=jax.ShapeDtypeStruct((n_rows, width), F32),
        scratch_shapes=[pltpu.VMEM((tile + 8, width), F32)],
        compiler_params=_params(dimension_semantics=("arbitrary",)),
    )(x.arr, x.arr, w, b)


def conv_bwd(x, dy, w, *, tile, name):
    n_rows, width = x.arr.shape[0], x.width
    blk = x.block[0] if x.block else 0
    n_tiles = n_rows // tile

    def body(x_ref, xh_ref, dy_ref, dyh_ref, w_ref, dx_ref, dw_ref, db_ref, xbuf, dbuf):
        i = pl.program_id(0)
        xh, dyh = xh_ref[...], dyh_ref[...]
        xbuf[0:8, :] = jnp.where(i == 0, jnp.zeros_like(xh), _round(xh))
        xbuf[8:, :] = _round(x_ref[...])
        dy_t = dy_ref[...]
        dy_r = _round(dy_t)
        dbuf[0:tile, :] = dy_r
        dbuf[tile:, :] = jnp.where(i == n_tiles - 1, jnp.zeros_like(dyh), _round(dyh))

        @pl.when(i == 0)
        def _():
            dw_ref[...] = jnp.zeros(dw_ref.shape, F32)
            db_ref[...] = jnp.zeros(db_ref.shape, F32)

        acc = jnp.zeros((tile, width), F32)
        for j in range(CONV_W):
            s = CONV_W - 1 - j
            acc = acc + _round(w_ref[j:j + 1, :]) * dbuf[s:s + tile, :]
            dw_ref[j:j + 1, :] += jnp.sum(dy_r * xbuf[8 - s:8 - s + tile, :], axis=0, keepdims=True)
        dx_ref[...] = acc.astype(dx_ref.dtype)
        db_ref[...] += jnp.sum(dy_t, axis=0, keepdims=True)

    hb = tile // 8
    return pl.pallas_call(
        body, name=name, grid=(n_tiles,),
        in_specs=[pl.BlockSpec((tile, width), lambda i: (i, blk)),
                  pl.BlockSpec((8, width), lambda i: (jnp.maximum(i * hb - 1, 0), blk)),
                  pl.BlockSpec((tile, width), lambda i: (i, 0)),
                  pl.BlockSpec((8, width), lambda i: (jnp.minimum((i + 1) * hb, n_tiles * hb - 1), 0)),
                  pl.BlockSpec((CONV_W, width), lambda i: (0, 0))],
        out_specs=[pl.BlockSpec((tile, width), lambda i: (i, 0)),
                   pl.BlockSpec((CONV_W, width), lambda i: (0, 0)),
                   pl.BlockSpec((1, width), lambda i: (0, 0))],
        out_shape=[jax.ShapeDtypeStruct((n_rows, width), BF16),
                   jax.ShapeDtypeStruct((CONV_W, width), F32),
                   jax.ShapeDtypeStruct((1, width), F32)],
        scratch_shapes=[pltpu.VMEM((tile + 8, width), F32), pltpu.VMEM((tile + 8, width), F32)],
        compiler_params=_params(dimension_semantics=("arbitrary",)),
    )(x.arr, x.arr, dy, dy, w)


def _pieces(ref, col_w, row_h):
    n_c, n_r = ref.shape[1] // col_w, ref.shape[0] // row_h
    return [[ref[r * row_h:(r + 1) * row_h, c * col_w:(c + 1) * col_w].astype(F32) for r in range(n_r)]
            for c in range(n_c)]


def _store_pieces(ref, vals, col_w, row_h):
    for c, col in enumerate(vals):
        for r, v in enumerate(col):
            ref[r * row_h:(r + 1) * row_h, c * col_w:(c + 1) * col_w] = v.astype(ref.dtype)


def _x_spec(x, n_chunks, reverse):
    blk = x.block[0] if x.block else 0
    if reverse:
        return pl.BlockSpec((CHUNK, x.width), functools.partial(lambda n, b: (n_chunks - 1 - n, b), b=blk))
    return pl.BlockSpec((CHUNK, x.width), functools.partial(lambda n, b: (n, b), b=blk))


def chunk_scan_fwd(step, xs, piece, state_shapes, out_widths, *, name):
    n_rows = xs[0].arr.shape[0]
    n_chunks = n_rows // CHUNK
    n_x, n_s, n_o = len(xs), len(state_shapes), len(out_widths)

    def body(*refs):
        x_refs, o_refs = refs[:n_x], refs[n_x:n_x + n_o]
        keep_refs, st_refs = refs[n_x + n_o:n_x + n_o + n_s], refs[n_x + n_o + n_s:]

        @pl.when(pl.program_id(0) == 0)
        def _():
            for st in st_refs:
                st[...] = jnp.zeros(st.shape, F32)

        states = [st[...] for st in st_refs]
        for keep, s in zip(keep_refs, states):
            keep[...] = s
        new_states, outs = step(states, [_pieces(x, *p) for x, p in zip(x_refs, piece)])
        for st, s in zip(st_refs, new_states):
            st[...] = s
        for o, v in zip(o_refs, outs):
            o[...] = v

    out_shape = [jax.ShapeDtypeStruct((n_rows, w), F32) for w in out_widths]
    out_specs = [pl.BlockSpec((CHUNK, w), lambda n: (n, 0)) for w in out_widths]
    for shp in state_shapes:
        out_shape.append(jax.ShapeDtypeStruct((n_chunks,) + shp, F32))
        out_specs.append(pl.BlockSpec((None,) + shp, lambda n: (n, 0, 0)))
    res = pl.pallas_call(
        body, name=name, grid=(n_chunks,),
        in_specs=[_x_spec(x, n_chunks, False) for x in xs],
        out_specs=out_specs, out_shape=out_shape,
        scratch_shapes=[pltpu.VMEM(shp, F32) for shp in state_shapes],
        compiler_params=_params(dimension_semantics=("arbitrary",)),
    )(*[x.arr for x in xs])
    return res[:n_o], res[n_o:]


def chunk_scan_bwd(step, xs, piece, kept, d_outs, *, name):
    n_rows = xs[0].arr.shape[0]
    n_chunks = n_rows // CHUNK
    n_x, n_s, n_o = len(xs), len(kept), len(d_outs)
    state_shapes = [k.shape[1:] for k in kept]

    def body(*refs):
        x_refs, k_refs = refs[:n_x], refs[n_x:n_x + n_s]
        do_refs = refs[n_x + n_s:n_x + n_s + n_o]
        dx_refs = refs[n_x + n_s + n_o:2 * n_x + n_s + n_o]
        ds_refs = refs[2 * n_x + n_s + n_o:]

        @pl.when(pl.program_id(0) == 0)
        def _():
            for ds in ds_refs:
                ds[...] = jnp.zeros(ds.shape, F32)

        states = [k[...] for k in k_refs]
        inputs = [_pieces(x, *p) for x, p in zip(x_refs, piece)]
        _, vjp = jax.vjp(step, states, inputs)
        d_states, d_inputs = vjp(([ds[...] for ds in ds_refs], [do[...] for do in do_refs]))
        for ds, v in zip(ds_refs, d_states):
            ds[...] = v
        for dx, v, p in zip(dx_refs, d_inputs, piece):
            _store_pieces(dx, v, *p)

    rev3 = lambda n: (n_chunks - 1 - n, 0, 0)
    rev2 = lambda n: (n_chunks - 1 - n, 0)
    return pl.pallas_call(
        body, name=name, grid=(n_chunks,),
        in_specs=[_x_spec(x, n_chunks, True) for x in xs]
        + [pl.BlockSpec((None,) + shp, rev3) for shp in state_shapes]
        + [pl.BlockSpec((CHUNK, d.shape[1]), rev2) for d in d_outs],
        out_specs=[pl.BlockSpec((CHUNK, x.width), rev2) for x in xs],
        out_shape=[jax.ShapeDtypeStruct((n_rows, x.width), F32) for x in xs],
        scratch_shapes=[pltpu.VMEM(shp, F32) for shp in state_shapes],
        compiler_params=_params(dimension_semantics=("arbitrary",)),
    )(*[x.arr for x in xs], *kept, *d_outs)


def _tri(n, strict=False):
    r = lax.broadcasted_iota(jnp.int32, (n, n), 0)
    c = lax.broadcasted_iota(jnp.int32, (n, n), 1)
    return (r > c) if strict else (r >= c)


def hgrn2_step(states, inputs):
    q_all, k_all, v_all, lf_all = inputs
    n_sub = CHUNK // SUB
    low = _tri(SUB).astype(F32)
    ones_sub = jnp.ones((SUB, SUB), F32)
    ones_chunk = jnp.ones((CHUNK, HEAD_W), F32)
    new_states, outs = [], []
    for h in range(HEADS):
        state = states[h]
        q, k, v, lf = q_all[h], k_all[h], v_all[h], lf_all[h]
        cum = [mid_nn(low, lf[i]) for i in range(n_sub)]
        tot = [mid_nn(ones_sub, lf[i]) for i in range(n_sub)]
        start = [jnp.zeros((SUB, HEAD_W), F32)]
        for i in range(n_sub):
            start.append(start[-1] + tot[i])
        q_in = [q[i] * jnp.exp(cum[i]) for i in range(n_sub)]
        intra = []
        for i in range(n_sub):
            keys = [k[j] * jnp.exp(start[i] - start[j] - cum[j]) for j in range(i)]
            keys.append(k[i] * jnp.exp(jnp.minimum(-cum[i], 80.0)))
            att = mid_nt(q_in[i], jnp.concatenate(keys, axis=0))
            r_id = lax.broadcasted_iota(jnp.int32, att.shape, 0)
            c_id = lax.broadcasted_iota(jnp.int32, att.shape, 1)
            att = jnp.where(c_id - SUB * i <= r_id, att, 0.0)
            intra.append(mm_nn(att, jnp.concatenate(v[:i + 1], axis=0)))
        q_state = jnp.concatenate([q_in[i] * jnp.exp(start[i]) for i in range(n_sub)], axis=0)
        out = mm_nn(q_state, state) + jnp.concatenate(intra, axis=0)
        k_end = jnp.concatenate([k[j] * jnp.exp(start[n_sub] - start[j] - cum[j]) for j in range(n_sub)], axis=0)
        decay = jnp.exp(mid_tn(jnp.concatenate(lf, axis=0), ones_chunk))
        new_states.append(decay * state + mm_tn(k_end, jnp.concatenate(v, axis=0)))
        outs.append(out)
    return new_states, [jnp.concatenate(outs, axis=1)]


def mlstm_step(states, inputs):
    q_all, k_all, v_all, gates = inputs
    gates = gates[0][0]
    c_st, n_st, m_st = states[:HEADS], states[HEADS:2 * HEADS], states[2 * HEADS:]
    lane = lax.broadcasted_iota(jnp.int32, (CHUNK, 128), 1)
    low = _tri(CHUNK).astype(F32)
    causal = _tri(CHUNK)
    gates_cum = hi_nn(low, gates)
    new_c, new_n, new_m, outs = [], [], [], []
    for h in range(HEADS):
        q, k, v = q_all[h][0], k_all[h][0], v_all[h][0]
        pick_i = (lane == h).astype(F32)
        pick_f = (lane == HEADS + h).astype(F32)
        li_col = jnp.sum(gates * pick_i, axis=1, keepdims=True)
        lf_col = jnp.sum(gates * pick_f, axis=1, keepdims=True)
        b_col = jnp.sum(gates_cum * pick_f, axis=1, keepdims=True)
        by_key = hi_nt(pick_i, gates) - hi_nt(pick_f, gates_cum)
        d_mat = jnp.where(causal, b_col + by_key, NEG)
        m_prev = lax.stop_gradient(jnp.max(m_st[h], axis=1, keepdims=True))
        g_inter = b_col + m_prev
        m_t = lax.stop_gradient(jnp.maximum(g_inter, jnp.max(d_mat, axis=1, keepdims=True)))
        w_inter = jnp.exp(g_inter - m_t)
        aw = jnp.exp(d_mat - m_t) * mm_nt(q, k)
        num = w_inter * mm_nn(q, c_st[h]) + mm_nn(aw, v)
        den = w_inter * row_dot(q, n_st[h]) + jnp.sum(aw, axis=1, keepdims=True)
        outs.append(num / jnp.maximum(jnp.abs(den), jnp.exp(-m_t)))
        b_end = jnp.sum(lf_col, axis=0, keepdims=True)
        g_state = b_end + m_prev
        s_w = b_end - b_col + li_col
        m_next = lax.stop_gradient(jnp.maximum(g_state, jnp.max(s_w, axis=0, keepdims=True)))
        dec = jnp.exp(g_state - m_next)
        w_s = jnp.exp(s_w - m_next)
        kw = k * w_s
        new_c.append(dec * c_st[h] + mm_tn(kw, v))
        new_n.append(dec * n_st[h] + col_dot(w_s, k))
        new_m.append(jnp.broadcast_to(m_next, (1, HEAD_W)))
    return new_c + new_n + new_m, [jnp.concatenate(outs, axis=1)]


def lru_fwd(a, u, *, tile, name):
    n_rows, width = a.shape

    def body(a_ref, u_ref, h_ref, carry):
        @pl.when(pl.program_id(0) == 0)
        def _():
            carry[...] = jnp.zeros(carry.shape, F32)

        h = carry[...]
        for t in range(tile):
            h = a_ref[t:t + 1, :] * h + u_ref[t:t + 1, :]
            h_ref[t:t + 1, :] = h
        carry[...] = h

    spec = pl.BlockSpec((tile, width), lambda i: (i, 0))
    return pl.pallas_call(
        body, name=name, grid=(n_rows // tile,), in_specs=[spec, spec], out_specs=spec,
        out_shape=jax.ShapeDtypeStruct((n_rows, width), F32),
        scratch_shapes=[pltpu.VMEM((1, width), F32)],
        compiler_params=_params(dimension_semantics=("arbitrary",)),
    )(a, u)


def lru_bwd(a, h, dh, *, tile, name):
    n_rows, width = a.shape
    n_tiles = n_rows // tile
    hb = tile // 8

    def body(a_ref, h_ref, hh_ref, dh_ref, da_ref, du_ref, carry):
        i = pl.program_id(0)

        @pl.when(i == 0)
        def _():
            carry[...] = jnp.zeros(carry.shape, F32)

        c = carry[...]
        for t in range(tile - 1, -1, -1):
            g = dh_ref[t:t + 1, :] + c
            du_ref[t:t + 1, :] = g
            if t:
                h_prev = h_ref[t - 1:t, :]
            else:
                h_prev = jnp.where(i == n_tiles - 1, 0.0, hh_ref[7:8, :])
            da_ref[t:t + 1, :] = g * h_prev
            c = a_ref[t:t + 1, :] * g
        carry[...] = c

    rev = lambda i: (n_tiles - 1 - i, 0)
    spec = pl.BlockSpec((tile, width), rev)
    halo = pl.BlockSpec((8, width), lambda i: (jnp.maximum((n_tiles - 1 - i) * hb - 1, 0), 0))
    return pl.pallas_call(
        body, name=name, grid=(n_tiles,), in_specs=[spec, spec, halo, spec], out_specs=[spec, spec],
        out_shape=[jax.ShapeDtypeStruct((n_rows, width), F32)] * 2,
        scratch_shapes=[pltpu.VMEM((1, width), F32)],
        compiler_params=_params(dimension_semantics=("arbitrary",)),
    )(a, h, h, dh)


def _layer_norm(z, g, b):
    mu = jnp.mean(z, axis=-1, keepdims=True)
    zc = z - mu
    var = jnp.mean(zc * zc, axis=-1, keepdims=True)
    return zc * lax.rsqrt(var + 1e-5) * g + b


def pre_fn(rows, params):
    (x,), (scale, shift) = rows, params
    return [x * (1.0 + scale) + shift], []


def make_post_fn(weight, with_next):
    def fn(rows, params):
        x, y = rows
        gate, g, b = params[:3]
        xo = _layer_norm(ALPHA * x + weight * (1.0 + gate) * y, g, b)
        if with_next:
            return [xo, xo * (1.0 + params[3]) + params[4]], []
        return [xo], []

    return fn


def make_last_fn(weight):
    post = make_post_fn(weight, False)

    def fn(rows, params):
        x, y, target = rows
        err = post([x, y], params)[0][0] - target
        loss = 0.5 * jnp.sum(jnp.mean(err * err, axis=-1, keepdims=True), axis=0, keepdims=True)
        return [err * (1.0 / D_MODEL)], [jnp.broadcast_to(loss, (1, 128))]

    return fn


def mix_a_fn(rows, params):
    a_q, a_f, xconv, graw = rows
    l0, l1, l2, wq, wk, gate_b = params
    mx = jnp.maximum(jnp.maximum(l0, l1), l2)
    e0, e1, e2 = jnp.exp(l0 - mx), jnp.exp(l1 - mx), jnp.exp(l2 - mx)
    lb = e0 / (e0 + e1 + e2)
    f = lb + (1.0 - lb) * jax.nn.sigmoid(a_f)
    xc = _silu(xconv)
    q_b = mm_nn(xc, wq)
    k_b = mm_nn(xc, wk) * (HEAD_W ** -0.5)
    g = graw + gate_b
    lane = lax.broadcasted_iota(jnp.int32, g.shape, 1)
    gates = jnp.where(lane < HEADS, g, _log_sigmoid(g))
    return [_silu(a_q), 1.0 - f, jnp.log(f), xc, q_b, k_b, gates], []


def _head_norm(v, g, center):
    if center:
        v = v - jnp.mean(v, axis=-1, keepdims=True)
    return v * lax.rsqrt(jnp.mean(v * v, axis=-1, keepdims=True) + 1e-6) * g


def mix_b_fn(rows, params):
    o_a, a_g, h_b, xc, b_z = rows
    hg, mg, skip = params
    y_a = [_head_norm(o_a[h], hg[h], False) * _silu(a_g[h]) for h in range(HEADS)]
    y_b = [(_head_norm(h_b[h], mg[h], True) + skip[h] * xc[h]) * _silu(b_z[h]) for h in range(HEADS)]
    return [y_a + y_b], []


def lru_a_fn(rows, params):
    (xr,) = rows
    wa, wx, ba, bx, lam = params
    a_out, u_out = [], []
    for n in range(C_BLOCKS):
        r = jax.nn.sigmoid(mm_nt(xr[n], wa[n]) + ba[n])
        i = jax.nn.sigmoid(mm_nt(xr[n], wx[n]) + bx[n])
        log_a = -RG_C * r * _softplus(-lam[n])
        a_out.append(jnp.exp(log_a))
        u_out.append(jnp.sqrt(_neg_expm1(2.0 * log_a)) * i * xr[n])
    return [a_out, u_out], []


def lru_b_fn(rows, params):
    h, y_br = rows
    return [h * jax.nn.gelu(y_br)], []


def swiglu_epi(accs, extras):
    h1, h3 = accs
    return [h1, h3, _silu(h1) * h3]


def swiglu_bwd_epi(accs, extras):
    (da,), (h1, h3) = accs, extras
    h1, h3 = h1.astype(F32), h3.astype(F32)
    sig = jax.nn.sigmoid(h1)
    return [da * h3 * sig * (1.0 + h1 * (1.0 - sig)), da * h1 * sig]


ROW_TILE = 256


def _chunks(v, n):
    return v.reshape(n, 1, v.shape[-1] // n)


def _dense_blocks(w):
    n, b, _ = w.shape
    by_row = jnp.swapaxes(w, 1, 2).reshape(n * b, b)
    spread = jnp.dot(by_row, _column_picker(n, b).T, precision=HIGHEST)
    return spread * _block_mask(n, b)


def _column_picker(n, b):
    return jnp.asarray(np.tile(np.eye(b, dtype=np.float32), (n, 1)))


def _block_mask(n, b):
    return jnp.asarray(np.kron(np.eye(n, dtype=np.float32), np.ones((b, b), np.float32)))


def _block_diag_of(m, n, b):
    by_row = jnp.dot(m * _block_mask(n, b), _column_picker(n, b), precision=HIGHEST)
    return jnp.swapaxes(by_row.reshape(n, b, b), 1, 2)


def local_step(x, target, mod, w, get_weights, put_grads, put_small, first_after=()):
    s_len = x.shape[0]
    tile = min(ROW_TILE, s_len)
    row = lambda v: v.reshape(1, -1)
    mrow = lambda l, j, k: mod[l, 3 * j + k].reshape(1, D_MODEL)
    g = {}
    d_mod = [[None] * 9 for _ in range(DEPTH)]
    d_ln_g = [[None] * 3 for _ in range(DEPTH)]
    d_ln_b = [[None] * 3 for _ in range(DEPTH)]
    subs = [(l, j) for l in range(DEPTH) for j in range(3)]
    weight_of = lambda j: 1.0 if j == 1 else FFN_RES_W

    wq_d = _dense_blocks(w["mlstm_wq"])
    wk_d = _dense_blocks(w["mlstm_wk"])
    wa_b, wx_b = w["rglru_wa"].astype(BF16), w["rglru_wx"].astype(BF16)
    gate_b = jnp.pad(w["mlstm_gate_b"].reshape(1, 8), ((0, 0), (0, 120)))
    lb_rows = [row(w["hgrn_lb_logits"][k]) for k in range(3)]
    mix_a_params = lb_rows + [wq_d, wk_d, gate_b]
    mix_b_params = [_chunks(row(w["hgrn_norm_g"]), HEADS), _chunks(row(w["mlstm_norm_g"]), HEADS),
                    _chunks(row(w["mlstm_skip"]), HEADS)]
    lru_a_params = [wa_b, wx_b, _chunks(row(w["rglru_ba"]), C_BLOCKS), _chunks(row(w["rglru_bx"]), C_BLOCKS),
                    _chunks(row(w["rglru_lambda"]), C_BLOCKS)]
    mconv_w, mconv_b = w["mlstm_conv_w"], row(w["mlstm_conv_b"])
    rconv_w, rconv_b = w["rglru_conv_w"], row(w["rglru_conv_b"])
    hg_piece = [(HEAD_W, SUB)] * 4
    ml_piece = [(HEAD_W, CHUNK)] * 3 + [(128, CHUNK)]

    (t,), _ = rowwise(pre_fn, [Rows(x)], [mrow(0, 0, 1), mrow(0, 0, 0)], [(D_MODEL, BF16)], [],
                      tile=tile, name="pre", after=tuple(first_after))
    saved = {}
    x_in = x
    for idx, (l, j) in enumerate(subs):
        sv = {"x": x_in, "t": t}
        if j != 1:
            w1, w3, w2 = get_weights(("ffn", l, j // 2), t)
            h1, h3, act = matmul([[(t, w1)], [(t, w3)]], "nn", [BF16, BF16, BF16], tm=512, tn=1408, tk=D_MODEL,
                                 epi=swiglu_epi, name="ffn_up", n_outer=True)
            (y,) = matmul([[(act, w2)]], "nn", [F32], tm=1024, tn=1024, tk=1408, name="ffn_down")
            sv.update(h1=h1, h3=h3, act=act, big=(w1, w3, w2))
        elif l == 0:
            ab_w_in, ab_w_out = get_weights(("ab",), t)
            sv["big"] = (ab_w_in, ab_w_out)
            (proj,) = matmul([[(t, ab_w_in)]], "nn", [F32], tm=256, tn=AB_ALL, tk=D_MODEL, name="ab_in")
            xconv = conv_fwd(Rows(proj, (4, MIX_W)), mconv_w, mconv_b, tile=tile, name="mconv")
            a_rows = [Rows(proj, (0, MIX_W)), Rows(proj, (1, MIX_W)), Rows(xconv), Rows(proj, (AB_MAIN // 128, 128))]
            (q_a, k_a, lf_a, xc, q_b, k_b, gates), _ = rowwise(
                mix_a_fn, a_rows, mix_a_params, [(MIX_W, F32)] * 6 + [(128, F32)], [], tile=tile, name="mix_a")
            hg_xs = [Rows(q_a), Rows(k_a), Rows(proj, (2, MIX_W)), Rows(lf_a)]
            (o_a,), hg_kept = chunk_scan_fwd(hgrn2_step, hg_xs, hg_piece, [(HEAD_W, HEAD_W)] * HEADS, [MIX_W],
                                             name="hgrn2_fwd")
            ml_xs = [Rows(q_b), Rows(k_b), Rows(proj, (5, MIX_W)), Rows(gates)]
            ml_states = [(HEAD_W, HEAD_W)] * HEADS + [(1, HEAD_W)] * (2 * HEADS)
            (h_b,), ml_kept = chunk_scan_fwd(mlstm_step, ml_xs, ml_piece, ml_states, [MIX_W], name="mlstm_fwd")
            b_rows = [Rows(o_a, split=HEAD_W), Rows(proj, (3, MIX_W), HEAD_W), Rows(h_b, split=HEAD_W),
                      Rows(xc, split=HEAD_W), Rows(proj, (6, MIX_W), HEAD_W)]
            (ycat,), _ = rowwise(mix_b_fn, b_rows, mix_b_params, [(2 * MIX_W, BF16)], [], tile=tile, name="mix_b")
            (y,) = matmul([[(ycat, ab_w_out)]], "nn", [F32], tm=512, tn=1024, tk=D_MODEL, name="ab_out")
            sv.update(proj=proj, xconv=xconv, a_rows=a_rows, hg_xs=hg_xs, hg_kept=hg_kept, ml_xs=ml_xs,
                      ml_kept=ml_kept, b_rows=b_rows, ycat=ycat)
        else:
            rg_w_in, rg_w_out = get_weights(("rg",), t)
            sv["big"] = (rg_w_in, rg_w_out)
            (proj,) = matmul([[(t, rg_w_in)]], "nn", [F32], tm=512, tn=1024, tk=D_MODEL, name="rg_in")
            xr = conv_fwd(Rows(proj, (1, D_MODEL)), rconv_w, rconv_b, tile=tile, name="rconv")
            (a_t, u_t), _ = rowwise(lru_a_fn, [Rows(xr, split=128)], lru_a_params, [(D_MODEL, F32)] * 2, [],
                                    tile=tile, name="lru_a")
            h = lru_fwd(a_t, u_t, tile=min(128, s_len), name="lru_fwd")
            b_rows = [Rows(h), Rows(proj, (0, D_MODEL))]
            (hgate,), _ = rowwise(lru_b_fn, b_rows, [], [(D_MODEL, BF16)], [], tile=tile, name="lru_b")
            (y,) = matmul([[(hgate, rg_w_out)]], "nn", [F32], tm=512, tn=1024, tk=D_MODEL, name="rg_out")
            sv.update(proj=proj, xr=xr, a_t=a_t, h=h, b_rows=b_rows, hgate=hgate)
        sv["y"] = y
        post_params = [mrow(l, j, 2), row(w["ln_g"][l, j]), row(w["ln_b"][l, j])]
        if idx + 1 < len(subs):
            nl, nj = subs[idx + 1]
            post_params += [mrow(nl, nj, 1), mrow(nl, nj, 0)]
            (x_out, t), _ = rowwise(make_post_fn(weight_of(j), True), [Rows(x_in), Rows(y)], post_params,
                                    [(D_MODEL, F32), (D_MODEL, BF16)], [], tile=tile, name="post")
        else:
            (d_xo,), (loss_row,) = rowwise(make_last_fn(weight_of(j)), [Rows(x_in), Rows(y), Rows(target)],
                                           post_params, [(D_MODEL, F32)], [(1, 128)], tile=tile, name="post_loss")
            x_out = None
        sv["post_params"] = post_params
        saved[(l, j)] = sv
        x_in = x_out
    loss = loss_row[0, 0]

    d_t_next = None
    sent = []
    deferred = []

    def hand_over(key, grads):
        token = put_grads(key, grads)
        if token is not None:
            sent.append(token)

    for idx in range(len(subs) - 1, -1, -1):
        l, j = subs[idx]
        sv = saved[(l, j)]
        has_next = idx + 1 < len(subs)
        cots = [Rows(d_xo)] + ([Rows(d_t_next)] if has_next else [])
        want_p = [0, 1, 2] + ([3, 4] if has_next else [])
        (d_xres, d_y), d_par = rowwise_bwd(
            make_post_fn(weight_of(j), has_next), [Rows(sv["x"]), Rows(sv["y"])], cots, sv["post_params"],
            [0, 1], want_p, [F32, BF16], tile=tile, name="post_bwd", after=tuple(sent))
        sent.clear()
        d_mod[l][3 * j + 2], d_ln_g[l][j], d_ln_b[l][j] = d_par[:3]
        if has_next:
            nl, nj = subs[idx + 1]
            d_mod[nl][3 * nj + 1], d_mod[nl][3 * nj] = d_par[3:]
        t = sv["t"]
        if j != 1:
            w1, w3, w2 = sv["big"]
            d_h1, d_h3 = matmul([[(d_y, w2)]], "nt", [BF16, BF16], tm=512, tn=1408, tk=D_MODEL,
                                extras=[sv["h1"], sv["h3"]], epi=swiglu_bwd_epi, name="ffn_down_bwd", n_outer=True)

            def ffn_weight_grads(after, ops=(sv["act"], t, d_y, d_h1, d_h3), key=("ffn", l, j // 2)):
                if after is not None:
                    ops, _ = lax.optimization_barrier((ops, after))
                act, t_in, d_out, d_1, d_3 = ops
                (g_w2,) = matmul([[(act, d_out)]], "tn", [GRAD_WIRE], tm=1408, tn=1024, tk=2048, name="ffn_dw2")
                (g_w1,) = matmul([[(t_in, d_1)]], "tn", [GRAD_WIRE], tm=1024, tn=1408, tk=2048, name="ffn_dw1")
                (g_w3,) = matmul([[(t_in, d_3)]], "tn", [GRAD_WIRE], tm=1024, tn=1408, tk=2048, name="ffn_dw3")
                hand_over(key, [g_w1, g_w3, g_w2])

            if idx:
                ffn_weight_grads(None)
            else:
                deferred.append(ffn_weight_grads)
            (d_t,) = matmul([[(d_h1, w1), (d_h3, w3)]], "nt", [F32], tm=1024, tn=1024, tk=1408, name="ffn_up_bwd")
        elif l == 0:
            ab_w_in, ab_w_out = sv["big"]
            (d_ycat,) = matmul([[(d_y, ab_w_out)]], "nt", [F32], tm=512, tn=1024, tk=D_MODEL, name="ab_out_bwd")
            (g_out,) = matmul([[(sv["ycat"], d_y)]], "tn", [GRAD_WIRE], tm=1024, tn=1024, tk=2048, name="ab_dwout")
            (d_oa, d_ag, d_hb, d_xc, d_bz), (d_hg, d_mg, d_skip) = rowwise_bwd(
                mix_b_fn, sv["b_rows"], [Rows(d_ycat, split=HEAD_W)], mix_b_params, [0, 1, 2, 3, 4], [0, 1, 2],
                [F32, BF16, F32, F32, BF16], tile=tile, name="mix_b_bwd")
            g["hgrn_norm_g"], g["mlstm_norm_g"], g["mlstm_skip"] = (v.reshape(1, MIX_W) for v in (d_hg, d_mg, d_skip))
            d_qb, d_kb, d_bv, d_gates = chunk_scan_bwd(mlstm_step, sv["ml_xs"], ml_piece, sv["ml_kept"], [d_hb],
                                                       name="mlstm_bwd")
            d_qa, d_ka, d_ai, d_lf = chunk_scan_bwd(hgrn2_step, sv["hg_xs"], hg_piece, sv["hg_kept"], [d_oa],
                                                    name="hgrn2_bwd")
            a_cots = [Rows(v) for v in (d_qa, d_ka, d_lf, d_xc, d_qb, d_kb, d_gates)]
            (d_aq, d_af, d_xconv, d_graw), (d_l0, d_l1, d_l2, d_wq, d_wk, d_gb) = rowwise_bwd(
                mix_a_fn, sv["a_rows"], a_cots, mix_a_params, [0, 1, 2, 3], [0, 1, 2, 3, 4, 5],
                [BF16, BF16, F32, BF16], tile=tile, name="mix_a_bwd")
            g["hgrn_lb_logits"] = jnp.concatenate([d_l0, d_l1, d_l2], axis=0)
            g["mlstm_wq"] = _block_diag_of(d_wq, MIX_W // 4, 4)
            g["mlstm_wk"] = _block_diag_of(d_wk, MIX_W // 4, 4)
            g["mlstm_gate_b"] = d_gb[:, :8]
            d_bx, g["mlstm_conv_w"], g["mlstm_conv_b"] = conv_bwd(Rows(sv["proj"], (4, MIX_W)), d_xconv, mconv_w,
                                                                  tile=tile, name="mconv_bwd")
            d_proj = jnp.concatenate([d_aq, d_af, d_ai.astype(BF16), d_ag, d_bx, d_bv.astype(BF16), d_bz, d_graw],
                                     axis=1)
            (g_in,) = matmul([[(t, d_proj)]], "tn", [GRAD_WIRE], tm=256, tn=AB_ALL, tk=1024, name="ab_dwin")
            hand_over(("ab",), [g_in, g_out])
            (d_t,) = matmul([[(d_proj, ab_w_in)]], "nt", [F32], tm=512, tn=1024, tk=AB_ALL, name="ab_in_bwd")
        else:
            rg_w_in, rg_w_out = sv["big"]
            (d_hgate,) = matmul([[(d_y, rg_w_out)]], "nt", [F32], tm=512, tn=1024, tk=D_MODEL,
                                name="rg_out_bwd")
            (g_out,) = matmul([[(sv["hgate"], d_y)]], "tn", [GRAD_WIRE], tm=1024, tn=1024, tk=2048,
                                         name="rg_dwout")
            (d_h, d_ybr), _ = rowwise_bwd(lru_b_fn, sv["b_rows"], [Rows(d_hgate)], [], [0, 1], [], [F32, BF16],
                                          tile=tile, name="lru_b_bwd")
            d_a, d_u = lru_bwd(sv["a_t"], sv["h"], d_h, tile=min(128, s_len), name="lru_bwd")
            (d_xr,), (d_wa, d_wx, d_ba, d_bx_, d_lam) = rowwise_bwd(
                lru_a_fn, [Rows(sv["xr"], split=128)], [Rows(d_a, split=128), Rows(d_u, split=128)], lru_a_params,
                [0], [0, 1, 2, 3, 4], [F32], tile=tile, name="lru_a_bwd")
            g["rglru_wa"], g["rglru_wx"] = d_wa, d_wx
            g["rglru_ba"], g["rglru_bx"], g["rglru_lambda"] = (v.reshape(1, D_MODEL) for v in (d_ba, d_bx_, d_lam))
            d_xbr, g["rglru_conv_w"], g["rglru_conv_b"] = conv_bwd(Rows(sv["proj"], (1, D_MODEL)), d_xr, rconv_w,
                                                                   tile=tile, name="rconv_bwd")
            d_proj = jnp.concatenate([d_ybr, d_xbr], axis=1)
            (g_in,) = matmul([[(t, d_proj)]], "tn", [GRAD_WIRE], tm=1024, tn=1024, tk=2048, name="rg_dwin")
            hand_over(("rg",), [g_in, g_out])
            (d_t,) = matmul([[(d_proj, rg_w_in)]], "nt", [F32], tm=512, tn=1024, tk=1024, name="rg_in_bwd")
        d_xo, d_t_next = d_xres, d_t

    def first_bwd(rows, params):
        x0, d_res, d_t0 = rows
        _, vjp = jax.vjp(lambda r, p: pre_fn(r, p)[0], [x0], params)
        (d_x0,), d_p = vjp([d_t0])
        return [d_res + d_x0], d_p

    (grad_x,), (d_mod[0][1], d_mod[0][0]) = rowwise(
        first_bwd, [Rows(x), Rows(d_xo), Rows(d_t_next)], [mrow(0, 0, 1), mrow(0, 0, 0)], [(D_MODEL, F32)],
        [(1, D_MODEL)] * 2, tile=tile, name="pre_bwd", after=tuple(sent))
    sent.clear()
    g["ln_g"] = jnp.stack([jnp.concatenate(r, axis=0) for r in d_ln_g])
    g["ln_b"] = jnp.stack([jnp.concatenate(r, axis=0) for r in d_ln_b])
    d_mod = jnp.stack([jnp.concatenate(r, axis=0) for r in d_mod])
    small_sent = put_small(g, d_mod)
    for weight_grads in deferred:
        weight_grads(small_sent)
    return loss, grad_x, d_mod, g, list(sent)


MESH_ID = pl.DeviceIdType.MESH
ANY_SPEC = pl.BlockSpec(memory_space=pl.ANY)


def _my_position():
    return lax.axis_index("x"), lax.axis_index("y"), lax.axis_index("c")


def _flat_index(pos):
    return 4 * pos[0] + 2 * pos[1] + pos[2]


def _peer_position(pos, k):
    return tuple(lax.rem(p + ((k >> s) & 1), 2) for p, s in zip(pos, (2, 1, 0)))


def _exchange(x, gather, name):
    out_shape = (NDEV,) + x.shape if gather else x.shape

    def body(x_ref, o_ref, send_sems, recv_sems, local_sem):
        pos = _my_position()
        me = _flat_index(pos)
        local = pltpu.make_async_copy(x_ref if gather else x_ref.at[me], o_ref.at[me], local_sem)
        local.start()
        copies = []
        for k in range(1, NDEV):
            peer = _peer_position(pos, k)
            src = x_ref if gather else x_ref.at[_flat_index(peer)]
            copies.append(pltpu.make_async_remote_copy(
                src_ref=src, dst_ref=o_ref.at[me], send_sem=send_sems.at[k - 1], recv_sem=recv_sems.at[k - 1],
                device_id=peer, device_id_type=MESH_ID))
            copies[-1].start()
        for cp in copies:
            cp.wait()
        local.wait()

    return pl.pallas_call(
        body, name=name, in_specs=[ANY_SPEC], out_specs=ANY_SPEC,
        out_shape=jax.ShapeDtypeStruct(out_shape, x.dtype),
        scratch_shapes=[pltpu.SemaphoreType.DMA((NDEV - 1,)), pltpu.SemaphoreType.DMA((NDEV - 1,)),
                        pltpu.SemaphoreType.DMA],
    )(x)


HBM_SPEC = pl.BlockSpec(memory_space=pltpu.HBM)
SEM_SPEC = pl.BlockSpec(memory_space=pltpu.SEMAPHORE)
SIDE_EFFECT = pltpu.SideEffectType.DATAFLOW_SIDE_EFFECTING


def _exchange_copies(x_refs, land_refs, send_sems, recv_sems, gather):
    pos = _my_position()
    me = _flat_index(pos)
    copies = []
    for k in range(1, NDEV):
        peer = _peer_position(pos, k)
        for x_ref, land_ref, s_sem, r_sem in zip(x_refs, land_refs, send_sems, recv_sems):
            src = x_ref if gather else x_ref.at[_flat_index(peer)]
            copies.append(pltpu.make_async_remote_copy(src_ref=src, dst_ref=land_ref.at[me], send_sem=s_sem,
                                                       recv_sem=r_sem, device_id=peer, device_id_type=MESH_ID))
    return copies


def exchange_start(xs, gather, name):
    n = len(xs)
    land_shapes = [(NDEV,) + x.shape if gather else x.shape for x in xs]

    def body(*refs):
        x_refs, land_refs = refs[:n], refs[n:2 * n]
        send_sems, recv_sems = refs[2 * n:3 * n], refs[3 * n:4 * n]
        token = refs[-1]
        for cp in _exchange_copies(x_refs, land_refs, send_sems, recv_sems, gather):
            cp.start()
        token[...] = jnp.zeros(token.shape, token.dtype)

    sem = pltpu.SemaphoreType.DMA(())
    res = pl.pallas_call(
        body, name=name,
        out_shape=[sem] * (2 * n) + [pltpu.HBM(x.shape, x.dtype) for x in xs]
        + [pltpu.HBM(s, x.dtype) for s, x in zip(land_shapes, xs)] + [jax.ShapeDtypeStruct((8, 128), F32)],
        in_specs=[HBM_SPEC] * (2 * n),
        out_specs=[SEM_SPEC] * (2 * n) + [HBM_SPEC] * (2 * n) + [pl.BlockSpec(memory_space=pltpu.VMEM)],
        input_output_aliases={i: 2 * n + i for i in range(2 * n)},
        compiler_params=pltpu.CompilerParams(has_side_effects=SIDE_EFFECT),
    )(*[pltpu.with_memory_space_constraint(x, pltpu.HBM) for x in xs],
      *[pltpu.with_memory_space_constraint(lax.empty(s, x.dtype), pltpu.HBM) for s, x in zip(land_shapes, xs)])
    return (res[:n], res[n:2 * n], res[2 * n:3 * n], res[3 * n:4 * n]), res[-1]


def exchange_wait(handles, after, name):
    send_sems, recv_sems, x_thru, land_thru = handles
    n = len(x_thru)
    after = jax.tree_util.tree_leaves(after)

    def body(*refs):
        land_refs = refs[n:2 * n]
        s_sems, r_sems = refs[2 * n:3 * n], refs[3 * n:4 * n]
        pos = _my_position()
        for land_ref, s_sem, r_sem in zip(land_refs, s_sems, r_sems):
            seven = land_ref.at[pl.ds(0, NDEV - 1)]
            all_seven = pltpu.make_async_remote_copy(src_ref=seven, dst_ref=seven, send_sem=s_sem, recv_sem=r_sem,
                                                     device_id=pos, device_id_type=MESH_ID)
            all_seven.wait_send()
            all_seven.wait_recv()

    res = pl.pallas_call(
        body, name=name,
        out_shape=[pltpu.HBM(x.shape, x.dtype) for x in x_thru] + [pltpu.HBM(x.shape, x.dtype) for x in land_thru],
        in_specs=[HBM_SPEC] * (2 * n) + [SEM_SPEC] * (2 * n) + [ANY_SPEC] * len(after),
        out_specs=[HBM_SPEC] * (2 * n),
        input_output_aliases={i: i for i in range(2 * n)},
        compiler_params=pltpu.CompilerParams(has_side_effects=SIDE_EFFECT),
    )(*x_thru, *land_thru, *send_sems, *recv_sems, *after)
    return res[:n], res[n:]


def all_gather(x, name):
    return _exchange(x, True, name)


def all_to_all(x, name):
    return _exchange(x, False, name)


def _row_tile(n_rows, cap):
    best = None
    for t in range(8, min(n_rows, cap) + 1, 8):
        if n_rows % t == 0:
            best = t
    return best if best else n_rows


def adamw(w, m, v, slots, *, name, index=(), prev=None):
    n_rows, width = w.shape[-2:]
    n_lead = w.ndim - 2
    assert len(index) == n_lead
    n_slots = slots.shape[0]
    lanes = -(-width // 128) * 128
    tile = _row_tile(n_rows, max(8, (1 << 20) // (4 * lanes) // 8 * 8))
    bc1 = 1.0 - ADAM_B1 ** ADAM_STEP
    bc2 = 1.0 - ADAM_B2 ** ADAM_STEP

    def body(w_ref, m_ref, v_ref, s_ref, *rest):
        g_ref, d_ref, nm_ref, nv_ref = rest[-4:]
        g = s_ref[0].astype(F32)
        for k in range(1, n_slots):
            g = g + s_ref[k].astype(F32)
        wv = w_ref[...]
        nm = ADAM_B1 * m_ref[...] + (1.0 - ADAM_B1) * g
        nv = ADAM_B2 * v_ref[...] + (1.0 - ADAM_B2) * (g * g)
        g_ref[...] = g
        nm_ref[...] = nm
        nv_ref[...] = nv
        d_ref[...] = -ADAM_LR * ((nm / bc1) / (jnp.sqrt(nv / bc2) + ADAM_EPS) + ADAM_WD * wv)

    spec = pl.BlockSpec((None,) * n_lead + (tile, width), lambda i: tuple(index) + (i, 0))
    prev = list(prev) if prev is not None else []
    return pl.pallas_call(
        body, name=name, grid=(n_rows // tile,),
        in_specs=[spec, spec, spec, pl.BlockSpec((n_slots, tile, width), lambda i: (0, i, 0))]
        + [ANY_SPEC] * len(prev),
        out_specs=[spec] * 4, out_shape=[jax.ShapeDtypeStruct(w.shape, F32)] * 4,
        input_output_aliases={4 + k: k for k in range(len(prev))},
        compiler_params=_params(dimension_semantics=("parallel",)),
    )(w, m, v, slots, *prev)


def sum_slots(slots, *, name):
    n_slots, n_rows, width = slots.shape

    def body(s_ref, o_ref):
        @pl.when(pl.program_id(0) == 0)
        def _():
            o_ref[...] = s_ref[...]

        @pl.when(pl.program_id(0) > 0)
        def _():
            o_ref[...] += s_ref[...]

    return pl.pallas_call(
        body, name=name, grid=(n_slots,),
        in_specs=[pl.BlockSpec((None, n_rows, width), lambda k: (k, 0, 0))],
        out_specs=pl.BlockSpec((n_rows, width), lambda k: (0, 0)),
        out_shape=jax.ShapeDtypeStruct((n_rows, width), F32),
        compiler_params=_params(dimension_semantics=("arbitrary",)),
    )(slots)


def adamw_nd(w, m, v, slots, *, name):
    shp = w.shape
    two = (-1, shp[-1])
    res = adamw(w.reshape(two), m.reshape(two), v.reshape(two), slots.reshape((slots.shape[0],) + (w.size // shp[-1], shp[-1])),
                name=name)
    return [r.reshape(shp) for r in res]


def _pack(arrs):
    parts = []
    for a in arrs:
        flat = a.reshape(-1).astype(F32)
        parts.append(jnp.pad(flat, (0, (-flat.shape[0]) % 1024)))
    return jnp.concatenate(parts).reshape(-1, 128)


def _unpack(buf, shapes):
    outs, at = [], 0
    lead = buf.shape[:-2]
    flat = buf.reshape(lead + (-1,))
    for shp in shapes:
        n = int(np.prod(shp))
        outs.append(flat[..., at:at + n].reshape(lead + tuple(shp)))
        at += n + (-n) % 1024
    return outs


ARG_NAMES = ["x", "c", "ada_w", "ada_b", "ln_g", "ln_b", "ffn_w1", "ffn_w3", "ffn_w2", "hgrn_lb_logits", "ab_w_in",
             "ab_w_out", "hgrn_norm_g", "mlstm_conv_w", "mlstm_conv_b", "mlstm_wq", "mlstm_wk", "mlstm_gate_b",
             "mlstm_skip", "mlstm_norm_g", "rglru_w_in", "rglru_conv_w", "rglru_conv_b", "rglru_wa", "rglru_ba",
             "rglru_wx", "rglru_bx", "rglru_lambda", "rglru_w_out", "loss_target"]
WEIGHTS = ARG_NAMES[2:-1]
BIG = ["ffn_w1", "ffn_w3", "ffn_w2", "ab_w_in", "ab_w_out", "rglru_w_in", "rglru_w_out"]
REPLICATED = ["ada_b", "hgrn_lb_logits", "hgrn_norm_g", "mlstm_conv_b", "mlstm_wq", "mlstm_wk", "mlstm_gate_b",
              "mlstm_skip", "mlstm_norm_g", "rglru_wa", "rglru_wx"]
SHARDED_SMALL = ["ln_g", "ln_b", "mlstm_conv_w", "rglru_conv_w", "rglru_conv_b", "rglru_ba", "rglru_bx", "rglru_lambda"]


def _unshard_last(gathered):
    moved = jnp.moveaxis(gathered, 0, -2)
    return moved.reshape(moved.shape[:-2] + (NDEV * moved.shape[-1],))


def _shard_last(full):
    split = full.reshape(full.shape[:-1] + (NDEV, full.shape[-1] // NDEV))
    return jnp.moveaxis(split, -2, 0)


def kernel(x, c, ada_w, ada_b, ln_g, ln_b, ffn_w1, ffn_w3, ffn_w2, hgrn_lb_logits, ab_w_in, ab_w_out, hgrn_norm_g, mlstm_conv_w, mlstm_conv_b, mlstm_wq, mlstm_wk, mlstm_gate_b, mlstm_skip, mlstm_norm_g, rglru_w_in, rglru_conv_w, rglru_conv_b, rglru_wa, rglru_ba, rglru_wx, rglru_bx, rglru_lambda, rglru_w_out, loss_target, m_ada_w, m_ada_b, m_ln_g, m_ln_b, m_ffn_w1, m_ffn_w3, m_ffn_w2, m_hgrn_lb_logits, m_ab_w_in, m_ab_w_out, m_hgrn_norm_g, m_mlstm_conv_w, m_mlstm_conv_b, m_mlstm_wq, m_mlstm_wk, m_mlstm_gate_b, m_mlstm_skip, m_mlstm_norm_g, m_rglru_w_in, m_rglru_conv_w, m_rglru_conv_b, m_rglru_wa, m_rglru_ba, m_rglru_wx, m_rglru_bx, m_rglru_lambda, m_rglru_w_out, v_ada_w, v_ada_b, v_ln_g, v_ln_b, v_ffn_w1, v_ffn_w3, v_ffn_w2, v_hgrn_lb_logits, v_ab_w_in, v_ab_w_out, v_hgrn_norm_g, v_mlstm_conv_w, v_mlstm_conv_b, v_mlstm_wq, v_mlstm_wk, v_mlstm_gate_b, v_mlstm_skip, v_mlstm_norm_g, v_rglru_w_in, v_rglru_conv_w, v_rglru_conv_b, v_rglru_wa, v_rglru_ba, v_rglru_wx, v_rglru_bx, v_rglru_lambda, v_rglru_w_out):
    args = locals()
    p = {n: args[n] for n in ARG_NAMES}
    mom = {n: (args["m_" + n], args["v_" + n]) for n in WEIGHTS}
    me = _flat_index(_my_position())

    keys = [("ffn", 0, 0), ("ab",), ("ffn", 0, 1), ("ffn", 1, 0), ("rg",), ("ffn", 1, 1)]
    names = {("ab",): ("ab_w_in", "ab_w_out"), ("rg",): ("rglru_w_in", "rglru_w_out")}
    for l in range(DEPTH):
        for i in range(2):
            names[("ffn", l, i)] = ("ffn_w1", "ffn_w3", "ffn_w2")

    def part(key):
        return (lambda a: a[key[1], key[2]]) if key[0] == "ffn" else (lambda a: a[0])

    gather_handles = {}

    def landed(handles, own_of, after, name):
        sources, lands = exchange_wait(handles, after, name)
        return [lax.dynamic_update_index_in_dim(ld, own_of(src), me, 0) for src, ld in zip(sources, lands)]

    starts_next = {("ffn", 0, 0): [("ab",)], ("ab",): [("ffn", 0, 1), ("ffn", 1, 0)],
                   ("ffn", 0, 1): [("rg",), ("ffn", 1, 1)]}

    def start_gather(key, after):
        shards = [part(key)(p[n]).astype(BF16) for n in names[key]]
        if after is not None:
            shards, _ = lax.optimization_barrier((shards, after))
        gather_handles[key], token = exchange_start(shards, True, "gather_start_" + "_".join(map(str, key)))
        return token

    def get_weights(key, after):
        got = landed(gather_handles[key], lambda src: src, after, "gather_wait_" + "_".join(map(str, key)))
        tokens = [start_gather(nxt, got) for nxt in starts_next.get(key, [])]
        if tokens:
            got, _ = lax.optimization_barrier((got, tokens))
        if key[0] == "ffn":
            return _unshard_last(got[0]), _unshard_last(got[1]), got[2].reshape(D_FF, D_MODEL)
        w_in = _unshard_last(got[0])
        if key[0] == "ab":
            w_in = jnp.concatenate([w_in[:, :AB_MAIN], jnp.pad(w_in[:, AB_MAIN:], ((0, 0), (0, 120)))], axis=1)
        return w_in, got[1].reshape(D_MODEL, D_MODEL)

    scatter_handles = {}

    def put_grads(key, grads):
        if key[0] == "ffn":
            slots = [_shard_last(grads[0]), _shard_last(grads[1]), grads[2].reshape(NDEV, D_FF // NDEV, D_MODEL)]
        else:
            g_in = grads[0][:, :AB_MAIN + 8] if key[0] == "ab" else grads[0]
            slots = [_shard_last(g_in), grads[1].reshape(NDEV, D_MODEL // NDEV, D_MODEL)]
        scatter_handles[key], token = exchange_start(slots, False, "scatter_start_" + "_".join(map(str, key)))
        return token

    started = [start_gather(keys[0], None)]

    sharded_shapes = [p[n].shape for n in SHARDED_SMALL]
    small = all_gather(_pack([p[n] for n in SHARDED_SMALL] + [c]) + started[0][0, 0], "gather_small")
    per_dev = _unpack(small, sharded_shapes + [c.shape])
    full_small = {n: _unshard_last(per_dev[i]) for i, n in enumerate(SHARDED_SMALL)}
    c_all = per_dev[-1].reshape(NDEV, D_MODEL)

    c16 = jnp.pad(c_all, ((0, 8), (0, 0)))
    (c_act,), _ = rowwise(lambda r, q: ([_silu(r[0])], []), [Rows(c16)], [], [(D_MODEL, BF16)], [], tile=16,
                          name="cond_act")
    n_ada = ada_w.shape[-1]
    ada_b_mine = lax.dynamic_slice_in_dim(ada_b, me * n_ada, n_ada, axis=1)
    ada_cols = []
    for l in range(DEPTH):
        bias = jnp.broadcast_to(ada_b_mine[l][None, :], (16, n_ada))
        (cols,) = matmul([[(c_act, ada_w[l])]], "nn", [F32], tm=16, tn=n_ada, tk=D_MODEL, extras=[bias],
                         epi=lambda accs, ex: [accs[0] + ex[0]], name="ada_fwd")
        ada_cols.append(cols[:8])
    ada_mine = all_to_all(jnp.stack(ada_cols, axis=1), "ada_to_owner")
    mod = jnp.moveaxis(ada_mine, 0, 1).reshape(DEPTH, 9, D_MODEL)


    w = {"ln_g": full_small["ln_g"], "ln_b": full_small["ln_b"], "hgrn_lb_logits": hgrn_lb_logits}
    for n in ("hgrn_norm_g", "mlstm_conv_b", "mlstm_wq", "mlstm_wk", "mlstm_gate_b", "mlstm_skip", "mlstm_norm_g",
              "rglru_wa", "rglru_wx"):
        w[n] = p[n][0]
    for n in ("mlstm_conv_w", "rglru_conv_w", "rglru_conv_b", "rglru_ba", "rglru_bx", "rglru_lambda"):
        w[n] = full_small[n][0]

    small_names = REPLICATED + SHARDED_SMALL
    small_handles = []

    def put_small(g_small_parts, d_modulation):
        parts_ = dict(g_small_parts, ada_b=d_modulation.reshape(DEPTH, 9 * D_MODEL))
        handles, token = exchange_start([_pack([parts_[n] for n in small_names])], True, "gather_start_small_grads")
        small_handles.append(handles)
        return token

    loss, grad_x, d_mod, g, last_sent = local_step(x[0], loss_target[0], mod, w, get_weights, put_grads, put_small,
                                                   started)
    loss = lax.psum(loss, ("x", "y", "c"))

    outs = {}
    def update_group(key, after):
        slots = landed(scatter_handles[key], lambda src: lax.dynamic_index_in_dim(src, me, 0, keepdims=False), after,
                       "scatter_wait_" + "_".join(map(str, key)))
        for n, sl in zip(names[key], slots):
            index = key[1:] if key[0] == "ffn" else (0,)
            outs[n] = adamw(p[n], mom[n][0], mom[n][1], sl, name="adamw_" + n, index=index, prev=outs.get(n))
        return outs[names[key][-1]][0]

    full_shapes = [p[n].shape for n in REPLICATED] + [full_small[n].shape for n in SHARDED_SMALL]
    (all_small,) = landed(small_handles[0], lambda src: src, (last_sent, grad_x), "gather_wait_small_grads")
    summed = sum_slots(all_small, name="sum_small_grads")
    g_small = dict(zip(small_names, _unpack(summed, full_shapes)))
    rep = adamw(*[_pack([t[n] for n in REPLICATED]) for t in (p, {n: mom[n][0] for n in WEIGHTS},
                                                               {n: mom[n][1] for n in WEIGHTS})],
                _pack([g_small[n] for n in REPLICATED])[None], name="adamw_replicated")
    rep = [_unpack(r, [p[n].shape for n in REPLICATED]) for r in rep]
    for i, n in enumerate(REPLICATED):
        outs[n] = [r[i] for r in rep]
    g_mine = {n: lax.dynamic_slice_in_dim(g_small[n], me * p[n].shape[-1], p[n].shape[-1], axis=-1)
              for n in SHARDED_SMALL}
    shd = adamw(*[_pack([t[n] for n in SHARDED_SMALL]) for t in (p, {n: mom[n][0] for n in WEIGHTS},
                                                                  {n: mom[n][1] for n in WEIGHTS})],
                _pack([g_mine[n] for n in SHARDED_SMALL])[None], name="adamw_sharded_small")
    done = shd[0]
    for key in keys[:0:-1] + keys[:1]:
        done = update_group(key, done)
    shd = [_unpack(r, sharded_shapes) for r in shd]
    for i, n in enumerate(SHARDED_SMALL):
        outs[n] = [r[i] for r in shd]

    d_ada = all_small[:, :DEPTH * 9 * D_MODEL // 128].reshape(NDEV, DEPTH, 9 * D_MODEL)
    d_mine = lax.dynamic_slice_in_dim(d_ada, me * n_ada, n_ada, axis=2)
    g_ada = []
    for l in range(DEPTH):
        d16 = jnp.pad(d_mine[:, l], ((0, 8), (0, 0)))
        (gl,) = matmul([[(c_act, d16)]], "tn", [F32], tm=D_MODEL, tn=n_ada, tk=16, name="ada_bwd")
        g_ada.append(gl)
    outs["ada_w"] = adamw_nd(ada_w, *mom["ada_w"], jnp.stack(g_ada)[None], name="adamw_ada_w")

    result = [loss, grad_x[None]]
    for k in range(4):
        result += [outs[n][k].reshape(p[n].shape) for n in WEIGHTS]
    return tuple(result)
```

```python
import functools

import jax
import jax.numpy as jnp
import numpy as np
from jax import lax
from jax.experimental import pallas as pl
from jax.experimental.pallas import tpu as pltpu

F32 = jnp.float32
BF16 = jnp.bfloat16
HIGHEST = lax.Precision.HIGHEST

NDEV = 8
D_MODEL = 1024
D_FF = 2816
DEPTH = 2
CHUNK = 64
SUB = 16
HEADS = 4
HEAD_W = 128
MIX_W = HEADS * HEAD_W
AB_MAIN = 7 * MIX_W
AB_ALL = AB_MAIN + 128
CONV_W = 4
C_BLOCKS = 8
RG_C = 8.0
ALPHA = (2 * DEPTH) ** 0.25
FFN_RES_W = 0.5
NEG = -1e30

ADAM_LR = 0.001
ADAM_B1 = 0.9
ADAM_B2 = 0.999
ADAM_EPS = 1e-08
ADAM_WD = 0.01
ADAM_STEP = 10

VMEM_LIMIT = 56 * 1024 * 1024
GRAD_WIRE = jnp.bfloat16

NN = ((1,), (0,))
NT = ((1,), (1,))
TN = ((0,), (0,))


def _dot(a, b, dims, precision=None):
    return lax.dot_general(a, b, (dims, ((), ())), precision=precision, preferred_element_type=F32)


def _make_mm(dims, d_lhs, d_rhs, swap_lhs, swap_rhs, prec):
    def cast(v):
        return v.astype(BF16) if prec is None else v.astype(F32)

    @jax.custom_vjp
    def mm(a, b):
        return _dot(cast(a), cast(b), dims, prec)

    def fwd(a, b):
        return mm(a, b), (a, b)

    def bwd(res, g):
        a, b = res
        g = cast(g)
        da = _dot(cast(b), g, d_lhs, prec) if swap_lhs else _dot(g, cast(b), d_lhs, prec)
        db = _dot(g, cast(a), d_rhs, prec) if swap_rhs else _dot(cast(a), g, d_rhs, prec)
        return da.astype(a.dtype), db.astype(b.dtype)

    mm.defvjp(fwd, bwd)
    return mm


def _mm_family(prec):
    return (_make_mm(NN, NT, TN, False, False, prec), _make_mm(NT, NN, TN, False, True, prec),
            _make_mm(TN, NT, NN, True, False, prec))


def _round(v):
    return v.astype(BF16).astype(F32)


@jax.custom_vjp
def row_dot(a, n):
    return jnp.sum(_round(a) * _round(n), axis=1, keepdims=True)


def _row_dot_fwd(a, n):
    return row_dot(a, n), (a, n)


def _row_dot_bwd(res, g):
    a, n = res
    g = _round(g)
    return g * _round(n), jnp.sum(g * _round(a), axis=0, keepdims=True)


row_dot.defvjp(_row_dot_fwd, _row_dot_bwd)


@jax.custom_vjp
def col_dot(s, a):
    return jnp.sum(_round(s) * _round(a), axis=0, keepdims=True)


def _col_dot_fwd(s, a):
    return col_dot(s, a), (s, a)


def _col_dot_bwd(res, g):
    s, a = res
    g = _round(g)
    return jnp.sum(g * _round(a), axis=1, keepdims=True), _round(s) * g


col_dot.defvjp(_col_dot_fwd, _col_dot_bwd)

mm_nn, mm_nt, mm_tn = _mm_family(None)
mid_nn, mid_nt, mid_tn = _mm_family(lax.Precision.HIGH)
hi_nn, hi_nt, hi_tn = _mm_family(HIGHEST)


def _silu(v):
    return v * jax.nn.sigmoid(v)


def _log_sigmoid(v):
    return jnp.minimum(v, 0.0) - jnp.log1p(jnp.exp(-jnp.abs(v)))


def _softplus(v):
    return jnp.maximum(v, 0.0) + jnp.log1p(jnp.exp(-jnp.abs(v)))


def _neg_expm1(v):
    series = -v * (1.0 + v * (0.5 + v * (1.0 / 6.0 + v * (1.0 / 24.0 + v * (1.0 / 120.0)))))
    return jnp.where(v > -0.05, series, 1.0 - jnp.exp(v))


def _params(**kw):
    return pltpu.CompilerParams(vmem_limit_bytes=VMEM_LIMIT, **kw)


class Rows:
    def __init__(self, arr, block=None, split=None):
        self.arr = arr
        self.block = block
        self.split = split

    @property
    def width(self):
        return self.block[1] if self.block else self.arr.shape[1]


def _load(ref, split):
    if ref.ndim == 3:
        return [ref[k].astype(F32) for k in range(ref.shape[0])]
    if split is None:
        return ref[...].astype(F32)
    return [ref[:, k * split:(k + 1) * split].astype(F32) for k in range(ref.shape[1] // split)]


def _store(ref, val, accumulate=False):
    if isinstance(val, (list, tuple)):
        if ref.ndim == 3:
            for k, v in enumerate(val):
                ref[k] = (ref[k] + v if accumulate else v).astype(ref.dtype)
            return
        w = ref.shape[1] // len(val)
        for k, v in enumerate(val):
            sl = slice(k * w, (k + 1) * w)
            ref[:, sl] = (ref[:, sl] + v if accumulate else v).astype(ref.dtype)
    else:
        ref[...] = (ref[...] + val if accumulate else val).astype(ref.dtype)


def rowwise(fn, rows, params, out_rows, out_sums, *, tile, name, after=()):
    n_rows = rows[0].arr.shape[0]
    n_r, n_p, n_o = len(rows), len(params), len(out_rows)
    n_in = n_r + n_p + len(after)
    splits = [r.split for r in rows]

    def body(*refs):
        r_refs, p_refs = refs[:n_r], refs[n_r:n_r + n_p]
        o_refs, s_refs = refs[n_in:n_in + n_o], refs[n_in + n_o:]
        row_out, sum_out = fn([_load(r, s) for r, s in zip(r_refs, splits)], [_load(p, None) for p in p_refs])
        for ref, val in zip(o_refs, row_out):
            _store(ref, val)
        if s_refs:
            @pl.when(pl.program_id(0) == 0)
            def _():
                for ref in s_refs:
                    ref[...] = jnp.zeros(ref.shape, ref.dtype)

            for ref, val in zip(s_refs, sum_out):
                _store(ref, val, accumulate=True)

    in_specs = []
    for r in rows:
        blk = r.block[0] if r.block else 0
        in_specs.append(pl.BlockSpec((tile, r.width), functools.partial(lambda i, b: (i, b), b=blk)))
    for p in params:
        in_specs.append(pl.BlockSpec(p.shape, functools.partial(lambda i, n: (0,) * n, n=p.ndim)))
    in_specs += [pl.BlockSpec(memory_space=pl.ANY)] * len(after)
    out_shape = [jax.ShapeDtypeStruct((n_rows, w), dt) for w, dt in out_rows]
    out_specs = [pl.BlockSpec((tile, w), lambda i: (i, 0)) for w, _ in out_rows]
    for shp in out_sums:
        out_shape.append(jax.ShapeDtypeStruct(shp, F32))
        out_specs.append(pl.BlockSpec(shp, functools.partial(lambda i, n: (0,) * n, n=len(shp))))
    res = pl.pallas_call(
        body, name=name, grid=(n_rows // tile,), in_specs=in_specs, out_specs=out_specs, out_shape=out_shape,
        compiler_params=_params(dimension_semantics=("arbitrary",)),
    )(*[r.arr for r in rows], *params, *after)
    return res[:n_o], res[n_o:]


def rowwise_bwd(fn, rows, cots, params, want_rows, want_params, out_dtypes, *, tile, name, after=()):
    n = len(rows)

    def bwd(row_vals, param_vals):
        prim, cot = row_vals[:n], row_vals[n:]
        _, vjp = jax.vjp(lambda r, p: fn(r, p)[0], prim, param_vals)
        d_rows, d_params = vjp(cot)
        return [d_rows[i] for i in want_rows], [d_params[j] for j in want_params]

    out_rows = [(rows[i].width, dt) for i, dt in zip(want_rows, out_dtypes)]
    out_sums = [params[j].shape for j in want_params]
    return rowwise(bwd, list(rows) + list(cots), params, out_rows, out_sums, tile=tile, name=name, after=after)


def matmul(groups, mode, outs, *, tm, tn, tk, extras=(), epi=None, name, n_outer=False, after=()):
    a0, b0 = groups[0][0]
    if mode == "tn":
        k_dim, m_dim = a0.shape
    else:
        m_dim, k_dim = a0.shape
    n_dim = b0.shape[0] if mode == "nt" else b0.shape[1]
    tm, tn, tk = min(tm, m_dim), min(tn, n_dim), min(tk, k_dim)
    assert m_dim % tm == 0 and n_dim % tn == 0 and k_dim % tk == 0, (name, m_dim, n_dim, k_dim)
    nk = k_dim // tk
    pairs = [p for g in groups for p in g]
    n_pairs, n_groups, n_ex, n_out = len(pairs), len(groups), len(extras), len(outs)
    dims = {"nn": NN, "nt": NT, "tn": TN}[mode]

    def body(*refs):
        ab = refs[:2 * n_pairs]
        ex = refs[2 * n_pairs:2 * n_pairs + n_ex]
        n_in = 2 * n_pairs + n_ex + len(after)
        o_refs = refs[n_in:n_in + n_out]
        accs = refs[n_in + n_out:]

        def partial_sums():
            sums, p = [], 0
            for g in groups:
                tot = None
                for _ in g:
                    d = _dot(ab[2 * p][...].astype(BF16), ab[2 * p + 1][...].astype(BF16), dims)
                    tot = d if tot is None else tot + d
                    p += 1
                sums.append(tot)
            return sums

        def finish(vals):
            res = epi(vals, [e[...] for e in ex]) if epi else vals
            for ref, v in zip(o_refs, res):
                ref[...] = v.astype(ref.dtype)

        if nk == 1:
            finish(partial_sums())
        else:
            k = pl.program_id(2)

            @pl.when(k == 0)
            def _():
                for acc in accs:
                    acc[...] = jnp.zeros(acc.shape, F32)

            for acc, s in zip(accs, partial_sums()):
                acc[...] += s

            @pl.when(k == nk - 1)
            def _():
                finish([acc[...] for acc in accs])

    def at(pick):
        return (lambda j, i, k: pick(i, j, k)) if n_outer else pick

    if mode == "nn":
        a_spec = pl.BlockSpec((tm, tk), at(lambda i, j, k: (i, k)))
        b_spec = pl.BlockSpec((tk, tn), at(lambda i, j, k: (k, j)))
    elif mode == "nt":
        a_spec = pl.BlockSpec((tm, tk), at(lambda i, j, k: (i, k)))
        b_spec = pl.BlockSpec((tn, tk), at(lambda i, j, k: (j, k)))
    else:
        a_spec = pl.BlockSpec((tk, tm), at(lambda i, j, k: (k, i)))
        b_spec = pl.BlockSpec((tk, tn), at(lambda i, j, k: (k, j)))
    mn_spec = pl.BlockSpec((tm, tn), at(lambda i, j, k: (i, j)))
    grid = (n_dim // tn, m_dim // tm, nk) if n_outer else (m_dim // tm, n_dim // tn, nk)
    return pl.pallas_call(
        body, name=name, grid=grid,
        in_specs=[a_spec, b_spec] * n_pairs + [mn_spec] * n_ex + [pl.BlockSpec(memory_space=pl.ANY)] * len(after),
        out_specs=[mn_spec] * n_out,
        out_shape=[jax.ShapeDtypeStruct((m_dim, n_dim), dt) for dt in outs],
        scratch_shapes=[pltpu.VMEM((tm, tn), F32)] * (n_groups if nk > 1 else 0),
        compiler_params=_params(dimension_semantics=("parallel", "parallel", "arbitrary")),
    )(*[x for p in pairs for x in p], *extras, *after)


def conv_fwd(x, w, b, *, tile, name):
    n_rows, width = x.arr.shape[0], x.width
    blk = x.block[0] if x.block else 0

    def body(x_ref, halo_ref, w_ref, b_ref, y_ref, buf):
        i = pl.program_id(0)
        halo = halo_ref[...]
        buf[0:8, :] = jnp.where(i == 0, jnp.zeros_like(halo), _round(halo))
        buf[8:, :] = _round(x_ref[...])
        acc = jnp.zeros((tile, width), F32)
        for j in range(CONV_W):
            s = CONV_W - 1 - j
            acc = acc + _round(w_ref[j:j + 1, :]) * buf[8 - s:8 - s + tile, :]
        y_ref[...] = acc + b_ref[...]

    hb = tile // 8
    return pl.pallas_call(
        body, name=name, grid=(n_rows // tile,),
        in_specs=[pl.BlockSpec((tile, width), lambda i: (i, blk)),
                  pl.BlockSpec((8, width), lambda i: (jnp.maximum(i * hb - 1, 0), blk)),
                  pl.BlockSpec((CONV_W, width), lambda i: (0, 0)),
                  pl.BlockSpec((1, width), lambda i: (0, 0))],
        out_specs=pl.BlockSpec((tile, width), lambda i: (i, 0)),
        out_shape=jax.ShapeDtypeStruct((n_rows, width), F32),
        scratch_shapes=[pltpu.VMEM((tile + 8, width), F32)],
        compiler_params=_params(dimension_semantics=("arbitrary",)),
    )(x.arr, x.arr, w, b)


def conv_bwd(x, dy, w, *, tile, name):
    n_rows, width = x.arr.shape[0], x.width
    blk = x.block[0] if x.block else 0
    n_tiles = n_rows // tile

    def body(x_ref, xh_ref, dy_ref, dyh_ref, w_ref, dx_ref, dw_ref, db_ref, xbuf, dbuf):
        i = pl.program_id(0)
        xh, dyh = xh_ref[...], dyh_ref[...]
        xbuf[0:8, :] = jnp.where(i == 0, jnp.zeros_like(xh), _round(xh))
        xbuf[8:, :] = _round(x_ref[...])
        dy_t = dy_ref[...]
        dy_r = _round(dy_t)
        dbuf[0:tile, :] = dy_r
        dbuf[tile:, :] = jnp.where(i == n_tiles - 1, jnp.zeros_like(dyh), _round(dyh))

        @pl.when(i == 0)
        def _():
            dw_ref[...] = jnp.zeros(dw_ref.shape, F32)
            db_ref[...] = jnp.zeros(db_ref.shape, F32)

        acc = jnp.zeros((tile, width), F32)
        for j in range(CONV_W):
            s = CONV_W - 1 - j
            acc = acc + _round(w_ref[j:j + 1, :]) * dbuf[s:s + tile, :]
            dw_ref[j:j + 1, :] += jnp.sum(dy_r * xbuf[8 - s:8 - s + tile, :], axis=0, keepdims=True)
        dx_ref[...] = acc.astype(dx_ref.dtype)
        db_ref[...] += jnp.sum(dy_t, axis=0, keepdims=True)

    hb = tile // 8
    return pl.pallas_call(
        body, name=name, grid=(n_tiles,),
        in_specs=[pl.BlockSpec((tile, width), lambda i: (i, blk)),
                  pl.BlockSpec((8, width), lambda i: (jnp.maximum(i * hb - 1, 0), blk)),
                  pl.BlockSpec((tile, width), lambda i: (i, 0)),
                  pl.BlockSpec((8, width), lambda i: (jnp.minimum((i + 1) * hb, n_tiles * hb - 1), 0)),
                  pl.BlockSpec((CONV_W, width), lambda i: (0, 0))],
        out_specs=[pl.BlockSpec((tile, width), lambda i: (i, 0)),
                   pl.BlockSpec((CONV_W, width), lambda i: (0, 0)),
                   pl.BlockSpec((1, width), lambda i: (0, 0))],
        out_shape=[jax.ShapeDtypeStruct((n_rows, width), BF16),
                   jax.ShapeDtypeStruct((CONV_W, width), F32),
                   jax.ShapeDtypeStruct((1, width), F32)],
        scratch_shapes=[pltpu.VMEM((tile + 8, width), F32), pltpu.VMEM((tile + 8, width), F32)],
        compiler_params=_params(dimension_semantics=("arbitrary",)),
    )(x.arr, x.arr, dy, dy, w)


def _pieces(ref, col_w, row_h):
    n_c, n_r = ref.shape[1] // col_w, ref.shape[0] // row_h
    return [[ref[r * row_h:(r + 1) * row_h, c * col_w:(c + 1) * col_w].astype(F32) for r in range(n_r)]
            for c in range(n_c)]


def _store_pieces(ref, vals, col_w, row_h):
    for c, col in enumerate(vals):
        for r, v in enumerate(col):
            ref[r * row_h:(r + 1) * row_h, c * col_w:(c + 1) * col_w] = v.astype(ref.dtype)


def _x_spec(x, n_chunks, reverse):
    blk = x.block[0] if x.block else 0
    if reverse:
        return pl.BlockSpec((CHUNK, x.width), functools.partial(lambda n, b: (n_chunks - 1 - n, b), b=blk))
    return pl.BlockSpec((CHUNK, x.width), functools.partial(lambda n, b: (n, b), b=blk))


def chunk_scan_fwd(step, xs, piece, state_shapes, out_widths, *, name):
    n_rows = xs[0].arr.shape[0]
    n_chunks = n_rows // CHUNK
    n_x, n_s, n_o = len(xs), len(state_shapes), len(out_widths)

    def body(*refs):
        x_refs, o_refs = refs[:n_x], refs[n_x:n_x + n_o]
        keep_refs, st_refs = refs[n_x + n_o:n_x + n_o + n_s], refs[n_x + n_o + n_s:]

        @pl.when(pl.program_id(0) == 0)
        def _():
            for st in st_refs:
                st[...] = jnp.zeros(st.shape, F32)

        states = [st[...] for st in st_refs]
        for keep, s in zip(keep_refs, states):
            keep[...] = s
        new_states, outs = step(states, [_pieces(x, *p) for x, p in zip(x_refs, piece)])
        for st, s in zip(st_refs, new_states):
            st[...] = s
        for o, v in zip(o_refs, outs):
            o[...] = v

    out_shape = [jax.ShapeDtypeStruct((n_rows, w), F32) for w in out_widths]
    out_specs = [pl.BlockSpec((CHUNK, w), lambda n: (n, 0)) for w in out_widths]
    for shp in state_shapes:
        out_shape.append(jax.ShapeDtypeStruct((n_chunks,) + shp, F32))
        out_specs.append(pl.BlockSpec((None,) + shp, lambda n: (n, 0, 0)))
    res = pl.pallas_call(
        body, name=name, grid=(n_chunks,),
        in_specs=[_x_spec(x, n_chunks, False) for x in xs],
        out_specs=out_specs, out_shape=out_shape,
        scratch_shapes=[pltpu.VMEM(shp, F32) for shp in state_shapes],
        compiler_params=_params(dimension_semantics=("arbitrary",)),
    )(*[x.arr for x in xs])
    return res[:n_o], res[n_o:]


def chunk_scan_bwd(step, xs, piece, kept, d_outs, *, name):
    n_rows = xs[0].arr.shape[0]
    n_chunks = n_rows // CHUNK
    n_x, n_s, n_o = len(xs), len(kept), len(d_outs)
    state_shapes = [k.shape[1:] for k in kept]

    def body(*refs):
        x_refs, k_refs = refs[:n_x], refs[n_x:n_x + n_s]
        do_refs = refs[n_x + n_s:n_x + n_s + n_o]
        dx_refs = refs[n_x + n_s + n_o:2 * n_x + n_s + n_o]
        ds_refs = refs[2 * n_x + n_s + n_o:]

        @pl.when(pl.program_id(0) == 0)
        def _():
            for ds in ds_refs:
                ds[...] = jnp.zeros(ds.shape, F32)

        states = [k[...] for k in k_refs]
        inputs = [_pieces(x, *p) for x, p in zip(x_refs, piece)]
        _, vjp = jax.vjp(step, states, inputs)
        d_states, d_inputs = vjp(([ds[...] for ds in ds_refs], [do[...] for do in do_refs]))
        for ds, v in zip(ds_refs, d_states):
            ds[...] = v
        for dx, v, p in zip(dx_refs, d_inputs, piece):
            _store_pieces(dx, v, *p)

    rev3 = lambda n: (n_chunks - 1 - n, 0, 0)
    rev2 = lambda n: (n_chunks - 1 - n, 0)
    return pl.pallas_call(
        body, name=name, grid=(n_chunks,),
        in_specs=[_x_spec(x, n_chunks, True) for x in xs]
        + [pl.BlockSpec((None,) + shp, rev3) for shp in state_shapes]
        + [pl.BlockSpec((CHUNK, d.shape[1]), rev2) for d in d_outs],
        out_specs=[pl.BlockSpec((CHUNK, x.width), rev2) for x in xs],
        out_shape=[jax.ShapeDtypeStruct((n_rows, x.width), F32) for x in xs],
        scratch_shapes=[pltpu.VMEM(shp, F32) for shp in state_shapes],
        compiler_params=_params(dimension_semantics=("arbitrary",)),
    )(*[x.arr for x in xs], *kept, *d_outs)


def _tri(n, strict=False):
    r = lax.broadcasted_iota(jnp.int32, (n, n), 0)
    c = lax.broadcasted_iota(jnp.int32, (n, n), 1)
    return (r > c) if strict else (r >= c)


def hgrn2_step(states, inputs):
    q_all, k_all, v_all, lf_all = inputs
    n_sub = CHUNK // SUB
    low = _tri(SUB).astype(F32)
    ones_sub = jnp.ones((SUB, SUB), F32)
    ones_chunk = jnp.ones((CHUNK, HEAD_W), F32)
    new_states, outs = [], []
    for h in range(HEADS):
        state = states[h]
        q, k, v, lf = q_all[h], k_all[h], v_all[h], lf_all[h]
        cum = [mid_nn(low, lf[i]) for i in range(n_sub)]
        tot = [mid_nn(ones_sub, lf[i]) for i in range(n_sub)]
        start = [jnp.zeros((SUB, HEAD_W), F32)]
        for i in range(n_sub):
            start.append(start[-1] + tot[i])
        q_in = [q[i] * jnp.exp(cum[i]) for i in range(n_sub)]
        intra = []
        for i in range(n_sub):
            keys = [k[j] * jnp.exp(start[i] - start[j] - cum[j]) for j in range(i)]
            keys.append(k[i] * jnp.exp(jnp.minimum(-cum[i], 80.0)))
            att = mid_nt(q_in[i], jnp.concatenate(keys, axis=0))
            r_id = lax.broadcasted_iota(jnp.int32, att.shape, 0)
            c_id = lax.broadcasted_iota(jnp.int32, att.shape, 1)
            att = jnp.where(c_id - SUB * i <= r_id, att, 0.0)
            intra.append(mm_nn(att, jnp.concatenate(v[:i + 1], axis=0)))
        q_state = jnp.concatenate([q_in[i] * jnp.exp(start[i]) for i in range(n_sub)], axis=0)
        out = mm_nn(q_state, state) + jnp.concatenate(intra, axis=0)
        k_end = jnp.concatenate([k[j] * jnp.exp(start[n_sub] - start[j] - cum[j]) for j in range(n_sub)], axis=0)
        decay = jnp.exp(mid_tn(jnp.concatenate(lf, axis=0), ones_chunk))
        new_states.append(decay * state + mm_tn(k_end, jnp.concatenate(v, axis=0)))
        outs.append(out)
    return new_states, [jnp.concatenate(outs, axis=1)]


def mlstm_step(states, inputs):
    q_all, k_all, v_all, gates = inputs
    gates = gates[0][0]
    c_st, n_st, m_st = states[:HEADS], states[HEADS:2 * HEADS], states[2 * HEADS:]
    lane = lax.broadcasted_iota(jnp.int32, (CHUNK, 128), 1)
    low = _tri(CHUNK).astype(F32)
    causal = _tri(CHUNK)
    gates_cum = hi_nn(low, gates)
    new_c, new_n, new_m, outs = [], [], [], []
    for h in range(HEADS):
        q, k, v = q_all[h][0], k_all[h][0], v_all[h][0]
        pick_i = (lane == h).astype(F32)
        pick_f = (lane == HEADS + h).astype(F32)
        li_col = jnp.sum(gates * pick_i, axis=1, keepdims=True)
        lf_col = jnp.sum(gates * pick_f, axis=1, keepdims=True)
        b_col = jnp.sum(gates_cum * pick_f, axis=1, keepdims=True)
        by_key = hi_nt(pick_i, gates) - hi_nt(pick_f, gates_cum)
        d_mat = jnp.where(causal, b_col + by_key, NEG)
        m_prev = lax.stop_gradient(jnp.max(m_st[h], axis=1, keepdims=True))
        g_inter = b_col + m_prev
        m_t = lax.stop_gradient(jnp.maximum(g_inter, jnp.max(d_mat, axis=1, keepdims=True)))
        w_inter = jnp.exp(g_inter - m_t)
        aw = jnp.exp(d_mat - m_t) * mm_nt(q, k)
        num = w_inter * mm_nn(q, c_st[h]) + mm_nn(aw, v)
        den = w_inter * row_dot(q, n_st[h]) + jnp.sum(aw, axis=1, keepdims=True)
        outs.append(num / jnp.maximum(jnp.abs(den), jnp.exp(-m_t)))
        b_end = jnp.sum(lf_col, axis=0, keepdims=True)
        g_state = b_end + m_prev
        s_w = b_end - b_col + li_col
        m_next = lax.stop_gradient(jnp.maximum(g_state, jnp.max(s_w, axis=0, keepdims=True)))
        dec = jnp.exp(g_state - m_next)
        w_s = jnp.exp(s_w - m_next)
        kw = k * w_s
        new_c.append(dec * c_st[h] + mm_tn(kw, v))
        new_n.append(dec * n_st[h] + col_dot(w_s, k))
        new_m.append(jnp.broadcast_to(m_next, (1, HEAD_W)))
    return new_c + new_n + new_m, [jnp.concatenate(outs, axis=1)]


def lru_fwd(a, u, *, tile, name):
    n_rows, width = a.shape

    def body(a_ref, u_ref, h_ref, carry):
        @pl.when(pl.program_id(0) == 0)
        def _():
            carry[...] = jnp.zeros(carry.shape, F32)

        h = carry[...]
        for t in range(tile):
            h = a_ref[t:t + 1, :] * h + u_ref[t:t + 1, :]
            h_ref[t:t + 1, :] = h
        carry[...] = h

    spec = pl.BlockSpec((tile, width), lambda i: (i, 0))
    return pl.pallas_call(
        body, name=name, grid=(n_rows // tile,), in_specs=[spec, spec], out_specs=spec,
        out_shape=jax.ShapeDtypeStruct((n_rows, width), F32),
        scratch_shapes=[pltpu.VMEM((1, width), F32)],
        compiler_params=_params(dimension_semantics=("arbitrary",)),
    )(a, u)


def lru_bwd(a, h, dh, *, tile, name):
    n_rows, width = a.shape
    n_tiles = n_rows // tile
    hb = tile // 8

    def body(a_ref, h_ref, hh_ref, dh_ref, da_ref, du_ref, carry):
        i = pl.program_id(0)

        @pl.when(i == 0)
        def _():
            carry[...] = jnp.zeros(carry.shape, F32)

        c = carry[...]
        for t in range(tile - 1, -1, -1):
            g = dh_ref[t:t + 1, :] + c
            du_ref[t:t + 1, :] = g
            if t:
                h_prev = h_ref[t - 1:t, :]
            else:
                h_prev = jnp.where(i == n_tiles - 1, 0.0, hh_ref[7:8, :])
            da_ref[t:t + 1, :] = g * h_prev
            c = a_ref[t:t + 1, :] * g
        carry[...] = c

    rev = lambda i: (n_tiles - 1 - i, 0)
    spec = pl.BlockSpec((tile, width), rev)
    halo = pl.BlockSpec((8, width), lambda i: (jnp.maximum((n_tiles - 1 - i) * hb - 1, 0), 0))
    return pl.pallas_call(
        body, name=name, grid=(n_tiles,), in_specs=[spec, spec, halo, spec], out_specs=[spec, spec],
        out_shape=[jax.ShapeDtypeStruct((n_rows, width), F32)] * 2,
        scratch_shapes=[pltpu.VMEM((1, width), F32)],
        compiler_params=_params(dimension_semantics=("arbitrary",)),
    )(a, h, h, dh)


def _layer_norm(z, g, b):
    mu = jnp.mean(z, axis=-1, keepdims=True)
    zc = z - mu
    var = jnp.mean(zc * zc, axis=-1, keepdims=True)
    return zc * lax.rsqrt(var + 1e-5) * g + b


def pre_fn(rows, params):
    (x,), (scale, shift) = rows, params
    return [x * (1.0 + scale) + shift], []


def make_post_fn(weight, with_next):
    def fn(rows, params):
        x, y = rows
        gate, g, b = params[:3]
        xo = _layer_norm(ALPHA * x + weight * (1.0 + gate) * y, g, b)
        if with_next:
            return [xo, xo * (1.0 + params[3]) + params[4]], []
        return [xo], []

    return fn


def make_last_fn(weight):
    post = make_post_fn(weight, False)

    def fn(rows, params):
        x, y, target = rows
        err = post([x, y], params)[0][0] - target
        loss = 0.5 * jnp.sum(jnp.mean(err * err, axis=-1, keepdims=True), axis=0, keepdims=True)
        return [err * (1.0 / D_MODEL)], [jnp.broadcast_to(loss, (1, 128))]

    return fn


def mix_a_fn(rows, params):
    a_q, a_f, xconv, graw = rows
    l0, l1, l2, wq, wk, gate_b = params
    mx = jnp.maximum(jnp.maximum(l0, l1), l2)
    e0, e1, e2 = jnp.exp(l0 - mx), jnp.exp(l1 - mx), jnp.exp(l2 - mx)
    lb = e0 / (e0 + e1 + e2)
    f = lb + (1.0 - lb) * jax.nn.sigmoid(a_f)
    xc = _silu(xconv)
    q_b = mm_nn(xc, wq)
    k_b = mm_nn(xc, wk) * (HEAD_W ** -0.5)
    g = graw + gate_b
    lane = lax.broadcasted_iota(jnp.int32, g.shape, 1)
    gates = jnp.where(lane < HEADS, g, _log_sigmoid(g))
    return [_silu(a_q), 1.0 - f, jnp.log(f), xc, q_b, k_b, gates], []


def _head_norm(v, g, center):
    if center:
        v = v - jnp.mean(v, axis=-1, keepdims=True)
    return v * lax.rsqrt(jnp.mean(v * v, axis=-1, keepdims=True) + 1e-6) * g


def mix_b_fn(rows, params):
    o_a, a_g, h_b, xc, b_z = rows
    hg, mg, skip = params
    y_a = [_head_norm(o_a[h], hg[h], False) * _silu(a_g[h]) for h in range(HEADS)]
    y_b = [(_head_norm(h_b[h], mg[h], True) + skip[h] * xc[h]) * _silu(b_z[h]) for h in range(HEADS)]
    return [y_a + y_b], []


def lru_a_fn(rows, params):
    (xr,) = rows
    wa, wx, ba, bx, lam = params
    a_out, u_out = [], []
    for n in range(C_BLOCKS):
        r = jax.nn.sigmoid(mm_nt(xr[n], wa[n]) + ba[n])
        i = jax.nn.sigmoid(mm_nt(xr[n], wx[n]) + bx[n])
        log_a = -RG_C * r * _softplus(-lam[n])
        a_out.append(jnp.exp(log_a))
        u_out.append(jnp.sqrt(_neg_expm1(2.0 * log_a)) * i * xr[n])
    return [a_out, u_out], []


def lru_b_fn(rows, params):
    h, y_br = rows
    return [h * jax.nn.gelu(y_br)], []


def swiglu_epi(accs, extras):
    h1, h3 = accs
    return [h1, h3, _silu(h1) * h3]


def swiglu_bwd_epi(accs, extras):
    (da,), (h1, h3) = accs, extras
    h1, h3 = h1.astype(F32), h3.astype(F32)
    sig = jax.nn.sigmoid(h1)
    return [da * h3 * sig * (1.0 + h1 * (1.0 - sig)), da * h1 * sig]


ROW_TILE = 256


def _chunks(v, n):
    return v.reshape(n, 1, v.shape[-1] // n)


def _dense_blocks(w):
    n, b, _ = w.shape
    by_row = jnp.swapaxes(w, 1, 2).reshape(n * b, b)
    spread = jnp.dot(by_row, _column_picker(n, b).T, precision=HIGHEST)
    return spread * _block_mask(n, b)


def _column_picker(n, b):
    return jnp.asarray(np.tile(np.eye(b, dtype=np.float32), (n, 1)))


def _block_mask(n, b):
    return jnp.asarray(np.kron(np.eye(n, dtype=np.float32), np.ones((b, b), np.float32)))


def _block_diag_of(m, n, b):
    by_row = jnp.dot(m * _block_mask(n, b), _column_picker(n, b), precision=HIGHEST)
    return jnp.swapaxes(by_row.reshape(n, b, b), 1, 2)


def local_step(x, target, mod, w, get_weights, put_grads, put_small, first_after=()):
    s_len = x.shape[0]
    tile = min(ROW_TILE, s_len)
    row = lambda v: v.reshape(1, -1)
    mrow = lambda l, j, k: mod[l, 3 * j + k].reshape(1, D_MODEL)
    g = {}
    d_mod = [[None] * 9 for _ in range(DEPTH)]
    d_ln_g = [[None] * 3 for _ in range(DEPTH)]
    d_ln_b = [[None] * 3 for _ in range(DEPTH)]
    subs = [(l, j) for l in range(DEPTH) for j in range(3)]
    weight_of = lambda j: 1.0 if j == 1 else FFN_RES_W

    wq_d = _dense_blocks(w["mlstm_wq"])
    wk_d = _dense_blocks(w["mlstm_wk"])
    wa_b, wx_b = w["rglru_wa"].astype(BF16), w["rglru_wx"].astype(BF16)
    gate_b = jnp.pad(w["mlstm_gate_b"].reshape(1, 8), ((0, 0), (0, 120)))
    lb_rows = [row(w["hgrn_lb_logits"][k]) for k in range(3)]
    mix_a_params = lb_rows + [wq_d, wk_d, gate_b]
    mix_b_params = [_chunks(row(w["hgrn_norm_g"]), HEADS), _chunks(row(w["mlstm_norm_g"]), HEADS),
                    _chunks(row(w["mlstm_skip"]), HEADS)]
    lru_a_params = [wa_b, wx_b, _chunks(row(w["rglru_ba"]), C_BLOCKS), _chunks(row(w["rglru_bx"]), C_BLOCKS),
                    _chunks(row(w["rglru_lambda"]), C_BLOCKS)]
    mconv_w, mconv_b = w["mlstm_conv_w"], row(w["mlstm_conv_b"])
    rconv_w, rconv_b = w["rglru_conv_w"], row(w["rglru_conv_b"])
    hg_piece = [(HEAD_W, SUB)] * 4
    ml_piece = [(HEAD_W, CHUNK)] * 3 + [(128, CHUNK)]

    (t,), _ = rowwise(pre_fn, [Rows(x)], [mrow(0, 0, 1), mrow(0, 0, 0)], [(D_MODEL, BF16)], [],
                      tile=tile, name="pre", after=tuple(first_after))
    saved = {}
    x_in = x
    for idx, (l, j) in enumerate(subs):
        sv = {"x": x_in, "t": t}
        if j != 1:
            w1, w3, w2 = get_weights(("ffn", l, j // 2), t)
            h1, h3, act = matmul([[(t, w1)], [(t, w3)]], "nn", [BF16, BF16, BF16], tm=512, tn=1408, tk=D_MODEL,
                                 epi=swiglu_epi, name="ffn_up", n_outer=True)
            (y,) = matmul([[(act, w2)]], "nn", [F32], tm=1024, tn=1024, tk=1408, name="ffn_down")
            sv.update(h1=h1, h3=h3, act=act, big=(w1, w3, w2))
        elif l == 0:
            ab_w_in, ab_w_out = get_weights(("ab",), t)
            sv["big"] = (ab_w_in, ab_w_out)
            (proj,) = matmul([[(t, ab_w_in)]], "nn", [F32], tm=256, tn=AB_ALL, tk=D_MODEL, name="ab_in")
            xconv = conv_fwd(Rows(proj, (4, MIX_W)), mconv_w, mconv_b, tile=tile, name="mconv")
            a_rows = [Rows(proj, (0, MIX_W)), Rows(proj, (1, MIX_W)), Rows(xconv), Rows(proj, (AB_MAIN // 128, 128))]
            (q_a, k_a, lf_a, xc, q_b, k_b, gates), _ = rowwise(
                mix_a_fn, a_rows, mix_a_params, [(MIX_W, F32)] * 6 + [(128, F32)], [], tile=tile, name="mix_a")
            hg_xs = [Rows(q_a), Rows(k_a), Rows(proj, (2, MIX_W)), Rows(lf_a)]
            (o_a,), hg_kept = chunk_scan_fwd(hgrn2_step, hg_xs, hg_piece, [(HEAD_W, HEAD_W)] * HEADS, [MIX_W],
                                             name="hgrn2_fwd")
            ml_xs = [Rows(q_b), Rows(k_b), Rows(proj, (5, MIX_W)), Rows(gates)]
            ml_states = [(HEAD_W, HEAD_W)] * HEADS + [(1, HEAD_W)] * (2 * HEADS)
            (h_b,), ml_kept = chunk_scan_fwd(mlstm_step, ml_xs, ml_piece, ml_states, [MIX_W], name="mlstm_fwd")
            b_rows = [Rows(o_a, split=HEAD_W), Rows(proj, (3, MIX_W), HEAD_W), Rows(h_b, split=HEAD_W),
                      Rows(xc, split=HEAD_W), Rows(proj, (6, MIX_W), HEAD_W)]
            (ycat,), _ = rowwise(mix_b_fn, b_rows, mix_b_params, [(2 * MIX_W, BF16)], [], tile=tile, name="mix_b")
            (y,) = matmul([[(ycat, ab_w_out)]], "nn", [F32], tm=512, tn=1024, tk=D_MODEL, name="ab_out")
            sv.update(proj=proj, xconv=xconv, a_rows=a_rows, hg_xs=hg_xs, hg_kept=hg_kept, ml_xs=ml_xs,
                      ml_kept=ml_kept, b_rows=b_rows, ycat=ycat)
        else:
            rg_w_in, rg_w_out = get_weights(("rg",), t)
            sv["big"] = (rg_w_in, rg_w_out)
            (proj,) = matmul([[(t, rg_w_in)]], "nn", [F32], tm=512, tn=1024, tk=D_MODEL, name="rg_in")
            xr = conv_fwd(Rows(proj, (1, D_MODEL)), rconv_w, rconv_b, tile=tile, name="rconv")
            (a_t, u_t), _ = rowwise(lru_a_fn, [Rows(xr, split=128)], lru_a_params, [(D_MODEL, F32)] * 2, [],
                                    tile=tile, name="lru_a")
            h = lru_fwd(a_t, u_t, tile=min(128, s_len), name="lru_fwd")
            b_rows = [Rows(h), Rows(proj, (0, D_MODEL))]
            (hgate,), _ = rowwise(lru_b_fn, b_rows, [], [(D_MODEL, BF16)], [], tile=tile, name="lru_b")
            (y,) = matmul([[(hgate, rg_w_out)]], "nn", [F32], tm=512, tn=1024, tk=D_MODEL, name="rg_out")
            sv.update(proj=proj, xr=xr, a_t=a_t, h=h, b_rows=b_rows, hgate=hgate)
        sv["y"] = y
        post_params = [mrow(l, j, 2), row(w["ln_g"][l, j]), row(w["ln_b"][l, j])]
        if idx + 1 < len(subs):
            nl, nj = subs[idx + 1]
            post_params += [mrow(nl, nj, 1), mrow(nl, nj, 0)]
            (x_out, t), _ = rowwise(make_post_fn(weight_of(j), True), [Rows(x_in), Rows(y)], post_params,
                                    [(D_MODEL, F32), (D_MODEL, BF16)], [], tile=tile, name="post")
        else:
            (d_xo,), (loss_row,) = rowwise(make_last_fn(weight_of(j)), [Rows(x_in), Rows(y), Rows(target)],
                                           post_params, [(D_MODEL, F32)], [(1, 128)], tile=tile, name="post_loss")
            x_out = None
        sv["post_params"] = post_params
        saved[(l, j)] = sv
        x_in = x_out
    loss = loss_row[0, 0]

    d_t_next = None
    sent = []
    deferred = []

    def hand_over(key, grads):
        token = put_grads(key, grads)
        if token is not None:
            sent.append(token)

    for idx in range(len(subs) - 1, -1, -1):
        l, j = subs[idx]
        sv = saved[(l, j)]
        has_next = idx + 1 < len(subs)
        cots = [Rows(d_xo)] + ([Rows(d_t_next)] if has_next else [])
        want_p = [0, 1, 2] + ([3, 4] if has_next else [])
        (d_xres, d_y), d_par = rowwise_bwd(
            make_post_fn(weight_of(j), has_next), [Rows(sv["x"]), Rows(sv["y"])], cots, sv["post_params"],
            [0, 1], want_p, [F32, BF16], tile=tile, name="post_bwd", after=tuple(sent))
        sent.clear()
        d_mod[l][3 * j + 2], d_ln_g[l][j], d_ln_b[l][j] = d_par[:3]
        if has_next:
            nl, nj = subs[idx + 1]
            d_mod[nl][3 * nj + 1], d_mod[nl][3 * nj] = d_par[3:]
        t = sv["t"]
        if j != 1:
            w1, w3, w2 = sv["big"]
            d_h1, d_h3 = matmul([[(d_y, w2)]], "nt", [BF16, BF16], tm=512, tn=1408, tk=D_MODEL,
                                extras=[sv["h1"], sv["h3"]], epi=swiglu_bwd_epi, name="ffn_down_bwd", n_outer=True)

            def ffn_weight_grads(after, ops=(sv["act"], t, d_y, d_h1, d_h3), key=("ffn", l, j // 2)):
                wait = () if after is None else (after,)
                act, t_in, d_out, d_1, d_3 = ops
                (g_w2,) = matmul([[(act, d_out)]], "tn", [GRAD_WIRE], tm=1408, tn=1024, tk=2048, name="ffn_dw2",
                                 after=wait)
                (g_w1,) = matmul([[(t_in, d_1)]], "tn", [GRAD_WIRE], tm=1024, tn=1408, tk=2048, name="ffn_dw1",
                                 after=wait)
                (g_w3,) = matmul([[(t_in, d_3)]], "tn", [GRAD_WIRE], tm=1024, tn=1408, tk=2048, name="ffn_dw3",
                                 after=wait)
                hand_over(key, [g_w1, g_w3, g_w2])

            if idx:
                ffn_weight_grads(None)
            else:
                deferred.append(ffn_weight_grads)
            (d_t,) = matmul([[(d_h1, w1), (d_h3, w3)]], "nt", [F32], tm=1024, tn=1024, tk=1408, name="ffn_up_bwd")
        elif l == 0:
            ab_w_in, ab_w_out = sv["big"]
            (d_ycat,) = matmul([[(d_y, ab_w_out)]], "nt", [F32], tm=512, tn=1024, tk=D_MODEL, name="ab_out_bwd")
            (g_out,) = matmul([[(sv["ycat"], d_y)]], "tn", [GRAD_WIRE], tm=1024, tn=1024, tk=2048, name="ab_dwout")
            (d_oa, d_ag, d_hb, d_xc, d_bz), (d_hg, d_mg, d_skip) = rowwise_bwd(
                mix_b_fn, sv["b_rows"], [Rows(d_ycat, split=HEAD_W)], mix_b_params, [0, 1, 2, 3, 4], [0, 1, 2],
                [F32, BF16, F32, F32, BF16], tile=tile, name="mix_b_bwd")
            g["hgrn_norm_g"], g["mlstm_norm_g"], g["mlstm_skip"] = (v.reshape(1, MIX_W) for v in (d_hg, d_mg, d_skip))
            d_qb, d_kb, d_bv, d_gates = chunk_scan_bwd(mlstm_step, sv["ml_xs"], ml_piece, sv["ml_kept"], [d_hb],
                                                       name="mlstm_bwd")
            d_qa, d_ka, d_ai, d_lf = chunk_scan_bwd(hgrn2_step, sv["hg_xs"], hg_piece, sv["hg_kept"], [d_oa],
                                                    name="hgrn2_bwd")
            a_cots = [Rows(v) for v in (d_qa, d_ka, d_lf, d_xc, d_qb, d_kb, d_gates)]
            (d_aq, d_af, d_xconv, d_graw), (d_l0, d_l1, d_l2, d_wq, d_wk, d_gb) = rowwise_bwd(
                mix_a_fn, sv["a_rows"], a_cots, mix_a_params, [0, 1, 2, 3], [0, 1, 2, 3, 4, 5],
                [BF16, BF16, F32, BF16], tile=tile, name="mix_a_bwd")
            g["hgrn_lb_logits"] = jnp.concatenate([d_l0, d_l1, d_l2], axis=0)
            g["mlstm_wq"] = _block_diag_of(d_wq, MIX_W // 4, 4)
            g["mlstm_wk"] = _block_diag_of(d_wk, MIX_W // 4, 4)
            g["mlstm_gate_b"] = d_gb[:, :8]
            d_bx, g["mlstm_conv_w"], g["mlstm_conv_b"] = conv_bwd(Rows(sv["proj"], (4, MIX_W)), d_xconv, mconv_w,
                                                                  tile=tile, name="mconv_bwd")
            d_proj = jnp.concatenate([d_aq, d_af, d_ai.astype(BF16), d_ag, d_bx, d_bv.astype(BF16), d_bz, d_graw],
                                     axis=1)
            (g_in,) = matmul([[(t, d_proj)]], "tn", [GRAD_WIRE], tm=256, tn=AB_ALL, tk=1024, name="ab_dwin")
            hand_over(("ab",), [g_in, g_out])
            (d_t,) = matmul([[(d_proj, ab_w_in)]], "nt", [F32], tm=512, tn=1024, tk=AB_ALL, name="ab_in_bwd")
        else:
            rg_w_in, rg_w_out = sv["big"]
            (d_hgate,) = matmul([[(d_y, rg_w_out)]], "nt", [F32], tm=512, tn=1024, tk=D_MODEL,
                                name="rg_out_bwd")
            (g_out,) = matmul([[(sv["hgate"], d_y)]], "tn", [GRAD_WIRE], tm=1024, tn=1024, tk=2048,
                                         name="rg_dwout")
            (d_h, d_ybr), _ = rowwise_bwd(lru_b_fn, sv["b_rows"], [Rows(d_hgate)], [], [0, 1], [], [F32, BF16],
                                          tile=tile, name="lru_b_bwd")
            d_a, d_u = lru_bwd(sv["a_t"], sv["h"], d_h, tile=min(128, s_len), name="lru_bwd")
            (d_xr,), (d_wa, d_wx, d_ba, d_bx_, d_lam) = rowwise_bwd(
                lru_a_fn, [Rows(sv["xr"], split=128)], [Rows(d_a, split=128), Rows(d_u, split=128)], lru_a_params,
                [0], [0, 1, 2, 3, 4], [F32], tile=tile, name="lru_a_bwd")
            g["rglru_wa"], g["rglru_wx"] = d_wa, d_wx
            g["rglru_ba"], g["rglru_bx"], g["rglru_lambda"] = (v.reshape(1, D_MODEL) for v in (d_ba, d_bx_, d_lam))
            d_xbr, g["rglru_conv_w"], g["rglru_conv_b"] = conv_bwd(Rows(sv["proj"], (1, D_MODEL)), d_xr, rconv_w,
                                                                   tile=tile, name="rconv_bwd")
            d_proj = jnp.concatenate([d_ybr, d_xbr], axis=1)
            (g_in,) = matmul([[(t, d_proj)]], "tn", [GRAD_WIRE], tm=1024, tn=1024, tk=2048, name="rg_dwin")
            hand_over(("rg",), [g_in, g_out])
            (d_t,) = matmul([[(d_proj, rg_w_in)]], "nt", [F32], tm=512, tn=1024, tk=1024, name="rg_in_bwd")
        d_xo, d_t_next = d_xres, d_t

    def first_bwd(rows, params):
        x0, d_res, d_t0 = rows
        _, vjp = jax.vjp(lambda r, p: pre_fn(r, p)[0], [x0], params)
        (d_x0,), d_p = vjp([d_t0])
        return [d_res + d_x0], d_p

    (grad_x,), (d_mod[0][1], d_mod[0][0]) = rowwise(
        first_bwd, [Rows(x), Rows(d_xo), Rows(d_t_next)], [mrow(0, 0, 1), mrow(0, 0, 0)], [(D_MODEL, F32)],
        [(1, D_MODEL)] * 2, tile=tile, name="pre_bwd", after=tuple(sent))
    sent.clear()
    g["ln_g"] = jnp.stack([jnp.concatenate(r, axis=0) for r in d_ln_g])
    g["ln_b"] = jnp.stack([jnp.concatenate(r, axis=0) for r in d_ln_b])
    d_mod = jnp.stack([jnp.concatenate(r, axis=0) for r in d_mod])
    small_sent = put_small(g, d_mod)
    for weight_grads in deferred:
        weight_grads(small_sent)
    return loss, grad_x, d_mod, g, list(sent)


MESH_ID = pl.DeviceIdType.MESH
ANY_SPEC = pl.BlockSpec(memory_space=pl.ANY)


def _my_position():
    return lax.axis_index("x"), lax.axis_index("y"), lax.axis_index("c")


def _flat_index(pos):
    return 4 * pos[0] + 2 * pos[1] + pos[2]


def _peer_position(pos, k):
    return tuple(lax.rem(p + ((k >> s) & 1), 2) for p, s in zip(pos, (2, 1, 0)))


def _exchange(x, gather, name):
    out_shape = (NDEV,) + x.shape if gather else x.shape

    def body(x_ref, o_ref, send_sems, recv_sems, local_sem):
        pos = _my_position()
        me = _flat_index(pos)
        local = pltpu.make_async_copy(x_ref if gather else x_ref.at[me], o_ref.at[me], local_sem)
        local.start()
        copies = []
        for k in range(1, NDEV):
            peer = _peer_position(pos, k)
            src = x_ref if gather else x_ref.at[_flat_index(peer)]
            copies.append(pltpu.make_async_remote_copy(
                src_ref=src, dst_ref=o_ref.at[me], send_sem=send_sems.at[k - 1], recv_sem=recv_sems.at[k - 1],
                device_id=peer, device_id_type=MESH_ID))
            copies[-1].start()
        for cp in copies:
            cp.wait()
        local.wait()

    return pl.pallas_call(
        body, name=name, in_specs=[ANY_SPEC], out_specs=ANY_SPEC,
        out_shape=jax.ShapeDtypeStruct(out_shape, x.dtype),
        scratch_shapes=[pltpu.SemaphoreType.DMA((NDEV - 1,)), pltpu.SemaphoreType.DMA((NDEV - 1,)),
                        pltpu.SemaphoreType.DMA],
    )(x)


HBM_SPEC = pl.BlockSpec(memory_space=pltpu.HBM)
SEM_SPEC = pl.BlockSpec(memory_space=pltpu.SEMAPHORE)
SIDE_EFFECT = pltpu.SideEffectType.DATAFLOW_SIDE_EFFECTING


def _exchange_copies(x_refs, land_refs, send_sems, recv_sems, gather):
    pos = _my_position()
    me = _flat_index(pos)
    copies = []
    for k in range(1, NDEV):
        peer = _peer_position(pos, k)
        for x_ref, land_ref, s_sem, r_sem in zip(x_refs, land_refs, send_sems, recv_sems):
            src = x_ref if gather else x_ref.at[_flat_index(peer)]
            copies.append(pltpu.make_async_remote_copy(src_ref=src, dst_ref=land_ref.at[me], send_sem=s_sem,
                                                       recv_sem=r_sem, device_id=peer, device_id_type=MESH_ID))
    return copies


def exchange_start(xs, gather, name):
    n = len(xs)
    land_shapes = [(NDEV,) + x.shape if gather else x.shape for x in xs]

    def body(*refs):
        x_refs, land_refs = refs[:n], refs[n:2 * n]
        send_sems, recv_sems = refs[2 * n:3 * n], refs[3 * n:4 * n]
        token = refs[-1]
        for cp in _exchange_copies(x_refs, land_refs, send_sems, recv_sems, gather):
            cp.start()
        token[...] = jnp.zeros(token.shape, token.dtype)

    sem = pltpu.SemaphoreType.DMA(())
    res = pl.pallas_call(
        body, name=name,
        out_shape=[sem] * (2 * n) + [pltpu.HBM(x.shape, x.dtype) for x in xs]
        + [pltpu.HBM(s, x.dtype) for s, x in zip(land_shapes, xs)] + [jax.ShapeDtypeStruct((8, 128), F32)],
        in_specs=[HBM_SPEC] * (2 * n),
        out_specs=[SEM_SPEC] * (2 * n) + [HBM_SPEC] * (2 * n) + [pl.BlockSpec(memory_space=pltpu.VMEM)],
        input_output_aliases={i: 2 * n + i for i in range(2 * n)},
        compiler_params=pltpu.CompilerParams(has_side_effects=SIDE_EFFECT),
    )(*[pltpu.with_memory_space_constraint(x, pltpu.HBM) for x in xs],
      *[pltpu.with_memory_space_constraint(lax.empty(s, x.dtype), pltpu.HBM) for s, x in zip(land_shapes, xs)])
    return (res[:n], res[n:2 * n], res[2 * n:3 * n], res[3 * n:4 * n]), res[-1]


def exchange_wait(handles, after, name):
    send_sems, recv_sems, x_thru, land_thru = handles
    n = len(x_thru)
    after = jax.tree_util.tree_leaves(after)

    def body(*refs):
        land_refs = refs[n:2 * n]
        s_sems, r_sems = refs[2 * n:3 * n], refs[3 * n:4 * n]
        pos = _my_position()
        for land_ref, s_sem, r_sem in zip(land_refs, s_sems, r_sems):
            seven = land_ref.at[pl.ds(0, NDEV - 1)]
            all_seven = pltpu.make_async_remote_copy(src_ref=seven, dst_ref=seven, send_sem=s_sem, recv_sem=r_sem,
                                                     device_id=pos, device_id_type=MESH_ID)
            all_seven.wait_send()
            all_seven.wait_recv()

    res = pl.pallas_call(
        body, name=name,
        out_shape=[pltpu.HBM(x.shape, x.dtype) for x in x_thru] + [pltpu.HBM(x.shape, x.dtype) for x in land_thru],
        in_specs=[HBM_SPEC] * (2 * n) + [SEM_SPEC] * (2 * n) + [ANY_SPEC] * len(after),
        out_specs=[HBM_SPEC] * (2 * n),
        input_output_aliases={i: i for i in range(2 * n)},
        compiler_params=pltpu.CompilerParams(has_side_effects=SIDE_EFFECT),
    )(*x_thru, *land_thru, *send_sems, *recv_sems, *after)
    return res[:n], res[n:]


def all_gather(x, name):
    return _exchange(x, True, name)


def all_to_all(x, name):
    return _exchange(x, False, name)


def _row_tile(n_rows, cap):
    best = None
    for t in range(8, min(n_rows, cap) + 1, 8):
        if n_rows % t == 0:
            best = t
    return best if best else n_rows


def adamw(w, m, v, slots, *, name, index=(), prev=None):
    n_rows, width = w.shape[-2:]
    n_lead = w.ndim - 2
    assert len(index) == n_lead
    n_slots = slots.shape[0]
    lanes = -(-width // 128) * 128
    tile = _row_tile(n_rows, max(8, (1 << 20) // (4 * lanes) // 8 * 8))
    bc1 = 1.0 - ADAM_B1 ** ADAM_STEP
    bc2 = 1.0 - ADAM_B2 ** ADAM_STEP

    def body(w_ref, m_ref, v_ref, s_ref, *rest):
        g_ref, d_ref, nm_ref, nv_ref = rest[-4:]
        g = s_ref[0].astype(F32)
        for k in range(1, n_slots):
            g = g + s_ref[k].astype(F32)
        wv = w_ref[...]
        nm = ADAM_B1 * m_ref[...] + (1.0 - ADAM_B1) * g
        nv = ADAM_B2 * v_ref[...] + (1.0 - ADAM_B2) * (g * g)
        g_ref[...] = g
        nm_ref[...] = nm
        nv_ref[...] = nv
        d_ref[...] = -ADAM_LR * ((nm / bc1) / (jnp.sqrt(nv / bc2) + ADAM_EPS) + ADAM_WD * wv)

    spec = pl.BlockSpec((None,) * n_lead + (tile, width), lambda i: tuple(index) + (i, 0))
    prev = list(prev) if prev is not None else []
    return pl.pallas_call(
        body, name=name, grid=(n_rows // tile,),
        in_specs=[spec, spec, spec, pl.BlockSpec((n_slots, tile, width), lambda i: (0, i, 0))]
        + [ANY_SPEC] * len(prev),
        out_specs=[spec] * 4, out_shape=[jax.ShapeDtypeStruct(w.shape, F32)] * 4,
        input_output_aliases={4 + k: k for k in range(len(prev))},
        compiler_params=_params(dimension_semantics=("parallel",)),
    )(w, m, v, slots, *prev)


def sum_slots(slots, *, name):
    n_slots, n_rows, width = slots.shape

    def body(s_ref, o_ref):
        @pl.when(pl.program_id(0) == 0)
        def _():
            o_ref[...] = s_ref[...]

        @pl.when(pl.program_id(0) > 0)
        def _():
            o_ref[...] += s_ref[...]

    return pl.pallas_call(
        body, name=name, grid=(n_slots,),
        in_specs=[pl.BlockSpec((None, n_rows, width), lambda k: (k, 0, 0))],
        out_specs=pl.BlockSpec((n_rows, width), lambda k: (0, 0)),
        out_shape=jax.ShapeDtypeStruct((n_rows, width), F32),
        compiler_params=_params(dimension_semantics=("arbitrary",)),
    )(slots)


def adamw_nd(w, m, v, slots, *, name):
    shp = w.shape
    two = (-1, shp[-1])
    res = adamw(w.reshape(two), m.reshape(two), v.reshape(two), slots.reshape((slots.shape[0],) + (w.size // shp[-1], shp[-1])),
                name=name)
    return [r.reshape(shp) for r in res]


def _pack(arrs):
    parts = []
    for a in arrs:
        flat = a.reshape(-1).astype(F32)
        parts.append(jnp.pad(flat, (0, (-flat.shape[0]) % 1024)))
    return jnp.concatenate(parts).reshape(-1, 128)


def _unpack(buf, shapes):
    outs, at = [], 0
    lead = buf.shape[:-2]
    flat = buf.reshape(lead + (-1,))
    for shp in shapes:
        n = int(np.prod(shp))
        outs.append(flat[..., at:at + n].reshape(lead + tuple(shp)))
        at += n + (-n) % 1024
    return outs


ARG_NAMES = ["x", "c", "ada_w", "ada_b", "ln_g", "ln_b", "ffn_w1", "ffn_w3", "ffn_w2", "hgrn_lb_logits", "ab_w_in",
             "ab_w_out", "hgrn_norm_g", "mlstm_conv_w", "mlstm_conv_b", "mlstm_wq", "mlstm_wk", "mlstm_gate_b",
             "mlstm_skip", "mlstm_norm_g", "rglru_w_in", "rglru_conv_w", "rglru_conv_b", "rglru_wa", "rglru_ba",
             "rglru_wx", "rglru_bx", "rglru_lambda", "rglru_w_out", "loss_target"]
WEIGHTS = ARG_NAMES[2:-1]
BIG = ["ffn_w1", "ffn_w3", "ffn_w2", "ab_w_in", "ab_w_out", "rglru_w_in", "rglru_w_out"]
REPLICATED = ["ada_b", "hgrn_lb_logits", "hgrn_norm_g", "mlstm_conv_b", "mlstm_wq", "mlstm_wk", "mlstm_gate_b",
              "mlstm_skip", "mlstm_norm_g", "rglru_wa", "rglru_wx"]
SHARDED_SMALL = ["ln_g", "ln_b", "mlstm_conv_w", "rglru_conv_w", "rglru_conv_b", "rglru_ba", "rglru_bx", "rglru_lambda"]


def _unshard_last(gathered):
    moved = jnp.moveaxis(gathered, 0, -2)
    return moved.reshape(moved.shape[:-2] + (NDEV * moved.shape[-1],))


def _shard_last(full):
    split = full.reshape(full.shape[:-1] + (NDEV, full.shape[-1] // NDEV))
    return jnp.moveaxis(split, -2, 0)


def kernel(x, c, ada_w, ada_b, ln_g, ln_b, ffn_w1, ffn_w3, ffn_w2, hgrn_lb_logits, ab_w_in, ab_w_out, hgrn_norm_g, mlstm_conv_w, mlstm_conv_b, mlstm_wq, mlstm_wk, mlstm_gate_b, mlstm_skip, mlstm_norm_g, rglru_w_in, rglru_conv_w, rglru_conv_b, rglru_wa, rglru_ba, rglru_wx, rglru_bx, rglru_lambda, rglru_w_out, loss_target, m_ada_w, m_ada_b, m_ln_g, m_ln_b, m_ffn_w1, m_ffn_w3, m_ffn_w2, m_hgrn_lb_logits, m_ab_w_in, m_ab_w_out, m_hgrn_norm_g, m_mlstm_conv_w, m_mlstm_conv_b, m_mlstm_wq, m_mlstm_wk, m_mlstm_gate_b, m_mlstm_skip, m_mlstm_norm_g, m_rglru_w_in, m_rglru_conv_w, m_rglru_conv_b, m_rglru_wa, m_rglru_ba, m_rglru_wx, m_rglru_bx, m_rglru_lambda, m_rglru_w_out, v_ada_w, v_ada_b, v_ln_g, v_ln_b, v_ffn_w1, v_ffn_w3, v_ffn_w2, v_hgrn_lb_logits, v_ab_w_in, v_ab_w_out, v_hgrn_norm_g, v_mlstm_conv_w, v_mlstm_conv_b, v_mlstm_wq, v_mlstm_wk, v_mlstm_gate_b, v_mlstm_skip, v_mlstm_norm_g, v_rglru_w_in, v_rglru_conv_w, v_rglru_conv_b, v_rglru_wa, v_rglru_ba, v_rglru_wx, v_rglru_bx, v_rglru_lambda, v_rglru_w_out):
    args = locals()
    p = {n: args[n] for n in ARG_NAMES}
    mom = {n: (args["m_" + n], args["v_" + n]) for n in WEIGHTS}
    me = _flat_index(_my_position())

    keys = [("ffn", 0, 0), ("ab",), ("ffn", 0, 1), ("ffn", 1, 0), ("rg",), ("ffn", 1, 1)]
    names = {("ab",): ("ab_w_in", "ab_w_out"), ("rg",): ("rglru_w_in", "rglru_w_out")}
    for l in range(DEPTH):
        for i in range(2):
            names[("ffn", l, i)] = ("ffn_w1", "ffn_w3", "ffn_w2")

    def part(key):
        return (lambda a: a[key[1], key[2]]) if key[0] == "ffn" else (lambda a: a[0])

    gather_handles = {}

    def landed(handles, own_of, after, name):
        sources, lands = exchange_wait(handles, after, name)
        return [lax.dynamic_update_index_in_dim(ld, own_of(src), me, 0) for src, ld in zip(sources, lands)]

    starts_next = {("ffn", 0, 0): [("ab",)], ("ab",): [("ffn", 0, 1), ("ffn", 1, 0)],
                   ("ffn", 0, 1): [("rg",), ("ffn", 1, 1)]}

    def start_gather(key, after):
        shards = [part(key)(p[n]).astype(BF16) for n in names[key]]
        if after is not None:
            shards, _ = lax.optimization_barrier((shards, after))
        gather_handles[key], token = exchange_start(shards, True, "gather_start_" + "_".join(map(str, key)))
        return token

    def get_weights(key, after):
        got = landed(gather_handles[key], lambda src: src, after, "gather_wait_" + "_".join(map(str, key)))
        tokens = [start_gather(nxt, got) for nxt in starts_next.get(key, [])]
        if tokens:
            got, _ = lax.optimization_barrier((got, tokens))
        if key[0] == "ffn":
            return _unshard_last(got[0]), _unshard_last(got[1]), got[2].reshape(D_FF, D_MODEL)
        w_in = _unshard_last(got[0])
        if key[0] == "ab":
            w_in = jnp.concatenate([w_in[:, :AB_MAIN], jnp.pad(w_in[:, AB_MAIN:], ((0, 0), (0, 120)))], axis=1)
        return w_in, got[1].reshape(D_MODEL, D_MODEL)

    scatter_handles = {}

    def put_grads(key, grads):
        if key[0] == "ffn":
            slots = [_shard_last(grads[0]), _shard_last(grads[1]), grads[2].reshape(NDEV, D_FF // NDEV, D_MODEL)]
        else:
            g_in = grads[0][:, :AB_MAIN + 8] if key[0] == "ab" else grads[0]
            slots = [_shard_last(g_in), grads[1].reshape(NDEV, D_MODEL // NDEV, D_MODEL)]
        scatter_handles[key], token = exchange_start(slots, False, "scatter_start_" + "_".join(map(str, key)))
        return token

    sharded_shapes = [p[n].shape for n in SHARDED_SMALL]
    small = all_gather(_pack([p[n] for n in SHARDED_SMALL] + [c]), "gather_small")
    started = [start_gather(keys[0], small)]
    small, _ = lax.optimization_barrier((small, started))
    per_dev = _unpack(small, sharded_shapes + [c.shape])
    full_small = {n: _unshard_last(per_dev[i]) for i, n in enumerate(SHARDED_SMALL)}
    c_all = per_dev[-1].reshape(NDEV, D_MODEL)

    c16 = jnp.pad(c_all, ((0, 8), (0, 0)))
    (c_act,), _ = rowwise(lambda r, q: ([_silu(r[0])], []), [Rows(c16)], [], [(D_MODEL, BF16)], [], tile=16,
                          name="cond_act")
    n_ada = ada_w.shape[-1]
    ada_b_mine = lax.dynamic_slice_in_dim(ada_b, me * n_ada, n_ada, axis=1)
    ada_cols = []
    for l in range(DEPTH):
        bias = jnp.broadcast_to(ada_b_mine[l][None, :], (16, n_ada))
        (cols,) = matmul([[(c_act, ada_w[l])]], "nn", [F32], tm=16, tn=n_ada, tk=D_MODEL, extras=[bias],
                         epi=lambda accs, ex: [accs[0] + ex[0]], name="ada_fwd")
        ada_cols.append(cols[:8])
    ada_mine = all_to_all(jnp.stack(ada_cols, axis=1), "ada_to_owner")
    mod = jnp.moveaxis(ada_mine, 0, 1).reshape(DEPTH, 9, D_MODEL)


    w = {"ln_g": full_small["ln_g"], "ln_b": full_small["ln_b"], "hgrn_lb_logits": hgrn_lb_logits}
    for n in ("hgrn_norm_g", "mlstm_conv_b", "mlstm_wq", "mlstm_wk", "mlstm_gate_b", "mlstm_skip", "mlstm_norm_g",
              "rglru_wa", "rglru_wx"):
        w[n] = p[n][0]
    for n in ("mlstm_conv_w", "rglru_conv_w", "rglru_conv_b", "rglru_ba", "rglru_bx", "rglru_lambda"):
        w[n] = full_small[n][0]

    small_names = REPLICATED + SHARDED_SMALL
    small_handles = []

    def put_small(g_small_parts, d_modulation):
        parts_ = dict(g_small_parts, ada_b=d_modulation.reshape(DEPTH, 9 * D_MODEL))
        handles, token = exchange_start([_pack([parts_[n] for n in small_names])], True, "gather_start_small_grads")
        small_handles.append(handles)
        return token

    loss, grad_x, d_mod, g, last_sent = local_step(x[0], loss_target[0], mod, w, get_weights, put_grads, put_small,
                                                   started)
    loss = lax.psum(loss, ("x", "y", "c"))

    outs = {}
    def update_group(key, after):
        slots = landed(scatter_handles[key], lambda src: lax.dynamic_index_in_dim(src, me, 0, keepdims=False), after,
                       "scatter_wait_" + "_".join(map(str, key)))
        for n, sl in zip(names[key], slots):
            index = key[1:] if key[0] == "ffn" else (0,)
            outs[n] = adamw(p[n], mom[n][0], mom[n][1], sl, name="adamw_" + n, index=index, prev=outs.get(n))
        return outs[names[key][-1]][0]

    full_shapes = [p[n].shape for n in REPLICATED] + [full_small[n].shape for n in SHARDED_SMALL]
    (all_small,) = landed(small_handles[0], lambda src: src, (last_sent, grad_x), "gather_wait_small_grads")
    summed = sum_slots(all_small, name="sum_small_grads")
    g_small = dict(zip(small_names, _unpack(summed, full_shapes)))
    rep = adamw(*[_pack([t[n] for n in REPLICATED]) for t in (p, {n: mom[n][0] for n in WEIGHTS},
                                                               {n: mom[n][1] for n in WEIGHTS})],
                _pack([g_small[n] for n in REPLICATED])[None], name="adamw_replicated")
    rep = [_unpack(r, [p[n].shape for n in REPLICATED]) for r in rep]
    for i, n in enumerate(REPLICATED):
        outs[n] = [r[i] for r in rep]
    g_mine = {n: lax.dynamic_slice_in_dim(g_small[n], me * p[n].shape[-1], p[n].shape[-1], axis=-1)
              for n in SHARDED_SMALL}
    shd = adamw(*[_pack([t[n] for n in SHARDED_SMALL]) for t in (p, {n: mom[n][0] for n in WEIGHTS},
                                                                  {n: mom[n][1] for n in WEIGHTS})],
                _pack([g_mine[n] for n in SHARDED_SMALL])[None], name="adamw_sharded_small")
    done = shd[0]
    for key in keys[:0:-1] + keys[:1]:
        done = update_group(key, done)
    shd = [_unpack(r, sharded_shapes) for r in shd]
    for i, n in enumerate(SHARDED_SMALL):
        outs[n] = [r[i] for r in shd]

    d_ada = all_small[:, :DEPTH * 9 * D_MODEL // 128].reshape(NDEV, DEPTH, 9 * D_MODEL)
    d_mine = lax.dynamic_slice_in_dim(d_ada, me * n_ada, n_ada, axis=2)
    g_ada = []
    for l in range(DEPTH):
        d16 = jnp.pad(d_mine[:, l], ((0, 8), (0, 0)))
        (gl,) = matmul([[(c_act, d16)]], "tn", [F32], tm=D_MODEL, tn=n_ada, tk=16, name="ada_bwd")
        g_ada.append(gl)
    outs["ada_w"] = adamw_nd(ada_w, *mom["ada_w"], jnp.stack(g_ada)[None], name="adamw_ada_w")

    result = [loss, grad_x[None]]
    for k in range(4):
        result += [outs[n][k].reshape(p[n].shape) for n in WEIGHTS]
    return tuple(result)
```

```python
import functools

import jax
import jax.numpy as jnp
import numpy as np
from jax import lax
from jax.experimental import pallas as pl
from jax.experimental.pallas import tpu as pltpu

F32 = jnp.float32
BF16 = jnp.bfloat16
HIGHEST = lax.Precision.HIGHEST

NDEV = 8
D_MODEL = 1024
D_FF = 2816
DEPTH = 2
CHUNK = 64
SUB = 16
HEADS = 4
HEAD_W = 128
MIX_W = HEADS * HEAD_W
AB_MAIN = 7 * MIX_W
AB_ALL = AB_MAIN + 128
CONV_W = 4
C_BLOCKS = 8
RG_C = 8.0
ALPHA = (2 * DEPTH) ** 0.25
FFN_RES_W = 0.5
NEG = -1e30

ADAM_LR = 0.001
ADAM_B1 = 0.9
ADAM_B2 = 0.999
ADAM_EPS = 1e-08
ADAM_WD = 0.01
ADAM_STEP = 10

VMEM_LIMIT = 56 * 1024 * 1024
GRAD_WIRE = jnp.bfloat16

NN = ((1,), (0,))
NT = ((1,), (1,))
TN = ((0,), (0,))


def _dot(a, b, dims, precision=None):
    return lax.dot_general(a, b, (dims, ((), ())), precision=precision, preferred_element_type=F32)


def _make_mm(dims, d_lhs, d_rhs, swap_lhs, swap_rhs, prec):
    def cast(v):
        return v.astype(BF16) if prec is None else v.astype(F32)

    @jax.custom_vjp
    def mm(a, b):
        return _dot(cast(a), cast(b), dims, prec)

    def fwd(a, b):
        return mm(a, b), (a, b)

    def bwd(res, g):
        a, b = res
        g = cast(g)
        da = _dot(cast(b), g, d_lhs, prec) if swap_lhs else _dot(g, cast(b), d_lhs, prec)
        db = _dot(g, cast(a), d_rhs, prec) if swap_rhs else _dot(cast(a), g, d_rhs, prec)
        return da.astype(a.dtype), db.astype(b.dtype)

    mm.defvjp(fwd, bwd)
    return mm


def _mm_family(prec):
    return (_make_mm(NN, NT, TN, False, False, prec), _make_mm(NT, NN, TN, False, True, prec),
            _make_mm(TN, NT, NN, True, False, prec))


def _round(v):
    return v.astype(BF16).astype(F32)


@jax.custom_vjp
def row_dot(a, n):
    return jnp.sum(_round(a) * _round(n), axis=1, keepdims=True)


def _row_dot_fwd(a, n):
    return row_dot(a, n), (a, n)


def _row_dot_bwd(res, g):
    a, n = res
    g = _round(g)
    return g * _round(n), jnp.sum(g * _round(a), axis=0, keepdims=True)


row_dot.defvjp(_row_dot_fwd, _row_dot_bwd)


@jax.custom_vjp
def col_dot(s, a):
    return jnp.sum(_round(s) * _round(a), axis=0, keepdims=True)


def _col_dot_fwd(s, a):
    return col_dot(s, a), (s, a)


def _col_dot_bwd(res, g):
    s, a = res
    g = _round(g)
    return jnp.sum(g * _round(a), axis=1, keepdims=True), _round(s) * g


col_dot.defvjp(_col_dot_fwd, _col_dot_bwd)

mm_nn, mm_nt, mm_tn = _mm_family(None)
mid_nn, mid_nt, mid_tn = _mm_family(lax.Precision.HIGH)
hi_nn, hi_nt, hi_tn = _mm_family(HIGHEST)


def _silu(v):
    return v * jax.nn.sigmoid(v)


def _log_sigmoid(v):
    return jnp.minimum(v, 0.0) - jnp.log1p(jnp.exp(-jnp.abs(v)))


def _softplus(v):
    return jnp.maximum(v, 0.0) + jnp.log1p(jnp.exp(-jnp.abs(v)))


def _neg_expm1(v):
    series = -v * (1.0 + v * (0.5 + v * (1.0 / 6.0 + v * (1.0 / 24.0 + v * (1.0 / 120.0)))))
    return jnp.where(v > -0.05, series, 1.0 - jnp.exp(v))


def _params(**kw):
    return pltpu.CompilerParams(vmem_limit_bytes=VMEM_LIMIT, **kw)


class Rows:
    def __init__(self, arr, block=None, split=None):
        self.arr = arr
        self.block = block
        self.split = split

    @property
    def width(self):
        return self.block[1] if self.block else self.arr.shape[1]


def _load(ref, split):
    if ref.ndim == 3:
        return [ref[k].astype(F32) for k in range(ref.shape[0])]
    if split is None:
        return ref[...].astype(F32)
    return [ref[:, k * split:(k + 1) * split].astype(F32) for k in range(ref.shape[1] // split)]


def _store(ref, val, accumulate=False):
    if isinstance(val, (list, tuple)):
        if ref.ndim == 3:
            for k, v in enumerate(val):
                ref[k] = (ref[k] + v if accumulate else v).astype(ref.dtype)
            return
        w = ref.shape[1] // len(val)
        for k, v in enumerate(val):
            sl = slice(k * w, (k + 1) * w)
            ref[:, sl] = (ref[:, sl] + v if accumulate else v).astype(ref.dtype)
    else:
        ref[...] = (ref[...] + val if accumulate else val).astype(ref.dtype)


def rowwise(fn, rows, params, out_rows, out_sums, *, tile, name, after=()):
    n_rows = rows[0].arr.shape[0]
    n_r, n_p, n_o = len(rows), len(params), len(out_rows)
    n_in = n_r + n_p + len(after)
    splits = [r.split for r in rows]

    def body(*refs):
        r_refs, p_refs = refs[:n_r], refs[n_r:n_r + n_p]
        o_refs, s_refs = refs[n_in:n_in + n_o], refs[n_in + n_o:]
        row_out, sum_out = fn([_load(r, s) for r, s in zip(r_refs, splits)], [_load(p, None) for p in p_refs])
        for ref, val in zip(o_refs, row_out):
            _store(ref, val)
        if s_refs:
            @pl.when(pl.program_id(0) == 0)
            def _():
                for ref in s_refs:
                    ref[...] = jnp.zeros(ref.shape, ref.dtype)

            for ref, val in zip(s_refs, sum_out):
                _store(ref, val, accumulate=True)

    in_specs = []
    for r in rows:
        blk = r.block[0] if r.block else 0
        in_specs.append(pl.BlockSpec((tile, r.width), functools.partial(lambda i, b: (i, b), b=blk)))
    for p in params:
        in_specs.append(pl.BlockSpec(p.shape, functools.partial(lambda i, n: (0,) * n, n=p.ndim)))
    in_specs += [pl.BlockSpec(memory_space=pl.ANY)] * len(after)
    out_shape = [jax.ShapeDtypeStruct((n_rows, w), dt) for w, dt in out_rows]
    out_specs = [pl.BlockSpec((tile, w), lambda i: (i, 0)) for w, _ in out_rows]
    for shp in out_sums:
        out_shape.append(jax.ShapeDtypeStruct(shp, F32))
        out_specs.append(pl.BlockSpec(shp, functools.partial(lambda i, n: (0,) * n, n=len(shp))))
    res = pl.pallas_call(
        body, name=name, grid=(n_rows // tile,), in_specs=in_specs, out_specs=out_specs, out_shape=out_shape,
        compiler_params=_params(dimension_semantics=("arbitrary",)),
    )(*[r.arr for r in rows], *params, *after)
    return res[:n_o], res[n_o:]


def rowwise_bwd(fn, rows, cots, params, want_rows, want_params, out_dtypes, *, tile, name, after=()):
    n = len(rows)

    def bwd(row_vals, param_vals):
        prim, cot = row_vals[:n], row_vals[n:]
        _, vjp = jax.vjp(lambda r, p: fn(r, p)[0], prim, param_vals)
        d_rows, d_params = vjp(cot)
        return [d_rows[i] for i in want_rows], [d_params[j] for j in want_params]

    out_rows = [(rows[i].width, dt) for i, dt in zip(want_rows, out_dtypes)]
    out_sums = [params[j].shape for j in want_params]
    return rowwise(bwd, list(rows) + list(cots), params, out_rows, out_sums, tile=tile, name=name, after=after)


def matmul(groups, mode, outs, *, tm, tn, tk, extras=(), epi=None, name, n_outer=False, after=()):
    a0, b0 = groups[0][0]
    if mode == "tn":
        k_dim, m_dim = a0.shape
    else:
        m_dim, k_dim = a0.shape
    n_dim = b0.shape[0] if mode == "nt" else b0.shape[1]
    tm, tn, tk = min(tm, m_dim), min(tn, n_dim), min(tk, k_dim)
    assert m_dim % tm == 0 and n_dim % tn == 0 and k_dim % tk == 0, (name, m_dim, n_dim, k_dim)
    nk = k_dim // tk
    pairs = [p for g in groups for p in g]
    n_pairs, n_groups, n_ex, n_out = len(pairs), len(groups), len(extras), len(outs)
    dims = {"nn": NN, "nt": NT, "tn": TN}[mode]

    def body(*refs):
        ab = refs[:2 * n_pairs]
        ex = refs[2 * n_pairs:2 * n_pairs + n_ex]
        n_in = 2 * n_pairs + n_ex + len(after)
        o_refs = refs[n_in:n_in + n_out]
        accs = refs[n_in + n_out:]

        def partial_sums():
            sums, p = [], 0
            for g in groups:
                tot = None
                for _ in g:
                    d = _dot(ab[2 * p][...].astype(BF16), ab[2 * p + 1][...].astype(BF16), dims)
                    tot = d if tot is None else tot + d
                    p += 1
                sums.append(tot)
            return sums

        def finish(vals):
            res = epi(vals, [e[...] for e in ex]) if epi else vals
            for ref, v in zip(o_refs, res):
                ref[...] = v.astype(ref.dtype)

        if nk == 1:
            finish(partial_sums())
        else:
            k = pl.program_id(2)

            @pl.when(k == 0)
            def _():
                for acc in accs:
                    acc[...] = jnp.zeros(acc.shape, F32)

            for acc, s in zip(accs, partial_sums()):
                acc[...] += s

            @pl.when(k == nk - 1)
            def _():
                finish([acc[...] for acc in accs])

    def at(pick):
        return (lambda j, i, k: pick(i, j, k)) if n_outer else pick

    if mode == "nn":
        a_spec = pl.BlockSpec((tm, tk), at(lambda i, j, k: (i, k)))
        b_spec = pl.BlockSpec((tk, tn), at(lambda i, j, k: (k, j)))
    elif mode == "nt":
        a_spec = pl.BlockSpec((tm, tk), at(lambda i, j, k: (i, k)))
        b_spec = pl.BlockSpec((tn, tk), at(lambda i, j, k: (j, k)))
    else:
        a_spec = pl.BlockSpec((tk, tm), at(lambda i, j, k: (k, i)))
        b_spec = pl.BlockSpec((tk, tn), at(lambda i, j, k: (k, j)))
    mn_spec = pl.BlockSpec((tm, tn), at(lambda i, j, k: (i, j)))
    grid = (n_dim // tn, m_dim // tm, nk) if n_outer else (m_dim // tm, n_dim // tn, nk)
    return pl.pallas_call(
        body, name=name, grid=grid,
        in_specs=[a_spec, b_spec] * n_pairs + [mn_spec] * n_ex + [pl.BlockSpec(memory_space=pl.ANY)] * len(after),
        out_specs=[mn_spec] * n_out,
        out_shape=[jax.ShapeDtypeStruct((m_dim, n_dim), dt) for dt in outs],
        scratch_shapes=[pltpu.VMEM((tm, tn), F32)] * (n_groups if nk > 1 else 0),
        compiler_params=_params(dimension_semantics=("parallel", "parallel", "arbitrary")),
    )(*[x for p in pairs for x in p], *extras, *after)


def conv_fwd(x, w, b, *, tile, name):
    n_rows, width = x.arr.shape[0], x.width
    blk = x.block[0] if x.block else 0

    def body(x_ref, halo_ref, w_ref, b_ref, y_ref, buf):
        i = pl.program_id(0)
        halo = halo_ref[...]
        buf[0:8, :] = jnp.where(i == 0, jnp.zeros_like(halo), _round(halo))
        buf[8:, :] = _round(x_ref[...])
        acc = jnp.zeros((tile, width), F32)
        for j in range(CONV_W):
            s = CONV_W - 1 - j
            acc = acc + _round(w_ref[j:j + 1, :]) * buf[8 - s:8 - s + tile, :]
        y_ref[...] = acc + b_ref[...]

    hb = tile // 8
    return pl.pallas_call(
        body, name=name, grid=(n_rows // tile,),
        in_specs=[pl.BlockSpec((tile, width), lambda i: (i, blk)),
                  pl.BlockSpec((8, width), lambda i: (jnp.maximum(i * hb - 1, 0), blk)),
                  pl.BlockSpec((CONV_W, width), lambda i: (0, 0)),
                  pl.BlockSpec((1, width), lambda i: (0, 0))],
        out_specs=pl.BlockSpec((tile, width), lambda i: (i, 0)),
        out_shape=jax.ShapeDtypeStruct((n_rows, width), F32),
        scratch_shapes=[pltpu.VMEM((tile + 8, width), F32)],
        compiler_params=_params(dimension_semantics=("arbitrary",)),
    )(x.arr, x.arr, w, b)


def conv_bwd(x, dy, w, *, tile, name):
    n_rows, width = x.arr.shape[0], x.width
    blk = x.block[0] if x.block else 0
    n_tiles = n_rows // tile

    def body(x_ref, xh_ref, dy_ref, dyh_ref, w_ref, dx_ref, dw_ref, db_ref, xbuf, dbuf):
        i = pl.program_id(0)
        xh, dyh = xh_ref[...], dyh_ref[...]
        xbuf[0:8, :] = jnp.where(i == 0, jnp.zeros_like(xh), _round(xh))
        xbuf[8:, :] = _round(x_ref[...])
        dy_t = dy_ref[...]
        dy_r = _round(dy_t)
        dbuf[0:tile, :] = dy_r
        dbuf[tile:, :] = jnp.where(i == n_tiles - 1, jnp.zeros_like(dyh), _round(dyh))

        @pl.when(i == 0)
        def _():
            dw_ref[...] = jnp.zeros(dw_ref.shape, F32)
            db_ref[...] = jnp.zeros(db_ref.shape, F32)

        acc = jnp.zeros((tile, width), F32)
        for j in range(CONV_W):
            s = CONV_W - 1 - j
            acc = acc + _round(w_ref[j:j + 1, :]) * dbuf[s:s + tile, :]
            dw_ref[j:j + 1, :] += jnp.sum(dy_r * xbuf[8 - s:8 - s + tile, :], axis=0, keepdims=True)
        dx_ref[...] = acc.astype(dx_ref.dtype)
        db_ref[...] += jnp.sum(dy_t, axis=0, keepdims=True)

    hb = tile // 8
    return pl.pallas_call(
        body, name=name, grid=(n_tiles,),
        in_specs=[pl.BlockSpec((tile, width), lambda i: (i, blk)),
                  pl.BlockSpec((8, width), lambda i: (jnp.maximum(i * hb - 1, 0), blk)),
                  pl.BlockSpec((tile, width), lambda i: (i, 0)),
                  pl.BlockSpec((8, width), lambda i: (jnp.minimum((i + 1) * hb, n_tiles * hb - 1), 0)),
                  pl.BlockSpec((CONV_W, width), lambda i: (0, 0))],
        out_specs=[pl.BlockSpec((tile, width), lambda i: (i, 0)),
                   pl.BlockSpec((CONV_W, width), lambda i: (0, 0)),
                   pl.BlockSpec((1, width), lambda i: (0, 0))],
        out_shape=[jax.ShapeDtypeStruct((n_rows, width), BF16),
                   jax.ShapeDtypeStruct((CONV_W, width), F32),
                   jax.ShapeDtypeStruct((1, width), F32)],
        scratch_shapes=[pltpu.VMEM((tile + 8, width), F32), pltpu.VMEM((tile + 8, width), F32)],
        compiler_params=_params(dimension_semantics=("arbitrary",)),
    )(x.arr, x.arr, dy, dy, w)


def _pieces(ref, col_w, row_h):
    n_c, n_r = ref.shape[1] // col_w, ref.shape[0] // row_h
    return [[ref[r * row_h:(r + 1) * row_h, c * col_w:(c + 1) * col_w].astype(F32) for r in range(n_r)]
            for c in range(n_c)]


def _store_pieces(ref, vals, col_w, row_h):
    for c, col in enumerate(vals):
        for r, v in enumerate(col):
            ref[r * row_h:(r + 1) * row_h, c * col_w:(c + 1) * col_w] = v.astype(ref.dtype)


def _x_spec(x, n_chunks, reverse):
    blk = x.block[0] if x.block else 0
    if reverse:
        return pl.BlockSpec((CHUNK, x.width), functools.partial(lambda n, b: (n_chunks - 1 - n, b), b=blk))
    return pl.BlockSpec((CHUNK, x.width), functools.partial(lambda n, b: (n, b), b=blk))


def chunk_scan_fwd(step, xs, piece, state_shapes, out_widths, *, name):
    n_rows = xs[0].arr.shape[0]
    n_chunks = n_rows // CHUNK
    n_x, n_s, n_o = len(xs), len(state_shapes), len(out_widths)

    def body(*refs):
        x_refs, o_refs = refs[:n_x], refs[n_x:n_x + n_o]
        keep_refs, st_refs = refs[n_x + n_o:n_x + n_o + n_s], refs[n_x + n_o + n_s:]

        @pl.when(pl.program_id(0) == 0)
        def _():
            for st in st_refs:
                st[...] = jnp.zeros(st.shape, F32)

        states = [st[...] for st in st_refs]
        for keep, s in zip(keep_refs, states):
            keep[...] = s
        new_states, outs = step(states, [_pieces(x, *p) for x, p in zip(x_refs, piece)])
        for st, s in zip(st_refs, new_states):
            st[...] = s
        for o, v in zip(o_refs, outs):
            o[...] = v

    out_shape = [jax.ShapeDtypeStruct((n_rows, w), F32) for w in out_widths]
    out_specs = [pl.BlockSpec((CHUNK, w), lambda n: (n, 0)) for w in out_widths]
    for shp in state_shapes:
        out_shape.append(jax.ShapeDtypeStruct((n_chunks,) + shp, F32))
        out_specs.append(pl.BlockSpec((None,) + shp, lambda n: (n, 0, 0)))
    res = pl.pallas_call(
        body, name=name, grid=(n_chunks,),
        in_specs=[_x_spec(x, n_chunks, False) for x in xs],
        out_specs=out_specs, out_shape=out_shape,
        scratch_shapes=[pltpu.VMEM(shp, F32) for shp in state_shapes],
        compiler_params=_params(dimension_semantics=("arbitrary",)),
    )(*[x.arr for x in xs])
    return res[:n_o], res[n_o:]


def chunk_scan_bwd(step, xs, piece, kept, d_outs, *, name):
    n_rows = xs[0].arr.shape[0]
    n_chunks = n_rows // CHUNK
    n_x, n_s, n_o = len(xs), len(kept), len(d_outs)
    state_shapes = [k.shape[1:] for k in kept]

    def body(*refs):
        x_refs, k_refs = refs[:n_x], refs[n_x:n_x + n_s]
        do_refs = refs[n_x + n_s:n_x + n_s + n_o]
        dx_refs = refs[n_x + n_s + n_o:2 * n_x + n_s + n_o]
        ds_refs = refs[2 * n_x + n_s + n_o:]

        @pl.when(pl.program_id(0) == 0)
        def _():
            for ds in ds_refs:
                ds[...] = jnp.zeros(ds.shape, F32)

        states = [k[...] for k in k_refs]
        inputs = [_pieces(x, *p) for x, p in zip(x_refs, piece)]
        _, vjp = jax.vjp(step, states, inputs)
        d_states, d_inputs = vjp(([ds[...] for ds in ds_refs], [do[...] for do in do_refs]))
        for ds, v in zip(ds_refs, d_states):
            ds[...] = v
        for dx, v, p in zip(dx_refs, d_inputs, piece):
            _store_pieces(dx, v, *p)

    rev3 = lambda n: (n_chunks - 1 - n, 0, 0)
    rev2 = lambda n: (n_chunks - 1 - n, 0)
    return pl.pallas_call(
        body, name=name, grid=(n_chunks,),
        in_specs=[_x_spec(x, n_chunks, True) for x in xs]
        + [pl.BlockSpec((None,) + shp, rev3) for shp in state_shapes]
        + [pl.BlockSpec((CHUNK, d.shape[1]), rev2) for d in d_outs],
        out_specs=[pl.BlockSpec((CHUNK, x.width), rev2) for x in xs],
        out_shape=[jax.ShapeDtypeStruct((n_rows, x.width), F32) for x in xs],
        scratch_shapes=[pltpu.VMEM(shp, F32) for shp in state_shapes],
        compiler_params=_params(dimension_semantics=("arbitrary",)),
    )(*[x.arr for x in xs], *kept, *d_outs)


def _tri(n, strict=False):
    r = lax.broadcasted_iota(jnp.int32, (n, n), 0)
    c = lax.broadcasted_iota(jnp.int32, (n, n), 1)
    return (r > c) if strict else (r >= c)


def hgrn2_step(states, inputs):
    q_all, k_all, v_all, lf_all = inputs
    n_sub = CHUNK // SUB
    low = _tri(SUB).astype(F32)
    ones_sub = jnp.ones((SUB, SUB), F32)
    ones_chunk = jnp.ones((CHUNK, HEAD_W), F32)
    new_states, outs = [], []
    for h in range(HEADS):
        state = states[h]
        q, k, v, lf = q_all[h], k_all[h], v_all[h], lf_all[h]
        cum = [mid_nn(low, lf[i]) for i in range(n_sub)]
        tot = [mid_nn(ones_sub, lf[i]) for i in range(n_sub)]
        start = [jnp.zeros((SUB, HEAD_W), F32)]
        for i in range(n_sub):
            start.append(start[-1] + tot[i])
        q_in = [q[i] * jnp.exp(cum[i]) for i in range(n_sub)]
        intra = []
        for i in range(n_sub):
            keys = [k[j] * jnp.exp(start[i] - start[j] - cum[j]) for j in range(i)]
            keys.append(k[i] * jnp.exp(jnp.minimum(-cum[i], 80.0)))
            att = mid_nt(q_in[i], jnp.concatenate(keys, axis=0))
            r_id = lax.broadcasted_iota(jnp.int32, att.shape, 0)
            c_id = lax.broadcasted_iota(jnp.int32, att.shape, 1)
            att = jnp.where(c_id - SUB * i <= r_id, att, 0.0)
            intra.append(mm_nn(att, jnp.concatenate(v[:i + 1], axis=0)))
        q_state = jnp.concatenate([q_in[i] * jnp.exp(start[i]) for i in range(n_sub)], axis=0)
        out = mm_nn(q_state, state) + jnp.concatenate(intra, axis=0)
        k_end = jnp.concatenate([k[j] * jnp.exp(start[n_sub] - start[j] - cum[j]) for j in range(n_sub)], axis=0)
        decay = jnp.exp(mid_tn(jnp.concatenate(lf, axis=0), ones_chunk))
        new_states.append(decay * state + mm_tn(k_end, jnp.concatenate(v, axis=0)))
        outs.append(out)
    return new_states, [jnp.concatenate(outs, axis=1)]


def mlstm_step(states, inputs):
    q_all, k_all, v_all, gates = inputs
    gates = gates[0][0]
    c_st, n_st, m_st = states[:HEADS], states[HEADS:2 * HEADS], states[2 * HEADS:]
    lane = lax.broadcasted_iota(jnp.int32, (CHUNK, 128), 1)
    low = _tri(CHUNK).astype(F32)
    causal = _tri(CHUNK)
    gates_cum = hi_nn(low, gates)
    new_c, new_n, new_m, outs = [], [], [], []
    for h in range(HEADS):
        q, k, v = q_all[h][0], k_all[h][0], v_all[h][0]
        pick_i = (lane == h).astype(F32)
        pick_f = (lane == HEADS + h).astype(F32)
        li_col = jnp.sum(gates * pick_i, axis=1, keepdims=True)
        lf_col = jnp.sum(gates * pick_f, axis=1, keepdims=True)
        b_col = jnp.sum(gates_cum * pick_f, axis=1, keepdims=True)
        by_key = hi_nt(pick_i, gates) - hi_nt(pick_f, gates_cum)
        d_mat = jnp.where(causal, b_col + by_key, NEG)
        m_prev = lax.stop_gradient(jnp.max(m_st[h], axis=1, keepdims=True))
        g_inter = b_col + m_prev
        m_t = lax.stop_gradient(jnp.maximum(g_inter, jnp.max(d_mat, axis=1, keepdims=True)))
        w_inter = jnp.exp(g_inter - m_t)
        aw = jnp.exp(d_mat - m_t) * mm_nt(q, k)
        num = w_inter * mm_nn(q, c_st[h]) + mm_nn(aw, v)
        den = w_inter * row_dot(q, n_st[h]) + jnp.sum(aw, axis=1, keepdims=True)
        outs.append(num / jnp.maximum(jnp.abs(den), jnp.exp(-m_t)))
        b_end = jnp.sum(lf_col, axis=0, keepdims=True)
        g_state = b_end + m_prev
        s_w = b_end - b_col + li_col
        m_next = lax.stop_gradient(jnp.maximum(g_state, jnp.max(s_w, axis=0, keepdims=True)))
        dec = jnp.exp(g_state - m_next)
        w_s = jnp.exp(s_w - m_next)
        kw = k * w_s
        new_c.append(dec * c_st[h] + mm_tn(kw, v))
        new_n.append(dec * n_st[h] + col_dot(w_s, k))
        new_m.append(jnp.broadcast_to(m_next, (1, HEAD_W)))
    return new_c + new_n + new_m, [jnp.concatenate(outs, axis=1)]


def lru_fwd(a, u, *, tile, name):
    n_rows, width = a.shape

    def body(a_ref, u_ref, h_ref, carry):
        @pl.when(pl.program_id(0) == 0)
        def _():
            carry[...] = jnp.zeros(carry.shape, F32)

        h = carry[...]
        for t in range(tile):
            h = a_ref[t:t + 1, :] * h + u_ref[t:t + 1, :]
            h_ref[t:t + 1, :] = h
        carry[...] = h

    spec = pl.BlockSpec((tile, width), lambda i: (i, 0))
    return pl.pallas_call(
        body, name=name, grid=(n_rows // tile,), in_specs=[spec, spec], out_specs=spec,
        out_shape=jax.ShapeDtypeStruct((n_rows, width), F32),
        scratch_shapes=[pltpu.VMEM((1, width), F32)],
        compiler_params=_params(dimension_semantics=("arbitrary",)),
    )(a, u)


def lru_bwd(a, h, dh, *, tile, name):
    n_rows, width = a.shape
    n_tiles = n_rows // tile
    hb = tile // 8

    def body(a_ref, h_ref, hh_ref, dh_ref, da_ref, du_ref, carry):
        i = pl.program_id(0)

        @pl.when(i == 0)
        def _():
            carry[...] = jnp.zeros(carry.shape, F32)

        c = carry[...]
        for t in range(tile - 1, -1, -1):
            g = dh_ref[t:t + 1, :] + c
            du_ref[t:t + 1, :] = g
            if t:
                h_prev = h_ref[t - 1:t, :]
            else:
                h_prev = jnp.where(i == n_tiles - 1, 0.0, hh_ref[7:8, :])
            da_ref[t:t + 1, :] = g * h_prev
            c = a_ref[t:t + 1, :] * g
        carry[...] = c

    rev = lambda i: (n_tiles - 1 - i, 0)
    spec = pl.BlockSpec((tile, width), rev)
    halo = pl.BlockSpec((8, width), lambda i: (jnp.maximum((n_tiles - 1 - i) * hb - 1, 0), 0))
    return pl.pallas_call(
        body, name=name, grid=(n_tiles,), in_specs=[spec, spec, halo, spec], out_specs=[spec, spec],
        out_shape=[jax.ShapeDtypeStruct((n_rows, width), F32)] * 2,
        scratch_shapes=[pltpu.VMEM((1, width), F32)],
        compiler_params=_params(dimension_semantics=("arbitrary",)),
    )(a, h, h, dh)


def _layer_norm(z, g, b):
    mu = jnp.mean(z, axis=-1, keepdims=True)
    zc = z - mu
    var = jnp.mean(zc * zc, axis=-1, keepdims=True)
    return zc * lax.rsqrt(var + 1e-5) * g + b


def pre_fn(rows, params):
    (x,), (scale, shift) = rows, params
    return [x * (1.0 + scale) + shift], []


def make_post_fn(weight, with_next):
    def fn(rows, params):
        x, y = rows
        gate, g, b = params[:3]
        xo = _layer_norm(ALPHA * x + weight * (1.0 + gate) * y, g, b)
        if with_next:
            return [xo, xo * (1.0 + params[3]) + params[4]], []
        return [xo], []

    return fn


def make_last_fn(weight):
    post = make_post_fn(weight, False)

    def fn(rows, params):
        x, y, target = rows
        err = post([x, y], params)[0][0] - target
        loss = 0.5 * jnp.sum(jnp.mean(err * err, axis=-1, keepdims=True), axis=0, keepdims=True)
        return [err * (1.0 / D_MODEL)], [jnp.broadcast_to(loss, (1, 128))]

    return fn


def mix_a_fn(rows, params):
    a_q, a_f, xconv, graw = rows
    l0, l1, l2, wq, wk, gate_b = params
    mx = jnp.maximum(jnp.maximum(l0, l1), l2)
    e0, e1, e2 = jnp.exp(l0 - mx), jnp.exp(l1 - mx), jnp.exp(l2 - mx)
    lb = e0 / (e0 + e1 + e2)
    f = lb + (1.0 - lb) * jax.nn.sigmoid(a_f)
    xc = _silu(xconv)
    q_b = mm_nn(xc, wq)
    k_b = mm_nn(xc, wk) * (HEAD_W ** -0.5)
    g = graw + gate_b
    lane = lax.broadcasted_iota(jnp.int32, g.shape, 1)
    gates = jnp.where(lane < HEADS, g, _log_sigmoid(g))
    return [_silu(a_q), 1.0 - f, jnp.log(f), xc, q_b, k_b, gates], []


def _head_norm(v, g, center):
    if center:
        v = v - jnp.mean(v, axis=-1, keepdims=True)
    return v * lax.rsqrt(jnp.mean(v * v, axis=-1, keepdims=True) + 1e-6) * g


def mix_b_fn(rows, params):
    o_a, a_g, h_b, xc, b_z = rows
    hg, mg, skip = params
    y_a = [_head_norm(o_a[h], hg[h], False) * _silu(a_g[h]) for h in range(HEADS)]
    y_b = [(_head_norm(h_b[h], mg[h], True) + skip[h] * xc[h]) * _silu(b_z[h]) for h in range(HEADS)]
    return [y_a + y_b], []


def lru_a_fn(rows, params):
    (xr,) = rows
    wa, wx, ba, bx, lam = params
    a_out, u_out = [], []
    for n in range(C_BLOCKS):
        r = jax.nn.sigmoid(mm_nt(xr[n], wa[n]) + ba[n])
        i = jax.nn.sigmoid(mm_nt(xr[n], wx[n]) + bx[n])
        log_a = -RG_C * r * _softplus(-lam[n])
        a_out.append(jnp.exp(log_a))
        u_out.append(jnp.sqrt(_neg_expm1(2.0 * log_a)) * i * xr[n])
    return [a_out, u_out], []


def lru_b_fn(rows, params):
    h, y_br = rows
    return [h * jax.nn.gelu(y_br)], []


def swiglu_epi(accs, extras):
    h1, h3 = accs
    return [h1, h3, _silu(h1) * h3]


def swiglu_bwd_epi(accs, extras):
    (da,), (h1, h3) = accs, extras
    h1, h3 = h1.astype(F32), h3.astype(F32)
    sig = jax.nn.sigmoid(h1)
    return [da * h3 * sig * (1.0 + h1 * (1.0 - sig)), da * h1 * sig]


ROW_TILE = 256


def _chunks(v, n):
    return v.reshape(n, 1, v.shape[-1] // n)


def _dense_blocks(w):
    n, b, _ = w.shape
    by_row = jnp.swapaxes(w, 1, 2).reshape(n * b, b)
    spread = jnp.dot(by_row, _column_picker(n, b).T, precision=HIGHEST)
    return spread * _block_mask(n, b)


def _column_picker(n, b):
    return jnp.asarray(np.tile(np.eye(b, dtype=np.float32), (n, 1)))


def _block_mask(n, b):
    return jnp.asarray(np.kron(np.eye(n, dtype=np.float32), np.ones((b, b), np.float32)))


def _block_diag_of(m, n, b):
    by_row = jnp.dot(m * _block_mask(n, b), _column_picker(n, b), precision=HIGHEST)
    return jnp.swapaxes(by_row.reshape(n, b, b), 1, 2)


def local_step(x, target, mod, w, get_weights, put_grads, put_small, first_after=()):
    s_len = x.shape[0]
    tile = min(ROW_TILE, s_len)
    row = lambda v: v.reshape(1, -1)
    mrow = lambda l, j, k: mod[l, 3 * j + k].reshape(1, D_MODEL)
    g = {}
    d_mod = [[None] * 9 for _ in range(DEPTH)]
    d_ln_g = [[None] * 3 for _ in range(DEPTH)]
    d_ln_b = [[None] * 3 for _ in range(DEPTH)]
    subs = [(l, j) for l in range(DEPTH) for j in range(3)]
    weight_of = lambda j: 1.0 if j == 1 else FFN_RES_W

    wq_d = _dense_blocks(w["mlstm_wq"])
    wk_d = _dense_blocks(w["mlstm_wk"])
    wa_b, wx_b = w["rglru_wa"].astype(BF16), w["rglru_wx"].astype(BF16)
    gate_b = jnp.pad(w["mlstm_gate_b"].reshape(1, 8), ((0, 0), (0, 120)))
    lb_rows = [row(w["hgrn_lb_logits"][k]) for k in range(3)]
    mix_a_params = lb_rows + [wq_d, wk_d, gate_b]
    mix_b_params = [_chunks(row(w["hgrn_norm_g"]), HEADS), _chunks(row(w["mlstm_norm_g"]), HEADS),
                    _chunks(row(w["mlstm_skip"]), HEADS)]
    lru_a_params = [wa_b, wx_b, _chunks(row(w["rglru_ba"]), C_BLOCKS), _chunks(row(w["rglru_bx"]), C_BLOCKS),
                    _chunks(row(w["rglru_lambda"]), C_BLOCKS)]
    mconv_w, mconv_b = w["mlstm_conv_w"], row(w["mlstm_conv_b"])
    rconv_w, rconv_b = w["rglru_conv_w"], row(w["rglru_conv_b"])
    hg_piece = [(HEAD_W, SUB)] * 4
    ml_piece = [(HEAD_W, CHUNK)] * 3 + [(128, CHUNK)]

    (t,), _ = rowwise(pre_fn, [Rows(x)], [mrow(0, 0, 1), mrow(0, 0, 0)], [(D_MODEL, BF16)], [],
                      tile=tile, name="pre", after=tuple(first_after))
    saved = {}
    x_in = x
    for idx, (l, j) in enumerate(subs):
        sv = {"x": x_in, "t": t}
        if j != 1:
            (w1, w3, w2), begun = get_weights(("ffn", l, j // 2), t)
            h1, h3, act = matmul([[(t, w1)], [(t, w3)]], "nn", [BF16, BF16, BF16], tm=512, tn=1408, tk=D_MODEL,
                                 epi=swiglu_epi, name="ffn_up", n_outer=True, after=begun)
            (y,) = matmul([[(act, w2)]], "nn", [F32], tm=1024, tn=1024, tk=1408, name="ffn_down")
            sv.update(h1=h1, h3=h3, act=act, big=(w1, w3, w2))
        elif l == 0:
            (ab_w_in, ab_w_out), begun = get_weights(("ab",), t)
            sv["big"] = (ab_w_in, ab_w_out)
            (proj,) = matmul([[(t, ab_w_in)]], "nn", [F32], tm=256, tn=AB_ALL, tk=D_MODEL, name="ab_in", after=begun)
            xconv = conv_fwd(Rows(proj, (4, MIX_W)), mconv_w, mconv_b, tile=tile, name="mconv")
            a_rows = [Rows(proj, (0, MIX_W)), Rows(proj, (1, MIX_W)), Rows(xconv), Rows(proj, (AB_MAIN // 128, 128))]
            (q_a, k_a, lf_a, xc, q_b, k_b, gates), _ = rowwise(
                mix_a_fn, a_rows, mix_a_params, [(MIX_W, F32)] * 6 + [(128, F32)], [], tile=tile, name="mix_a")
            hg_xs = [Rows(q_a), Rows(k_a), Rows(proj, (2, MIX_W)), Rows(lf_a)]
            (o_a,), hg_kept = chunk_scan_fwd(hgrn2_step, hg_xs, hg_piece, [(HEAD_W, HEAD_W)] * HEADS, [MIX_W],
                                             name="hgrn2_fwd")
            ml_xs = [Rows(q_b), Rows(k_b), Rows(proj, (5, MIX_W)), Rows(gates)]
            ml_states = [(HEAD_W, HEAD_W)] * HEADS + [(1, HEAD_W)] * (2 * HEADS)
            (h_b,), ml_kept = chunk_scan_fwd(mlstm_step, ml_xs, ml_piece, ml_states, [MIX_W], name="mlstm_fwd")
            b_rows = [Rows(o_a, split=HEAD_W), Rows(proj, (3, MIX_W), HEAD_W), Rows(h_b, split=HEAD_W),
                      Rows(xc, split=HEAD_W), Rows(proj, (6, MIX_W), HEAD_W)]
            (ycat,), _ = rowwise(mix_b_fn, b_rows, mix_b_params, [(2 * MIX_W, BF16)], [], tile=tile, name="mix_b")
            (y,) = matmul([[(ycat, ab_w_out)]], "nn", [F32], tm=512, tn=1024, tk=D_MODEL, name="ab_out")
            sv.update(proj=proj, xconv=xconv, a_rows=a_rows, hg_xs=hg_xs, hg_kept=hg_kept, ml_xs=ml_xs,
                      ml_kept=ml_kept, b_rows=b_rows, ycat=ycat)
        else:
            (rg_w_in, rg_w_out), begun = get_weights(("rg",), t)
            sv["big"] = (rg_w_in, rg_w_out)
            (proj,) = matmul([[(t, rg_w_in)]], "nn", [F32], tm=512, tn=1024, tk=D_MODEL, name="rg_in", after=begun)
            xr = conv_fwd(Rows(proj, (1, D_MODEL)), rconv_w, rconv_b, tile=tile, name="rconv")
            (a_t, u_t), _ = rowwise(lru_a_fn, [Rows(xr, split=128)], lru_a_params, [(D_MODEL, F32)] * 2, [],
                                    tile=tile, name="lru_a")
            h = lru_fwd(a_t, u_t, tile=min(128, s_len), name="lru_fwd")
            b_rows = [Rows(h), Rows(proj, (0, D_MODEL))]
            (hgate,), _ = rowwise(lru_b_fn, b_rows, [], [(D_MODEL, BF16)], [], tile=tile, name="lru_b")
            (y,) = matmul([[(hgate, rg_w_out)]], "nn", [F32], tm=512, tn=1024, tk=D_MODEL, name="rg_out")
            sv.update(proj=proj, xr=xr, a_t=a_t, h=h, b_rows=b_rows, hgate=hgate)
        sv["y"] = y
        post_params = [mrow(l, j, 2), row(w["ln_g"][l, j]), row(w["ln_b"][l, j])]
        if idx + 1 < len(subs):
            nl, nj = subs[idx + 1]
            post_params += [mrow(nl, nj, 1), mrow(nl, nj, 0)]
            (x_out, t), _ = rowwise(make_post_fn(weight_of(j), True), [Rows(x_in), Rows(y)], post_params,
                                    [(D_MODEL, F32), (D_MODEL, BF16)], [], tile=tile, name="post")
        else:
            (d_xo,), (loss_row,) = rowwise(make_last_fn(weight_of(j)), [Rows(x_in), Rows(y), Rows(target)],
                                           post_params, [(D_MODEL, F32)], [(1, 128)], tile=tile, name="post_loss")
            x_out = None
        sv["post_params"] = post_params
        saved[(l, j)] = sv
        x_in = x_out
    loss = loss_row[0, 0]

    d_t_next = None
    sent = []
    deferred = []

    def hand_over(key, grads):
        token = put_grads(key, grads)
        if token is not None:
            sent.append(token)

    for idx in range(len(subs) - 1, -1, -1):
        l, j = subs[idx]
        sv = saved[(l, j)]
        has_next = idx + 1 < len(subs)
        cots = [Rows(d_xo)] + ([Rows(d_t_next)] if has_next else [])
        want_p = [0, 1, 2] + ([3, 4] if has_next else [])
        (d_xres, d_y), d_par = rowwise_bwd(
            make_post_fn(weight_of(j), has_next), [Rows(sv["x"]), Rows(sv["y"])], cots, sv["post_params"],
            [0, 1], want_p, [F32, BF16], tile=tile, name="post_bwd", after=tuple(sent))
        sent.clear()
        d_mod[l][3 * j + 2], d_ln_g[l][j], d_ln_b[l][j] = d_par[:3]
        if has_next:
            nl, nj = subs[idx + 1]
            d_mod[nl][3 * nj + 1], d_mod[nl][3 * nj] = d_par[3:]
        t = sv["t"]
        if j != 1:
            w1, w3, w2 = sv["big"]
            d_h1, d_h3 = matmul([[(d_y, w2)]], "nt", [BF16, BF16], tm=512, tn=1408, tk=D_MODEL,
                                extras=[sv["h1"], sv["h3"]], epi=swiglu_bwd_epi, name="ffn_down_bwd", n_outer=True)

            def ffn_weight_grads(after, ops=(sv["act"], t, d_y, d_h1, d_h3), key=("ffn", l, j // 2)):
                wait = () if after is None else (after,)
                act, t_in, d_out, d_1, d_3 = ops
                (g_w2,) = matmul([[(act, d_out)]], "tn", [GRAD_WIRE], tm=1408, tn=1024, tk=2048, name="ffn_dw2",
                                 after=wait)
                (g_w1,) = matmul([[(t_in, d_1)]], "tn", [GRAD_WIRE], tm=1024, tn=1408, tk=2048, name="ffn_dw1",
                                 after=wait)
                (g_w3,) = matmul([[(t_in, d_3)]], "tn", [GRAD_WIRE], tm=1024, tn=1408, tk=2048, name="ffn_dw3",
                                 after=wait)
                hand_over(key, [g_w1, g_w3, g_w2])

            if idx:
                ffn_weight_grads(None)
            else:
                deferred.append(ffn_weight_grads)
            (d_t,) = matmul([[(d_h1, w1), (d_h3, w3)]], "nt", [F32], tm=1024, tn=1024, tk=1408, name="ffn_up_bwd")
        elif l == 0:
            ab_w_in, ab_w_out = sv["big"]
            (d_ycat,) = matmul([[(d_y, ab_w_out)]], "nt", [F32], tm=512, tn=1024, tk=D_MODEL, name="ab_out_bwd")
            (g_out,) = matmul([[(sv["ycat"], d_y)]], "tn", [GRAD_WIRE], tm=1024, tn=1024, tk=2048, name="ab_dwout")
            (d_oa, d_ag, d_hb, d_xc, d_bz), (d_hg, d_mg, d_skip) = rowwise_bwd(
                mix_b_fn, sv["b_rows"], [Rows(d_ycat, split=HEAD_W)], mix_b_params, [0, 1, 2, 3, 4], [0, 1, 2],
                [F32, BF16, F32, F32, BF16], tile=tile, name="mix_b_bwd")
            g["hgrn_norm_g"], g["mlstm_norm_g"], g["mlstm_skip"] = (v.reshape(1, MIX_W) for v in (d_hg, d_mg, d_skip))
            d_qb, d_kb, d_bv, d_gates = chunk_scan_bwd(mlstm_step, sv["ml_xs"], ml_piece, sv["ml_kept"], [d_hb],
                                                       name="mlstm_bwd")
            d_qa, d_ka, d_ai, d_lf = chunk_scan_bwd(hgrn2_step, sv["hg_xs"], hg_piece, sv["hg_kept"], [d_oa],
                                                    name="hgrn2_bwd")
            a_cots = [Rows(v) for v in (d_qa, d_ka, d_lf, d_xc, d_qb, d_kb, d_gates)]
            (d_aq, d_af, d_xconv, d_graw), (d_l0, d_l1, d_l2, d_wq, d_wk, d_gb) = rowwise_bwd(
                mix_a_fn, sv["a_rows"], a_cots, mix_a_params, [0, 1, 2, 3], [0, 1, 2, 3, 4, 5],
                [BF16, BF16, F32, BF16], tile=tile, name="mix_a_bwd")
            g["hgrn_lb_logits"] = jnp.concatenate([d_l0, d_l1, d_l2], axis=0)
            g["mlstm_wq"] = _block_diag_of(d_wq, MIX_W // 4, 4)
            g["mlstm_wk"] = _block_diag_of(d_wk, MIX_W // 4, 4)
            g["mlstm_gate_b"] = d_gb[:, :8]
            d_bx, g["mlstm_conv_w"], g["mlstm_conv_b"] = conv_bwd(Rows(sv["proj"], (4, MIX_W)), d_xconv, mconv_w,
                                                                  tile=tile, name="mconv_bwd")
            d_proj = jnp.concatenate([d_aq, d_af, d_ai.astype(BF16), d_ag, d_bx, d_bv.astype(BF16), d_bz, d_graw],
                                     axis=1)
            (g_in,) = matmul([[(t, d_proj)]], "tn", [GRAD_WIRE], tm=256, tn=AB_ALL, tk=1024, name="ab_dwin")
            hand_over(("ab",), [g_in, g_out])
            (d_t,) = matmul([[(d_proj, ab_w_in)]], "nt", [F32], tm=512, tn=1024, tk=AB_ALL, name="ab_in_bwd")
        else:
            rg_w_in, rg_w_out = sv["big"]
            (d_hgate,) = matmul([[(d_y, rg_w_out)]], "nt", [F32], tm=512, tn=1024, tk=D_MODEL,
                                name="rg_out_bwd")
            (g_out,) = matmul([[(sv["hgate"], d_y)]], "tn", [GRAD_WIRE], tm=1024, tn=1024, tk=2048,
                                         name="rg_dwout")
            (d_h, d_ybr), _ = rowwise_bwd(lru_b_fn, sv["b_rows"], [Rows(d_hgate)], [], [0, 1], [], [F32, BF16],
                                          tile=tile, name="lru_b_bwd")
            d_a, d_u = lru_bwd(sv["a_t"], sv["h"], d_h, tile=min(128, s_len), name="lru_bwd")
            (d_xr,), (d_wa, d_wx, d_ba, d_bx_, d_lam) = rowwise_bwd(
                lru_a_fn, [Rows(sv["xr"], split=128)], [Rows(d_a, split=128), Rows(d_u, split=128)], lru_a_params,
                [0], [0, 1, 2, 3, 4], [F32], tile=tile, name="lru_a_bwd")
            g["rglru_wa"], g["rglru_wx"] = d_wa, d_wx
            g["rglru_ba"], g["rglru_bx"], g["rglru_lambda"] = (v.reshape(1, D_MODEL) for v in (d_ba, d_bx_, d_lam))
            d_xbr, g["rglru_conv_w"], g["rglru_conv_b"] = conv_bwd(Rows(sv["proj"], (1, D_MODEL)), d_xr, rconv_w,
                                                                   tile=tile, name="rconv_bwd")
            d_proj = jnp.concatenate([d_ybr, d_xbr], axis=1)
            (g_in,) = matmul([[(t, d_proj)]], "tn", [GRAD_WIRE], tm=1024, tn=1024, tk=2048, name="rg_dwin")
            hand_over(("rg",), [g_in, g_out])
            (d_t,) = matmul([[(d_proj, rg_w_in)]], "nt", [F32], tm=512, tn=1024, tk=1024, name="rg_in_bwd")
        d_xo, d_t_next = d_xres, d_t

    def first_bwd(rows, params):
        x0, d_res, d_t0 = rows
        _, vjp = jax.vjp(lambda r, p: pre_fn(r, p)[0], [x0], params)
        (d_x0,), d_p = vjp([d_t0])
        return [d_res + d_x0], d_p

    (grad_x,), (d_mod[0][1], d_mod[0][0]) = rowwise(
        first_bwd, [Rows(x), Rows(d_xo), Rows(d_t_next)], [mrow(0, 0, 1), mrow(0, 0, 0)], [(D_MODEL, F32)],
        [(1, D_MODEL)] * 2, tile=tile, name="pre_bwd", after=tuple(sent))
    sent.clear()
    g["ln_g"] = jnp.stack([jnp.concatenate(r, axis=0) for r in d_ln_g])
    g["ln_b"] = jnp.stack([jnp.concatenate(r, axis=0) for r in d_ln_b])
    d_mod = jnp.stack([jnp.concatenate(r, axis=0) for r in d_mod])
    small_sent = put_small(g, d_mod)
    for weight_grads in deferred:
        weight_grads(small_sent)
    return loss, grad_x, d_mod, g, list(sent)


MESH_ID = pl.DeviceIdType.MESH
ANY_SPEC = pl.BlockSpec(memory_space=pl.ANY)


def _my_position():
    return lax.axis_index("x"), lax.axis_index("y"), lax.axis_index("c")


def _flat_index(pos):
    return 4 * pos[0] + 2 * pos[1] + pos[2]


def _peer_position(pos, k):
    return tuple(lax.rem(p + ((k >> s) & 1), 2) for p, s in zip(pos, (2, 1, 0)))


def _exchange(x, gather, name):
    out_shape = (NDEV,) + x.shape if gather else x.shape

    def body(x_ref, o_ref, send_sems, recv_sems, local_sem):
        pos = _my_position()
        me = _flat_index(pos)
        local = pltpu.make_async_copy(x_ref if gather else x_ref.at[me], o_ref.at[me], local_sem)
        local.start()
        copies = []
        for k in range(1, NDEV):
            peer = _peer_position(pos, k)
            src = x_ref if gather else x_ref.at[_flat_index(peer)]
            copies.append(pltpu.make_async_remote_copy(
                src_ref=src, dst_ref=o_ref.at[me], send_sem=send_sems.at[k - 1], recv_sem=recv_sems.at[k - 1],
                device_id=peer, device_id_type=MESH_ID))
            copies[-1].start()
        for cp in copies:
            cp.wait()
        local.wait()

    return pl.pallas_call(
        body, name=name, in_specs=[ANY_SPEC], out_specs=ANY_SPEC,
        out_shape=jax.ShapeDtypeStruct(out_shape, x.dtype),
        scratch_shapes=[pltpu.SemaphoreType.DMA((NDEV - 1,)), pltpu.SemaphoreType.DMA((NDEV - 1,)),
                        pltpu.SemaphoreType.DMA],
    )(x)


HBM_SPEC = pl.BlockSpec(memory_space=pltpu.HBM)
SEM_SPEC = pl.BlockSpec(memory_space=pltpu.SEMAPHORE)
SIDE_EFFECT = pltpu.SideEffectType.DATAFLOW_SIDE_EFFECTING


def _exchange_copies(x_refs, land_refs, send_sems, recv_sems, gather):
    pos = _my_position()
    me = _flat_index(pos)
    copies = []
    for k in range(1, NDEV):
        peer = _peer_position(pos, k)
        for x_ref, land_ref, s_sem, r_sem in zip(x_refs, land_refs, send_sems, recv_sems):
            src = x_ref if gather else x_ref.at[_flat_index(peer)]
            copies.append(pltpu.make_async_remote_copy(src_ref=src, dst_ref=land_ref.at[me], send_sem=s_sem,
                                                       recv_sem=r_sem, device_id=peer, device_id_type=MESH_ID))
    return copies


def exchange_start(xs, gather, name):
    n = len(xs)
    land_shapes = [(NDEV,) + x.shape if gather else x.shape for x in xs]

    def body(*refs):
        x_refs, land_refs = refs[:n], refs[n:2 * n]
        send_sems, recv_sems = refs[2 * n:3 * n], refs[3 * n:4 * n]
        token = refs[-1]
        for cp in _exchange_copies(x_refs, land_refs, send_sems, recv_sems, gather):
            cp.start()
        token[...] = jnp.zeros(token.shape, token.dtype)

    sem = pltpu.SemaphoreType.DMA(())
    res = pl.pallas_call(
        body, name=name,
        out_shape=[sem] * (2 * n) + [pltpu.HBM(x.shape, x.dtype) for x in xs]
        + [pltpu.HBM(s, x.dtype) for s, x in zip(land_shapes, xs)] + [jax.ShapeDtypeStruct((8, 128), F32)],
        in_specs=[HBM_SPEC] * (2 * n),
        out_specs=[SEM_SPEC] * (2 * n) + [HBM_SPEC] * (2 * n) + [pl.BlockSpec(memory_space=pltpu.VMEM)],
        input_output_aliases={i: 2 * n + i for i in range(2 * n)},
        compiler_params=pltpu.CompilerParams(has_side_effects=SIDE_EFFECT),
    )(*[pltpu.with_memory_space_constraint(x, pltpu.HBM) for x in xs],
      *[pltpu.with_memory_space_constraint(lax.empty(s, x.dtype), pltpu.HBM) for s, x in zip(land_shapes, xs)])
    return (res[:n], res[n:2 * n], res[2 * n:3 * n], res[3 * n:4 * n]), res[-1]


def exchange_wait(handles, after, name):
    send_sems, recv_sems, x_thru, land_thru = handles
    n = len(x_thru)
    after = jax.tree_util.tree_leaves(after)

    def body(*refs):
        land_refs = refs[n:2 * n]
        s_sems, r_sems = refs[2 * n:3 * n], refs[3 * n:4 * n]
        pos = _my_position()
        for land_ref, s_sem, r_sem in zip(land_refs, s_sems, r_sems):
            seven = land_ref.at[pl.ds(0, NDEV - 1)]
            all_seven = pltpu.make_async_remote_copy(src_ref=seven, dst_ref=seven, send_sem=s_sem, recv_sem=r_sem,
                                                     device_id=pos, device_id_type=MESH_ID)
            all_seven.wait_send()
            all_seven.wait_recv()

    res = pl.pallas_call(
        body, name=name,
        out_shape=[pltpu.HBM(x.shape, x.dtype) for x in x_thru] + [pltpu.HBM(x.shape, x.dtype) for x in land_thru],
        in_specs=[HBM_SPEC] * (2 * n) + [SEM_SPEC] * (2 * n) + [ANY_SPEC] * len(after),
        out_specs=[HBM_SPEC] * (2 * n),
        input_output_aliases={i: i for i in range(2 * n)},
        compiler_params=pltpu.CompilerParams(has_side_effects=SIDE_EFFECT),
    )(*x_thru, *land_thru, *send_sems, *recv_sems, *after)
    return res[:n], res[n:]


def all_gather(x, name):
    return _exchange(x, True, name)


def all_to_all(x, name):
    return _exchange(x, False, name)


def _row_tile(n_rows, cap):
    best = None
    for t in range(8, min(n_rows, cap) + 1, 8):
        if n_rows % t == 0:
            best = t
    return best if best else n_rows


def adamw(w, m, v, slots, *, name, index=(), prev=None):
    n_rows, width = w.shape[-2:]
    n_lead = w.ndim - 2
    assert len(index) == n_lead
    n_slots = slots.shape[0]
    lanes = -(-width // 128) * 128
    tile = _row_tile(n_rows, max(8, (1 << 20) // (4 * lanes) // 8 * 8))
    bc1 = 1.0 - ADAM_B1 ** ADAM_STEP
    bc2 = 1.0 - ADAM_B2 ** ADAM_STEP

    def body(w_ref, m_ref, v_ref, s_ref, *rest):
        g_ref, d_ref, nm_ref, nv_ref = rest[-4:]
        g = s_ref[0].astype(F32)
        for k in range(1, n_slots):
            g = g + s_ref[k].astype(F32)
        wv = w_ref[...]
        nm = ADAM_B1 * m_ref[...] + (1.0 - ADAM_B1) * g
        nv = ADAM_B2 * v_ref[...] + (1.0 - ADAM_B2) * (g * g)
        g_ref[...] = g
        nm_ref[...] = nm
        nv_ref[...] = nv
        d_ref[...] = -ADAM_LR * ((nm / bc1) / (jnp.sqrt(nv / bc2) + ADAM_EPS) + ADAM_WD * wv)

    spec = pl.BlockSpec((None,) * n_lead + (tile, width), lambda i: tuple(index) + (i, 0))
    prev = list(prev) if prev is not None else []
    return pl.pallas_call(
        body, name=name, grid=(n_rows // tile,),
        in_specs=[spec, spec, spec, pl.BlockSpec((n_slots, tile, width), lambda i: (0, i, 0))]
        + [ANY_SPEC] * len(prev),
        out_specs=[spec] * 4, out_shape=[jax.ShapeDtypeStruct(w.shape, F32)] * 4,
        input_output_aliases={4 + k: k for k in range(len(prev))},
        compiler_params=_params(dimension_semantics=("parallel",)),
    )(w, m, v, slots, *prev)


def sum_slots(slots, *, name):
    n_slots, n_rows, width = slots.shape

    def body(s_ref, o_ref):
        @pl.when(pl.program_id(0) == 0)
        def _():
            o_ref[...] = s_ref[...]

        @pl.when(pl.program_id(0) > 0)
        def _():
            o_ref[...] += s_ref[...]

    return pl.pallas_call(
        body, name=name, grid=(n_slots,),
        in_specs=[pl.BlockSpec((None, n_rows, width), lambda k: (k, 0, 0))],
        out_specs=pl.BlockSpec((n_rows, width), lambda k: (0, 0)),
        out_shape=jax.ShapeDtypeStruct((n_rows, width), F32),
        compiler_params=_params(dimension_semantics=("arbitrary",)),
    )(slots)


def adamw_nd(w, m, v, slots, *, name):
    shp = w.shape
    two = (-1, shp[-1])
    res = adamw(w.reshape(two), m.reshape(two), v.reshape(two), slots.reshape((slots.shape[0],) + (w.size // shp[-1], shp[-1])),
                name=name)
    return [r.reshape(shp) for r in res]


def _pack(arrs):
    parts = []
    for a in arrs:
        flat = a.reshape(-1).astype(F32)
        parts.append(jnp.pad(flat, (0, (-flat.shape[0]) % 1024)))
    return jnp.concatenate(parts).reshape(-1, 128)


def _unpack(buf, shapes):
    outs, at = [], 0
    lead = buf.shape[:-2]
    flat = buf.reshape(lead + (-1,))
    for shp in shapes:
        n = int(np.prod(shp))
        outs.append(flat[..., at:at + n].reshape(lead + tuple(shp)))
        at += n + (-n) % 1024
    return outs


ARG_NAMES = ["x", "c", "ada_w", "ada_b", "ln_g", "ln_b", "ffn_w1", "ffn_w3", "ffn_w2", "hgrn_lb_logits", "ab_w_in",
             "ab_w_out", "hgrn_norm_g", "mlstm_conv_w", "mlstm_conv_b", "mlstm_wq", "mlstm_wk", "mlstm_gate_b",
             "mlstm_skip", "mlstm_norm_g", "rglru_w_in", "rglru_conv_w", "rglru_conv_b", "rglru_wa", "rglru_ba",
             "rglru_wx", "rglru_bx", "rglru_lambda", "rglru_w_out", "loss_target"]
WEIGHTS = ARG_NAMES[2:-1]
BIG = ["ffn_w1", "ffn_w3", "ffn_w2", "ab_w_in", "ab_w_out", "rglru_w_in", "rglru_w_out"]
REPLICATED = ["ada_b", "hgrn_lb_logits", "hgrn_norm_g", "mlstm_conv_b", "mlstm_wq", "mlstm_wk", "mlstm_gate_b",
              "mlstm_skip", "mlstm_norm_g", "rglru_wa", "rglru_wx"]
SHARDED_SMALL = ["ln_g", "ln_b", "mlstm_conv_w", "rglru_conv_w", "rglru_conv_b", "rglru_ba", "rglru_bx", "rglru_lambda"]


def _unshard_last(gathered):
    moved = jnp.moveaxis(gathered, 0, -2)
    return moved.reshape(moved.shape[:-2] + (NDEV * moved.shape[-1],))


def _shard_last(full):
    split = full.reshape(full.shape[:-1] + (NDEV, full.shape[-1] // NDEV))
    return jnp.moveaxis(split, -2, 0)


def kernel(x, c, ada_w, ada_b, ln_g, ln_b, ffn_w1, ffn_w3, ffn_w2, hgrn_lb_logits, ab_w_in, ab_w_out, hgrn_norm_g, mlstm_conv_w, mlstm_conv_b, mlstm_wq, mlstm_wk, mlstm_gate_b, mlstm_skip, mlstm_norm_g, rglru_w_in, rglru_conv_w, rglru_conv_b, rglru_wa, rglru_ba, rglru_wx, rglru_bx, rglru_lambda, rglru_w_out, loss_target, m_ada_w, m_ada_b, m_ln_g, m_ln_b, m_ffn_w1, m_ffn_w3, m_ffn_w2, m_hgrn_lb_logits, m_ab_w_in, m_ab_w_out, m_hgrn_norm_g, m_mlstm_conv_w, m_mlstm_conv_b, m_mlstm_wq, m_mlstm_wk, m_mlstm_gate_b, m_mlstm_skip, m_mlstm_norm_g, m_rglru_w_in, m_rglru_conv_w, m_rglru_conv_b, m_rglru_wa, m_rglru_ba, m_rglru_wx, m_rglru_bx, m_rglru_lambda, m_rglru_w_out, v_ada_w, v_ada_b, v_ln_g, v_ln_b, v_ffn_w1, v_ffn_w3, v_ffn_w2, v_hgrn_lb_logits, v_ab_w_in, v_ab_w_out, v_hgrn_norm_g, v_mlstm_conv_w, v_mlstm_conv_b, v_mlstm_wq, v_mlstm_wk, v_mlstm_gate_b, v_mlstm_skip, v_mlstm_norm_g, v_rglru_w_in, v_rglru_conv_w, v_rglru_conv_b, v_rglru_wa, v_rglru_ba, v_rglru_wx, v_rglru_bx, v_rglru_lambda, v_rglru_w_out):
    args = locals()
    p = {n: args[n] for n in ARG_NAMES}
    mom = {n: (args["m_" + n], args["v_" + n]) for n in WEIGHTS}
    me = _flat_index(_my_position())

    keys = [("ffn", 0, 0), ("ab",), ("ffn", 0, 1), ("ffn", 1, 0), ("rg",), ("ffn", 1, 1)]
    names = {("ab",): ("ab_w_in", "ab_w_out"), ("rg",): ("rglru_w_in", "rglru_w_out")}
    for l in range(DEPTH):
        for i in range(2):
            names[("ffn", l, i)] = ("ffn_w1", "ffn_w3", "ffn_w2")

    def part(key):
        return (lambda a: a[key[1], key[2]]) if key[0] == "ffn" else (lambda a: a[0])

    gather_handles = {}

    def landed(handles, own_of, after, name):
        sources, lands = exchange_wait(handles, after, name)
        return [lax.dynamic_update_index_in_dim(ld, own_of(src), me, 0) for src, ld in zip(sources, lands)]

    starts_next = {("ffn", 0, 0): [("ab",)], ("ab",): [("ffn", 0, 1), ("ffn", 1, 0)],
                   ("ffn", 0, 1): [("rg",), ("ffn", 1, 1)]}

    def start_gather(key, after):
        shards = [part(key)(p[n]).astype(BF16) for n in names[key]]
        if after is not None:
            shards, _ = lax.optimization_barrier((shards, after))
        gather_handles[key], token = exchange_start(shards, True, "gather_start_" + "_".join(map(str, key)))
        return token

    def get_weights(key, after):
        got = landed(gather_handles[key], lambda src: src, after, "gather_wait_" + "_".join(map(str, key)))
        tokens = tuple(start_gather(nxt, got) for nxt in starts_next.get(key, []))
        if key[0] == "ffn":
            return (_unshard_last(got[0]), _unshard_last(got[1]), got[2].reshape(D_FF, D_MODEL)), tokens
        w_in = _unshard_last(got[0])
        if key[0] == "ab":
            w_in = jnp.concatenate([w_in[:, :AB_MAIN], jnp.pad(w_in[:, AB_MAIN:], ((0, 0), (0, 120)))], axis=1)
        return (w_in, got[1].reshape(D_MODEL, D_MODEL)), tokens

    scatter_handles = {}

    def put_grads(key, grads):
        if key[0] == "ffn":
            slots = [_shard_last(grads[0]), _shard_last(grads[1]), grads[2].reshape(NDEV, D_FF // NDEV, D_MODEL)]
        else:
            g_in = grads[0][:, :AB_MAIN + 8] if key[0] == "ab" else grads[0]
            slots = [_shard_last(g_in), grads[1].reshape(NDEV, D_MODEL // NDEV, D_MODEL)]
        scatter_handles[key], token = exchange_start(slots, False, "scatter_start_" + "_".join(map(str, key)))
        return token

    sharded_shapes = [p[n].shape for n in SHARDED_SMALL]
    small = all_gather(_pack([p[n] for n in SHARDED_SMALL] + [c]), "gather_small")
    started = [start_gather(keys[0], small)]
    small, _ = lax.optimization_barrier((small, started))
    per_dev = _unpack(small, sharded_shapes + [c.shape])
    full_small = {n: _unshard_last(per_dev[i]) for i, n in enumerate(SHARDED_SMALL)}
    c_all = per_dev[-1].reshape(NDEV, D_MODEL)

    c16 = jnp.pad(c_all, ((0, 8), (0, 0)))
    (c_act,), _ = rowwise(lambda r, q: ([_silu(r[0])], []), [Rows(c16)], [], [(D_MODEL, BF16)], [], tile=16,
                          name="cond_act")
    n_ada = ada_w.shape[-1]
    ada_b_mine = lax.dynamic_slice_in_dim(ada_b, me * n_ada, n_ada, axis=1)
    ada_cols = []
    for l in range(DEPTH):
        bias = jnp.broadcast_to(ada_b_mine[l][None, :], (16, n_ada))
        (cols,) = matmul([[(c_act, ada_w[l])]], "nn", [F32], tm=16, tn=n_ada, tk=D_MODEL, extras=[bias],
                         epi=lambda accs, ex: [accs[0] + ex[0]], name="ada_fwd")
        ada_cols.append(cols[:8])
    ada_mine = all_to_all(jnp.stack(ada_cols, axis=1), "ada_to_owner")
    mod = jnp.moveaxis(ada_mine, 0, 1).reshape(DEPTH, 9, D_MODEL)


    w = {"ln_g": full_small["ln_g"], "ln_b": full_small["ln_b"], "hgrn_lb_logits": hgrn_lb_logits}
    for n in ("hgrn_norm_g", "mlstm_conv_b", "mlstm_wq", "mlstm_wk", "mlstm_gate_b", "mlstm_skip", "mlstm_norm_g",
              "rglru_wa", "rglru_wx"):
        w[n] = p[n][0]
    for n in ("mlstm_conv_w", "rglru_conv_w", "rglru_conv_b", "rglru_ba", "rglru_bx", "rglru_lambda"):
        w[n] = full_small[n][0]

    small_names = REPLICATED + SHARDED_SMALL
    small_handles = []

    def put_small(g_small_parts, d_modulation):
        parts_ = dict(g_small_parts, ada_b=d_modulation.reshape(DEPTH, 9 * D_MODEL))
        handles, token = exchange_start([_pack([parts_[n] for n in small_names])], True, "gather_start_small_grads")
        small_handles.append(handles)
        return token

    loss, grad_x, d_mod, g, last_sent = local_step(x[0], loss_target[0], mod, w, get_weights, put_grads, put_small,
                                                   started)
    loss = lax.psum(loss, ("x", "y", "c"))

    outs = {}
    def update_group(key, after):
        slots = landed(scatter_handles[key], lambda src: lax.dynamic_index_in_dim(src, me, 0, keepdims=False), after,
                       "scatter_wait_" + "_".join(map(str, key)))
        for n, sl in zip(names[key], slots):
            index = key[1:] if key[0] == "ffn" else (0,)
            outs[n] = adamw(p[n], mom[n][0], mom[n][1], sl, name="adamw_" + n, index=index, prev=outs.get(n))
        return [outs[n][0] for n in names[key]]

    full_shapes = [p[n].shape for n in REPLICATED] + [full_small[n].shape for n in SHARDED_SMALL]
    (all_small,) = landed(small_handles[0], lambda src: src, (last_sent, grad_x), "gather_wait_small_grads")
    summed = sum_slots(all_small, name="sum_small_grads")
    g_small = dict(zip(small_names, _unpack(summed, full_shapes)))
    rep = adamw(*[_pack([t[n] for n in REPLICATED]) for t in (p, {n: mom[n][0] for n in WEIGHTS},
                                                               {n: mom[n][1] for n in WEIGHTS})],
                _pack([g_small[n] for n in REPLICATED])[None], name="adamw_replicated")
    rep = [_unpack(r, [p[n].shape for n in REPLICATED]) for r in rep]
    for i, n in enumerate(REPLICATED):
        outs[n] = [r[i] for r in rep]
    g_mine = {n: lax.dynamic_slice_in_dim(g_small[n], me * p[n].shape[-1], p[n].shape[-1], axis=-1)
              for n in SHARDED_SMALL}
    shd = adamw(*[_pack([t[n] for n in SHARDED_SMALL]) for t in (p, {n: mom[n][0] for n in WEIGHTS},
                                                                  {n: mom[n][1] for n in WEIGHTS})],
                _pack([g_mine[n] for n in SHARDED_SMALL])[None], name="adamw_sharded_small")
    done = shd[0]
    for key in keys[:0:-1] + keys[:1]:
        done = update_group(key, done)
    shd = [_unpack(r, sharded_shapes) for r in shd]
    for i, n in enumerate(SHARDED_SMALL):
        outs[n] = [r[i] for r in shd]

    d_ada = all_small[:, :DEPTH * 9 * D_MODEL // 128].reshape(NDEV, DEPTH, 9 * D_MODEL)
    d_mine = lax.dynamic_slice_in_dim(d_ada, me * n_ada, n_ada, axis=2)
    g_ada = []
    for l in range(DEPTH):
        d16 = jnp.pad(d_mine[:, l], ((0, 8), (0, 0)))
        (gl,) = matmul([[(c_act, d16)]], "tn", [F32], tm=D_MODEL, tn=n_ada, tk=16, name="ada_bwd")
        g_ada.append(gl)
    outs["ada_w"] = adamw_nd(ada_w, *mom["ada_w"], jnp.stack(g_ada)[None], name="adamw_ada_w")

    result = [loss, grad_x[None]]
    for k in range(4):
        result += [outs[n][k].reshape(p[n].shape) for n in WEIGHTS]
    return tuple(result)
```

```python
import functools

import jax
import jax.numpy as jnp
import numpy as np
from jax import lax
from jax.experimental import pallas as pl
from jax.experimental.pallas import tpu as pltpu

F32 = jnp.float32
BF16 = jnp.bfloat16
HIGHEST = lax.Precision.HIGHEST

NDEV = 8
D_MODEL = 1024
D_FF = 2816
DEPTH = 2
CHUNK = 64
SUB = 16
HEADS = 4
HEAD_W = 128
MIX_W = HEADS * HEAD_W
AB_MAIN = 7 * MIX_W
AB_ALL = AB_MAIN + 128
CONV_W = 4
C_BLOCKS = 8
RG_C = 8.0
ALPHA = (2 * DEPTH) ** 0.25
FFN_RES_W = 0.5
NEG = -1e30

ADAM_LR = 0.001
ADAM_B1 = 0.9
ADAM_B2 = 0.999
ADAM_EPS = 1e-08
ADAM_WD = 0.01
ADAM_STEP = 10

VMEM_LIMIT = 56 * 1024 * 1024
GRAD_WIRE = jnp.bfloat16

NN = ((1,), (0,))
NT = ((1,), (1,))
TN = ((0,), (0,))


def _dot(a, b, dims, precision=None):
    return lax.dot_general(a, b, (dims, ((), ())), precision=precision, preferred_element_type=F32)


def _make_mm(dims, d_lhs, d_rhs, swap_lhs, swap_rhs, prec):
    def cast(v):
        return v.astype(BF16) if prec is None else v.astype(F32)

    @jax.custom_vjp
    def mm(a, b):
        return _dot(cast(a), cast(b), dims, prec)

    def fwd(a, b):
        return mm(a, b), (a, b)

    def bwd(res, g):
        a, b = res
        g = cast(g)
        da = _dot(cast(b), g, d_lhs, prec) if swap_lhs else _dot(g, cast(b), d_lhs, prec)
        db = _dot(g, cast(a), d_rhs, prec) if swap_rhs else _dot(cast(a), g, d_rhs, prec)
        return da.astype(a.dtype), db.astype(b.dtype)

    mm.defvjp(fwd, bwd)
    return mm


def _mm_family(prec):
    return (_make_mm(NN, NT, TN, False, False, prec), _make_mm(NT, NN, TN, False, True, prec),
            _make_mm(TN, NT, NN, True, False, prec))


def _round(v):
    return v.astype(BF16).astype(F32)


@jax.custom_vjp
def row_dot(a, n):
    return jnp.sum(_round(a) * _round(n), axis=1, keepdims=True)


def _row_dot_fwd(a, n):
    return row_dot(a, n), (a, n)


def _row_dot_bwd(res, g):
    a, n = res
    g = _round(g)
    return g * _round(n), jnp.sum(g * _round(a), axis=0, keepdims=True)


row_dot.defvjp(_row_dot_fwd, _row_dot_bwd)


@jax.custom_vjp
def col_dot(s, a):
    return jnp.sum(_round(s) * _round(a), axis=0, keepdims=True)


def _col_dot_fwd(s, a):
    return col_dot(s, a), (s, a)


def _col_dot_bwd(res, g):
    s, a = res
    g = _round(g)
    return jnp.sum(g * _round(a), axis=1, keepdims=True), _round(s) * g


col_dot.defvjp(_col_dot_fwd, _col_dot_bwd)

mm_nn, mm_nt, mm_tn = _mm_family(None)
mid_nn, mid_nt, mid_tn = _mm_family(lax.Precision.HIGH)
hi_nn, hi_nt, hi_tn = _mm_family(HIGHEST)


def _silu(v):
    return v * jax.nn.sigmoid(v)


def _log_sigmoid(v):
    return jnp.minimum(v, 0.0) - jnp.log1p(jnp.exp(-jnp.abs(v)))


def _softplus(v):
    return jnp.maximum(v, 0.0) + jnp.log1p(jnp.exp(-jnp.abs(v)))


def _neg_expm1(v):
    series = -v * (1.0 + v * (0.5 + v * (1.0 / 6.0 + v * (1.0 / 24.0 + v * (1.0 / 120.0)))))
    return jnp.where(v > -0.05, series, 1.0 - jnp.exp(v))


def _params(**kw):
    return pltpu.CompilerParams(vmem_limit_bytes=VMEM_LIMIT, **kw)


class Rows:
    def __init__(self, arr, block=None, split=None):
        self.arr = arr
        self.block = block
        self.split = split

    @property
    def width(self):
        return self.block[1] if self.block else self.arr.shape[1]


def _load(ref, split):
    if ref.ndim == 3:
        return [ref[k].astype(F32) for k in range(ref.shape[0])]
    if split is None:
        return ref[...].astype(F32)
    return [ref[:, k * split:(k + 1) * split].astype(F32) for k in range(ref.shape[1] // split)]


def _store(ref, val, accumulate=False):
    if isinstance(val, (list, tuple)):
        if ref.ndim == 3:
            for k, v in enumerate(val):
                ref[k] = (ref[k] + v if accumulate else v).astype(ref.dtype)
            return
        w = ref.shape[1] // len(val)
        for k, v in enumerate(val):
            sl = slice(k * w, (k + 1) * w)
            ref[:, sl] = (ref[:, sl] + v if accumulate else v).astype(ref.dtype)
    else:
        ref[...] = (ref[...] + val if accumulate else val).astype(ref.dtype)


def rowwise(fn, rows, params, out_rows, out_sums, *, tile, name, after=()):
    n_rows = rows[0].arr.shape[0]
    n_r, n_p, n_o = len(rows), len(params), len(out_rows)
    n_in = n_r + n_p + len(after)
    splits = [r.split for r in rows]

    def body(*refs):
        r_refs, p_refs = refs[:n_r], refs[n_r:n_r + n_p]
        o_refs, s_refs = refs[n_in:n_in + n_o], refs[n_in + n_o:]
        row_out, sum_out = fn([_load(r, s) for r, s in zip(r_refs, splits)], [_load(p, None) for p in p_refs])
        for ref, val in zip(o_refs, row_out):
            _store(ref, val)
        if s_refs:
            @pl.when(pl.program_id(0) == 0)
            def _():
                for ref in s_refs:
                    ref[...] = jnp.zeros(ref.shape, ref.dtype)

            for ref, val in zip(s_refs, sum_out):
                _store(ref, val, accumulate=True)

    in_specs = []
    for r in rows:
        blk = r.block[0] if r.block else 0
        in_specs.append(pl.BlockSpec((tile, r.width), functools.partial(lambda i, b: (i, b), b=blk)))
    for p in params:
        in_specs.append(pl.BlockSpec(p.shape, functools.partial(lambda i, n: (0,) * n, n=p.ndim)))
    in_specs += [pl.BlockSpec(memory_space=pl.ANY)] * len(after)
    out_shape = [jax.ShapeDtypeStruct((n_rows, w), dt) for w, dt in out_rows]
    out_specs = [pl.BlockSpec((tile, w), lambda i: (i, 0)) for w, _ in out_rows]
    for shp in out_sums:
        out_shape.append(jax.ShapeDtypeStruct(shp, F32))
        out_specs.append(pl.BlockSpec(shp, functools.partial(lambda i, n: (0,) * n, n=len(shp))))
    res = pl.pallas_call(
        body, name=name, grid=(n_rows // tile,), in_specs=in_specs, out_specs=out_specs, out_shape=out_shape,
        compiler_params=_params(dimension_semantics=("arbitrary",)),
    )(*[r.arr for r in rows], *params, *after)
    return res[:n_o], res[n_o:]


def rowwise_bwd(fn, rows, cots, params, want_rows, want_params, out_dtypes, *, tile, name, after=()):
    n = len(rows)

    def bwd(row_vals, param_vals):
        prim, cot = row_vals[:n], row_vals[n:]
        _, vjp = jax.vjp(lambda r, p: fn(r, p)[0], prim, param_vals)
        d_rows, d_params = vjp(cot)
        return [d_rows[i] for i in want_rows], [d_params[j] for j in want_params]

    out_rows = [(rows[i].width, dt) for i, dt in zip(want_rows, out_dtypes)]
    out_sums = [params[j].shape for j in want_params]
    return rowwise(bwd, list(rows) + list(cots), params, out_rows, out_sums, tile=tile, name=name, after=after)


def matmul(groups, mode, outs, *, tm, tn, tk, extras=(), epi=None, name, n_outer=False, after=()):
    a0, b0 = groups[0][0]
    if mode == "tn":
        k_dim, m_dim = a0.shape
    else:
        m_dim, k_dim = a0.shape
    n_dim = b0.shape[0] if mode == "nt" else b0.shape[1]
    tm, tn, tk = min(tm, m_dim), min(tn, n_dim), min(tk, k_dim)
    assert m_dim % tm == 0 and n_dim % tn == 0 and k_dim % tk == 0, (name, m_dim, n_dim, k_dim)
    nk = k_dim // tk
    pairs = [p for g in groups for p in g]
    n_pairs, n_groups, n_ex, n_out = len(pairs), len(groups), len(extras), len(outs)
    dims = {"nn": NN, "nt": NT, "tn": TN}[mode]

    def body(*refs):
        ab = refs[:2 * n_pairs]
        ex = refs[2 * n_pairs:2 * n_pairs + n_ex]
        n_in = 2 * n_pairs + n_ex + len(after)
        o_refs = refs[n_in:n_in + n_out]
        accs = refs[n_in + n_out:]

        def partial_sums():
            sums, p = [], 0
            for g in groups:
                tot = None
                for _ in g:
                    d = _dot(ab[2 * p][...].astype(BF16), ab[2 * p + 1][...].astype(BF16), dims)
                    tot = d if tot is None else tot + d
                    p += 1
                sums.append(tot)
            return sums

        def finish(vals):
            res = epi(vals, [e[...] for e in ex]) if epi else vals
            for ref, v in zip(o_refs, res):
                ref[...] = v.astype(ref.dtype)

        if nk == 1:
            finish(partial_sums())
        else:
            k = pl.program_id(2)

            @pl.when(k == 0)
            def _():
                for acc in accs:
                    acc[...] = jnp.zeros(acc.shape, F32)

            for acc, s in zip(accs, partial_sums()):
                acc[...] += s

            @pl.when(k == nk - 1)
            def _():
                finish([acc[...] for acc in accs])

    def at(pick):
        return (lambda j, i, k: pick(i, j, k)) if n_outer else pick

    if mode == "nn":
        a_spec = pl.BlockSpec((tm, tk), at(lambda i, j, k: (i, k)))
        b_spec = pl.BlockSpec((tk, tn), at(lambda i, j, k: (k, j)))
    elif mode == "nt":
        a_spec = pl.BlockSpec((tm, tk), at(lambda i, j, k: (i, k)))
        b_spec = pl.BlockSpec((tn, tk), at(lambda i, j, k: (j, k)))
    else:
        a_spec = pl.BlockSpec((tk, tm), at(lambda i, j, k: (k, i)))
        b_spec = pl.BlockSpec((tk, tn), at(lambda i, j, k: (k, j)))
    mn_spec = pl.BlockSpec((tm, tn), at(lambda i, j, k: (i, j)))
    grid = (n_dim // tn, m_dim // tm, nk) if n_outer else (m_dim // tm, n_dim // tn, nk)
    return pl.pallas_call(
        body, name=name, grid=grid,
        in_specs=[a_spec, b_spec] * n_pairs + [mn_spec] * n_ex + [pl.BlockSpec(memory_space=pl.ANY)] * len(after),
        out_specs=[mn_spec] * n_out,
        out_shape=[jax.ShapeDtypeStruct((m_dim, n_dim), dt) for dt in outs],
        scratch_shapes=[pltpu.VMEM((tm, tn), F32)] * (n_groups if nk > 1 else 0),
        compiler_params=_params(dimension_semantics=("parallel", "parallel", "arbitrary")),
    )(*[x for p in pairs for x in p], *extras, *after)


def conv_fwd(x, w, b, *, tile, name):
    n_rows, width = x.arr.shape[0], x.width
    blk = x.block[0] if x.block else 0

    def body(x_ref, halo_ref, w_ref, b_ref, y_ref, buf):
        i = pl.program_id(0)
        halo = halo_ref[...]
        buf[0:8, :] = jnp.where(i == 0, jnp.zeros_like(halo), _round(halo))
        buf[8:, :] = _round(x_ref[...])
        acc = jnp.zeros((tile, width), F32)
        for j in range(CONV_W):
            s = CONV_W - 1 - j
            acc = acc + _round(w_ref[j:j + 1, :]) * buf[8 - s:8 - s + tile, :]
        y_ref[...] = acc + b_ref[...]

    hb = tile // 8
    return pl.pallas_call(
        body, name=name, grid=(n_rows // tile,),
        in_specs=[pl.BlockSpec((tile, width), lambda i: (i, blk)),
                  pl.BlockSpec((8, width), lambda i: (jnp.maximum(i * hb - 1, 0), blk)),
                  pl.BlockSpec((CONV_W, width), lambda i: (0, 0)),
                  pl.BlockSpec((1, width), lambda i: (0, 0))],
        out_specs=pl.BlockSpec((tile, width), lambda i: (i, 0)),
        out_shape=jax.ShapeDtypeStruct((n_rows, width), F32),
        scratch_shapes=[pltpu.VMEM((tile + 8, width), F32)],
        compiler_params=_params(dimension_semantics=("arbitrary",)),
    )(x.arr, x.arr, w, b)


def conv_bwd(x, dy, w, *, tile, name):
    n_rows, width = x.arr.shape[0], x.width
    blk = x.block[0] if x.block else 0
    n_tiles = n_rows // tile

    def body(x_ref, xh_ref, dy_ref, dyh_ref, w_ref, dx_ref, dw_ref, db_ref, xbuf, dbuf):
        i = pl.program_id(0)
        xh, dyh = xh_ref[...], dyh_ref[...]
        xbuf[0:8, :] = jnp.where(i == 0, jnp.zeros_like(xh), _round(xh))
        xbuf[8:, :] = _round(x_ref[...])
        dy_t = dy_ref[...]
        dy_r = _round(dy_t)
        dbuf[0:tile, :] = dy_r
        dbuf[tile:, :] = jnp.where(i == n_tiles - 1, jnp.zeros_like(dyh), _round(dyh))

        @pl.when(i == 0)
        def _():
            dw_ref[...] = jnp.zeros(dw_ref.shape, F32)
            db_ref[...] = jnp.zeros(db_ref.shape, F32)

        acc = jnp.zeros((tile, width), F32)
        for j in range(CONV_W):
            s = CONV_W - 1 - j
            acc = acc + _round(w_ref[j:j + 1, :]) * dbuf[s:s + tile, :]
            dw_ref[j:j + 1, :] += jnp.sum(dy_r * xbuf[8 - s:8 - s + tile, :], axis=0, keepdims=True)
        dx_ref[...] = acc.astype(dx_ref.dtype)
        db_ref[...] += jnp.sum(dy_t, axis=0, keepdims=True)

    hb = tile // 8
    return pl.pallas_call(
        body, name=name, grid=(n_tiles,),
        in_specs=[pl.BlockSpec((tile, width), lambda i: (i, blk)),
                  pl.BlockSpec((8, width), lambda i: (jnp.maximum(i * hb - 1, 0), blk)),
                  pl.BlockSpec((tile, width), lambda i: (i, 0)),
                  pl.BlockSpec((8, width), lambda i: (jnp.minimum((i + 1) * hb, n_tiles * hb - 1), 0)),
                  pl.BlockSpec((CONV_W, width), lambda i: (0, 0))],
        out_specs=[pl.BlockSpec((tile, width), lambda i: (i, 0)),
                   pl.BlockSpec((CONV_W, width), lambda i: (0, 0)),
                   pl.BlockSpec((1, width), lambda i: (0, 0))],
        out_shape=[jax.ShapeDtypeStruct((n_rows, width), BF16),
                   jax.ShapeDtypeStruct((CONV_W, width), F32),
                   jax.ShapeDtypeStruct((1, width), F32)],
        scratch_shapes=[pltpu.VMEM((tile + 8, width), F32), pltpu.VMEM((tile + 8, width), F32)],
        compiler_params=_params(dimension_semantics=("arbitrary",)),
    )(x.arr, x.arr, dy, dy, w)


def _pieces(ref, col_w, row_h):
    n_c, n_r = ref.shape[1] // col_w, ref.shape[0] // row_h
    return [[ref[r * row_h:(r + 1) * row_h, c * col_w:(c + 1) * col_w].astype(F32) for r in range(n_r)]
            for c in range(n_c)]


def _store_pieces(ref, vals, col_w, row_h):
    for c, col in enumerate(vals):
        for r, v in enumerate(col):
            ref[r * row_h:(r + 1) * row_h, c * col_w:(c + 1) * col_w] = v.astype(ref.dtype)


def _x_spec(x, n_chunks, reverse):
    blk = x.block[0] if x.block else 0
    if reverse:
        return pl.BlockSpec((CHUNK, x.width), functools.partial(lambda n, b: (n_chunks - 1 - n, b), b=blk))
    return pl.BlockSpec((CHUNK, x.width), functools.partial(lambda n, b: (n, b), b=blk))


def chunk_scan_fwd(step, xs, piece, state_shapes, out_widths, *, name):
    n_rows = xs[0].arr.shape[0]
    n_chunks = n_rows // CHUNK
    n_x, n_s, n_o = len(xs), len(state_shapes), len(out_widths)

    def body(*refs):
        x_refs, o_refs = refs[:n_x], refs[n_x:n_x + n_o]
        keep_refs, st_refs = refs[n_x + n_o:n_x + n_o + n_s], refs[n_x + n_o + n_s:]

        @pl.when(pl.program_id(0) == 0)
        def _():
            for st in st_refs:
                st[...] = jnp.zeros(st.shape, F32)

        states = [st[...] for st in st_refs]
        for keep, s in zip(keep_refs, states):
            keep[...] = s
        new_states, outs = step(states, [_pieces(x, *p) for x, p in zip(x_refs, piece)])
        for st, s in zip(st_refs, new_states):
            st[...] = s
        for o, v in zip(o_refs, outs):
            o[...] = v

    out_shape = [jax.ShapeDtypeStruct((n_rows, w), F32) for w in out_widths]
    out_specs = [pl.BlockSpec((CHUNK, w), lambda n: (n, 0)) for w in out_widths]
    for shp in state_shapes:
        out_shape.append(jax.ShapeDtypeStruct((n_chunks,) + shp, F32))
        out_specs.append(pl.BlockSpec((None,) + shp, lambda n: (n, 0, 0)))
    res = pl.pallas_call(
        body, name=name, grid=(n_chunks,),
        in_specs=[_x_spec(x, n_chunks, False) for x in xs],
        out_specs=out_specs, out_shape=out_shape,
        scratch_shapes=[pltpu.VMEM(shp, F32) for shp in state_shapes],
        compiler_params=_params(dimension_semantics=("arbitrary",)),
    )(*[x.arr for x in xs])
    return res[:n_o], res[n_o:]


def chunk_scan_bwd(step, xs, piece, kept, d_outs, *, name):
    n_rows = xs[0].arr.shape[0]
    n_chunks = n_rows // CHUNK
    n_x, n_s, n_o = len(xs), len(kept), len(d_outs)
    state_shapes = [k.shape[1:] for k in kept]

    def body(*refs):
        x_refs, k_refs = refs[:n_x], refs[n_x:n_x + n_s]
        do_refs = refs[n_x + n_s:n_x + n_s + n_o]
        dx_refs = refs[n_x + n_s + n_o:2 * n_x + n_s + n_o]
        ds_refs = refs[2 * n_x + n_s + n_o:]

        @pl.when(pl.program_id(0) == 0)
        def _():
            for ds in ds_refs:
                ds[...] = jnp.zeros(ds.shape, F32)

        states = [k[...] for k in k_refs]
        inputs = [_pieces(x, *p) for x, p in zip(x_refs, piece)]
        _, vjp = jax.vjp(step, states, inputs)
        d_states, d_inputs = vjp(([ds[...] for ds in ds_refs], [do[...] for do in do_refs]))
        for ds, v in zip(ds_refs, d_states):
            ds[...] = v
        for dx, v, p in zip(dx_refs, d_inputs, piece):
            _store_pieces(dx, v, *p)

    rev3 = lambda n: (n_chunks - 1 - n, 0, 0)
    rev2 = lambda n: (n_chunks - 1 - n, 0)
    return pl.pallas_call(
        body, name=name, grid=(n_chunks,),
        in_specs=[_x_spec(x, n_chunks, True) for x in xs]
        + [pl.BlockSpec((None,) + shp, rev3) for shp in state_shapes]
        + [pl.BlockSpec((CHUNK, d.shape[1]), rev2) for d in d_outs],
        out_specs=[pl.BlockSpec((CHUNK, x.width), rev2) for x in xs],
        out_shape=[jax.ShapeDtypeStruct((n_rows, x.width), F32) for x in xs],
        scratch_shapes=[pltpu.VMEM(shp, F32) for shp in state_shapes],
        compiler_params=_params(dimension_semantics=("arbitrary",)),
    )(*[x.arr for x in xs], *kept, *d_outs)


def _tri(n, strict=False):
    r = lax.broadcasted_iota(jnp.int32, (n, n), 0)
    c = lax.broadcasted_iota(jnp.int32, (n, n), 1)
    return (r > c) if strict else (r >= c)


def hgrn2_step(states, inputs):
    q_all, k_all, v_all, lf_all = inputs
    n_sub = CHUNK // SUB
    low = _tri(SUB).astype(F32)
    ones_sub = jnp.ones((SUB, SUB), F32)
    ones_chunk = jnp.ones((CHUNK, HEAD_W), F32)
    new_states, outs = [], []
    for h in range(HEADS):
        state = states[h]
        q, k, v, lf = q_all[h], k_all[h], v_all[h], lf_all[h]
        cum = [mid_nn(low, lf[i]) for i in range(n_sub)]
        tot = [mid_nn(ones_sub, lf[i]) for i in range(n_sub)]
        start = [jnp.zeros((SUB, HEAD_W), F32)]
        for i in range(n_sub):
            start.append(start[-1] + tot[i])
        q_in = [q[i] * jnp.exp(cum[i]) for i in range(n_sub)]
        intra = []
        for i in range(n_sub):
            keys = [k[j] * jnp.exp(start[i] - start[j] - cum[j]) for j in range(i)]
            keys.append(k[i] * jnp.exp(jnp.minimum(-cum[i], 80.0)))
            att = mid_nt(q_in[i], jnp.concatenate(keys, axis=0))
            r_id = lax.broadcasted_iota(jnp.int32, att.shape, 0)
            c_id = lax.broadcasted_iota(jnp.int32, att.shape, 1)
            att = jnp.where(c_id - SUB * i <= r_id, att, 0.0)
            intra.append(mm_nn(att, jnp.concatenate(v[:i + 1], axis=0)))
        q_state = jnp.concatenate([q_in[i] * jnp.exp(start[i]) for i in range(n_sub)], axis=0)
        out = mm_nn(q_state, state) + jnp.concatenate(intra, axis=0)
        k_end = jnp.concatenate([k[j] * jnp.exp(start[n_sub] - start[j] - cum[j]) for j in range(n_sub)], axis=0)
        decay = jnp.exp(mid_tn(jnp.concatenate(lf, axis=0), ones_chunk))
        new_states.append(decay * state + mm_tn(k_end, jnp.concatenate(v, axis=0)))
        outs.append(out)
    return new_states, [jnp.concatenate(outs, axis=1)]


def mlstm_step(states, inputs):
    q_all, k_all, v_all, gates = inputs
    gates = gates[0][0]
    c_st, n_st, m_st = states[:HEADS], states[HEADS:2 * HEADS], states[2 * HEADS:]
    lane = lax.broadcasted_iota(jnp.int32, (CHUNK, 128), 1)
    low = _tri(CHUNK).astype(F32)
    causal = _tri(CHUNK)
    gates_cum = hi_nn(low, gates)
    new_c, new_n, new_m, outs = [], [], [], []
    for h in range(HEADS):
        q, k, v = q_all[h][0], k_all[h][0], v_all[h][0]
        pick_i = (lane == h).astype(F32)
        pick_f = (lane == HEADS + h).astype(F32)
        li_col = jnp.sum(gates * pick_i, axis=1, keepdims=True)
        lf_col = jnp.sum(gates * pick_f, axis=1, keepdims=True)
        b_col = jnp.sum(gates_cum * pick_f, axis=1, keepdims=True)
        by_key = hi_nt(pick_i, gates) - hi_nt(pick_f, gates_cum)
        d_mat = jnp.where(causal, b_col + by_key, NEG)
        m_prev = lax.stop_gradient(jnp.max(m_st[h], axis=1, keepdims=True))
        g_inter = b_col + m_prev
        m_t = lax.stop_gradient(jnp.maximum(g_inter, jnp.max(d_mat, axis=1, keepdims=True)))
        w_inter = jnp.exp(g_inter - m_t)
        aw = jnp.exp(d_mat - m_t) * mm_nt(q, k)
        num = w_inter * mm_nn(q, c_st[h]) + mm_nn(aw, v)
        den = w_inter * row_dot(q, n_st[h]) + jnp.sum(aw, axis=1, keepdims=True)
        outs.append(num / jnp.maximum(jnp.abs(den), jnp.exp(-m_t)))
        b_end = jnp.sum(lf_col, axis=0, keepdims=True)
        g_state = b_end + m_prev
        s_w = b_end - b_col + li_col
        m_next = lax.stop_gradient(jnp.maximum(g_state, jnp.max(s_w, axis=0, keepdims=True)))
        dec = jnp.exp(g_state - m_next)
        w_s = jnp.exp(s_w - m_next)
        kw = k * w_s
        new_c.append(dec * c_st[h] + mm_tn(kw, v))
        new_n.append(dec * n_st[h] + col_dot(w_s, k))
        new_m.append(jnp.broadcast_to(m_next, (1, HEAD_W)))
    return new_c + new_n + new_m, [jnp.concatenate(outs, axis=1)]


def lru_fwd(a, u, *, tile, name):
    n_rows, width = a.shape

    def body(a_ref, u_ref, h_ref, carry):
        @pl.when(pl.program_id(0) == 0)
        def _():
            carry[...] = jnp.zeros(carry.shape, F32)

        h = carry[...]
        for t in range(tile):
            h = a_ref[t:t + 1, :] * h + u_ref[t:t + 1, :]
            h_ref[t:t + 1, :] = h
        carry[...] = h

    spec = pl.BlockSpec((tile, width), lambda i: (i, 0))
    return pl.pallas_call(
        body, name=name, grid=(n_rows // tile,), in_specs=[spec, spec], out_specs=spec,
        out_shape=jax.ShapeDtypeStruct((n_rows, width), F32),
        scratch_shapes=[pltpu.VMEM((1, width), F32)],
        compiler_params=_params(dimension_semantics=("arbitrary",)),
    )(a, u)


def lru_bwd(a, h, dh, *, tile, name):
    n_rows, width = a.shape
    n_tiles = n_rows // tile
    hb = tile // 8

    def body(a_ref, h_ref, hh_ref, dh_ref, da_ref, du_ref, carry):
        i = pl.program_id(0)

        @pl.when(i == 0)
        def _():
            carry[...] = jnp.zeros(carry.shape, F32)

        c = carry[...]
        for t in range(tile - 1, -1, -1):
            g = dh_ref[t:t + 1, :] + c
            du_ref[t:t + 1, :] = g
            if t:
                h_prev = h_ref[t - 1:t, :]
            else:
                h_prev = jnp.where(i == n_tiles - 1, 0.0, hh_ref[7:8, :])
            da_ref[t:t + 1, :] = g * h_prev
            c = a_ref[t:t + 1, :] * g
        carry[...] = c

    rev = lambda i: (n_tiles - 1 - i, 0)
    spec = pl.BlockSpec((tile, width), rev)
    halo = pl.BlockSpec((8, width), lambda i: (jnp.maximum((n_tiles - 1 - i) * hb - 1, 0), 0))
    return pl.pallas_call(
        body, name=name, grid=(n_tiles,), in_specs=[spec, spec, halo, spec], out_specs=[spec, spec],
        out_shape=[jax.ShapeDtypeStruct((n_rows, width), F32)] * 2,
        scratch_shapes=[pltpu.VMEM((1, width), F32)],
        compiler_params=_params(dimension_semantics=("arbitrary",)),
    )(a, h, h, dh)


def _layer_norm(z, g, b):
    mu = jnp.mean(z, axis=-1, keepdims=True)
    zc = z - mu
    var = jnp.mean(zc * zc, axis=-1, keepdims=True)
    return zc * lax.rsqrt(var + 1e-5) * g + b


def pre_fn(rows, params):
    (x,), (scale, shift) = rows, params
    return [x * (1.0 + scale) + shift], []


def make_post_fn(weight, with_next):
    def fn(rows, params):
        x, y = rows
        gate, g, b = params[:3]
        xo = _layer_norm(ALPHA * x + weight * (1.0 + gate) * y, g, b)
        if with_next:
            return [xo, xo * (1.0 + params[3]) + params[4]], []
        return [xo], []

    return fn


def make_last_fn(weight):
    post = make_post_fn(weight, False)

    def fn(rows, params):
        x, y, target = rows
        err = post([x, y], params)[0][0] - target
        loss = 0.5 * jnp.sum(jnp.mean(err * err, axis=-1, keepdims=True), axis=0, keepdims=True)
        return [err * (1.0 / D_MODEL)], [jnp.broadcast_to(loss, (1, 128))]

    return fn


def mix_a_fn(rows, params):
    a_q, a_f, xconv, graw = rows
    l0, l1, l2, wq, wk, gate_b = params
    mx = jnp.maximum(jnp.maximum(l0, l1), l2)
    e0, e1, e2 = jnp.exp(l0 - mx), jnp.exp(l1 - mx), jnp.exp(l2 - mx)
    lb = e0 / (e0 + e1 + e2)
    f = lb + (1.0 - lb) * jax.nn.sigmoid(a_f)
    xc = _silu(xconv)
    q_b = mm_nn(xc, wq)
    k_b = mm_nn(xc, wk) * (HEAD_W ** -0.5)
    g = graw + gate_b
    lane = lax.broadcasted_iota(jnp.int32, g.shape, 1)
    gates = jnp.where(lane < HEADS, g, _log_sigmoid(g))
    return [_silu(a_q), 1.0 - f, jnp.log(f), xc, q_b, k_b, gates], []


def _head_norm(v, g, center):
    if center:
        v = v - jnp.mean(v, axis=-1, keepdims=True)
    return v * lax.rsqrt(jnp.mean(v * v, axis=-1, keepdims=True) + 1e-6) * g


def mix_b_fn(rows, params):
    o_a, a_g, h_b, xc, b_z = rows
    hg, mg, skip = params
    y_a = [_head_norm(o_a[h], hg[h], False) * _silu(a_g[h]) for h in range(HEADS)]
    y_b = [(_head_norm(h_b[h], mg[h], True) + skip[h] * xc[h]) * _silu(b_z[h]) for h in range(HEADS)]
    return [y_a + y_b], []


def lru_a_fn(rows, params):
    (xr,) = rows
    wa, wx, ba, bx, lam = params
    a_out, u_out = [], []
    for n in range(C_BLOCKS):
        r = jax.nn.sigmoid(mm_nt(xr[n], wa[n]) + ba[n])
        i = jax.nn.sigmoid(mm_nt(xr[n], wx[n]) + bx[n])
        log_a = -RG_C * r * _softplus(-lam[n])
        a_out.append(jnp.exp(log_a))
        u_out.append(jnp.sqrt(_neg_expm1(2.0 * log_a)) * i * xr[n])
    return [a_out, u_out], []


def lru_b_fn(rows, params):
    h, y_br = rows
    return [h * jax.nn.gelu(y_br)], []


def swiglu_epi(accs, extras):
    h1, h3 = accs
    return [h1, h3, _silu(h1) * h3]


def swiglu_bwd_epi(accs, extras):
    (da,), (h1, h3) = accs, extras
    h1, h3 = h1.astype(F32), h3.astype(F32)
    sig = jax.nn.sigmoid(h1)
    return [da * h3 * sig * (1.0 + h1 * (1.0 - sig)), da * h1 * sig]


ROW_TILE = 512


def _chunks(v, n):
    return v.reshape(n, 1, v.shape[-1] // n)


def _dense_blocks(w):
    n, b, _ = w.shape
    by_row = jnp.swapaxes(w, 1, 2).reshape(n * b, b)
    spread = jnp.dot(by_row, _column_picker(n, b).T, precision=HIGHEST)
    return spread * _block_mask(n, b)


def _column_picker(n, b):
    return jnp.asarray(np.tile(np.eye(b, dtype=np.float32), (n, 1)))


def _block_mask(n, b):
    return jnp.asarray(np.kron(np.eye(n, dtype=np.float32), np.ones((b, b), np.float32)))


def _block_diag_of(m, n, b):
    by_row = jnp.dot(m * _block_mask(n, b), _column_picker(n, b), precision=HIGHEST)
    return jnp.swapaxes(by_row.reshape(n, b, b), 1, 2)


def local_step(x, target, mod, w, get_weights, put_grads, put_small, first_after=()):
    s_len = x.shape[0]
    tile = min(ROW_TILE, s_len)
    row = lambda v: v.reshape(1, -1)
    mrow = lambda l, j, k: mod[l, 3 * j + k].reshape(1, D_MODEL)
    g = {}
    d_mod = [[None] * 9 for _ in range(DEPTH)]
    d_ln_g = [[None] * 3 for _ in range(DEPTH)]
    d_ln_b = [[None] * 3 for _ in range(DEPTH)]
    subs = [(l, j) for l in range(DEPTH) for j in range(3)]
    weight_of = lambda j: 1.0 if j == 1 else FFN_RES_W

    wq_d = _dense_blocks(w["mlstm_wq"])
    wk_d = _dense_blocks(w["mlstm_wk"])
    wa_b, wx_b = w["rglru_wa"].astype(BF16), w["rglru_wx"].astype(BF16)
    gate_b = jnp.pad(w["mlstm_gate_b"].reshape(1, 8), ((0, 0), (0, 120)))
    lb_rows = [row(w["hgrn_lb_logits"][k]) for k in range(3)]
    mix_a_params = lb_rows + [wq_d, wk_d, gate_b]
    mix_b_params = [_chunks(row(w["hgrn_norm_g"]), HEADS), _chunks(row(w["mlstm_norm_g"]), HEADS),
                    _chunks(row(w["mlstm_skip"]), HEADS)]
    lru_a_params = [wa_b, wx_b, _chunks(row(w["rglru_ba"]), C_BLOCKS), _chunks(row(w["rglru_bx"]), C_BLOCKS),
                    _chunks(row(w["rglru_lambda"]), C_BLOCKS)]
    mconv_w, mconv_b = w["mlstm_conv_w"], row(w["mlstm_conv_b"])
    rconv_w, rconv_b = w["rglru_conv_w"], row(w["rglru_conv_b"])
    hg_piece = [(HEAD_W, SUB)] * 4
    ml_piece = [(HEAD_W, CHUNK)] * 3 + [(128, CHUNK)]

    (t,), _ = rowwise(pre_fn, [Rows(x)], [mrow(0, 0, 1), mrow(0, 0, 0)], [(D_MODEL, BF16)], [],
                      tile=tile, name="pre", after=tuple(first_after))
    saved = {}
    x_in = x
    for idx, (l, j) in enumerate(subs):
        sv = {"x": x_in, "t": t}
        if j != 1:
            (w1, w3, w2), begun = get_weights(("ffn", l, j // 2), t)
            h1, h3, act = matmul([[(t, w1)], [(t, w3)]], "nn", [BF16, BF16, BF16], tm=512, tn=1408, tk=D_MODEL,
                                 epi=swiglu_epi, name="ffn_up", n_outer=True, after=begun)
            (y,) = matmul([[(act, w2)]], "nn", [F32], tm=1024, tn=1024, tk=1408, name="ffn_down")
            sv.update(h1=h1, h3=h3, act=act, big=(w1, w3, w2))
        elif l == 0:
            (ab_w_in, ab_w_out), begun = get_weights(("ab",), t)
            sv["big"] = (ab_w_in, ab_w_out)
            (proj,) = matmul([[(t, ab_w_in)]], "nn", [F32], tm=256, tn=AB_ALL, tk=D_MODEL, name="ab_in", after=begun)
            xconv = conv_fwd(Rows(proj, (4, MIX_W)), mconv_w, mconv_b, tile=tile, name="mconv")
            a_rows = [Rows(proj, (0, MIX_W)), Rows(proj, (1, MIX_W)), Rows(xconv), Rows(proj, (AB_MAIN // 128, 128))]
            (q_a, k_a, lf_a, xc, q_b, k_b, gates), _ = rowwise(
                mix_a_fn, a_rows, mix_a_params, [(MIX_W, F32)] * 6 + [(128, F32)], [], tile=tile, name="mix_a")
            hg_xs = [Rows(q_a), Rows(k_a), Rows(proj, (2, MIX_W)), Rows(lf_a)]
            (o_a,), hg_kept = chunk_scan_fwd(hgrn2_step, hg_xs, hg_piece, [(HEAD_W, HEAD_W)] * HEADS, [MIX_W],
                                             name="hgrn2_fwd")
            ml_xs = [Rows(q_b), Rows(k_b), Rows(proj, (5, MIX_W)), Rows(gates)]
            ml_states = [(HEAD_W, HEAD_W)] * HEADS + [(1, HEAD_W)] * (2 * HEADS)
            (h_b,), ml_kept = chunk_scan_fwd(mlstm_step, ml_xs, ml_piece, ml_states, [MIX_W], name="mlstm_fwd")
            b_rows = [Rows(o_a, split=HEAD_W), Rows(proj, (3, MIX_W), HEAD_W), Rows(h_b, split=HEAD_W),
                      Rows(xc, split=HEAD_W), Rows(proj, (6, MIX_W), HEAD_W)]
            (ycat,), _ = rowwise(mix_b_fn, b_rows, mix_b_params, [(2 * MIX_W, BF16)], [], tile=tile, name="mix_b")
            (y,) = matmul([[(ycat, ab_w_out)]], "nn", [F32], tm=512, tn=1024, tk=D_MODEL, name="ab_out")
            sv.update(proj=proj, xconv=xconv, a_rows=a_rows, hg_xs=hg_xs, hg_kept=hg_kept, ml_xs=ml_xs,
                      ml_kept=ml_kept, b_rows=b_rows, ycat=ycat)
        else:
            (rg_w_in, rg_w_out), begun = get_weights(("rg",), t)
            sv["big"] = (rg_w_in, rg_w_out)
            (proj,) = matmul([[(t, rg_w_in)]], "nn", [F32], tm=512, tn=1024, tk=D_MODEL, name="rg_in", after=begun)
            xr = conv_fwd(Rows(proj, (1, D_MODEL)), rconv_w, rconv_b, tile=tile, name="rconv")
            (a_t, u_t), _ = rowwise(lru_a_fn, [Rows(xr, split=128)], lru_a_params, [(D_MODEL, F32)] * 2, [],
                                    tile=tile, name="lru_a")
            h = lru_fwd(a_t, u_t, tile=min(128, s_len), name="lru_fwd")
            b_rows = [Rows(h), Rows(proj, (0, D_MODEL))]
            (hgate,), _ = rowwise(lru_b_fn, b_rows, [], [(D_MODEL, BF16)], [], tile=tile, name="lru_b")
            (y,) = matmul([[(hgate, rg_w_out)]], "nn", [F32], tm=512, tn=1024, tk=D_MODEL, name="rg_out")
            sv.update(proj=proj, xr=xr, a_t=a_t, h=h, b_rows=b_rows, hgate=hgate)
        sv["y"] = y
        post_params = [mrow(l, j, 2), row(w["ln_g"][l, j]), row(w["ln_b"][l, j])]
        if idx + 1 < len(subs):
            nl, nj = subs[idx + 1]
            post_params += [mrow(nl, nj, 1), mrow(nl, nj, 0)]
            (x_out, t), _ = rowwise(make_post_fn(weight_of(j), True), [Rows(x_in), Rows(y)], post_params,
                                    [(D_MODEL, F32), (D_MODEL, BF16)], [], tile=tile, name="post")
        else:
            (d_xo,), (loss_row,) = rowwise(make_last_fn(weight_of(j)), [Rows(x_in), Rows(y), Rows(target)],
                                           post_params, [(D_MODEL, F32)], [(1, 128)], tile=tile, name="post_loss")
            x_out = None
        sv["post_params"] = post_params
        saved[(l, j)] = sv
        x_in = x_out
    loss = loss_row[0, 0]

    d_t_next = None
    sent = []
    deferred = []

    def hand_over(key, grads):
        token = put_grads(key, grads)
        if token is not None:
            sent.append(token)

    for idx in range(len(subs) - 1, -1, -1):
        l, j = subs[idx]
        sv = saved[(l, j)]
        has_next = idx + 1 < len(subs)
        cots = [Rows(d_xo)] + ([Rows(d_t_next)] if has_next else [])
        want_p = [0, 1, 2] + ([3, 4] if has_next else [])
        (d_xres, d_y), d_par = rowwise_bwd(
            make_post_fn(weight_of(j), has_next), [Rows(sv["x"]), Rows(sv["y"])], cots, sv["post_params"],
            [0, 1], want_p, [F32, BF16], tile=tile, name="post_bwd", after=tuple(sent))
        sent.clear()
        d_mod[l][3 * j + 2], d_ln_g[l][j], d_ln_b[l][j] = d_par[:3]
        if has_next:
            nl, nj = subs[idx + 1]
            d_mod[nl][3 * nj + 1], d_mod[nl][3 * nj] = d_par[3:]
        t = sv["t"]
        if j != 1:
            w1, w3, w2 = sv["big"]
            d_h1, d_h3 = matmul([[(d_y, w2)]], "nt", [BF16, BF16], tm=512, tn=1408, tk=D_MODEL,
                                extras=[sv["h1"], sv["h3"]], epi=swiglu_bwd_epi, name="ffn_down_bwd", n_outer=True)

            def ffn_weight_grads(after, ops=(sv["act"], t, d_y, d_h1, d_h3), key=("ffn", l, j // 2)):
                wait = () if after is None else (after,)
                act, t_in, d_out, d_1, d_3 = ops
                (g_w2,) = matmul([[(act, d_out)]], "tn", [GRAD_WIRE], tm=1408, tn=1024, tk=2048, name="ffn_dw2",
                                 after=wait)
                (g_w1,) = matmul([[(t_in, d_1)]], "tn", [GRAD_WIRE], tm=1024, tn=1408, tk=2048, name="ffn_dw1",
                                 after=wait)
                (g_w3,) = matmul([[(t_in, d_3)]], "tn", [GRAD_WIRE], tm=1024, tn=1408, tk=2048, name="ffn_dw3",
                                 after=wait)
                hand_over(key, [g_w1, g_w3, g_w2])

            if idx:
                ffn_weight_grads(None)
            else:
                deferred.append(ffn_weight_grads)
            (d_t,) = matmul([[(d_h1, w1), (d_h3, w3)]], "nt", [F32], tm=1024, tn=1024, tk=1408, name="ffn_up_bwd")
        elif l == 0:
            ab_w_in, ab_w_out = sv["big"]
            (d_ycat,) = matmul([[(d_y, ab_w_out)]], "nt", [F32], tm=512, tn=1024, tk=D_MODEL, name="ab_out_bwd")
            (g_out,) = matmul([[(sv["ycat"], d_y)]], "tn", [GRAD_WIRE], tm=1024, tn=1024, tk=2048, name="ab_dwout")
            (d_oa, d_ag, d_hb, d_xc, d_bz), (d_hg, d_mg, d_skip) = rowwise_bwd(
                mix_b_fn, sv["b_rows"], [Rows(d_ycat, split=HEAD_W)], mix_b_params, [0, 1, 2, 3, 4], [0, 1, 2],
                [F32, BF16, F32, F32, BF16], tile=tile, name="mix_b_bwd")
            g["hgrn_norm_g"], g["mlstm_norm_g"], g["mlstm_skip"] = (v.reshape(1, MIX_W) for v in (d_hg, d_mg, d_skip))
            d_qb, d_kb, d_bv, d_gates = chunk_scan_bwd(mlstm_step, sv["ml_xs"], ml_piece, sv["ml_kept"], [d_hb],
                                                       name="mlstm_bwd")
            d_qa, d_ka, d_ai, d_lf = chunk_scan_bwd(hgrn2_step, sv["hg_xs"], hg_piece, sv["hg_kept"], [d_oa],
                                                    name="hgrn2_bwd")
            a_cots = [Rows(v) for v in (d_qa, d_ka, d_lf, d_xc, d_qb, d_kb, d_gates)]
            (d_aq, d_af, d_xconv, d_graw), (d_l0, d_l1, d_l2, d_wq, d_wk, d_gb) = rowwise_bwd(
                mix_a_fn, sv["a_rows"], a_cots, mix_a_params, [0, 1, 2, 3], [0, 1, 2, 3, 4, 5],
                [BF16, BF16, F32, BF16], tile=tile, name="mix_a_bwd")
            g["hgrn_lb_logits"] = jnp.concatenate([d_l0, d_l1, d_l2], axis=0)
            g["mlstm_wq"] = _block_diag_of(d_wq, MIX_W // 4, 4)
            g["mlstm_wk"] = _block_diag_of(d_wk, MIX_W // 4, 4)
            g["mlstm_gate_b"] = d_gb[:, :8]
            d_bx, g["mlstm_conv_w"], g["mlstm_conv_b"] = conv_bwd(Rows(sv["proj"], (4, MIX_W)), d_xconv, mconv_w,
                                                                  tile=tile, name="mconv_bwd")
            d_proj = jnp.concatenate([d_aq, d_af, d_ai.astype(BF16), d_ag, d_bx, d_bv.astype(BF16), d_bz, d_graw],
                                     axis=1)
            (g_in,) = matmul([[(t, d_proj)]], "tn", [GRAD_WIRE], tm=256, tn=AB_ALL, tk=1024, name="ab_dwin")
            hand_over(("ab",), [g_in, g_out])
            (d_t,) = matmul([[(d_proj, ab_w_in)]], "nt", [F32], tm=512, tn=1024, tk=AB_ALL, name="ab_in_bwd")
        else:
            rg_w_in, rg_w_out = sv["big"]
            (d_hgate,) = matmul([[(d_y, rg_w_out)]], "nt", [F32], tm=512, tn=1024, tk=D_MODEL,
                                name="rg_out_bwd")
            (g_out,) = matmul([[(sv["hgate"], d_y)]], "tn", [GRAD_WIRE], tm=1024, tn=1024, tk=2048,
                                         name="rg_dwout")
            (d_h, d_ybr), _ = rowwise_bwd(lru_b_fn, sv["b_rows"], [Rows(d_hgate)], [], [0, 1], [], [F32, BF16],
                                          tile=tile, name="lru_b_bwd")
            d_a, d_u = lru_bwd(sv["a_t"], sv["h"], d_h, tile=min(128, s_len), name="lru_bwd")
            (d_xr,), (d_wa, d_wx, d_ba, d_bx_, d_lam) = rowwise_bwd(
                lru_a_fn, [Rows(sv["xr"], split=128)], [Rows(d_a, split=128), Rows(d_u, split=128)], lru_a_params,
                [0], [0, 1, 2, 3, 4], [F32], tile=tile, name="lru_a_bwd")
            g["rglru_wa"], g["rglru_wx"] = d_wa, d_wx
            g["rglru_ba"], g["rglru_bx"], g["rglru_lambda"] = (v.reshape(1, D_MODEL) for v in (d_ba, d_bx_, d_lam))
            d_xbr, g["rglru_conv_w"], g["rglru_conv_b"] = conv_bwd(Rows(sv["proj"], (1, D_MODEL)), d_xr, rconv_w,
                                                                   tile=tile, name="rconv_bwd")
            d_proj = jnp.concatenate([d_ybr, d_xbr], axis=1)
            (g_in,) = matmul([[(t, d_proj)]], "tn", [GRAD_WIRE], tm=1024, tn=1024, tk=2048, name="rg_dwin")
            hand_over(("rg",), [g_in, g_out])
            (d_t,) = matmul([[(d_proj, rg_w_in)]], "nt", [F32], tm=512, tn=1024, tk=1024, name="rg_in_bwd")
        d_xo, d_t_next = d_xres, d_t

    def first_bwd(rows, params):
        x0, d_res, d_t0 = rows
        _, vjp = jax.vjp(lambda r, p: pre_fn(r, p)[0], [x0], params)
        (d_x0,), d_p = vjp([d_t0])
        return [d_res + d_x0], d_p

    (grad_x,), (d_mod[0][1], d_mod[0][0]) = rowwise(
        first_bwd, [Rows(x), Rows(d_xo), Rows(d_t_next)], [mrow(0, 0, 1), mrow(0, 0, 0)], [(D_MODEL, F32)],
        [(1, D_MODEL)] * 2, tile=tile, name="pre_bwd", after=tuple(sent))
    sent.clear()
    g["ln_g"] = jnp.stack([jnp.concatenate(r, axis=0) for r in d_ln_g])
    g["ln_b"] = jnp.stack([jnp.concatenate(r, axis=0) for r in d_ln_b])
    d_mod = jnp.stack([jnp.concatenate(r, axis=0) for r in d_mod])
    small_sent = put_small(g, d_mod)
    for weight_grads in deferred:
        weight_grads(small_sent)
    return loss, grad_x, d_mod, g, list(sent)


MESH_ID = pl.DeviceIdType.MESH
ANY_SPEC = pl.BlockSpec(memory_space=pl.ANY)


def _my_position():
    return lax.axis_index("x"), lax.axis_index("y"), lax.axis_index("c")


def _flat_index(pos):
    return 4 * pos[0] + 2 * pos[1] + pos[2]


def _peer_position(pos, k):
    return tuple(lax.rem(p + ((k >> s) & 1), 2) for p, s in zip(pos, (2, 1, 0)))


def _exchange(x, gather, name):
    out_shape = (NDEV,) + x.shape if gather else x.shape

    def body(x_ref, o_ref, send_sems, recv_sems, local_sem):
        pos = _my_position()
        me = _flat_index(pos)
        local = pltpu.make_async_copy(x_ref if gather else x_ref.at[me], o_ref.at[me], local_sem)
        local.start()
        copies = []
        for k in range(1, NDEV):
            peer = _peer_position(pos, k)
            src = x_ref if gather else x_ref.at[_flat_index(peer)]
            copies.append(pltpu.make_async_remote_copy(
                src_ref=src, dst_ref=o_ref.at[me], send_sem=send_sems.at[k - 1], recv_sem=recv_sems.at[k - 1],
                device_id=peer, device_id_type=MESH_ID))
            copies[-1].start()
        for cp in copies:
            cp.wait()
        local.wait()

    return pl.pallas_call(
        body, name=name, in_specs=[ANY_SPEC], out_specs=ANY_SPEC,
        out_shape=jax.ShapeDtypeStruct(out_shape, x.dtype),
        scratch_shapes=[pltpu.SemaphoreType.DMA((NDEV - 1,)), pltpu.SemaphoreType.DMA((NDEV - 1,)),
                        pltpu.SemaphoreType.DMA],
    )(x)


HBM_SPEC = pl.BlockSpec(memory_space=pltpu.HBM)
SEM_SPEC = pl.BlockSpec(memory_space=pltpu.SEMAPHORE)
SIDE_EFFECT = pltpu.SideEffectType.DATAFLOW_SIDE_EFFECTING


def _exchange_copies(x_refs, land_refs, send_sems, recv_sems, gather):
    pos = _my_position()
    me = _flat_index(pos)
    copies = []
    for k in range(1, NDEV):
        peer = _peer_position(pos, k)
        for x_ref, land_ref, s_sem, r_sem in zip(x_refs, land_refs, send_sems, recv_sems):
            src = x_ref if gather else x_ref.at[_flat_index(peer)]
            copies.append(pltpu.make_async_remote_copy(src_ref=src, dst_ref=land_ref.at[me], send_sem=s_sem,
                                                       recv_sem=r_sem, device_id=peer, device_id_type=MESH_ID))
    return copies


def exchange_start(xs, gather, name):
    n = len(xs)
    land_shapes = [(NDEV,) + x.shape if gather else x.shape for x in xs]

    def body(*refs):
        x_refs, land_refs = refs[:n], refs[n:2 * n]
        send_sems, recv_sems = refs[2 * n:3 * n], refs[3 * n:4 * n]
        token = refs[-1]
        for cp in _exchange_copies(x_refs, land_refs, send_sems, recv_sems, gather):
            cp.start()
        token[...] = jnp.zeros(token.shape, token.dtype)

    sem = pltpu.SemaphoreType.DMA(())
    res = pl.pallas_call(
        body, name=name,
        out_shape=[sem] * (2 * n) + [pltpu.HBM(x.shape, x.dtype) for x in xs]
        + [pltpu.HBM(s, x.dtype) for s, x in zip(land_shapes, xs)] + [jax.ShapeDtypeStruct((8, 128), F32)],
        in_specs=[HBM_SPEC] * (2 * n),
        out_specs=[SEM_SPEC] * (2 * n) + [HBM_SPEC] * (2 * n) + [pl.BlockSpec(memory_space=pltpu.VMEM)],
        input_output_aliases={i: 2 * n + i for i in range(2 * n)},
        compiler_params=pltpu.CompilerParams(has_side_effects=SIDE_EFFECT),
    )(*[pltpu.with_memory_space_constraint(x, pltpu.HBM) for x in xs],
      *[pltpu.with_memory_space_constraint(lax.empty(s, x.dtype), pltpu.HBM) for s, x in zip(land_shapes, xs)])
    return (res[:n], res[n:2 * n], res[2 * n:3 * n], res[3 * n:4 * n]), res[-1]


def exchange_wait(handles, after, name):
    send_sems, recv_sems, x_thru, land_thru = handles
    n = len(x_thru)
    after = jax.tree_util.tree_leaves(after)

    def body(*refs):
        land_refs = refs[n:2 * n]
        s_sems, r_sems = refs[2 * n:3 * n], refs[3 * n:4 * n]
        pos = _my_position()
        for land_ref, s_sem, r_sem in zip(land_refs, s_sems, r_sems):
            seven = land_ref.at[pl.ds(0, NDEV - 1)]
            all_seven = pltpu.make_async_remote_copy(src_ref=seven, dst_ref=seven, send_sem=s_sem, recv_sem=r_sem,
                                                     device_id=pos, device_id_type=MESH_ID)
            all_seven.wait_send()
            all_seven.wait_recv()

    res = pl.pallas_call(
        body, name=name,
        out_shape=[pltpu.HBM(x.shape, x.dtype) for x in x_thru] + [pltpu.HBM(x.shape, x.dtype) for x in land_thru],
        in_specs=[HBM_SPEC] * (2 * n) + [SEM_SPEC] * (2 * n) + [ANY_SPEC] * len(after),
        out_specs=[HBM_SPEC] * (2 * n),
        input_output_aliases={i: i for i in range(2 * n)},
        compiler_params=pltpu.CompilerParams(has_side_effects=SIDE_EFFECT),
    )(*x_thru, *land_thru, *send_sems, *recv_sems, *after)
    return res[:n], res[n:]


def all_gather(x, name):
    return _exchange(x, True, name)


def all_to_all(x, name):
    return _exchange(x, False, name)


def _row_tile(n_rows, cap):
    best = None
    for t in range(8, min(n_rows, cap) + 1, 8):
        if n_rows % t == 0:
            best = t
    return best if best else n_rows


def adamw(w, m, v, slots, *, name, index=(), prev=None):
    n_rows, width = w.shape[-2:]
    n_lead = w.ndim - 2
    assert len(index) == n_lead
    n_slots = slots.shape[0]
    lanes = -(-width // 128) * 128
    tile = _row_tile(n_rows, max(8, (1 << 20) // (4 * lanes) // 8 * 8))
    bc1 = 1.0 - ADAM_B1 ** ADAM_STEP
    bc2 = 1.0 - ADAM_B2 ** ADAM_STEP

    def body(w_ref, m_ref, v_ref, s_ref, *rest):
        g_ref, d_ref, nm_ref, nv_ref = rest[-4:]
        g = s_ref[0].astype(F32)
        for k in range(1, n_slots):
            g = g + s_ref[k].astype(F32)
        wv = w_ref[...]
        nm = ADAM_B1 * m_ref[...] + (1.0 - ADAM_B1) * g
        nv = ADAM_B2 * v_ref[...] + (1.0 - ADAM_B2) * (g * g)
        g_ref[...] = g
        nm_ref[...] = nm
        nv_ref[...] = nv
        d_ref[...] = -ADAM_LR * ((nm / bc1) / (jnp.sqrt(nv / bc2) + ADAM_EPS) + ADAM_WD * wv)

    spec = pl.BlockSpec((None,) * n_lead + (tile, width), lambda i: tuple(index) + (i, 0))
    prev = list(prev) if prev is not None else []
    return pl.pallas_call(
        body, name=name, grid=(n_rows // tile,),
        in_specs=[spec, spec, spec, pl.BlockSpec((n_slots, tile, width), lambda i: (0, i, 0))]
        + [ANY_SPEC] * len(prev),
        out_specs=[spec] * 4, out_shape=[jax.ShapeDtypeStruct(w.shape, F32)] * 4,
        input_output_aliases={4 + k: k for k in range(len(prev))},
        compiler_params=_params(dimension_semantics=("parallel",)),
    )(w, m, v, slots, *prev)


def sum_slots(slots, *, name):
    n_slots, n_rows, width = slots.shape

    def body(s_ref, o_ref):
        @pl.when(pl.program_id(0) == 0)
        def _():
            o_ref[...] = s_ref[...]

        @pl.when(pl.program_id(0) > 0)
        def _():
            o_ref[...] += s_ref[...]

    return pl.pallas_call(
        body, name=name, grid=(n_slots,),
        in_specs=[pl.BlockSpec((None, n_rows, width), lambda k: (k, 0, 0))],
        out_specs=pl.BlockSpec((n_rows, width), lambda k: (0, 0)),
        out_shape=jax.ShapeDtypeStruct((n_rows, width), F32),
        compiler_params=_params(dimension_semantics=("arbitrary",)),
    )(slots)


def adamw_nd(w, m, v, slots, *, name):
    shp = w.shape
    two = (-1, shp[-1])
    res = adamw(w.reshape(two), m.reshape(two), v.reshape(two), slots.reshape((slots.shape[0],) + (w.size // shp[-1], shp[-1])),
                name=name)
    return [r.reshape(shp) for r in res]


def _pack(arrs):
    parts = []
    for a in arrs:
        flat = a.reshape(-1).astype(F32)
        parts.append(jnp.pad(flat, (0, (-flat.shape[0]) % 1024)))
    return jnp.concatenate(parts).reshape(-1, 128)


def _unpack(buf, shapes):
    outs, at = [], 0
    lead = buf.shape[:-2]
    flat = buf.reshape(lead + (-1,))
    for shp in shapes:
        n = int(np.prod(shp))
        outs.append(flat[..., at:at + n].reshape(lead + tuple(shp)))
        at += n + (-n) % 1024
    return outs


ARG_NAMES = ["x", "c", "ada_w", "ada_b", "ln_g", "ln_b", "ffn_w1", "ffn_w3", "ffn_w2", "hgrn_lb_logits", "ab_w_in",
             "ab_w_out", "hgrn_norm_g", "mlstm_conv_w", "mlstm_conv_b", "mlstm_wq", "mlstm_wk", "mlstm_gate_b",
             "mlstm_skip", "mlstm_norm_g", "rglru_w_in", "rglru_conv_w", "rglru_conv_b", "rglru_wa", "rglru_ba",
             "rglru_wx", "rglru_bx", "rglru_lambda", "rglru_w_out", "loss_target"]
WEIGHTS = ARG_NAMES[2:-1]
BIG = ["ffn_w1", "ffn_w3", "ffn_w2", "ab_w_in", "ab_w_out", "rglru_w_in", "rglru_w_out"]
REPLICATED = ["ada_b", "hgrn_lb_logits", "hgrn_norm_g", "mlstm_conv_b", "mlstm_wq", "mlstm_wk", "mlstm_gate_b",
              "mlstm_skip", "mlstm_norm_g", "rglru_wa", "rglru_wx"]
SHARDED_SMALL = ["ln_g", "ln_b", "mlstm_conv_w", "rglru_conv_w", "rglru_conv_b", "rglru_ba", "rglru_bx", "rglru_lambda"]


def _unshard_last(gathered):
    moved = jnp.moveaxis(gathered, 0, -2)
    return moved.reshape(moved.shape[:-2] + (NDEV * moved.shape[-1],))


def _shard_last(full):
    split = full.reshape(full.shape[:-1] + (NDEV, full.shape[-1] // NDEV))
    return jnp.moveaxis(split, -2, 0)


def kernel(x, c, ada_w, ada_b, ln_g, ln_b, ffn_w1, ffn_w3, ffn_w2, hgrn_lb_logits, ab_w_in, ab_w_out, hgrn_norm_g, mlstm_conv_w, mlstm_conv_b, mlstm_wq, mlstm_wk, mlstm_gate_b, mlstm_skip, mlstm_norm_g, rglru_w_in, rglru_conv_w, rglru_conv_b, rglru_wa, rglru_ba, rglru_wx, rglru_bx, rglru_lambda, rglru_w_out, loss_target, m_ada_w, m_ada_b, m_ln_g, m_ln_b, m_ffn_w1, m_ffn_w3, m_ffn_w2, m_hgrn_lb_logits, m_ab_w_in, m_ab_w_out, m_hgrn_norm_g, m_mlstm_conv_w, m_mlstm_conv_b, m_mlstm_wq, m_mlstm_wk, m_mlstm_gate_b, m_mlstm_skip, m_mlstm_norm_g, m_rglru_w_in, m_rglru_conv_w, m_rglru_conv_b, m_rglru_wa, m_rglru_ba, m_rglru_wx, m_rglru_bx, m_rglru_lambda, m_rglru_w_out, v_ada_w, v_ada_b, v_ln_g, v_ln_b, v_ffn_w1, v_ffn_w3, v_ffn_w2, v_hgrn_lb_logits, v_ab_w_in, v_ab_w_out, v_hgrn_norm_g, v_mlstm_conv_w, v_mlstm_conv_b, v_mlstm_wq, v_mlstm_wk, v_mlstm_gate_b, v_mlstm_skip, v_mlstm_norm_g, v_rglru_w_in, v_rglru_conv_w, v_rglru_conv_b, v_rglru_wa, v_rglru_ba, v_rglru_wx, v_rglru_bx, v_rglru_lambda, v_rglru_w_out):
    args = locals()
    p = {n: args[n] for n in ARG_NAMES}
    mom = {n: (args["m_" + n], args["v_" + n]) for n in WEIGHTS}
    me = _flat_index(_my_position())

    keys = [("ffn", 0, 0), ("ab",), ("ffn", 0, 1), ("ffn", 1, 0), ("rg",), ("ffn", 1, 1)]
    names = {("ab",): ("ab_w_in", "ab_w_out"), ("rg",): ("rglru_w_in", "rglru_w_out")}
    for l in range(DEPTH):
        for i in range(2):
            names[("ffn", l, i)] = ("ffn_w1", "ffn_w3", "ffn_w2")

    def part(key):
        return (lambda a: a[key[1], key[2]]) if key[0] == "ffn" else (lambda a: a[0])

    gather_handles = {}

    def landed(handles, own_of, after, name):
        sources, lands = exchange_wait(handles, after, name)
        return [lax.dynamic_update_index_in_dim(ld, own_of(src), me, 0) for src, ld in zip(sources, lands)]

    starts_next = {("ffn", 0, 0): [("ab",)], ("ab",): [("ffn", 0, 1), ("ffn", 1, 0)],
                   ("ffn", 0, 1): [("rg",), ("ffn", 1, 1)]}

    def start_gather(key, after):
        shards = [part(key)(p[n]).astype(BF16) for n in names[key]]
        if after is not None:
            shards, _ = lax.optimization_barrier((shards, after))
        gather_handles[key], token = exchange_start(shards, True, "gather_start_" + "_".join(map(str, key)))
        return token

    def get_weights(key, after):
        got = landed(gather_handles[key], lambda src: src, after, "gather_wait_" + "_".join(map(str, key)))
        tokens = tuple(start_gather(nxt, got) for nxt in starts_next.get(key, []))
        if key[0] == "ffn":
            return (_unshard_last(got[0]), _unshard_last(got[1]), got[2].reshape(D_FF, D_MODEL)), tokens
        w_in = _unshard_last(got[0])
        if key[0] == "ab":
            w_in = jnp.concatenate([w_in[:, :AB_MAIN], jnp.pad(w_in[:, AB_MAIN:], ((0, 0), (0, 120)))], axis=1)
        return (w_in, got[1].reshape(D_MODEL, D_MODEL)), tokens

    scatter_handles = {}

    def put_grads(key, grads):
        if key[0] == "ffn":
            slots = [_shard_last(grads[0]), _shard_last(grads[1]), grads[2].reshape(NDEV, D_FF // NDEV, D_MODEL)]
        else:
            g_in = grads[0][:, :AB_MAIN + 8] if key[0] == "ab" else grads[0]
            slots = [_shard_last(g_in), grads[1].reshape(NDEV, D_MODEL // NDEV, D_MODEL)]
        scatter_handles[key], token = exchange_start(slots, False, "scatter_start_" + "_".join(map(str, key)))
        return token

    sharded_shapes = [p[n].shape for n in SHARDED_SMALL]
    small = all_gather(_pack([p[n] for n in SHARDED_SMALL] + [c]), "gather_small")
    started = [start_gather(keys[0], small)]
    small, _ = lax.optimization_barrier((small, started))
    per_dev = _unpack(small, sharded_shapes + [c.shape])
    full_small = {n: _unshard_last(per_dev[i]) for i, n in enumerate(SHARDED_SMALL)}
    c_all = per_dev[-1].reshape(NDEV, D_MODEL)

    c16 = jnp.pad(c_all, ((0, 8), (0, 0)))
    (c_act,), _ = rowwise(lambda r, q: ([_silu(r[0])], []), [Rows(c16)], [], [(D_MODEL, BF16)], [], tile=16,
                          name="cond_act")
    n_ada = ada_w.shape[-1]
    ada_b_mine = lax.dynamic_slice_in_dim(ada_b, me * n_ada, n_ada, axis=1)
    ada_cols = []
    for l in range(DEPTH):
        bias = jnp.broadcast_to(ada_b_mine[l][None, :], (16, n_ada))
        (cols,) = matmul([[(c_act, ada_w[l])]], "nn", [F32], tm=16, tn=n_ada, tk=D_MODEL, extras=[bias],
                         epi=lambda accs, ex: [accs[0] + ex[0]], name="ada_fwd")
        ada_cols.append(cols[:8])
    ada_mine = all_to_all(jnp.stack(ada_cols, axis=1), "ada_to_owner")
    mod = jnp.moveaxis(ada_mine, 0, 1).reshape(DEPTH, 9, D_MODEL)


    w = {"ln_g": full_small["ln_g"], "ln_b": full_small["ln_b"], "hgrn_lb_logits": hgrn_lb_logits}
    for n in ("hgrn_norm_g", "mlstm_conv_b", "mlstm_wq", "mlstm_wk", "mlstm_gate_b", "mlstm_skip", "mlstm_norm_g",
              "rglru_wa", "rglru_wx"):
        w[n] = p[n][0]
    for n in ("mlstm_conv_w", "rglru_conv_w", "rglru_conv_b", "rglru_ba", "rglru_bx", "rglru_lambda"):
        w[n] = full_small[n][0]

    small_names = REPLICATED + SHARDED_SMALL
    small_handles = []

    def put_small(g_small_parts, d_modulation):
        parts_ = dict(g_small_parts, ada_b=d_modulation.reshape(DEPTH, 9 * D_MODEL))
        handles, token = exchange_start([_pack([parts_[n] for n in small_names])], True, "gather_start_small_grads")
        small_handles.append(handles)
        return token

    loss, grad_x, d_mod, g, last_sent = local_step(x[0], loss_target[0], mod, w, get_weights, put_grads, put_small,
                                                   started)
    loss = lax.psum(loss, ("x", "y", "c"))

    outs = {}
    def update_group(key, after):
        slots = landed(scatter_handles[key], lambda src: lax.dynamic_index_in_dim(src, me, 0, keepdims=False), after,
                       "scatter_wait_" + "_".join(map(str, key)))
        for n, sl in zip(names[key], slots):
            index = key[1:] if key[0] == "ffn" else (0,)
            outs[n] = adamw(p[n], mom[n][0], mom[n][1], sl, name="adamw_" + n, index=index, prev=outs.get(n))
        return [outs[n][0] for n in names[key]]

    full_shapes = [p[n].shape for n in REPLICATED] + [full_small[n].shape for n in SHARDED_SMALL]
    (all_small,) = landed(small_handles[0], lambda src: src, (last_sent, grad_x), "gather_wait_small_grads")
    summed = sum_slots(all_small, name="sum_small_grads")
    g_small = dict(zip(small_names, _unpack(summed, full_shapes)))
    rep = adamw(*[_pack([t[n] for n in REPLICATED]) for t in (p, {n: mom[n][0] for n in WEIGHTS},
                                                               {n: mom[n][1] for n in WEIGHTS})],
                _pack([g_small[n] for n in REPLICATED])[None], name="adamw_replicated")
    rep = [_unpack(r, [p[n].shape for n in REPLICATED]) for r in rep]
    for i, n in enumerate(REPLICATED):
        outs[n] = [r[i] for r in rep]
    g_mine = {n: lax.dynamic_slice_in_dim(g_small[n], me * p[n].shape[-1], p[n].shape[-1], axis=-1)
              for n in SHARDED_SMALL}
    shd = adamw(*[_pack([t[n] for n in SHARDED_SMALL]) for t in (p, {n: mom[n][0] for n in WEIGHTS},
                                                                  {n: mom[n][1] for n in WEIGHTS})],
                _pack([g_mine[n] for n in SHARDED_SMALL])[None], name="adamw_sharded_small")
    done = shd[0]
    for key in keys[:0:-1] + keys[:1]:
        done = update_group(key, done)
    shd = [_unpack(r, sharded_shapes) for r in shd]
    for i, n in enumerate(SHARDED_SMALL):
        outs[n] = [r[i] for r in shd]

    d_ada = all_small[:, :DEPTH * 9 * D_MODEL // 128].reshape(NDEV, DEPTH, 9 * D_MODEL)
    d_mine = lax.dynamic_slice_in_dim(d_ada, me * n_ada, n_ada, axis=2)
    g_ada = []
    for l in range(DEPTH):
        d16 = jnp.pad(d_mine[:, l], ((0, 8), (0, 0)))
        (gl,) = matmul([[(c_act, d16)]], "tn", [F32], tm=D_MODEL, tn=n_ada, tk=16, name="ada_bwd")
        g_ada.append(gl)
    outs["ada_w"] = adamw_nd(ada_w, *mom["ada_w"], jnp.stack(g_ada)[None], name="adamw_ada_w")

    result = [loss, grad_x[None]]
    for k in range(4):
        result += [outs[n][k].reshape(p[n].shape) for n in WEIGHTS]
    return tuple(result)
```

```python
import functools

import jax
import jax.numpy as jnp
import numpy as np
from jax import lax
from jax.experimental import pallas as pl
from jax.experimental.pallas import tpu as pltpu

F32 = jnp.float32
BF16 = jnp.bfloat16
HIGHEST = lax.Precision.HIGHEST

NDEV = 8
D_MODEL = 1024
D_FF = 2816
DEPTH = 2
CHUNK = 64
SUB = 16
HEADS = 4
HEAD_W = 128
MIX_W = HEADS * HEAD_W
AB_MAIN = 7 * MIX_W
AB_ALL = AB_MAIN + 128
CONV_W = 4
C_BLOCKS = 8
RG_C = 8.0
ALPHA = (2 * DEPTH) ** 0.25
FFN_RES_W = 0.5
NEG = -1e30

ADAM_LR = 0.001
ADAM_B1 = 0.9
ADAM_B2 = 0.999
ADAM_EPS = 1e-08
ADAM_WD = 0.01
ADAM_STEP = 10

VMEM_LIMIT = 56 * 1024 * 1024
GRAD_WIRE = jnp.bfloat16

NN = ((1,), (0,))
NT = ((1,), (1,))
TN = ((0,), (0,))


def _dot(a, b, dims, precision=None):
    return lax.dot_general(a, b, (dims, ((), ())), precision=precision, preferred_element_type=F32)


def _make_mm(dims, d_lhs, d_rhs, swap_lhs, swap_rhs, prec):
    def cast(v):
        return v.astype(BF16) if prec is None else v.astype(F32)

    @jax.custom_vjp
    def mm(a, b):
        return _dot(cast(a), cast(b), dims, prec)

    def fwd(a, b):
        return mm(a, b), (a, b)

    def bwd(res, g):
        a, b = res
        g = cast(g)
        da = _dot(cast(b), g, d_lhs, prec) if swap_lhs else _dot(g, cast(b), d_lhs, prec)
        db = _dot(g, cast(a), d_rhs, prec) if swap_rhs else _dot(cast(a), g, d_rhs, prec)
        return da.astype(a.dtype), db.astype(b.dtype)

    mm.defvjp(fwd, bwd)
    return mm


def _mm_family(prec):
    return (_make_mm(NN, NT, TN, False, False, prec), _make_mm(NT, NN, TN, False, True, prec),
            _make_mm(TN, NT, NN, True, False, prec))


def _round(v):
    return v.astype(BF16).astype(F32)


@jax.custom_vjp
def row_dot(a, n):
    return jnp.sum(_round(a) * _round(n), axis=1, keepdims=True)


def _row_dot_fwd(a, n):
    return row_dot(a, n), (a, n)


def _row_dot_bwd(res, g):
    a, n = res
    g = _round(g)
    return g * _round(n), jnp.sum(g * _round(a), axis=0, keepdims=True)


row_dot.defvjp(_row_dot_fwd, _row_dot_bwd)


@jax.custom_vjp
def col_dot(s, a):
    return jnp.sum(_round(s) * _round(a), axis=0, keepdims=True)


def _col_dot_fwd(s, a):
    return col_dot(s, a), (s, a)


def _col_dot_bwd(res, g):
    s, a = res
    g = _round(g)
    return jnp.sum(g * _round(a), axis=1, keepdims=True), _round(s) * g


col_dot.defvjp(_col_dot_fwd, _col_dot_bwd)

mm_nn, mm_nt, mm_tn = _mm_family(None)
mid_nn, mid_nt, mid_tn = _mm_family(lax.Precision.HIGH)
hi_nn, hi_nt, hi_tn = _mm_family(HIGHEST)


def _silu(v):
    return v * jax.nn.sigmoid(v)


def _log_sigmoid(v):
    return jnp.minimum(v, 0.0) - jnp.log1p(jnp.exp(-jnp.abs(v)))


def _softplus(v):
    return jnp.maximum(v, 0.0) + jnp.log1p(jnp.exp(-jnp.abs(v)))


def _neg_expm1(v):
    series = -v * (1.0 + v * (0.5 + v * (1.0 / 6.0 + v * (1.0 / 24.0 + v * (1.0 / 120.0)))))
    return jnp.where(v > -0.05, series, 1.0 - jnp.exp(v))


def _params(**kw):
    return pltpu.CompilerParams(vmem_limit_bytes=VMEM_LIMIT, **kw)


class Rows:
    def __init__(self, arr, block=None, split=None):
        self.arr = arr
        self.block = block
        self.split = split

    @property
    def width(self):
        return self.block[1] if self.block else self.arr.shape[1]


def _load(ref, split):
    if ref.ndim == 3:
        return [ref[k].astype(F32) for k in range(ref.shape[0])]
    if split is None:
        return ref[...].astype(F32)
    return [ref[:, k * split:(k + 1) * split].astype(F32) for k in range(ref.shape[1] // split)]


def _store(ref, val, accumulate=False):
    if isinstance(val, (list, tuple)):
        if ref.ndim == 3:
            for k, v in enumerate(val):
                ref[k] = (ref[k] + v if accumulate else v).astype(ref.dtype)
            return
        w = ref.shape[1] // len(val)
        for k, v in enumerate(val):
            sl = slice(k * w, (k + 1) * w)
            ref[:, sl] = (ref[:, sl] + v if accumulate else v).astype(ref.dtype)
    else:
        ref[...] = (ref[...] + val if accumulate else val).astype(ref.dtype)


def rowwise(fn, rows, params, out_rows, out_sums, *, tile, name, after=()):
    n_rows = rows[0].arr.shape[0]
    n_r, n_p, n_o = len(rows), len(params), len(out_rows)
    n_in = n_r + n_p + len(after)
    splits = [r.split for r in rows]

    def body(*refs):
        r_refs, p_refs = refs[:n_r], refs[n_r:n_r + n_p]
        o_refs, s_refs = refs[n_in:n_in + n_o], refs[n_in + n_o:]
        row_out, sum_out = fn([_load(r, s) for r, s in zip(r_refs, splits)], [_load(p, None) for p in p_refs])
        for ref, val in zip(o_refs, row_out):
            _store(ref, val)
        if s_refs:
            @pl.when(pl.program_id(0) == 0)
            def _():
                for ref in s_refs:
                    ref[...] = jnp.zeros(ref.shape, ref.dtype)

            for ref, val in zip(s_refs, sum_out):
                _store(ref, val, accumulate=True)

    in_specs = []
    for r in rows:
        blk = r.block[0] if r.block else 0
        in_specs.append(pl.BlockSpec((tile, r.width), functools.partial(lambda i, b: (i, b), b=blk)))
    for p in params:
        in_specs.append(pl.BlockSpec(p.shape, functools.partial(lambda i, n: (0,) * n, n=p.ndim)))
    in_specs += [pl.BlockSpec(memory_space=pl.ANY)] * len(after)
    out_shape = [jax.ShapeDtypeStruct((n_rows, w), dt) for w, dt in out_rows]
    out_specs = [pl.BlockSpec((tile, w), lambda i: (i, 0)) for w, _ in out_rows]
    for shp in out_sums:
        out_shape.append(jax.ShapeDtypeStruct(shp, F32))
        out_specs.append(pl.BlockSpec(shp, functools.partial(lambda i, n: (0,) * n, n=len(shp))))
    res = pl.pallas_call(
        body, name=name, grid=(n_rows // tile,), in_specs=in_specs, out_specs=out_specs, out_shape=out_shape,
        compiler_params=_params(dimension_semantics=("arbitrary",)),
    )(*[r.arr for r in rows], *params, *after)
    return res[:n_o], res[n_o:]


def rowwise_bwd(fn, rows, cots, params, want_rows, want_params, out_dtypes, *, tile, name, after=()):
    n = len(rows)

    def bwd(row_vals, param_vals):
        prim, cot = row_vals[:n], row_vals[n:]
        _, vjp = jax.vjp(lambda r, p: fn(r, p)[0], prim, param_vals)
        d_rows, d_params = vjp(cot)
        return [d_rows[i] for i in want_rows], [d_params[j] for j in want_params]

    out_rows = [(rows[i].width, dt) for i, dt in zip(want_rows, out_dtypes)]
    out_sums = [params[j].shape for j in want_params]
    return rowwise(bwd, list(rows) + list(cots), params, out_rows, out_sums, tile=tile, name=name, after=after)


def matmul(groups, mode, outs, *, tm, tn, tk, extras=(), epi=None, name, n_outer=False, after=()):
    a0, b0 = groups[0][0]
    if mode == "tn":
        k_dim, m_dim = a0.shape
    else:
        m_dim, k_dim = a0.shape
    n_dim = b0.shape[0] if mode == "nt" else b0.shape[1]
    tm, tn, tk = min(tm, m_dim), min(tn, n_dim), min(tk, k_dim)
    assert m_dim % tm == 0 and n_dim % tn == 0 and k_dim % tk == 0, (name, m_dim, n_dim, k_dim)
    nk = k_dim // tk
    pairs = [p for g in groups for p in g]
    n_pairs, n_groups, n_ex, n_out = len(pairs), len(groups), len(extras), len(outs)
    dims = {"nn": NN, "nt": NT, "tn": TN}[mode]

    def body(*refs):
        ab = refs[:2 * n_pairs]
        ex = refs[2 * n_pairs:2 * n_pairs + n_ex]
        n_in = 2 * n_pairs + n_ex + len(after)
        o_refs = refs[n_in:n_in + n_out]
        accs = refs[n_in + n_out:]

        def partial_sums():
            sums, p = [], 0
            for g in groups:
                tot = None
                for _ in g:
                    d = _dot(ab[2 * p][...].astype(BF16), ab[2 * p + 1][...].astype(BF16), dims)
                    tot = d if tot is None else tot + d
                    p += 1
                sums.append(tot)
            return sums

        def finish(vals):
            res = epi(vals, [e[...] for e in ex]) if epi else vals
            for ref, v in zip(o_refs, res):
                ref[...] = v.astype(ref.dtype)

        if nk == 1:
            finish(partial_sums())
        else:
            k = pl.program_id(2)

            @pl.when(k == 0)
            def _():
                for acc in accs:
                    acc[...] = jnp.zeros(acc.shape, F32)

            for acc, s in zip(accs, partial_sums()):
                acc[...] += s

            @pl.when(k == nk - 1)
            def _():
                finish([acc[...] for acc in accs])

    def at(pick):
        return (lambda j, i, k: pick(i, j, k)) if n_outer else pick

    if mode == "nn":
        a_spec = pl.BlockSpec((tm, tk), at(lambda i, j, k: (i, k)))
        b_spec = pl.BlockSpec((tk, tn), at(lambda i, j, k: (k, j)))
    elif mode == "nt":
        a_spec = pl.BlockSpec((tm, tk), at(lambda i, j, k: (i, k)))
        b_spec = pl.BlockSpec((tn, tk), at(lambda i, j, k: (j, k)))
    else:
        a_spec = pl.BlockSpec((tk, tm), at(lambda i, j, k: (k, i)))
        b_spec = pl.BlockSpec((tk, tn), at(lambda i, j, k: (k, j)))
    mn_spec = pl.BlockSpec((tm, tn), at(lambda i, j, k: (i, j)))
    grid = (n_dim // tn, m_dim // tm, nk) if n_outer else (m_dim // tm, n_dim // tn, nk)
    return pl.pallas_call(
        body, name=name, grid=grid,
        in_specs=[a_spec, b_spec] * n_pairs + [mn_spec] * n_ex + [pl.BlockSpec(memory_space=pl.ANY)] * len(after),
        out_specs=[mn_spec] * n_out,
        out_shape=[jax.ShapeDtypeStruct((m_dim, n_dim), dt) for dt in outs],
        scratch_shapes=[pltpu.VMEM((tm, tn), F32)] * (n_groups if nk > 1 else 0),
        compiler_params=_params(dimension_semantics=("parallel", "parallel", "arbitrary")),
    )(*[x for p in pairs for x in p], *extras, *after)


def conv_fwd(x, w, b, *, tile, name):
    n_rows, width = x.arr.shape[0], x.width
    blk = x.block[0] if x.block else 0

    def body(x_ref, halo_ref, w_ref, b_ref, y_ref, buf):
        i = pl.program_id(0)
        halo = halo_ref[...]
        buf[0:8, :] = jnp.where(i == 0, jnp.zeros_like(halo), _round(halo))
        buf[8:, :] = _round(x_ref[...])
        acc = jnp.zeros((tile, width), F32)
        for j in range(CONV_W):
            s = CONV_W - 1 - j
            acc = acc + _round(w_ref[j:j + 1, :]) * buf[8 - s:8 - s + tile, :]
        y_ref[...] = acc + b_ref[...]

    hb = tile // 8
    return pl.pallas_call(
        body, name=name, grid=(n_rows // tile,),
        in_specs=[pl.BlockSpec((tile, width), lambda i: (i, blk)),
                  pl.BlockSpec((8, width), lambda i: (jnp.maximum(i * hb - 1, 0), blk)),
                  pl.BlockSpec((CONV_W, width), lambda i: (0, 0)),
                  pl.BlockSpec((1, width), lambda i: (0, 0))],
        out_specs=pl.BlockSpec((tile, width), lambda i: (i, 0)),
        out_shape=jax.ShapeDtypeStruct((n_rows, width), F32),
        scratch_shapes=[pltpu.VMEM((tile + 8, width), F32)],
        compiler_params=_params(dimension_semantics=("arbitrary",)),
    )(x.arr, x.arr, w, b)


def conv_bwd(x, dy, w, *, tile, name):
    n_rows, width = x.arr.shape[0], x.width
    blk = x.block[0] if x.block else 0
    n_tiles = n_rows // tile

    def body(x_ref, xh_ref, dy_ref, dyh_ref, w_ref, dx_ref, dw_ref, db_ref, xbuf, dbuf):
        i = pl.program_id(0)
        xh, dyh = xh_ref[...], dyh_ref[...]
        xbuf[0:8, :] = jnp.where(i == 0, jnp.zeros_like(xh), _round(xh))
        xbuf[8:, :] = _round(x_ref[...])
        dy_t = dy_ref[...]
        dy_r = _round(dy_t)
        dbuf[0:tile, :] = dy_r
        dbuf[tile:, :] = jnp.where(i == n_tiles - 1, jnp.zeros_like(dyh), _round(dyh))

        @pl.when(i == 0)
        def _():
            dw_ref[...] = jnp.zeros(dw_ref.shape, F32)
            db_ref[...] = jnp.zeros(db_ref.shape, F32)

        acc = jnp.zeros((tile, width), F32)
        for j in range(CONV_W):
            s = CONV_W - 1 - j
            acc = acc + _round(w_ref[j:j + 1, :]) * dbuf[s:s + tile, :]
            dw_ref[j:j + 1, :] += jnp.sum(dy_r * xbuf[8 - s:8 - s + tile, :], axis=0, keepdims=True)
        dx_ref[...] = acc.astype(dx_ref.dtype)
        db_ref[...] += jnp.sum(dy_t, axis=0, keepdims=True)

    hb = tile // 8
    return pl.pallas_call(
        body, name=name, grid=(n_tiles,),
        in_specs=[pl.BlockSpec((tile, width), lambda i: (i, blk)),
                  pl.BlockSpec((8, width), lambda i: (jnp.maximum(i * hb - 1, 0), blk)),
                  pl.BlockSpec((tile, width), lambda i: (i, 0)),
                  pl.BlockSpec((8, width), lambda i: (jnp.minimum((i + 1) * hb, n_tiles * hb - 1), 0)),
                  pl.BlockSpec((CONV_W, width), lambda i: (0, 0))],
        out_specs=[pl.BlockSpec((tile, width), lambda i: (i, 0)),
                   pl.BlockSpec((CONV_W, width), lambda i: (0, 0)),
                   pl.BlockSpec((1, width), lambda i: (0, 0))],
        out_shape=[jax.ShapeDtypeStruct((n_rows, width), BF16),
                   jax.ShapeDtypeStruct((CONV_W, width), F32),
                   jax.ShapeDtypeStruct((1, width), F32)],
        scratch_shapes=[pltpu.VMEM((tile + 8, width), F32), pltpu.VMEM((tile + 8, width), F32)],
        compiler_params=_params(dimension_semantics=("arbitrary",)),
    )(x.arr, x.arr, dy, dy, w)


def _pieces(ref, col_w, row_h):
    n_c, n_r = ref.shape[1] // col_w, ref.shape[0] // row_h
    return [[ref[r * row_h:(r + 1) * row_h, c * col_w:(c + 1) * col_w].astype(F32) for r in range(n_r)]
            for c in range(n_c)]


def _store_pieces(ref, vals, col_w, row_h):
    for c, col in enumerate(vals):
        for r, v in enumerate(col):
            ref[r * row_h:(r + 1) * row_h, c * col_w:(c + 1) * col_w] = v.astype(ref.dtype)


def _x_spec(x, n_chunks, reverse):
    blk = x.block[0] if x.block else 0
    if reverse:
        return pl.BlockSpec((CHUNK, x.width), functools.partial(lambda n, b: (n_chunks - 1 - n, b), b=blk))
    return pl.BlockSpec((CHUNK, x.width), functools.partial(lambda n, b: (n, b), b=blk))


def chunk_scan_fwd(step, xs, piece, state_shapes, out_widths, *, name):
    n_rows = xs[0].arr.shape[0]
    n_chunks = n_rows // CHUNK
    n_x, n_s, n_o = len(xs), len(state_shapes), len(out_widths)

    def body(*refs):
        x_refs, o_refs = refs[:n_x], refs[n_x:n_x + n_o]
        keep_refs, st_refs = refs[n_x + n_o:n_x + n_o + n_s], refs[n_x + n_o + n_s:]

        @pl.when(pl.program_id(0) == 0)
        def _():
            for st in st_refs:
                st[...] = jnp.zeros(st.shape, F32)

        states = [st[...] for st in st_refs]
        for keep, s in zip(keep_refs, states):
            keep[...] = s
        new_states, outs = step(states, [_pieces(x, *p) for x, p in zip(x_refs, piece)])
        for st, s in zip(st_refs, new_states):
            st[...] = s
        for o, v in zip(o_refs, outs):
            o[...] = v

    out_shape = [jax.ShapeDtypeStruct((n_rows, w), F32) for w in out_widths]
    out_specs = [pl.BlockSpec((CHUNK, w), lambda n: (n, 0)) for w in out_widths]
    for shp in state_shapes:
        out_shape.append(jax.ShapeDtypeStruct((n_chunks,) + shp, F32))
        out_specs.append(pl.BlockSpec((None,) + shp, lambda n: (n, 0, 0)))
    res = pl.pallas_call(
        body, name=name, grid=(n_chunks,),
        in_specs=[_x_spec(x, n_chunks, False) for x in xs],
        out_specs=out_specs, out_shape=out_shape,
        scratch_shapes=[pltpu.VMEM(shp, F32) for shp in state_shapes],
        compiler_params=_params(dimension_semantics=("arbitrary",)),
    )(*[x.arr for x in xs])
    return res[:n_o], res[n_o:]


def chunk_scan_bwd(step, xs, piece, kept, d_outs, *, name):
    n_rows = xs[0].arr.shape[0]
    n_chunks = n_rows // CHUNK
    n_x, n_s, n_o = len(xs), len(kept), len(d_outs)
    state_shapes = [k.shape[1:] for k in kept]

    def body(*refs):
        x_refs, k_refs = refs[:n_x], refs[n_x:n_x + n_s]
        do_refs = refs[n_x + n_s:n_x + n_s + n_o]
        dx_refs = refs[n_x + n_s + n_o:2 * n_x + n_s + n_o]
        ds_refs = refs[2 * n_x + n_s + n_o:]

        @pl.when(pl.program_id(0) == 0)
        def _():
            for ds in ds_refs:
                ds[...] = jnp.zeros(ds.shape, F32)

        states = [k[...] for k in k_refs]
        inputs = [_pieces(x, *p) for x, p in zip(x_refs, piece)]
        _, vjp = jax.vjp(step, states, inputs)
        d_states, d_inputs = vjp(([ds[...] for ds in ds_refs], [do[...] for do in do_refs]))
        for ds, v in zip(ds_refs, d_states):
            ds[...] = v
        for dx, v, p in zip(dx_refs, d_inputs, piece):
            _store_pieces(dx, v, *p)

    rev3 = lambda n: (n_chunks - 1 - n, 0, 0)
    rev2 = lambda n: (n_chunks - 1 - n, 0)
    return pl.pallas_call(
        body, name=name, grid=(n_chunks,),
        in_specs=[_x_spec(x, n_chunks, True) for x in xs]
        + [pl.BlockSpec((None,) + shp, rev3) for shp in state_shapes]
        + [pl.BlockSpec((CHUNK, d.shape[1]), rev2) for d in d_outs],
        out_specs=[pl.BlockSpec((CHUNK, x.width), rev2) for x in xs],
        out_shape=[jax.ShapeDtypeStruct((n_rows, x.width), F32) for x in xs],
        scratch_shapes=[pltpu.VMEM(shp, F32) for shp in state_shapes],
        compiler_params=_params(dimension_semantics=("arbitrary",)),
    )(*[x.arr for x in xs], *kept, *d_outs)


def _tri(n, strict=False):
    r = lax.broadcasted_iota(jnp.int32, (n, n), 0)
    c = lax.broadcasted_iota(jnp.int32, (n, n), 1)
    return (r > c) if strict else (r >= c)


def hgrn2_step(states, inputs):
    q_all, k_all, v_all, lf_all = inputs
    n_sub = CHUNK // SUB
    low = _tri(SUB).astype(F32)
    ones_sub = jnp.ones((SUB, SUB), F32)
    ones_chunk = jnp.ones((CHUNK, HEAD_W), F32)
    new_states, outs = [], []
    for h in range(HEADS):
        state = states[h]
        q, k, v, lf = q_all[h], k_all[h], v_all[h], lf_all[h]
        cum = [mid_nn(low, lf[i]) for i in range(n_sub)]
        tot = [mid_nn(ones_sub, lf[i]) for i in range(n_sub)]
        start = [jnp.zeros((SUB, HEAD_W), F32)]
        for i in range(n_sub):
            start.append(start[-1] + tot[i])
        q_in = [q[i] * jnp.exp(cum[i]) for i in range(n_sub)]
        intra = []
        for i in range(n_sub):
            keys = [k[j] * jnp.exp(start[i] - start[j] - cum[j]) for j in range(i)]
            keys.append(k[i] * jnp.exp(jnp.minimum(-cum[i], 80.0)))
            att = mid_nt(q_in[i], jnp.concatenate(keys, axis=0))
            r_id = lax.broadcasted_iota(jnp.int32, att.shape, 0)
            c_id = lax.broadcasted_iota(jnp.int32, att.shape, 1)
            att = jnp.where(c_id - SUB * i <= r_id, att, 0.0)
            intra.append(mm_nn(att, jnp.concatenate(v[:i + 1], axis=0)))
        q_state = jnp.concatenate([q_in[i] * jnp.exp(start[i]) for i in range(n_sub)], axis=0)
        out = mm_nn(q_state, state) + jnp.concatenate(intra, axis=0)
        k_end = jnp.concatenate([k[j] * jnp.exp(start[n_sub] - start[j] - cum[j]) for j in range(n_sub)], axis=0)
        decay = jnp.exp(mid_tn(jnp.concatenate(lf, axis=0), ones_chunk))
        new_states.append(decay * state + mm_tn(k_end, jnp.concatenate(v, axis=0)))
        outs.append(out)
    return new_states, [jnp.concatenate(outs, axis=1)]


def mlstm_step(states, inputs):
    q_all, k_all, v_all, gates = inputs
    gates = gates[0][0]
    c_st, n_st, m_st = states[:HEADS], states[HEADS:2 * HEADS], states[2 * HEADS:]
    lane = lax.broadcasted_iota(jnp.int32, (CHUNK, 128), 1)
    low = _tri(CHUNK).astype(F32)
    causal = _tri(CHUNK)
    gates_cum = hi_nn(low, gates)
    new_c, new_n, new_m, outs = [], [], [], []
    for h in range(HEADS):
        q, k, v = q_all[h][0], k_all[h][0], v_all[h][0]
        pick_i = (lane == h).astype(F32)
        pick_f = (lane == HEADS + h).astype(F32)
        li_col = jnp.sum(gates * pick_i, axis=1, keepdims=True)
        lf_col = jnp.sum(gates * pick_f, axis=1, keepdims=True)
        b_col = jnp.sum(gates_cum * pick_f, axis=1, keepdims=True)
        by_key = hi_nt(pick_i, gates) - hi_nt(pick_f, gates_cum)
        d_mat = jnp.where(causal, b_col + by_key, NEG)
        m_prev = lax.stop_gradient(jnp.max(m_st[h], axis=1, keepdims=True))
        g_inter = b_col + m_prev
        m_t = lax.stop_gradient(jnp.maximum(g_inter, jnp.max(d_mat, axis=1, keepdims=True)))
        w_inter = jnp.exp(g_inter - m_t)
        aw = jnp.exp(d_mat - m_t) * mm_nt(q, k)
        num = w_inter * mm_nn(q, c_st[h]) + mm_nn(aw, v)
        den = w_inter * row_dot(q, n_st[h]) + jnp.sum(aw, axis=1, keepdims=True)
        outs.append(num / jnp.maximum(jnp.abs(den), jnp.exp(-m_t)))
        b_end = jnp.sum(lf_col, axis=0, keepdims=True)
        g_state = b_end + m_prev
        s_w = b_end - b_col + li_col
        m_next = lax.stop_gradient(jnp.maximum(g_state, jnp.max(s_w, axis=0, keepdims=True)))
        dec = jnp.exp(g_state - m_next)
        w_s = jnp.exp(s_w - m_next)
        kw = k * w_s
        new_c.append(dec * c_st[h] + mm_tn(kw, v))
        new_n.append(dec * n_st[h] + col_dot(w_s, k))
        new_m.append(jnp.broadcast_to(m_next, (1, HEAD_W)))
    return new_c + new_n + new_m, [jnp.concatenate(outs, axis=1)]


def lru_fwd(a, u, *, tile, name):
    n_rows, width = a.shape

    def body(a_ref, u_ref, h_ref, carry):
        @pl.when(pl.program_id(0) == 0)
        def _():
            carry[...] = jnp.zeros(carry.shape, F32)

        h = carry[...]
        for t in range(tile):
            h = a_ref[t:t + 1, :] * h + u_ref[t:t + 1, :]
            h_ref[t:t + 1, :] = h
        carry[...] = h

    spec = pl.BlockSpec((tile, width), lambda i: (i, 0))
    return pl.pallas_call(
        body, name=name, grid=(n_rows // tile,), in_specs=[spec, spec], out_specs=spec,
        out_shape=jax.ShapeDtypeStruct((n_rows, width), F32),
        scratch_shapes=[pltpu.VMEM((1, width), F32)],
        compiler_params=_params(dimension_semantics=("arbitrary",)),
    )(a, u)


def lru_bwd(a, h, dh, *, tile, name):
    n_rows, width = a.shape
    n_tiles = n_rows // tile
    hb = tile // 8

    def body(a_ref, h_ref, hh_ref, dh_ref, da_ref, du_ref, carry):
        i = pl.program_id(0)

        @pl.when(i == 0)
        def _():
            carry[...] = jnp.zeros(carry.shape, F32)

        c = carry[...]
        for t in range(tile - 1, -1, -1):
            g = dh_ref[t:t + 1, :] + c
            du_ref[t:t + 1, :] = g
            if t:
                h_prev = h_ref[t - 1:t, :]
            else:
                h_prev = jnp.where(i == n_tiles - 1, 0.0, hh_ref[7:8, :])
            da_ref[t:t + 1, :] = g * h_prev
            c = a_ref[t:t + 1, :] * g
        carry[...] = c

    rev = lambda i: (n_tiles - 1 - i, 0)
    spec = pl.BlockSpec((tile, width), rev)
    halo = pl.BlockSpec((8, width), lambda i: (jnp.maximum((n_tiles - 1 - i) * hb - 1, 0), 0))
    return pl.pallas_call(
        body, name=name, grid=(n_tiles,), in_specs=[spec, spec, halo, spec], out_specs=[spec, spec],
        out_shape=[jax.ShapeDtypeStruct((n_rows, width), F32)] * 2,
        scratch_shapes=[pltpu.VMEM((1, width), F32)],
        compiler_params=_params(dimension_semantics=("arbitrary",)),
    )(a, h, h, dh)


def _layer_norm(z, g, b):
    mu = jnp.mean(z, axis=-1, keepdims=True)
    zc = z - mu
    var = jnp.mean(zc * zc, axis=-1, keepdims=True)
    return zc * lax.rsqrt(var + 1e-5) * g + b


def pre_fn(rows, params):
    (x,), (scale, shift) = rows, params
    return [x * (1.0 + scale) + shift], []


def make_post_fn(weight, with_next):
    def fn(rows, params):
        x, y = rows
        gate, g, b = params[:3]
        xo = _layer_norm(ALPHA * x + weight * (1.0 + gate) * y, g, b)
        if with_next:
            return [xo, xo * (1.0 + params[3]) + params[4]], []
        return [xo], []

    return fn


def make_last_fn(weight):
    post = make_post_fn(weight, False)

    def fn(rows, params):
        x, y, target = rows
        err = post([x, y], params)[0][0] - target
        loss = 0.5 * jnp.sum(jnp.mean(err * err, axis=-1, keepdims=True), axis=0, keepdims=True)
        return [err * (1.0 / D_MODEL)], [jnp.broadcast_to(loss, (1, 128))]

    return fn


def mix_a_fn(rows, params):
    a_q, a_f, xconv, graw = rows
    l0, l1, l2, wq, wk, gate_b = params
    mx = jnp.maximum(jnp.maximum(l0, l1), l2)
    e0, e1, e2 = jnp.exp(l0 - mx), jnp.exp(l1 - mx), jnp.exp(l2 - mx)
    lb = e0 / (e0 + e1 + e2)
    f = lb + (1.0 - lb) * jax.nn.sigmoid(a_f)
    xc = _silu(xconv)
    q_b = mm_nn(xc, wq)
    k_b = mm_nn(xc, wk) * (HEAD_W ** -0.5)
    g = graw + gate_b
    lane = lax.broadcasted_iota(jnp.int32, g.shape, 1)
    gates = jnp.where(lane < HEADS, g, _log_sigmoid(g))
    return [_silu(a_q), 1.0 - f, jnp.log(f), xc, q_b, k_b, gates], []


def _head_norm(v, g, center):
    if center:
        v = v - jnp.mean(v, axis=-1, keepdims=True)
    return v * lax.rsqrt(jnp.mean(v * v, axis=-1, keepdims=True) + 1e-6) * g


def mix_b_fn(rows, params):
    o_a, a_g, h_b, xc, b_z = rows
    hg, mg, skip = params
    y_a = [_head_norm(o_a[h], hg[h], False) * _silu(a_g[h]) for h in range(HEADS)]
    y_b = [(_head_norm(h_b[h], mg[h], True) + skip[h] * xc[h]) * _silu(b_z[h]) for h in range(HEADS)]
    return [y_a + y_b], []


def lru_a_fn(rows, params):
    (xr,) = rows
    wa, wx, ba, bx, lam = params
    a_out, u_out = [], []
    for n in range(C_BLOCKS):
        r = jax.nn.sigmoid(mm_nt(xr[n], wa[n]) + ba[n])
        i = jax.nn.sigmoid(mm_nt(xr[n], wx[n]) + bx[n])
        log_a = -RG_C * r * _softplus(-lam[n])
        a_out.append(jnp.exp(log_a))
        u_out.append(jnp.sqrt(_neg_expm1(2.0 * log_a)) * i * xr[n])
    return [a_out, u_out], []


def lru_b_fn(rows, params):
    h, y_br = rows
    return [h * jax.nn.gelu(y_br)], []


def swiglu_epi(accs, extras):
    h1, h3 = accs
    return [h1, h3, _silu(h1) * h3]


def swiglu_bwd_epi(accs, extras):
    (da,), (h1, h3) = accs, extras
    h1, h3 = h1.astype(F32), h3.astype(F32)
    sig = jax.nn.sigmoid(h1)
    return [da * h3 * sig * (1.0 + h1 * (1.0 - sig)), da * h1 * sig]


ROW_TILE = 512


def _chunks(v, n):
    return v.reshape(n, 1, v.shape[-1] // n)


def _dense_blocks(w):
    n, b, _ = w.shape
    by_row = jnp.swapaxes(w, 1, 2).reshape(n * b, b)
    spread = jnp.dot(by_row, _column_picker(n, b).T, precision=HIGHEST)
    return spread * _block_mask(n, b)


def _column_picker(n, b):
    return jnp.asarray(np.tile(np.eye(b, dtype=np.float32), (n, 1)))


def _block_mask(n, b):
    return jnp.asarray(np.kron(np.eye(n, dtype=np.float32), np.ones((b, b), np.float32)))


def _block_diag_of(m, n, b):
    by_row = jnp.dot(m * _block_mask(n, b), _column_picker(n, b), precision=HIGHEST)
    return jnp.swapaxes(by_row.reshape(n, b, b), 1, 2)


def local_step(x, target, mod, w, get_weights, put_grads, put_small, first_after=()):
    s_len = x.shape[0]
    tile = min(ROW_TILE, s_len)
    row = lambda v: v.reshape(1, -1)
    mrow = lambda l, j, k: mod[l, 3 * j + k].reshape(1, D_MODEL)
    g = {}
    d_mod = [[None] * 9 for _ in range(DEPTH)]
    d_ln_g = [[None] * 3 for _ in range(DEPTH)]
    d_ln_b = [[None] * 3 for _ in range(DEPTH)]
    subs = [(l, j) for l in range(DEPTH) for j in range(3)]
    weight_of = lambda j: 1.0 if j == 1 else FFN_RES_W

    wq_d = _dense_blocks(w["mlstm_wq"])
    wk_d = _dense_blocks(w["mlstm_wk"])
    wa_b, wx_b = w["rglru_wa"].astype(BF16), w["rglru_wx"].astype(BF16)
    gate_b = jnp.pad(w["mlstm_gate_b"].reshape(1, 8), ((0, 0), (0, 120)))
    lb_rows = [row(w["hgrn_lb_logits"][k]) for k in range(3)]
    mix_a_params = lb_rows + [wq_d, wk_d, gate_b]
    mix_b_params = [_chunks(row(w["hgrn_norm_g"]), HEADS), _chunks(row(w["mlstm_norm_g"]), HEADS),
                    _chunks(row(w["mlstm_skip"]), HEADS)]
    lru_a_params = [wa_b, wx_b, _chunks(row(w["rglru_ba"]), C_BLOCKS), _chunks(row(w["rglru_bx"]), C_BLOCKS),
                    _chunks(row(w["rglru_lambda"]), C_BLOCKS)]
    mconv_w, mconv_b = w["mlstm_conv_w"], row(w["mlstm_conv_b"])
    rconv_w, rconv_b = w["rglru_conv_w"], row(w["rglru_conv_b"])
    hg_piece = [(HEAD_W, SUB)] * 4
    ml_piece = [(HEAD_W, CHUNK)] * 3 + [(128, CHUNK)]

    (t,), _ = rowwise(pre_fn, [Rows(x)], [mrow(0, 0, 1), mrow(0, 0, 0)], [(D_MODEL, BF16)], [],
                      tile=tile, name="pre", after=tuple(first_after))
    saved = {}
    x_in = x
    for idx, (l, j) in enumerate(subs):
        sv = {"x": x_in, "t": t}
        if j != 1:
            (w1, w3, w2), begun = get_weights(("ffn", l, j // 2), t)
            h1, h3, act = matmul([[(t, w1)], [(t, w3)]], "nn", [BF16, BF16, BF16], tm=512, tn=1408, tk=D_MODEL,
                                 epi=swiglu_epi, name="ffn_up", n_outer=True, after=begun)
            (y,) = matmul([[(act, w2)]], "nn", [F32], tm=1024, tn=1024, tk=D_FF, name="ffn_down")
            sv.update(h1=h1, h3=h3, act=act, big=(w1, w3, w2))
        elif l == 0:
            (ab_w_in, ab_w_out), begun = get_weights(("ab",), t)
            sv["big"] = (ab_w_in, ab_w_out)
            (proj,) = matmul([[(t, ab_w_in)]], "nn", [F32], tm=256, tn=AB_ALL, tk=D_MODEL, name="ab_in", after=begun)
            xconv = conv_fwd(Rows(proj, (4, MIX_W)), mconv_w, mconv_b, tile=tile, name="mconv")
            a_rows = [Rows(proj, (0, MIX_W)), Rows(proj, (1, MIX_W)), Rows(xconv), Rows(proj, (AB_MAIN // 128, 128))]
            (q_a, k_a, lf_a, xc, q_b, k_b, gates), _ = rowwise(
                mix_a_fn, a_rows, mix_a_params, [(MIX_W, F32)] * 6 + [(128, F32)], [], tile=tile, name="mix_a")
            hg_xs = [Rows(q_a), Rows(k_a), Rows(proj, (2, MIX_W)), Rows(lf_a)]
            (o_a,), hg_kept = chunk_scan_fwd(hgrn2_step, hg_xs, hg_piece, [(HEAD_W, HEAD_W)] * HEADS, [MIX_W],
                                             name="hgrn2_fwd")
            ml_xs = [Rows(q_b), Rows(k_b), Rows(proj, (5, MIX_W)), Rows(gates)]
            ml_states = [(HEAD_W, HEAD_W)] * HEADS + [(1, HEAD_W)] * (2 * HEADS)
            (h_b,), ml_kept = chunk_scan_fwd(mlstm_step, ml_xs, ml_piece, ml_states, [MIX_W], name="mlstm_fwd")
            b_rows = [Rows(o_a, split=HEAD_W), Rows(proj, (3, MIX_W), HEAD_W), Rows(h_b, split=HEAD_W),
                      Rows(xc, split=HEAD_W), Rows(proj, (6, MIX_W), HEAD_W)]
            (ycat,), _ = rowwise(mix_b_fn, b_rows, mix_b_params, [(2 * MIX_W, BF16)], [], tile=tile, name="mix_b")
            (y,) = matmul([[(ycat, ab_w_out)]], "nn", [F32], tm=512, tn=1024, tk=D_MODEL, name="ab_out")
            sv.update(proj=proj, xconv=xconv, a_rows=a_rows, hg_xs=hg_xs, hg_kept=hg_kept, ml_xs=ml_xs,
                      ml_kept=ml_kept, b_rows=b_rows, ycat=ycat)
        else:
            (rg_w_in, rg_w_out), begun = get_weights(("rg",), t)
            sv["big"] = (rg_w_in, rg_w_out)
            (proj,) = matmul([[(t, rg_w_in)]], "nn", [F32], tm=512, tn=1024, tk=D_MODEL, name="rg_in", after=begun)
            xr = conv_fwd(Rows(proj, (1, D_MODEL)), rconv_w, rconv_b, tile=tile, name="rconv")
            (a_t, u_t), _ = rowwise(lru_a_fn, [Rows(xr, split=128)], lru_a_params, [(D_MODEL, F32)] * 2, [],
                                    tile=tile, name="lru_a")
            h = lru_fwd(a_t, u_t, tile=min(128, s_len), name="lru_fwd")
            b_rows = [Rows(h), Rows(proj, (0, D_MODEL))]
            (hgate,), _ = rowwise(lru_b_fn, b_rows, [], [(D_MODEL, BF16)], [], tile=tile, name="lru_b")
            (y,) = matmul([[(hgate, rg_w_out)]], "nn", [F32], tm=512, tn=1024, tk=D_MODEL, name="rg_out")
            sv.update(proj=proj, xr=xr, a_t=a_t, h=h, b_rows=b_rows, hgate=hgate)
        sv["y"] = y
        post_params = [mrow(l, j, 2), row(w["ln_g"][l, j]), row(w["ln_b"][l, j])]
        if idx + 1 < len(subs):
            nl, nj = subs[idx + 1]
            post_params += [mrow(nl, nj, 1), mrow(nl, nj, 0)]
            (x_out, t), _ = rowwise(make_post_fn(weight_of(j), True), [Rows(x_in), Rows(y)], post_params,
                                    [(D_MODEL, F32), (D_MODEL, BF16)], [], tile=tile, name="post")
        else:
            (d_xo,), (loss_row,) = rowwise(make_last_fn(weight_of(j)), [Rows(x_in), Rows(y), Rows(target)],
                                           post_params, [(D_MODEL, F32)], [(1, 128)], tile=tile, name="post_loss")
            x_out = None
        sv["post_params"] = post_params
        saved[(l, j)] = sv
        x_in = x_out
    loss = loss_row[0, 0]

    d_t_next = None
    sent = []
    deferred = []

    def hand_over(key, grads):
        token = put_grads(key, grads)
        if token is not None:
            sent.append(token)

    for idx in range(len(subs) - 1, -1, -1):
        l, j = subs[idx]
        sv = saved[(l, j)]
        has_next = idx + 1 < len(subs)
        cots = [Rows(d_xo)] + ([Rows(d_t_next)] if has_next else [])
        want_p = [0, 1, 2] + ([3, 4] if has_next else [])
        (d_xres, d_y), d_par = rowwise_bwd(
            make_post_fn(weight_of(j), has_next), [Rows(sv["x"]), Rows(sv["y"])], cots, sv["post_params"],
            [0, 1], want_p, [F32, BF16], tile=tile, name="post_bwd", after=tuple(sent))
        sent.clear()
        d_mod[l][3 * j + 2], d_ln_g[l][j], d_ln_b[l][j] = d_par[:3]
        if has_next:
            nl, nj = subs[idx + 1]
            d_mod[nl][3 * nj + 1], d_mod[nl][3 * nj] = d_par[3:]
        t = sv["t"]
        if j != 1:
            w1, w3, w2 = sv["big"]
            d_h1, d_h3 = matmul([[(d_y, w2)]], "nt", [BF16, BF16], tm=512, tn=1408, tk=D_MODEL,
                                extras=[sv["h1"], sv["h3"]], epi=swiglu_bwd_epi, name="ffn_down_bwd", n_outer=True)

            def ffn_weight_grads(after, ops=(sv["act"], t, d_y, d_h1, d_h3), key=("ffn", l, j // 2)):
                wait = () if after is None else (after,)
                act, t_in, d_out, d_1, d_3 = ops
                (g_w2,) = matmul([[(act, d_out)]], "tn", [GRAD_WIRE], tm=1408, tn=1024, tk=2048, name="ffn_dw2",
                                 after=wait)
                (g_w1,) = matmul([[(t_in, d_1)]], "tn", [GRAD_WIRE], tm=1024, tn=1408, tk=2048, name="ffn_dw1",
                                 after=wait)
                (g_w3,) = matmul([[(t_in, d_3)]], "tn", [GRAD_WIRE], tm=1024, tn=1408, tk=2048, name="ffn_dw3",
                                 after=wait)
                hand_over(key, [g_w1, g_w3, g_w2])

            if idx:
                ffn_weight_grads(None)
            else:
                deferred.append(ffn_weight_grads)
            (d_t,) = matmul([[(d_h1, w1), (d_h3, w3)]], "nt", [F32], tm=1024, tn=1024, tk=1408, name="ffn_up_bwd")
        elif l == 0:
            ab_w_in, ab_w_out = sv["big"]
            (d_ycat,) = matmul([[(d_y, ab_w_out)]], "nt", [F32], tm=512, tn=1024, tk=D_MODEL, name="ab_out_bwd")
            (g_out,) = matmul([[(sv["ycat"], d_y)]], "tn", [GRAD_WIRE], tm=1024, tn=1024, tk=2048, name="ab_dwout")
            (d_oa, d_ag, d_hb, d_xc, d_bz), (d_hg, d_mg, d_skip) = rowwise_bwd(
                mix_b_fn, sv["b_rows"], [Rows(d_ycat, split=HEAD_W)], mix_b_params, [0, 1, 2, 3, 4], [0, 1, 2],
                [F32, BF16, F32, F32, BF16], tile=tile, name="mix_b_bwd")
            g["hgrn_norm_g"], g["mlstm_norm_g"], g["mlstm_skip"] = (v.reshape(1, MIX_W) for v in (d_hg, d_mg, d_skip))
            d_qb, d_kb, d_bv, d_gates = chunk_scan_bwd(mlstm_step, sv["ml_xs"], ml_piece, sv["ml_kept"], [d_hb],
                                                       name="mlstm_bwd")
            d_qa, d_ka, d_ai, d_lf = chunk_scan_bwd(hgrn2_step, sv["hg_xs"], hg_piece, sv["hg_kept"], [d_oa],
                                                    name="hgrn2_bwd")
            a_cots = [Rows(v) for v in (d_qa, d_ka, d_lf, d_xc, d_qb, d_kb, d_gates)]
            (d_aq, d_af, d_xconv, d_graw), (d_l0, d_l1, d_l2, d_wq, d_wk, d_gb) = rowwise_bwd(
                mix_a_fn, sv["a_rows"], a_cots, mix_a_params, [0, 1, 2, 3], [0, 1, 2, 3, 4, 5],
                [BF16, BF16, F32, BF16], tile=tile, name="mix_a_bwd")
            g["hgrn_lb_logits"] = jnp.concatenate([d_l0, d_l1, d_l2], axis=0)
            g["mlstm_wq"] = _block_diag_of(d_wq, MIX_W // 4, 4)
            g["mlstm_wk"] = _block_diag_of(d_wk, MIX_W // 4, 4)
            g["mlstm_gate_b"] = d_gb[:, :8]
            d_bx, g["mlstm_conv_w"], g["mlstm_conv_b"] = conv_bwd(Rows(sv["proj"], (4, MIX_W)), d_xconv, mconv_w,
                                                                  tile=tile, name="mconv_bwd")
            d_proj = jnp.concatenate([d_aq, d_af, d_ai.astype(BF16), d_ag, d_bx, d_bv.astype(BF16), d_bz, d_graw],
                                     axis=1)
            (g_in,) = matmul([[(t, d_proj)]], "tn", [GRAD_WIRE], tm=256, tn=AB_ALL, tk=1024, name="ab_dwin")
            hand_over(("ab",), [g_in, g_out])
            (d_t,) = matmul([[(d_proj, ab_w_in)]], "nt", [F32], tm=512, tn=1024, tk=AB_ALL, name="ab_in_bwd")
        else:
            rg_w_in, rg_w_out = sv["big"]
            (d_hgate,) = matmul([[(d_y, rg_w_out)]], "nt", [F32], tm=512, tn=1024, tk=D_MODEL,
                                name="rg_out_bwd")
            (g_out,) = matmul([[(sv["hgate"], d_y)]], "tn", [GRAD_WIRE], tm=1024, tn=1024, tk=2048,
                                         name="rg_dwout")
            (d_h, d_ybr), _ = rowwise_bwd(lru_b_fn, sv["b_rows"], [Rows(d_hgate)], [], [0, 1], [], [F32, BF16],
                                          tile=tile, name="lru_b_bwd")
            d_a, d_u = lru_bwd(sv["a_t"], sv["h"], d_h, tile=min(128, s_len), name="lru_bwd")
            (d_xr,), (d_wa, d_wx, d_ba, d_bx_, d_lam) = rowwise_bwd(
                lru_a_fn, [Rows(sv["xr"], split=128)], [Rows(d_a, split=128), Rows(d_u, split=128)], lru_a_params,
                [0], [0, 1, 2, 3, 4], [F32], tile=tile, name="lru_a_bwd")
            g["rglru_wa"], g["rglru_wx"] = d_wa, d_wx
            g["rglru_ba"], g["rglru_bx"], g["rglru_lambda"] = (v.reshape(1, D_MODEL) for v in (d_ba, d_bx_, d_lam))
            d_xbr, g["rglru_conv_w"], g["rglru_conv_b"] = conv_bwd(Rows(sv["proj"], (1, D_MODEL)), d_xr, rconv_w,
                                                                   tile=tile, name="rconv_bwd")
            d_proj = jnp.concatenate([d_ybr, d_xbr], axis=1)
            (g_in,) = matmul([[(t, d_proj)]], "tn", [GRAD_WIRE], tm=1024, tn=1024, tk=2048, name="rg_dwin")
            hand_over(("rg",), [g_in, g_out])
            (d_t,) = matmul([[(d_proj, rg_w_in)]], "nt", [F32], tm=512, tn=1024, tk=1024, name="rg_in_bwd")
        d_xo, d_t_next = d_xres, d_t

    def first_bwd(rows, params):
        x0, d_res, d_t0 = rows
        _, vjp = jax.vjp(lambda r, p: pre_fn(r, p)[0], [x0], params)
        (d_x0,), d_p = vjp([d_t0])
        return [d_res + d_x0], d_p

    (grad_x,), (d_mod[0][1], d_mod[0][0]) = rowwise(
        first_bwd, [Rows(x), Rows(d_xo), Rows(d_t_next)], [mrow(0, 0, 1), mrow(0, 0, 0)], [(D_MODEL, F32)],
        [(1, D_MODEL)] * 2, tile=tile, name="pre_bwd", after=tuple(sent))
    sent.clear()
    g["ln_g"] = jnp.stack([jnp.concatenate(r, axis=0) for r in d_ln_g])
    g["ln_b"] = jnp.stack([jnp.concatenate(r, axis=0) for r in d_ln_b])
    d_mod = jnp.stack([jnp.concatenate(r, axis=0) for r in d_mod])
    small_sent = put_small(g, d_mod)
    for weight_grads in deferred:
        weight_grads(small_sent)
    return loss, grad_x, d_mod, g, list(sent)


MESH_ID = pl.DeviceIdType.MESH
ANY_SPEC = pl.BlockSpec(memory_space=pl.ANY)


def _my_position():
    return lax.axis_index("x"), lax.axis_index("y"), lax.axis_index("c")


def _flat_index(pos):
    return 4 * pos[0] + 2 * pos[1] + pos[2]


def _peer_position(pos, k):
    return tuple(lax.rem(p + ((k >> s) & 1), 2) for p, s in zip(pos, (2, 1, 0)))


def _exchange(x, gather, name):
    out_shape = (NDEV,) + x.shape if gather else x.shape

    def body(x_ref, o_ref, send_sems, recv_sems, local_sem):
        pos = _my_position()
        me = _flat_index(pos)
        local = pltpu.make_async_copy(x_ref if gather else x_ref.at[me], o_ref.at[me], local_sem)
        local.start()
        copies = []
        for k in range(1, NDEV):
            peer = _peer_position(pos, k)
            src = x_ref if gather else x_ref.at[_flat_index(peer)]
            copies.append(pltpu.make_async_remote_copy(
                src_ref=src, dst_ref=o_ref.at[me], send_sem=send_sems.at[k - 1], recv_sem=recv_sems.at[k - 1],
                device_id=peer, device_id_type=MESH_ID))
            copies[-1].start()
        for cp in copies:
            cp.wait()
        local.wait()

    return pl.pallas_call(
        body, name=name, in_specs=[ANY_SPEC], out_specs=ANY_SPEC,
        out_shape=jax.ShapeDtypeStruct(out_shape, x.dtype),
        scratch_shapes=[pltpu.SemaphoreType.DMA((NDEV - 1,)), pltpu.SemaphoreType.DMA((NDEV - 1,)),
                        pltpu.SemaphoreType.DMA],
    )(x)


HBM_SPEC = pl.BlockSpec(memory_space=pltpu.HBM)
SEM_SPEC = pl.BlockSpec(memory_space=pltpu.SEMAPHORE)
SIDE_EFFECT = pltpu.SideEffectType.DATAFLOW_SIDE_EFFECTING


def _exchange_copies(x_refs, land_refs, send_sems, recv_sems, gather):
    pos = _my_position()
    me = _flat_index(pos)
    copies = []
    for k in range(1, NDEV):
        peer = _peer_position(pos, k)
        for x_ref, land_ref, s_sem, r_sem in zip(x_refs, land_refs, send_sems, recv_sems):
            src = x_ref if gather else x_ref.at[_flat_index(peer)]
            copies.append(pltpu.make_async_remote_copy(src_ref=src, dst_ref=land_ref.at[me], send_sem=s_sem,
                                                       recv_sem=r_sem, device_id=peer, device_id_type=MESH_ID))
    return copies


def exchange_start(xs, gather, name):
    n = len(xs)
    land_shapes = [(NDEV,) + x.shape if gather else x.shape for x in xs]

    def body(*refs):
        x_refs, land_refs = refs[:n], refs[n:2 * n]
        send_sems, recv_sems = refs[2 * n:3 * n], refs[3 * n:4 * n]
        token = refs[-1]
        for cp in _exchange_copies(x_refs, land_refs, send_sems, recv_sems, gather):
            cp.start()
        token[...] = jnp.zeros(token.shape, token.dtype)

    sem = pltpu.SemaphoreType.DMA(())
    res = pl.pallas_call(
        body, name=name,
        out_shape=[sem] * (2 * n) + [pltpu.HBM(x.shape, x.dtype) for x in xs]
        + [pltpu.HBM(s, x.dtype) for s, x in zip(land_shapes, xs)] + [jax.ShapeDtypeStruct((8, 128), F32)],
        in_specs=[HBM_SPEC] * (2 * n),
        out_specs=[SEM_SPEC] * (2 * n) + [HBM_SPEC] * (2 * n) + [pl.BlockSpec(memory_space=pltpu.VMEM)],
        input_output_aliases={i: 2 * n + i for i in range(2 * n)},
        compiler_params=pltpu.CompilerParams(has_side_effects=SIDE_EFFECT),
    )(*[pltpu.with_memory_space_constraint(x, pltpu.HBM) for x in xs],
      *[pltpu.with_memory_space_constraint(lax.empty(s, x.dtype), pltpu.HBM) for s, x in zip(land_shapes, xs)])
    return (res[:n], res[n:2 * n], res[2 * n:3 * n], res[3 * n:4 * n]), res[-1]


def exchange_wait(handles, after, name):
    send_sems, recv_sems, x_thru, land_thru = handles
    n = len(x_thru)
    after = jax.tree_util.tree_leaves(after)

    def body(*refs):
        land_refs = refs[n:2 * n]
        s_sems, r_sems = refs[2 * n:3 * n], refs[3 * n:4 * n]
        pos = _my_position()
        for land_ref, s_sem, r_sem in zip(land_refs, s_sems, r_sems):
            seven = land_ref.at[pl.ds(0, NDEV - 1)]
            all_seven = pltpu.make_async_remote_copy(src_ref=seven, dst_ref=seven, send_sem=s_sem, recv_sem=r_sem,
                                                     device_id=pos, device_id_type=MESH_ID)
            all_seven.wait_send()
            all_seven.wait_recv()

    res = pl.pallas_call(
        body, name=name,
        out_shape=[pltpu.HBM(x.shape, x.dtype) for x in x_thru] + [pltpu.HBM(x.shape, x.dtype) for x in land_thru],
        in_specs=[HBM_SPEC] * (2 * n) + [SEM_SPEC] * (2 * n) + [ANY_SPEC] * len(after),
        out_specs=[HBM_SPEC] * (2 * n),
        input_output_aliases={i: i for i in range(2 * n)},
        compiler_params=pltpu.CompilerParams(has_side_effects=SIDE_EFFECT),
    )(*x_thru, *land_thru, *send_sems, *recv_sems, *after)
    return res[:n], res[n:]


def all_gather(x, name):
    return _exchange(x, True, name)


def all_to_all(x, name):
    return _exchange(x, False, name)


def _row_tile(n_rows, cap):
    best = None
    for t in range(8, min(n_rows, cap) + 1, 8):
        if n_rows % t == 0:
            best = t
    return best if best else n_rows


def adamw(w, m, v, slots, *, name, index=(), prev=None):
    n_rows, width = w.shape[-2:]
    n_lead = w.ndim - 2
    assert len(index) == n_lead
    n_slots = slots.shape[0]
    lanes = -(-width // 128) * 128
    tile = _row_tile(n_rows, max(8, (1 << 20) // (4 * lanes) // 8 * 8))
    bc1 = 1.0 - ADAM_B1 ** ADAM_STEP
    bc2 = 1.0 - ADAM_B2 ** ADAM_STEP

    def body(w_ref, m_ref, v_ref, s_ref, *rest):
        g_ref, d_ref, nm_ref, nv_ref = rest[-4:]
        g = s_ref[0].astype(F32)
        for k in range(1, n_slots):
            g = g + s_ref[k].astype(F32)
        wv = w_ref[...]
        nm = ADAM_B1 * m_ref[...] + (1.0 - ADAM_B1) * g
        nv = ADAM_B2 * v_ref[...] + (1.0 - ADAM_B2) * (g * g)
        g_ref[...] = g
        nm_ref[...] = nm
        nv_ref[...] = nv
        d_ref[...] = -ADAM_LR * ((nm / bc1) / (jnp.sqrt(nv / bc2) + ADAM_EPS) + ADAM_WD * wv)

    spec = pl.BlockSpec((None,) * n_lead + (tile, width), lambda i: tuple(index) + (i, 0))
    prev = list(prev) if prev is not None else []
    return pl.pallas_call(
        body, name=name, grid=(n_rows // tile,),
        in_specs=[spec, spec, spec, pl.BlockSpec((n_slots, tile, width), lambda i: (0, i, 0))]
        + [ANY_SPEC] * len(prev),
        out_specs=[spec] * 4, out_shape=[jax.ShapeDtypeStruct(w.shape, F32)] * 4,
        input_output_aliases={4 + k: k for k in range(len(prev))},
        compiler_params=_params(dimension_semantics=("parallel",)),
    )(w, m, v, slots, *prev)


def sum_slots(slots, *, name):
    n_slots, n_rows, width = slots.shape

    def body(s_ref, o_ref):
        @pl.when(pl.program_id(0) == 0)
        def _():
            o_ref[...] = s_ref[...]

        @pl.when(pl.program_id(0) > 0)
        def _():
            o_ref[...] += s_ref[...]

    return pl.pallas_call(
        body, name=name, grid=(n_slots,),
        in_specs=[pl.BlockSpec((None, n_rows, width), lambda k: (k, 0, 0))],
        out_specs=pl.BlockSpec((n_rows, width), lambda k: (0, 0)),
        out_shape=jax.ShapeDtypeStruct((n_rows, width), F32),
        compiler_params=_params(dimension_semantics=("arbitrary",)),
    )(slots)


def adamw_nd(w, m, v, slots, *, name):
    shp = w.shape
    two = (-1, shp[-1])
    res = adamw(w.reshape(two), m.reshape(two), v.reshape(two), slots.reshape((slots.shape[0],) + (w.size // shp[-1], shp[-1])),
                name=name)
    return [r.reshape(shp) for r in res]


def _pack(arrs):
    parts = []
    for a in arrs:
        flat = a.reshape(-1).astype(F32)
        parts.append(jnp.pad(flat, (0, (-flat.shape[0]) % 1024)))
    return jnp.concatenate(parts).reshape(-1, 128)


def _unpack(buf, shapes):
    outs, at = [], 0
    lead = buf.shape[:-2]
    flat = buf.reshape(lead + (-1,))
    for shp in shapes:
        n = int(np.prod(shp))
        outs.append(flat[..., at:at + n].reshape(lead + tuple(shp)))
        at += n + (-n) % 1024
    return outs


ARG_NAMES = ["x", "c", "ada_w", "ada_b", "ln_g", "ln_b", "ffn_w1", "ffn_w3", "ffn_w2", "hgrn_lb_logits", "ab_w_in",
             "ab_w_out", "hgrn_norm_g", "mlstm_conv_w", "mlstm_conv_b", "mlstm_wq", "mlstm_wk", "mlstm_gate_b",
             "mlstm_skip", "mlstm_norm_g", "rglru_w_in", "rglru_conv_w", "rglru_conv_b", "rglru_wa", "rglru_ba",
             "rglru_wx", "rglru_bx", "rglru_lambda", "rglru_w_out", "loss_target"]
WEIGHTS = ARG_NAMES[2:-1]
BIG = ["ffn_w1", "ffn_w3", "ffn_w2", "ab_w_in", "ab_w_out", "rglru_w_in", "rglru_w_out"]
REPLICATED = ["ada_b", "hgrn_lb_logits", "hgrn_norm_g", "mlstm_conv_b", "mlstm_wq", "mlstm_wk", "mlstm_gate_b",
              "mlstm_skip", "mlstm_norm_g", "rglru_wa", "rglru_wx"]
SHARDED_SMALL = ["ln_g", "ln_b", "mlstm_conv_w", "rglru_conv_w", "rglru_conv_b", "rglru_ba", "rglru_bx", "rglru_lambda"]


def _unshard_last(gathered):
    moved = jnp.moveaxis(gathered, 0, -2)
    return moved.reshape(moved.shape[:-2] + (NDEV * moved.shape[-1],))


def _shard_last(full):
    split = full.reshape(full.shape[:-1] + (NDEV, full.shape[-1] // NDEV))
    return jnp.moveaxis(split, -2, 0)


def kernel(x, c, ada_w, ada_b, ln_g, ln_b, ffn_w1, ffn_w3, ffn_w2, hgrn_lb_logits, ab_w_in, ab_w_out, hgrn_norm_g, mlstm_conv_w, mlstm_conv_b, mlstm_wq, mlstm_wk, mlstm_gate_b, mlstm_skip, mlstm_norm_g, rglru_w_in, rglru_conv_w, rglru_conv_b, rglru_wa, rglru_ba, rglru_wx, rglru_bx, rglru_lambda, rglru_w_out, loss_target, m_ada_w, m_ada_b, m_ln_g, m_ln_b, m_ffn_w1, m_ffn_w3, m_ffn_w2, m_hgrn_lb_logits, m_ab_w_in, m_ab_w_out, m_hgrn_norm_g, m_mlstm_conv_w, m_mlstm_conv_b, m_mlstm_wq, m_mlstm_wk, m_mlstm_gate_b, m_mlstm_skip, m_mlstm_norm_g, m_rglru_w_in, m_rglru_conv_w, m_rglru_conv_b, m_rglru_wa, m_rglru_ba, m_rglru_wx, m_rglru_bx, m_rglru_lambda, m_rglru_w_out, v_ada_w, v_ada_b, v_ln_g, v_ln_b, v_ffn_w1, v_ffn_w3, v_ffn_w2, v_hgrn_lb_logits, v_ab_w_in, v_ab_w_out, v_hgrn_norm_g, v_mlstm_conv_w, v_mlstm_conv_b, v_mlstm_wq, v_mlstm_wk, v_mlstm_gate_b, v_mlstm_skip, v_mlstm_norm_g, v_rglru_w_in, v_rglru_conv_w, v_rglru_conv_b, v_rglru_wa, v_rglru_ba, v_rglru_wx, v_rglru_bx, v_rglru_lambda, v_rglru_w_out):
    args = locals()
    p = {n: args[n] for n in ARG_NAMES}
    mom = {n: (args["m_" + n], args["v_" + n]) for n in WEIGHTS}
    me = _flat_index(_my_position())

    keys = [("ffn", 0, 0), ("ab",), ("ffn", 0, 1), ("ffn", 1, 0), ("rg",), ("ffn", 1, 1)]
    names = {("ab",): ("ab_w_in", "ab_w_out"), ("rg",): ("rglru_w_in", "rglru_w_out")}
    for l in range(DEPTH):
        for i in range(2):
            names[("ffn", l, i)] = ("ffn_w1", "ffn_w3", "ffn_w2")

    def part(key):
        return (lambda a: a[key[1], key[2]]) if key[0] == "ffn" else (lambda a: a[0])

    gather_handles = {}

    def landed(handles, own_of, after, name):
        sources, lands = exchange_wait(handles, after, name)
        return [lax.dynamic_update_index_in_dim(ld, own_of(src), me, 0) for src, ld in zip(sources, lands)]

    starts_next = {("ffn", 0, 0): [("ab",)], ("ab",): [("ffn", 0, 1), ("ffn", 1, 0)],
                   ("ffn", 0, 1): [("rg",), ("ffn", 1, 1)]}

    def start_gather(key, after):
        shards = [part(key)(p[n]).astype(BF16) for n in names[key]]
        if after is not None:
            shards, _ = lax.optimization_barrier((shards, after))
        gather_handles[key], token = exchange_start(shards, True, "gather_start_" + "_".join(map(str, key)))
        return token

    def get_weights(key, after):
        got = landed(gather_handles[key], lambda src: src, after, "gather_wait_" + "_".join(map(str, key)))
        tokens = tuple(start_gather(nxt, got) for nxt in starts_next.get(key, []))
        if key[0] == "ffn":
            return (_unshard_last(got[0]), _unshard_last(got[1]), got[2].reshape(D_FF, D_MODEL)), tokens
        w_in = _unshard_last(got[0])
        if key[0] == "ab":
            w_in = jnp.concatenate([w_in[:, :AB_MAIN], jnp.pad(w_in[:, AB_MAIN:], ((0, 0), (0, 120)))], axis=1)
        return (w_in, got[1].reshape(D_MODEL, D_MODEL)), tokens

    scatter_handles = {}

    def put_grads(key, grads):
        if key[0] == "ffn":
            slots = [_shard_last(grads[0]), _shard_last(grads[1]), grads[2].reshape(NDEV, D_FF // NDEV, D_MODEL)]
        else:
            g_in = grads[0][:, :AB_MAIN + 8] if key[0] == "ab" else grads[0]
            slots = [_shard_last(g_in), grads[1].reshape(NDEV, D_MODEL // NDEV, D_MODEL)]
        scatter_handles[key], token = exchange_start(slots, False, "scatter_start_" + "_".join(map(str, key)))
        return token

    sharded_shapes = [p[n].shape for n in SHARDED_SMALL]
    small = all_gather(_pack([p[n] for n in SHARDED_SMALL] + [c]), "gather_small")
    started = [start_gather(keys[0], small)]
    small, _ = lax.optimization_barrier((small, started))
    per_dev = _unpack(small, sharded_shapes + [c.shape])
    full_small = {n: _unshard_last(per_dev[i]) for i, n in enumerate(SHARDED_SMALL)}
    c_all = per_dev[-1].reshape(NDEV, D_MODEL)

    c16 = jnp.pad(c_all, ((0, 8), (0, 0)))
    (c_act,), _ = rowwise(lambda r, q: ([_silu(r[0])], []), [Rows(c16)], [], [(D_MODEL, BF16)], [], tile=16,
                          name="cond_act")
    n_ada = ada_w.shape[-1]
    ada_b_mine = lax.dynamic_slice_in_dim(ada_b, me * n_ada, n_ada, axis=1)
    ada_cols = []
    for l in range(DEPTH):
        bias = jnp.broadcast_to(ada_b_mine[l][None, :], (16, n_ada))
        (cols,) = matmul([[(c_act, ada_w[l])]], "nn", [F32], tm=16, tn=n_ada, tk=D_MODEL, extras=[bias],
                         epi=lambda accs, ex: [accs[0] + ex[0]], name="ada_fwd")
        ada_cols.append(cols[:8])
    ada_mine = all_to_all(jnp.stack(ada_cols, axis=1), "ada_to_owner")
    mod = jnp.moveaxis(ada_mine, 0, 1).reshape(DEPTH, 9, D_MODEL)


    w = {"ln_g": full_small["ln_g"], "ln_b": full_small["ln_b"], "hgrn_lb_logits": hgrn_lb_logits}
    for n in ("hgrn_norm_g", "mlstm_conv_b", "mlstm_wq", "mlstm_wk", "mlstm_gate_b", "mlstm_skip", "mlstm_norm_g",
              "rglru_wa", "rglru_wx"):
        w[n] = p[n][0]
    for n in ("mlstm_conv_w", "rglru_conv_w", "rglru_conv_b", "rglru_ba", "rglru_bx", "rglru_lambda"):
        w[n] = full_small[n][0]

    small_names = REPLICATED + SHARDED_SMALL
    small_handles = []

    def put_small(g_small_parts, d_modulation):
        parts_ = dict(g_small_parts, ada_b=d_modulation.reshape(DEPTH, 9 * D_MODEL))
        handles, token = exchange_start([_pack([parts_[n] for n in small_names])], True, "gather_start_small_grads")
        small_handles.append(handles)
        return token

    loss, grad_x, d_mod, g, last_sent = local_step(x[0], loss_target[0], mod, w, get_weights, put_grads, put_small,
                                                   started)
    loss = lax.psum(loss, ("x", "y", "c"))

    outs = {}
    def update_group(key, after):
        slots = landed(scatter_handles[key], lambda src: lax.dynamic_index_in_dim(src, me, 0, keepdims=False), after,
                       "scatter_wait_" + "_".join(map(str, key)))
        for n, sl in zip(names[key], slots):
            index = key[1:] if key[0] == "ffn" else (0,)
            outs[n] = adamw(p[n], mom[n][0], mom[n][1], sl, name="adamw_" + n, index=index, prev=outs.get(n))
        return [outs[n][0] for n in names[key]]

    full_shapes = [p[n].shape for n in REPLICATED] + [full_small[n].shape for n in SHARDED_SMALL]
    (all_small,) = landed(small_handles[0], lambda src: src, (last_sent, grad_x), "gather_wait_small_grads")
    summed = sum_slots(all_small, name="sum_small_grads")
    g_small = dict(zip(small_names, _unpack(summed, full_shapes)))
    rep = adamw(*[_pack([t[n] for n in REPLICATED]) for t in (p, {n: mom[n][0] for n in WEIGHTS},
                                                               {n: mom[n][1] for n in WEIGHTS})],
                _pack([g_small[n] for n in REPLICATED])[None], name="adamw_replicated")
    rep = [_unpack(r, [p[n].shape for n in REPLICATED]) for r in rep]
    for i, n in enumerate(REPLICATED):
        outs[n] = [r[i] for r in rep]
    g_mine = {n: lax.dynamic_slice_in_dim(g_small[n], me * p[n].shape[-1], p[n].shape[-1], axis=-1)
              for n in SHARDED_SMALL}
    shd = adamw(*[_pack([t[n] for n in SHARDED_SMALL]) for t in (p, {n: mom[n][0] for n in WEIGHTS},
                                                                  {n: mom[n][1] for n in WEIGHTS})],
                _pack([g_mine[n] for n in SHARDED_SMALL])[None], name="adamw_sharded_small")
    done = shd[0]
    for key in keys[:0:-1] + keys[:1]:
        done = update_group(key, done)
    shd = [_unpack(r, sharded_shapes) for r in shd]
    for i, n in enumerate(SHARDED_SMALL):
        outs[n] = [r[i] for r in shd]

    d_ada = all_small[:, :DEPTH * 9 * D_MODEL // 128].reshape(NDEV, DEPTH, 9 * D_MODEL)
    d_mine = lax.dynamic_slice_in_dim(d_ada, me * n_ada, n_ada, axis=2)
    g_ada = []
    for l in range(DEPTH):
        d16 = jnp.pad(d_mine[:, l], ((0, 8), (0, 0)))
        (gl,) = matmul([[(c_act, d16)]], "tn", [F32], tm=D_MODEL, tn=n_ada, tk=16, name="ada_bwd")
        g_ada.append(gl)
    outs["ada_w"] = adamw_nd(ada_w, *mom["ada_w"], jnp.stack(g_ada)[None], name="adamw_ada_w")

    result = [loss, grad_x[None]]
    for k in range(4):
        result += [outs[n][k].reshape(p[n].shape) for n in WEIGHTS]
    return tuple(result)
```

```python
import functools

import jax
import jax.numpy as jnp
import numpy as np
from jax import lax
from jax.experimental import pallas as pl
from jax.experimental.pallas import tpu as pltpu

F32 = jnp.float32
BF16 = jnp.bfloat16
HIGHEST = lax.Precision.HIGHEST

NDEV = 8
D_MODEL = 1024
D_FF = 2816
DEPTH = 2
CHUNK = 64
SUB = 16
HEADS = 4
HEAD_W = 128
MIX_W = HEADS * HEAD_W
AB_MAIN = 7 * MIX_W
AB_ALL = AB_MAIN + 128
CONV_W = 4
C_BLOCKS = 8
RG_C = 8.0
ALPHA = (2 * DEPTH) ** 0.25
FFN_RES_W = 0.5
NEG = -1e30

ADAM_LR = 0.001
ADAM_B1 = 0.9
ADAM_B2 = 0.999
ADAM_EPS = 1e-08
ADAM_WD = 0.01
ADAM_STEP = 10

VMEM_LIMIT = 56 * 1024 * 1024
GRAD_WIRE = jnp.bfloat16

NN = ((1,), (0,))
NT = ((1,), (1,))
TN = ((0,), (0,))


def _dot(a, b, dims, precision=None):
    return lax.dot_general(a, b, (dims, ((), ())), precision=precision, preferred_element_type=F32)


def _make_mm(dims, d_lhs, d_rhs, swap_lhs, swap_rhs, prec):
    def cast(v):
        return v.astype(BF16) if prec is None else v.astype(F32)

    @jax.custom_vjp
    def mm(a, b):
        return _dot(cast(a), cast(b), dims, prec)

    def fwd(a, b):
        return mm(a, b), (a, b)

    def bwd(res, g):
        a, b = res
        g = cast(g)
        da = _dot(cast(b), g, d_lhs, prec) if swap_lhs else _dot(g, cast(b), d_lhs, prec)
        db = _dot(g, cast(a), d_rhs, prec) if swap_rhs else _dot(cast(a), g, d_rhs, prec)
        return da.astype(a.dtype), db.astype(b.dtype)

    mm.defvjp(fwd, bwd)
    return mm


def _mm_family(prec):
    return (_make_mm(NN, NT, TN, False, False, prec), _make_mm(NT, NN, TN, False, True, prec),
            _make_mm(TN, NT, NN, True, False, prec))


def _round(v):
    return v.astype(BF16).astype(F32)


@jax.custom_vjp
def row_dot(a, n):
    return jnp.sum(_round(a) * _round(n), axis=1, keepdims=True)


def _row_dot_fwd(a, n):
    return row_dot(a, n), (a, n)


def _row_dot_bwd(res, g):
    a, n = res
    g = _round(g)
    return g * _round(n), jnp.sum(g * _round(a), axis=0, keepdims=True)


row_dot.defvjp(_row_dot_fwd, _row_dot_bwd)


@jax.custom_vjp
def col_dot(s, a):
    return jnp.sum(_round(s) * _round(a), axis=0, keepdims=True)


def _col_dot_fwd(s, a):
    return col_dot(s, a), (s, a)


def _col_dot_bwd(res, g):
    s, a = res
    g = _round(g)
    return jnp.sum(g * _round(a), axis=1, keepdims=True), _round(s) * g


col_dot.defvjp(_col_dot_fwd, _col_dot_bwd)

mm_nn, mm_nt, mm_tn = _mm_family(None)
mid_nn, mid_nt, mid_tn = _mm_family(lax.Precision.HIGH)
hi_nn, hi_nt, hi_tn = _mm_family(HIGHEST)


def _silu(v):
    return v * jax.nn.sigmoid(v)


def _log_sigmoid(v):
    return jnp.minimum(v, 0.0) - jnp.log1p(jnp.exp(-jnp.abs(v)))


def _softplus(v):
    return jnp.maximum(v, 0.0) + jnp.log1p(jnp.exp(-jnp.abs(v)))


def _neg_expm1(v):
    series = -v * (1.0 + v * (0.5 + v * (1.0 / 6.0 + v * (1.0 / 24.0 + v * (1.0 / 120.0)))))
    return jnp.where(v > -0.05, series, 1.0 - jnp.exp(v))


def _params(**kw):
    return pltpu.CompilerParams(vmem_limit_bytes=VMEM_LIMIT, **kw)


class Rows:
    def __init__(self, arr, block=None, split=None):
        self.arr = arr
        self.block = block
        self.split = split

    @property
    def width(self):
        return self.block[1] if self.block else self.arr.shape[1]


def _load(ref, split):
    if ref.ndim == 3:
        return [ref[k].astype(F32) for k in range(ref.shape[0])]
    if split is None:
        return ref[...].astype(F32)
    return [ref[:, k * split:(k + 1) * split].astype(F32) for k in range(ref.shape[1] // split)]


def _store(ref, val, accumulate=False):
    if isinstance(val, (list, tuple)):
        if ref.ndim == 3:
            for k, v in enumerate(val):
                ref[k] = (ref[k] + v if accumulate else v).astype(ref.dtype)
            return
        w = ref.shape[1] // len(val)
        for k, v in enumerate(val):
            sl = slice(k * w, (k + 1) * w)
            ref[:, sl] = (ref[:, sl] + v if accumulate else v).astype(ref.dtype)
    else:
        ref[...] = (ref[...] + val if accumulate else val).astype(ref.dtype)


def rowwise(fn, rows, params, out_rows, out_sums, *, tile, name, after=()):
    n_rows = rows[0].arr.shape[0]
    n_r, n_p, n_o = len(rows), len(params), len(out_rows)
    n_in = n_r + n_p + len(after)
    splits = [r.split for r in rows]

    def body(*refs):
        r_refs, p_refs = refs[:n_r], refs[n_r:n_r + n_p]
        o_refs, s_refs = refs[n_in:n_in + n_o], refs[n_in + n_o:]
        row_out, sum_out = fn([_load(r, s) for r, s in zip(r_refs, splits)], [_load(p, None) for p in p_refs])
        for ref, val in zip(o_refs, row_out):
            _store(ref, val)
        if s_refs:
            @pl.when(pl.program_id(0) == 0)
            def _():
                for ref in s_refs:
                    ref[...] = jnp.zeros(ref.shape, ref.dtype)

            for ref, val in zip(s_refs, sum_out):
                _store(ref, val, accumulate=True)

    in_specs = []
    for r in rows:
        blk = r.block[0] if r.block else 0
        in_specs.append(pl.BlockSpec((tile, r.width), functools.partial(lambda i, b: (i, b), b=blk)))
    for p in params:
        in_specs.append(pl.BlockSpec(p.shape, functools.partial(lambda i, n: (0,) * n, n=p.ndim)))
    in_specs += [pl.BlockSpec(memory_space=pl.ANY)] * len(after)
    out_shape = [jax.ShapeDtypeStruct((n_rows, w), dt) for w, dt in out_rows]
    out_specs = [pl.BlockSpec((tile, w), lambda i: (i, 0)) for w, _ in out_rows]
    for shp in out_sums:
        out_shape.append(jax.ShapeDtypeStruct(shp, F32))
        out_specs.append(pl.BlockSpec(shp, functools.partial(lambda i, n: (0,) * n, n=len(shp))))
    res = pl.pallas_call(
        body, name=name, grid=(n_rows // tile,), in_specs=in_specs, out_specs=out_specs, out_shape=out_shape,
        compiler_params=_params(dimension_semantics=("arbitrary",)),
    )(*[r.arr for r in rows], *params, *after)
    return res[:n_o], res[n_o:]


def rowwise_bwd(fn, rows, cots, params, want_rows, want_params, out_dtypes, *, tile, name, after=()):
    n = len(rows)

    def bwd(row_vals, param_vals):
        prim, cot = row_vals[:n], row_vals[n:]
        _, vjp = jax.vjp(lambda r, p: fn(r, p)[0], prim, param_vals)
        d_rows, d_params = vjp(cot)
        return [d_rows[i] for i in want_rows], [d_params[j] for j in want_params]

    out_rows = [(rows[i].width, dt) for i, dt in zip(want_rows, out_dtypes)]
    out_sums = [params[j].shape for j in want_params]
    return rowwise(bwd, list(rows) + list(cots), params, out_rows, out_sums, tile=tile, name=name, after=after)


def matmul(groups, mode, outs, *, tm, tn, tk, extras=(), epi=None, name, n_outer=False, after=()):
    a0, b0 = groups[0][0]
    if mode == "tn":
        k_dim, m_dim = a0.shape
    else:
        m_dim, k_dim = a0.shape
    n_dim = b0.shape[0] if mode == "nt" else b0.shape[1]
    tm, tn, tk = min(tm, m_dim), min(tn, n_dim), min(tk, k_dim)
    assert m_dim % tm == 0 and n_dim % tn == 0 and k_dim % tk == 0, (name, m_dim, n_dim, k_dim)
    nk = k_dim // tk
    pairs = [p for g in groups for p in g]
    n_pairs, n_groups, n_ex, n_out = len(pairs), len(groups), len(extras), len(outs)
    dims = {"nn": NN, "nt": NT, "tn": TN}[mode]

    def body(*refs):
        ab = refs[:2 * n_pairs]
        ex = refs[2 * n_pairs:2 * n_pairs + n_ex]
        n_in = 2 * n_pairs + n_ex + len(after)
        o_refs = refs[n_in:n_in + n_out]
        accs = refs[n_in + n_out:]

        def partial_sums():
            sums, p = [], 0
            for g in groups:
                tot = None
                for _ in g:
                    d = _dot(ab[2 * p][...].astype(BF16), ab[2 * p + 1][...].astype(BF16), dims)
                    tot = d if tot is None else tot + d
                    p += 1
                sums.append(tot)
            return sums

        def finish(vals):
            res = epi(vals, [e[...] for e in ex]) if epi else vals
            for ref, v in zip(o_refs, res):
                ref[...] = v.astype(ref.dtype)

        if nk == 1:
            finish(partial_sums())
        else:
            k = pl.program_id(2)

            @pl.when(k == 0)
            def _():
                for acc in accs:
                    acc[...] = jnp.zeros(acc.shape, F32)

            for acc, s in zip(accs, partial_sums()):
                acc[...] += s

            @pl.when(k == nk - 1)
            def _():
                finish([acc[...] for acc in accs])

    def at(pick):
        return (lambda j, i, k: pick(i, j, k)) if n_outer else pick

    if mode == "nn":
        a_spec = pl.BlockSpec((tm, tk), at(lambda i, j, k: (i, k)))
        b_spec = pl.BlockSpec((tk, tn), at(lambda i, j, k: (k, j)))
    elif mode == "nt":
        a_spec = pl.BlockSpec((tm, tk), at(lambda i, j, k: (i, k)))
        b_spec = pl.BlockSpec((tn, tk), at(lambda i, j, k: (j, k)))
    else:
        a_spec = pl.BlockSpec((tk, tm), at(lambda i, j, k: (k, i)))
        b_spec = pl.BlockSpec((tk, tn), at(lambda i, j, k: (k, j)))
    mn_spec = pl.BlockSpec((tm, tn), at(lambda i, j, k: (i, j)))
    grid = (n_dim // tn, m_dim // tm, nk) if n_outer else (m_dim // tm, n_dim // tn, nk)
    return pl.pallas_call(
        body, name=name, grid=grid,
        in_specs=[a_spec, b_spec] * n_pairs + [mn_spec] * n_ex + [pl.BlockSpec(memory_space=pl.ANY)] * len(after),
        out_specs=[mn_spec] * n_out,
        out_shape=[jax.ShapeDtypeStruct((m_dim, n_dim), dt) for dt in outs],
        scratch_shapes=[pltpu.VMEM((tm, tn), F32)] * (n_groups if nk > 1 else 0),
        compiler_params=_params(dimension_semantics=("parallel", "parallel", "arbitrary")),
    )(*[x for p in pairs for x in p], *extras, *after)


def conv_fwd(x, w, b, *, tile, name):
    n_rows, width = x.arr.shape[0], x.width
    blk = x.block[0] if x.block else 0

    def body(x_ref, halo_ref, w_ref, b_ref, y_ref, buf):
        i = pl.program_id(0)
        halo = halo_ref[...]
        buf[0:8, :] = jnp.where(i == 0, jnp.zeros_like(halo), _round(halo))
        buf[8:, :] = _round(x_ref[...])
        acc = jnp.zeros((tile, width), F32)
        for j in range(CONV_W):
            s = CONV_W - 1 - j
            acc = acc + _round(w_ref[j:j + 1, :]) * buf[8 - s:8 - s + tile, :]
        y_ref[...] = acc + b_ref[...]

    hb = tile // 8
    return pl.pallas_call(
        body, name=name, grid=(n_rows // tile,),
        in_specs=[pl.BlockSpec((tile, width), lambda i: (i, blk)),
                  pl.BlockSpec((8, width), lambda i: (jnp.maximum(i * hb - 1, 0), blk)),
                  pl.BlockSpec((CONV_W, width), lambda i: (0, 0)),
                  pl.BlockSpec((1, width), lambda i: (0, 0))],
        out_specs=pl.BlockSpec((tile, width), lambda i: (i, 0)),
        out_shape=jax.ShapeDtypeStruct((n_rows, width), F32),
        scratch_shapes=[pltpu.VMEM((tile + 8, width), F32)],
        compiler_params=_params(dimension_semantics=("arbitrary",)),
    )(x.arr, x.arr, w, b)


def conv_bwd(x, dy, w, *, tile, name):
    n_rows, width = x.arr.shape[0], x.width
    blk = x.block[0] if x.block else 0
    n_tiles = n_rows // tile

    def body(x_ref, xh_ref, dy_ref, dyh_ref, w_ref, dx_ref, dw_ref, db_ref, xbuf, dbuf):
        i = pl.program_id(0)
        xh, dyh = xh_ref[...], dyh_ref[...]
        xbuf[0:8, :] = jnp.where(i == 0, jnp.zeros_like(xh), _round(xh))
        xbuf[8:, :] = _round(x_ref[...])
        dy_t = dy_ref[...]
        dy_r = _round(dy_t)
        dbuf[0:tile, :] = dy_r
        dbuf[tile:, :] = jnp.where(i == n_tiles - 1, jnp.zeros_like(dyh), _round(dyh))

        @pl.when(i == 0)
        def _():
            dw_ref[...] = jnp.zeros(dw_ref.shape, F32)
            db_ref[...] = jnp.zeros(db_ref.shape, F32)

        acc = jnp.zeros((tile, width), F32)
        for j in range(CONV_W):
            s = CONV_W - 1 - j
            acc = acc + _round(w_ref[j:j + 1, :]) * dbuf[s:s + tile, :]
            dw_ref[j:j + 1, :] += jnp.sum(dy_r * xbuf[8 - s:8 - s + tile, :], axis=0, keepdims=True)
        dx_ref[...] = acc.astype(dx_ref.dtype)
        db_ref[...] += jnp.sum(dy_t, axis=0, keepdims=True)

    hb = tile // 8
    return pl.pallas_call(
        body, name=name, grid=(n_tiles,),
        in_specs=[pl.BlockSpec((tile, width), lambda i: (i, blk)),
                  pl.BlockSpec((8, width), lambda i: (jnp.maximum(i * hb - 1, 0), blk)),
                  pl.BlockSpec((tile, width), lambda i: (i, 0)),
                  pl.BlockSpec((8, width), lambda i: (jnp.minimum((i + 1) * hb, n_tiles * hb - 1), 0)),
                  pl.BlockSpec((CONV_W, width), lambda i: (0, 0))],
        out_specs=[pl.BlockSpec((tile, width), lambda i: (i, 0)),
                   pl.BlockSpec((CONV_W, width), lambda i: (0, 0)),
                   pl.BlockSpec((1, width), lambda i: (0, 0))],
        out_shape=[jax.ShapeDtypeStruct((n_rows, width), BF16),
                   jax.ShapeDtypeStruct((CONV_W, width), F32),
                   jax.ShapeDtypeStruct((1, width), F32)],
        scratch_shapes=[pltpu.VMEM((tile + 8, width), F32), pltpu.VMEM((tile + 8, width), F32)],
        compiler_params=_params(dimension_semantics=("arbitrary",)),
    )(x.arr, x.arr, dy, dy, w)


def _pieces(ref, col_w, row_h):
    n_c, n_r = ref.shape[1] // col_w, ref.shape[0] // row_h
    return [[ref[r * row_h:(r + 1) * row_h, c * col_w:(c + 1) * col_w].astype(F32) for r in range(n_r)]
            for c in range(n_c)]


def _store_pieces(ref, vals, col_w, row_h):
    for c, col in enumerate(vals):
        for r, v in enumerate(col):
            ref[r * row_h:(r + 1) * row_h, c * col_w:(c + 1) * col_w] = v.astype(ref.dtype)


def _x_spec(x, n_chunks, reverse):
    blk = x.block[0] if x.block else 0
    if reverse:
        return pl.BlockSpec((CHUNK, x.width), functools.partial(lambda n, b: (n_chunks - 1 - n, b), b=blk))
    return pl.BlockSpec((CHUNK, x.width), functools.partial(lambda n, b: (n, b), b=blk))


def chunk_scan_fwd(step, xs, piece, state_shapes, out_widths, *, name):
    n_rows = xs[0].arr.shape[0]
    n_chunks = n_rows // CHUNK
    n_x, n_s, n_o = len(xs), len(state_shapes), len(out_widths)

    def body(*refs):
        x_refs, o_refs = refs[:n_x], refs[n_x:n_x + n_o]
        keep_refs, st_refs = refs[n_x + n_o:n_x + n_o + n_s], refs[n_x + n_o + n_s:]

        @pl.when(pl.program_id(0) == 0)
        def _():
            for st in st_refs:
                st[...] = jnp.zeros(st.shape, F32)

        states = [st[...] for st in st_refs]
        for keep, s in zip(keep_refs, states):
            keep[...] = s
        new_states, outs = step(states, [_pieces(x, *p) for x, p in zip(x_refs, piece)])
        for st, s in zip(st_refs, new_states):
            st[...] = s
        for o, v in zip(o_refs, outs):
            o[...] = v

    out_shape = [jax.ShapeDtypeStruct((n_rows, w), F32) for w in out_widths]
    out_specs = [pl.BlockSpec((CHUNK, w), lambda n: (n, 0)) for w in out_widths]
    for shp in state_shapes:
        out_shape.append(jax.ShapeDtypeStruct((n_chunks,) + shp, F32))
        out_specs.append(pl.BlockSpec((None,) + shp, lambda n: (n, 0, 0)))
    res = pl.pallas_call(
        body, name=name, grid=(n_chunks,),
        in_specs=[_x_spec(x, n_chunks, False) for x in xs],
        out_specs=out_specs, out_shape=out_shape,
        scratch_shapes=[pltpu.VMEM(shp, F32) for shp in state_shapes],
        compiler_params=_params(dimension_semantics=("arbitrary",)),
    )(*[x.arr for x in xs])
    return res[:n_o], res[n_o:]


def chunk_scan_bwd(step, xs, piece, kept, d_outs, *, name):
    n_rows = xs[0].arr.shape[0]
    n_chunks = n_rows // CHUNK
    n_x, n_s, n_o = len(xs), len(kept), len(d_outs)
    state_shapes = [k.shape[1:] for k in kept]

    def body(*refs):
        x_refs, k_refs = refs[:n_x], refs[n_x:n_x + n_s]
        do_refs = refs[n_x + n_s:n_x + n_s + n_o]
        dx_refs = refs[n_x + n_s + n_o:2 * n_x + n_s + n_o]
        ds_refs = refs[2 * n_x + n_s + n_o:]

        @pl.when(pl.program_id(0) == 0)
        def _():
            for ds in ds_refs:
                ds[...] = jnp.zeros(ds.shape, F32)

        states = [k[...] for k in k_refs]
        inputs = [_pieces(x, *p) for x, p in zip(x_refs, piece)]
        _, vjp = jax.vjp(step, states, inputs)
        d_states, d_inputs = vjp(([ds[...] for ds in ds_refs], [do[...] for do in do_refs]))
        for ds, v in zip(ds_refs, d_states):
            ds[...] = v
        for dx, v, p in zip(dx_refs, d_inputs, piece):
            _store_pieces(dx, v, *p)

    rev3 = lambda n: (n_chunks - 1 - n, 0, 0)
    rev2 = lambda n: (n_chunks - 1 - n, 0)
    return pl.pallas_call(
        body, name=name, grid=(n_chunks,),
        in_specs=[_x_spec(x, n_chunks, True) for x in xs]
        + [pl.BlockSpec((None,) + shp, rev3) for shp in state_shapes]
        + [pl.BlockSpec((CHUNK, d.shape[1]), rev2) for d in d_outs],
        out_specs=[pl.BlockSpec((CHUNK, x.width), rev2) for x in xs],
        out_shape=[jax.ShapeDtypeStruct((n_rows, x.width), F32) for x in xs],
        scratch_shapes=[pltpu.VMEM(shp, F32) for shp in state_shapes],
        compiler_params=_params(dimension_semantics=("arbitrary",)),
    )(*[x.arr for x in xs], *kept, *d_outs)


def _tri(n, strict=False):
    r = lax.broadcasted_iota(jnp.int32, (n, n), 0)
    c = lax.broadcasted_iota(jnp.int32, (n, n), 1)
    return (r > c) if strict else (r >= c)


def hgrn2_step(states, inputs):
    q_all, k_all, v_all, lf_all = inputs
    n_sub = CHUNK // SUB
    low = _tri(SUB).astype(F32)
    ones_sub = jnp.ones((SUB, SUB), F32)
    ones_chunk = jnp.ones((CHUNK, HEAD_W), F32)
    new_states, outs = [], []
    for h in range(HEADS):
        state = states[h]
        q, k, v, lf = q_all[h], k_all[h], v_all[h], lf_all[h]
        cum = [mid_nn(low, lf[i]) for i in range(n_sub)]
        tot = [mid_nn(ones_sub, lf[i]) for i in range(n_sub)]
        start = [jnp.zeros((SUB, HEAD_W), F32)]
        for i in range(n_sub):
            start.append(start[-1] + tot[i])
        q_in = [q[i] * jnp.exp(cum[i]) for i in range(n_sub)]
        intra = []
        for i in range(n_sub):
            keys = [k[j] * jnp.exp(start[i] - start[j] - cum[j]) for j in range(i)]
            keys.append(k[i] * jnp.exp(jnp.minimum(-cum[i], 80.0)))
            att = mid_nt(q_in[i], jnp.concatenate(keys, axis=0))
            r_id = lax.broadcasted_iota(jnp.int32, att.shape, 0)
            c_id = lax.broadcasted_iota(jnp.int32, att.shape, 1)
            att = jnp.where(c_id - SUB * i <= r_id, att, 0.0)
            intra.append(mm_nn(att, jnp.concatenate(v[:i + 1], axis=0)))
        q_state = jnp.concatenate([q_in[i] * jnp.exp(start[i]) for i in range(n_sub)], axis=0)
        out = mm_nn(q_state, state) + jnp.concatenate(intra, axis=0)
        k_end = jnp.concatenate([k[j] * jnp.exp(start[n_sub] - start[j] - cum[j]) for j in range(n_sub)], axis=0)
        decay = jnp.exp(mid_tn(jnp.concatenate(lf, axis=0), ones_chunk))
        new_states.append(decay * state + mm_tn(k_end, jnp.concatenate(v, axis=0)))
        outs.append(out)
    return new_states, [jnp.concatenate(outs, axis=1)]


def mlstm_step(states, inputs):
    q_all, k_all, v_all, gates = inputs
    gates = gates[0][0]
    c_st, n_st, m_st = states[:HEADS], states[HEADS:2 * HEADS], states[2 * HEADS:]
    lane = lax.broadcasted_iota(jnp.int32, (CHUNK, 128), 1)
    low = _tri(CHUNK).astype(F32)
    causal = _tri(CHUNK)
    gates_cum = hi_nn(low, gates)
    new_c, new_n, new_m, outs = [], [], [], []
    for h in range(HEADS):
        q, k, v = q_all[h][0], k_all[h][0], v_all[h][0]
        pick_i = (lane == h).astype(F32)
        pick_f = (lane == HEADS + h).astype(F32)
        li_col = jnp.sum(gates * pick_i, axis=1, keepdims=True)
        lf_col = jnp.sum(gates * pick_f, axis=1, keepdims=True)
        b_col = jnp.sum(gates_cum * pick_f, axis=1, keepdims=True)
        by_key = hi_nt(pick_i, gates) - hi_nt(pick_f, gates_cum)
        d_mat = jnp.where(causal, b_col + by_key, NEG)
        m_prev = lax.stop_gradient(jnp.max(m_st[h], axis=1, keepdims=True))
        g_inter = b_col + m_prev
        m_t = lax.stop_gradient(jnp.maximum(g_inter, jnp.max(d_mat, axis=1, keepdims=True)))
        w_inter = jnp.exp(g_inter - m_t)
        aw = jnp.exp(d_mat - m_t) * mm_nt(q, k)
        num = w_inter * mm_nn(q, c_st[h]) + mm_nn(aw, v)
        den = w_inter * row_dot(q, n_st[h]) + jnp.sum(aw, axis=1, keepdims=True)
        outs.append(num / jnp.maximum(jnp.abs(den), jnp.exp(-m_t)))
        b_end = jnp.sum(lf_col, axis=0, keepdims=True)
        g_state = b_end + m_prev
        s_w = b_end - b_col + li_col
        m_next = lax.stop_gradient(jnp.maximum(g_state, jnp.max(s_w, axis=0, keepdims=True)))
        dec = jnp.exp(g_state - m_next)
        w_s = jnp.exp(s_w - m_next)
        kw = k * w_s
        new_c.append(dec * c_st[h] + mm_tn(kw, v))
        new_n.append(dec * n_st[h] + col_dot(w_s, k))
        new_m.append(jnp.broadcast_to(m_next, (1, HEAD_W)))
    return new_c + new_n + new_m, [jnp.concatenate(outs, axis=1)]


def lru_fwd(a, u, *, tile, name):
    n_rows, width = a.shape

    def body(a_ref, u_ref, h_ref, carry):
        @pl.when(pl.program_id(0) == 0)
        def _():
            carry[...] = jnp.zeros(carry.shape, F32)

        h = carry[...]
        for t in range(tile):
            h = a_ref[t:t + 1, :] * h + u_ref[t:t + 1, :]
            h_ref[t:t + 1, :] = h
        carry[...] = h

    spec = pl.BlockSpec((tile, width), lambda i: (i, 0))
    return pl.pallas_call(
        body, name=name, grid=(n_rows // tile,), in_specs=[spec, spec], out_specs=spec,
        out_shape=jax.ShapeDtypeStruct((n_rows, width), F32),
        scratch_shapes=[pltpu.VMEM((1, width), F32)],
        compiler_params=_params(dimension_semantics=("arbitrary",)),
    )(a, u)


def lru_bwd(a, h, dh, *, tile, name):
    n_rows, width = a.shape
    n_tiles = n_rows // tile
    hb = tile // 8

    def body(a_ref, h_ref, hh_ref, dh_ref, da_ref, du_ref, carry):
        i = pl.program_id(0)

        @pl.when(i == 0)
        def _():
            carry[...] = jnp.zeros(carry.shape, F32)

        c = carry[...]
        for t in range(tile - 1, -1, -1):
            g = dh_ref[t:t + 1, :] + c
            du_ref[t:t + 1, :] = g
            if t:
                h_prev = h_ref[t - 1:t, :]
            else:
                h_prev = jnp.where(i == n_tiles - 1, 0.0, hh_ref[7:8, :])
            da_ref[t:t + 1, :] = g * h_prev
            c = a_ref[t:t + 1, :] * g
        carry[...] = c

    rev = lambda i: (n_tiles - 1 - i, 0)
    spec = pl.BlockSpec((tile, width), rev)
    halo = pl.BlockSpec((8, width), lambda i: (jnp.maximum((n_tiles - 1 - i) * hb - 1, 0), 0))
    return pl.pallas_call(
        body, name=name, grid=(n_tiles,), in_specs=[spec, spec, halo, spec], out_specs=[spec, spec],
        out_shape=[jax.ShapeDtypeStruct((n_rows, width), F32)] * 2,
        scratch_shapes=[pltpu.VMEM((1, width), F32)],
        compiler_params=_params(dimension_semantics=("arbitrary",)),
    )(a, h, h, dh)


def _layer_norm(z, g, b):
    mu = jnp.mean(z, axis=-1, keepdims=True)
    zc = z - mu
    var = jnp.mean(zc * zc, axis=-1, keepdims=True)
    return zc * lax.rsqrt(var + 1e-5) * g + b


def pre_fn(rows, params):
    (x,), (scale, shift) = rows, params
    return [x * (1.0 + scale) + shift], []


def make_post_fn(weight, with_next):
    def fn(rows, params):
        x, y = rows
        gate, g, b = params[:3]
        xo = _layer_norm(ALPHA * x + weight * (1.0 + gate) * y, g, b)
        if with_next:
            return [xo, xo * (1.0 + params[3]) + params[4]], []
        return [xo], []

    return fn


def make_last_fn(weight):
    post = make_post_fn(weight, False)

    def fn(rows, params):
        x, y, target = rows
        err = post([x, y], params)[0][0] - target
        loss = 0.5 * jnp.sum(jnp.mean(err * err, axis=-1, keepdims=True), axis=0, keepdims=True)
        return [err * (1.0 / D_MODEL)], [jnp.broadcast_to(loss, (1, 128))]

    return fn


def mix_a_fn(rows, params):
    a_q, a_f, xconv, graw = rows
    l0, l1, l2, wq, wk, gate_b = params
    mx = jnp.maximum(jnp.maximum(l0, l1), l2)
    e0, e1, e2 = jnp.exp(l0 - mx), jnp.exp(l1 - mx), jnp.exp(l2 - mx)
    lb = e0 / (e0 + e1 + e2)
    f = lb + (1.0 - lb) * jax.nn.sigmoid(a_f)
    xc = _silu(xconv)
    q_b = mm_nn(xc, wq)
    k_b = mm_nn(xc, wk) * (HEAD_W ** -0.5)
    g = graw + gate_b
    lane = lax.broadcasted_iota(jnp.int32, g.shape, 1)
    gates = jnp.where(lane < HEADS, g, _log_sigmoid(g))
    return [_silu(a_q), 1.0 - f, jnp.log(f), xc, q_b, k_b, gates], []


def _head_norm(v, g, center):
    if center:
        v = v - jnp.mean(v, axis=-1, keepdims=True)
    return v * lax.rsqrt(jnp.mean(v * v, axis=-1, keepdims=True) + 1e-6) * g


def mix_b_fn(rows, params):
    o_a, a_g, h_b, xc, b_z = rows
    hg, mg, skip = params
    y_a = [_head_norm(o_a[h], hg[h], False) * _silu(a_g[h]) for h in range(HEADS)]
    y_b = [(_head_norm(h_b[h], mg[h], True) + skip[h] * xc[h]) * _silu(b_z[h]) for h in range(HEADS)]
    return [y_a + y_b], []


def lru_a_fn(rows, params):
    (xr,) = rows
    wa, wx, ba, bx, lam = params
    a_out, u_out = [], []
    for n in range(C_BLOCKS):
        r = jax.nn.sigmoid(mm_nt(xr[n], wa[n]) + ba[n])
        i = jax.nn.sigmoid(mm_nt(xr[n], wx[n]) + bx[n])
        log_a = -RG_C * r * _softplus(-lam[n])
        a_out.append(jnp.exp(log_a))
        u_out.append(jnp.sqrt(_neg_expm1(2.0 * log_a)) * i * xr[n])
    return [a_out, u_out], []


def lru_b_fn(rows, params):
    h, y_br = rows
    return [h * jax.nn.gelu(y_br)], []


def swiglu_epi(accs, extras):
    h1, h3 = accs
    return [h1, h3, _silu(h1) * h3]


def swiglu_bwd_epi(accs, extras):
    (da,), (h1, h3) = accs, extras
    h1, h3 = h1.astype(F32), h3.astype(F32)
    sig = jax.nn.sigmoid(h1)
    return [da * h3 * sig * (1.0 + h1 * (1.0 - sig)), da * h1 * sig]


ROW_TILE = 512


def _chunks(v, n):
    return v.reshape(n, 1, v.shape[-1] // n)


def _dense_blocks(w):
    n, b, _ = w.shape
    by_row = jnp.swapaxes(w, 1, 2).reshape(n * b, b)
    spread = jnp.dot(by_row, _column_picker(n, b).T, precision=HIGHEST)
    return spread * _block_mask(n, b)


def _column_picker(n, b):
    return jnp.asarray(np.tile(np.eye(b, dtype=np.float32), (n, 1)))


def _block_mask(n, b):
    return jnp.asarray(np.kron(np.eye(n, dtype=np.float32), np.ones((b, b), np.float32)))


def _block_diag_of(m, n, b):
    by_row = jnp.dot(m * _block_mask(n, b), _column_picker(n, b), precision=HIGHEST)
    return jnp.swapaxes(by_row.reshape(n, b, b), 1, 2)


def local_step(x, target, mod, w, get_weights, put_grads, put_small, first_after=()):
    s_len = x.shape[0]
    tile = min(ROW_TILE, s_len)
    row = lambda v: v.reshape(1, -1)
    mrow = lambda l, j, k: mod[l, 3 * j + k].reshape(1, D_MODEL)
    g = {}
    d_mod = [[None] * 9 for _ in range(DEPTH)]
    d_ln_g = [[None] * 3 for _ in range(DEPTH)]
    d_ln_b = [[None] * 3 for _ in range(DEPTH)]
    subs = [(l, j) for l in range(DEPTH) for j in range(3)]
    weight_of = lambda j: 1.0 if j == 1 else FFN_RES_W

    wq_d = _dense_blocks(w["mlstm_wq"])
    wk_d = _dense_blocks(w["mlstm_wk"])
    wa_b, wx_b = w["rglru_wa"].astype(BF16), w["rglru_wx"].astype(BF16)
    gate_b = jnp.pad(w["mlstm_gate_b"].reshape(1, 8), ((0, 0), (0, 120)))
    lb_rows = [row(w["hgrn_lb_logits"][k]) for k in range(3)]
    mix_a_params = lb_rows + [wq_d, wk_d, gate_b]
    mix_b_params = [_chunks(row(w["hgrn_norm_g"]), HEADS), _chunks(row(w["mlstm_norm_g"]), HEADS),
                    _chunks(row(w["mlstm_skip"]), HEADS)]
    lru_a_params = [wa_b, wx_b, _chunks(row(w["rglru_ba"]), C_BLOCKS), _chunks(row(w["rglru_bx"]), C_BLOCKS),
                    _chunks(row(w["rglru_lambda"]), C_BLOCKS)]
    mconv_w, mconv_b = w["mlstm_conv_w"], row(w["mlstm_conv_b"])
    rconv_w, rconv_b = w["rglru_conv_w"], row(w["rglru_conv_b"])
    hg_piece = [(HEAD_W, SUB)] * 4
    ml_piece = [(HEAD_W, CHUNK)] * 3 + [(128, CHUNK)]

    (t,), _ = rowwise(pre_fn, [Rows(x)], [mrow(0, 0, 1), mrow(0, 0, 0)], [(D_MODEL, BF16)], [],
                      tile=tile, name="pre", after=tuple(first_after))
    saved = {}
    x_in = x
    for idx, (l, j) in enumerate(subs):
        sv = {"x": x_in, "t": t}
        if j != 1:
            (w1, w3, w2), begun = get_weights(("ffn", l, j // 2), t)
            h1, h3, act = matmul([[(t, w1)], [(t, w3)]], "nn", [BF16, BF16, BF16], tm=512, tn=1408, tk=D_MODEL,
                                 epi=swiglu_epi, name="ffn_up", n_outer=True, after=begun)
            (y,) = matmul([[(act, w2)]], "nn", [F32], tm=1024, tn=1024, tk=D_FF, name="ffn_down")
            sv.update(h1=h1, h3=h3, act=act, big=(w1, w3, w2))
        elif l == 0:
            (ab_w_in, ab_w_out), begun = get_weights(("ab",), t)
            sv["big"] = (ab_w_in, ab_w_out)
            (proj,) = matmul([[(t, ab_w_in)]], "nn", [F32], tm=256, tn=AB_ALL, tk=D_MODEL, name="ab_in", after=begun)
            xconv = conv_fwd(Rows(proj, (4, MIX_W)), mconv_w, mconv_b, tile=tile, name="mconv")
            a_rows = [Rows(proj, (0, MIX_W)), Rows(proj, (1, MIX_W)), Rows(xconv), Rows(proj, (AB_MAIN // 128, 128))]
            (q_a, k_a, lf_a, xc, q_b, k_b, gates), _ = rowwise(
                mix_a_fn, a_rows, mix_a_params, [(MIX_W, F32)] * 6 + [(128, F32)], [], tile=tile, name="mix_a")
            hg_xs = [Rows(q_a), Rows(k_a), Rows(proj, (2, MIX_W)), Rows(lf_a)]
            (o_a,), hg_kept = chunk_scan_fwd(hgrn2_step, hg_xs, hg_piece, [(HEAD_W, HEAD_W)] * HEADS, [MIX_W],
                                             name="hgrn2_fwd")
            ml_xs = [Rows(q_b), Rows(k_b), Rows(proj, (5, MIX_W)), Rows(gates)]
            ml_states = [(HEAD_W, HEAD_W)] * HEADS + [(1, HEAD_W)] * (2 * HEADS)
            (h_b,), ml_kept = chunk_scan_fwd(mlstm_step, ml_xs, ml_piece, ml_states, [MIX_W], name="mlstm_fwd")
            b_rows = [Rows(o_a, split=HEAD_W), Rows(proj, (3, MIX_W), HEAD_W), Rows(h_b, split=HEAD_W),
                      Rows(xc, split=HEAD_W), Rows(proj, (6, MIX_W), HEAD_W)]
            (ycat,), _ = rowwise(mix_b_fn, b_rows, mix_b_params, [(2 * MIX_W, BF16)], [], tile=tile, name="mix_b")
            (y,) = matmul([[(ycat, ab_w_out)]], "nn", [F32], tm=512, tn=1024, tk=D_MODEL, name="ab_out")
            sv.update(proj=proj, xconv=xconv, a_rows=a_rows, hg_xs=hg_xs, hg_kept=hg_kept, ml_xs=ml_xs,
                      ml_kept=ml_kept, b_rows=b_rows, ycat=ycat)
        else:
            (rg_w_in, rg_w_out), begun = get_weights(("rg",), t)
            sv["big"] = (rg_w_in, rg_w_out)
            (proj,) = matmul([[(t, rg_w_in)]], "nn", [F32], tm=512, tn=1024, tk=D_MODEL, name="rg_in", after=begun)
            xr = conv_fwd(Rows(proj, (1, D_MODEL)), rconv_w, rconv_b, tile=tile, name="rconv")
            (a_t, u_t), _ = rowwise(lru_a_fn, [Rows(xr, split=128)], lru_a_params, [(D_MODEL, F32)] * 2, [],
                                    tile=tile, name="lru_a")
            h = lru_fwd(a_t, u_t, tile=min(128, s_len), name="lru_fwd")
            b_rows = [Rows(h), Rows(proj, (0, D_MODEL))]
            (hgate,), _ = rowwise(lru_b_fn, b_rows, [], [(D_MODEL, BF16)], [], tile=tile, name="lru_b")
            (y,) = matmul([[(hgate, rg_w_out)]], "nn", [F32], tm=512, tn=1024, tk=D_MODEL, name="rg_out")
            sv.update(proj=proj, xr=xr, a_t=a_t, h=h, b_rows=b_rows, hgate=hgate)
        sv["y"] = y
        post_params = [mrow(l, j, 2), row(w["ln_g"][l, j]), row(w["ln_b"][l, j])]
        if idx + 1 < len(subs):
            nl, nj = subs[idx + 1]
            post_params += [mrow(nl, nj, 1), mrow(nl, nj, 0)]
            (x_out, t), _ = rowwise(make_post_fn(weight_of(j), True), [Rows(x_in), Rows(y)], post_params,
                                    [(D_MODEL, F32), (D_MODEL, BF16)], [], tile=tile, name="post")
        else:
            (d_xo,), (loss_row,) = rowwise(make_last_fn(weight_of(j)), [Rows(x_in), Rows(y), Rows(target)],
                                           post_params, [(D_MODEL, F32)], [(1, 128)], tile=tile, name="post_loss")
            x_out = None
        sv["post_params"] = post_params
        saved[(l, j)] = sv
        x_in = x_out
    loss = loss_row[0, 0]

    d_t_next = None
    sent = []
    deferred = []

    def hand_over(key, grads):
        token = put_grads(key, grads)
        if token is not None:
            sent.append(token)

    for idx in range(len(subs) - 1, -1, -1):
        l, j = subs[idx]
        sv = saved[(l, j)]
        has_next = idx + 1 < len(subs)
        cots = [Rows(d_xo)] + ([Rows(d_t_next)] if has_next else [])
        want_p = [0, 1, 2] + ([3, 4] if has_next else [])
        (d_xres, d_y), d_par = rowwise_bwd(
            make_post_fn(weight_of(j), has_next), [Rows(sv["x"]), Rows(sv["y"])], cots, sv["post_params"],
            [0, 1], want_p, [F32, BF16], tile=tile, name="post_bwd", after=tuple(sent))
        sent.clear()
        d_mod[l][3 * j + 2], d_ln_g[l][j], d_ln_b[l][j] = d_par[:3]
        if has_next:
            nl, nj = subs[idx + 1]
            d_mod[nl][3 * nj + 1], d_mod[nl][3 * nj] = d_par[3:]
        t = sv["t"]
        if j != 1:
            w1, w3, w2 = sv["big"]
            d_h1, d_h3 = matmul([[(d_y, w2)]], "nt", [BF16, BF16], tm=512, tn=1408, tk=D_MODEL,
                                extras=[sv["h1"], sv["h3"]], epi=swiglu_bwd_epi, name="ffn_down_bwd", n_outer=True)

            def ffn_weight_grads(after, ops=(sv["act"], t, d_y, d_h1, d_h3), key=("ffn", l, j // 2)):
                wait = () if after is None else (after,)
                act, t_in, d_out, d_1, d_3 = ops
                (g_w2,) = matmul([[(act, d_out)]], "tn", [GRAD_WIRE], tm=1408, tn=1024, tk=4096, name="ffn_dw2",
                                 after=wait)
                (g_w1,) = matmul([[(t_in, d_1)]], "tn", [GRAD_WIRE], tm=1024, tn=1408, tk=4096, name="ffn_dw1",
                                 after=wait)
                (g_w3,) = matmul([[(t_in, d_3)]], "tn", [GRAD_WIRE], tm=1024, tn=1408, tk=4096, name="ffn_dw3",
                                 after=wait)
                hand_over(key, [g_w1, g_w3, g_w2])

            if idx:
                ffn_weight_grads(None)
            else:
                deferred.append(ffn_weight_grads)
            (d_t,) = matmul([[(d_h1, w1), (d_h3, w3)]], "nt", [F32], tm=1024, tn=1024, tk=1408, name="ffn_up_bwd")
        elif l == 0:
            ab_w_in, ab_w_out = sv["big"]
            (d_ycat,) = matmul([[(d_y, ab_w_out)]], "nt", [F32], tm=512, tn=1024, tk=D_MODEL, name="ab_out_bwd")
            (g_out,) = matmul([[(sv["ycat"], d_y)]], "tn", [GRAD_WIRE], tm=1024, tn=1024, tk=2048, name="ab_dwout")
            (d_oa, d_ag, d_hb, d_xc, d_bz), (d_hg, d_mg, d_skip) = rowwise_bwd(
                mix_b_fn, sv["b_rows"], [Rows(d_ycat, split=HEAD_W)], mix_b_params, [0, 1, 2, 3, 4], [0, 1, 2],
                [F32, BF16, F32, F32, BF16], tile=tile, name="mix_b_bwd")
            g["hgrn_norm_g"], g["mlstm_norm_g"], g["mlstm_skip"] = (v.reshape(1, MIX_W) for v in (d_hg, d_mg, d_skip))
            d_qb, d_kb, d_bv, d_gates = chunk_scan_bwd(mlstm_step, sv["ml_xs"], ml_piece, sv["ml_kept"], [d_hb],
                                                       name="mlstm_bwd")
            d_qa, d_ka, d_ai, d_lf = chunk_scan_bwd(hgrn2_step, sv["hg_xs"], hg_piece, sv["hg_kept"], [d_oa],
                                                    name="hgrn2_bwd")
            a_cots = [Rows(v) for v in (d_qa, d_ka, d_lf, d_xc, d_qb, d_kb, d_gates)]
            (d_aq, d_af, d_xconv, d_graw), (d_l0, d_l1, d_l2, d_wq, d_wk, d_gb) = rowwise_bwd(
                mix_a_fn, sv["a_rows"], a_cots, mix_a_params, [0, 1, 2, 3], [0, 1, 2, 3, 4, 5],
                [BF16, BF16, F32, BF16], tile=tile, name="mix_a_bwd")
            g["hgrn_lb_logits"] = jnp.concatenate([d_l0, d_l1, d_l2], axis=0)
            g["mlstm_wq"] = _block_diag_of(d_wq, MIX_W // 4, 4)
            g["mlstm_wk"] = _block_diag_of(d_wk, MIX_W // 4, 4)
            g["mlstm_gate_b"] = d_gb[:, :8]
            d_bx, g["mlstm_conv_w"], g["mlstm_conv_b"] = conv_bwd(Rows(sv["proj"], (4, MIX_W)), d_xconv, mconv_w,
                                                                  tile=tile, name="mconv_bwd")
            d_proj = jnp.concatenate([d_aq, d_af, d_ai.astype(BF16), d_ag, d_bx, d_bv.astype(BF16), d_bz, d_graw],
                                     axis=1)
            (g_in,) = matmul([[(t, d_proj)]], "tn", [GRAD_WIRE], tm=256, tn=AB_ALL, tk=1024, name="ab_dwin")
            hand_over(("ab",), [g_in, g_out])
            (d_t,) = matmul([[(d_proj, ab_w_in)]], "nt", [F32], tm=512, tn=1024, tk=AB_ALL, name="ab_in_bwd")
        else:
            rg_w_in, rg_w_out = sv["big"]
            (d_hgate,) = matmul([[(d_y, rg_w_out)]], "nt", [F32], tm=512, tn=1024, tk=D_MODEL,
                                name="rg_out_bwd")
            (g_out,) = matmul([[(sv["hgate"], d_y)]], "tn", [GRAD_WIRE], tm=1024, tn=1024, tk=2048,
                                         name="rg_dwout")
            (d_h, d_ybr), _ = rowwise_bwd(lru_b_fn, sv["b_rows"], [Rows(d_hgate)], [], [0, 1], [], [F32, BF16],
                                          tile=tile, name="lru_b_bwd")
            d_a, d_u = lru_bwd(sv["a_t"], sv["h"], d_h, tile=min(128, s_len), name="lru_bwd")
            (d_xr,), (d_wa, d_wx, d_ba, d_bx_, d_lam) = rowwise_bwd(
                lru_a_fn, [Rows(sv["xr"], split=128)], [Rows(d_a, split=128), Rows(d_u, split=128)], lru_a_params,
                [0], [0, 1, 2, 3, 4], [F32], tile=tile, name="lru_a_bwd")
            g["rglru_wa"], g["rglru_wx"] = d_wa, d_wx
            g["rglru_ba"], g["rglru_bx"], g["rglru_lambda"] = (v.reshape(1, D_MODEL) for v in (d_ba, d_bx_, d_lam))
            d_xbr, g["rglru_conv_w"], g["rglru_conv_b"] = conv_bwd(Rows(sv["proj"], (1, D_MODEL)), d_xr, rconv_w,
                                                                   tile=tile, name="rconv_bwd")
            d_proj = jnp.concatenate([d_ybr, d_xbr], axis=1)
            (g_in,) = matmul([[(t, d_proj)]], "tn", [GRAD_WIRE], tm=1024, tn=1024, tk=2048, name="rg_dwin")
            hand_over(("rg",), [g_in, g_out])
            (d_t,) = matmul([[(d_proj, rg_w_in)]], "nt", [F32], tm=512, tn=1024, tk=1024, name="rg_in_bwd")
        d_xo, d_t_next = d_xres, d_t

    def first_bwd(rows, params):
        x0, d_res, d_t0 = rows
        _, vjp = jax.vjp(lambda r, p: pre_fn(r, p)[0], [x0], params)
        (d_x0,), d_p = vjp([d_t0])
        return [d_res + d_x0], d_p

    (grad_x,), (d_mod[0][1], d_mod[0][0]) = rowwise(
        first_bwd, [Rows(x), Rows(d_xo), Rows(d_t_next)], [mrow(0, 0, 1), mrow(0, 0, 0)], [(D_MODEL, F32)],
        [(1, D_MODEL)] * 2, tile=tile, name="pre_bwd", after=tuple(sent))
    sent.clear()
    g["ln_g"] = jnp.stack([jnp.concatenate(r, axis=0) for r in d_ln_g])
    g["ln_b"] = jnp.stack([jnp.concatenate(r, axis=0) for r in d_ln_b])
    d_mod = jnp.stack([jnp.concatenate(r, axis=0) for r in d_mod])
    small_sent = put_small(g, d_mod)
    for weight_grads in deferred:
        weight_grads(small_sent)
    return loss, grad_x, d_mod, g, list(sent)


MESH_ID = pl.DeviceIdType.MESH
ANY_SPEC = pl.BlockSpec(memory_space=pl.ANY)


def _my_position():
    return lax.axis_index("x"), lax.axis_index("y"), lax.axis_index("c")


def _flat_index(pos):
    return 4 * pos[0] + 2 * pos[1] + pos[2]


def _peer_position(pos, k):
    return tuple(lax.rem(p + ((k >> s) & 1), 2) for p, s in zip(pos, (2, 1, 0)))


def _exchange(x, gather, name):
    out_shape = (NDEV,) + x.shape if gather else x.shape

    def body(x_ref, o_ref, send_sems, recv_sems, local_sem):
        pos = _my_position()
        me = _flat_index(pos)
        local = pltpu.make_async_copy(x_ref if gather else x_ref.at[me], o_ref.at[me], local_sem)
        local.start()
        copies = []
        for k in range(1, NDEV):
            peer = _peer_position(pos, k)
            src = x_ref if gather else x_ref.at[_flat_index(peer)]
            copies.append(pltpu.make_async_remote_copy(
                src_ref=src, dst_ref=o_ref.at[me], send_sem=send_sems.at[k - 1], recv_sem=recv_sems.at[k - 1],
                device_id=peer, device_id_type=MESH_ID))
            copies[-1].start()
        for cp in copies:
            cp.wait()
        local.wait()

    return pl.pallas_call(
        body, name=name, in_specs=[ANY_SPEC], out_specs=ANY_SPEC,
        out_shape=jax.ShapeDtypeStruct(out_shape, x.dtype),
        scratch_shapes=[pltpu.SemaphoreType.DMA((NDEV - 1,)), pltpu.SemaphoreType.DMA((NDEV - 1,)),
                        pltpu.SemaphoreType.DMA],
    )(x)


HBM_SPEC = pl.BlockSpec(memory_space=pltpu.HBM)
SEM_SPEC = pl.BlockSpec(memory_space=pltpu.SEMAPHORE)
SIDE_EFFECT = pltpu.SideEffectType.DATAFLOW_SIDE_EFFECTING


def _exchange_copies(x_refs, land_refs, send_sems, recv_sems, gather):
    pos = _my_position()
    me = _flat_index(pos)
    copies = []
    for k in range(1, NDEV):
        peer = _peer_position(pos, k)
        for x_ref, land_ref, s_sem, r_sem in zip(x_refs, land_refs, send_sems, recv_sems):
            src = x_ref if gather else x_ref.at[_flat_index(peer)]
            copies.append(pltpu.make_async_remote_copy(src_ref=src, dst_ref=land_ref.at[me], send_sem=s_sem,
                                                       recv_sem=r_sem, device_id=peer, device_id_type=MESH_ID))
    return copies


def exchange_start(xs, gather, name):
    n = len(xs)
    land_shapes = [(NDEV,) + x.shape if gather else x.shape for x in xs]

    def body(*refs):
        x_refs, land_refs = refs[:n], refs[n:2 * n]
        send_sems, recv_sems = refs[2 * n:3 * n], refs[3 * n:4 * n]
        token = refs[-1]
        for cp in _exchange_copies(x_refs, land_refs, send_sems, recv_sems, gather):
            cp.start()
        token[...] = jnp.zeros(token.shape, token.dtype)

    sem = pltpu.SemaphoreType.DMA(())
    res = pl.pallas_call(
        body, name=name,
        out_shape=[sem] * (2 * n) + [pltpu.HBM(x.shape, x.dtype) for x in xs]
        + [pltpu.HBM(s, x.dtype) for s, x in zip(land_shapes, xs)] + [jax.ShapeDtypeStruct((8, 128), F32)],
        in_specs=[HBM_SPEC] * (2 * n),
        out_specs=[SEM_SPEC] * (2 * n) + [HBM_SPEC] * (2 * n) + [pl.BlockSpec(memory_space=pltpu.VMEM)],
        input_output_aliases={i: 2 * n + i for i in range(2 * n)},
        compiler_params=pltpu.CompilerParams(has_side_effects=SIDE_EFFECT),
    )(*[pltpu.with_memory_space_constraint(x, pltpu.HBM) for x in xs],
      *[pltpu.with_memory_space_constraint(lax.empty(s, x.dtype), pltpu.HBM) for s, x in zip(land_shapes, xs)])
    return (res[:n], res[n:2 * n], res[2 * n:3 * n], res[3 * n:4 * n]), res[-1]


def exchange_wait(handles, after, name):
    send_sems, recv_sems, x_thru, land_thru = handles
    n = len(x_thru)
    after = jax.tree_util.tree_leaves(after)

    def body(*refs):
        land_refs = refs[n:2 * n]
        s_sems, r_sems = refs[2 * n:3 * n], refs[3 * n:4 * n]
        pos = _my_position()
        for land_ref, s_sem, r_sem in zip(land_refs, s_sems, r_sems):
            seven = land_ref.at[pl.ds(0, NDEV - 1)]
            all_seven = pltpu.make_async_remote_copy(src_ref=seven, dst_ref=seven, send_sem=s_sem, recv_sem=r_sem,
                                                     device_id=pos, device_id_type=MESH_ID)
            all_seven.wait_send()
            all_seven.wait_recv()

    res = pl.pallas_call(
        body, name=name,
        out_shape=[pltpu.HBM(x.shape, x.dtype) for x in x_thru] + [pltpu.HBM(x.shape, x.dtype) for x in land_thru],
        in_specs=[HBM_SPEC] * (2 * n) + [SEM_SPEC] * (2 * n) + [ANY_SPEC] * len(after),
        out_specs=[HBM_SPEC] * (2 * n),
        input_output_aliases={i: i for i in range(2 * n)},
        compiler_params=pltpu.CompilerParams(has_side_effects=SIDE_EFFECT),
    )(*x_thru, *land_thru, *send_sems, *recv_sems, *after)
    return res[:n], res[n:]


def all_gather(x, name):
    return _exchange(x, True, name)


def all_to_all(x, name):
    return _exchange(x, False, name)


def _row_tile(n_rows, cap):
    best = None
    for t in range(8, min(n_rows, cap) + 1, 8):
        if n_rows % t == 0:
            best = t
    return best if best else n_rows


def adamw(w, m, v, slots, *, name, index=(), prev=None):
    n_rows, width = w.shape[-2:]
    n_lead = w.ndim - 2
    assert len(index) == n_lead
    n_slots = slots.shape[0]
    lanes = -(-width // 128) * 128
    tile = _row_tile(n_rows, max(8, (1 << 20) // (4 * lanes) // 8 * 8))
    bc1 = 1.0 - ADAM_B1 ** ADAM_STEP
    bc2 = 1.0 - ADAM_B2 ** ADAM_STEP

    def body(w_ref, m_ref, v_ref, s_ref, *rest):
        g_ref, d_ref, nm_ref, nv_ref = rest[-4:]
        g = s_ref[0].astype(F32)
        for k in range(1, n_slots):
            g = g + s_ref[k].astype(F32)
        wv = w_ref[...]
        nm = ADAM_B1 * m_ref[...] + (1.0 - ADAM_B1) * g
        nv = ADAM_B2 * v_ref[...] + (1.0 - ADAM_B2) * (g * g)
        g_ref[...] = g
        nm_ref[...] = nm
        nv_ref[...] = nv
        d_ref[...] = -ADAM_LR * ((nm / bc1) / (jnp.sqrt(nv / bc2) + ADAM_EPS) + ADAM_WD * wv)

    spec = pl.BlockSpec((None,) * n_lead + (tile, width), lambda i: tuple(index) + (i, 0))
    prev = list(prev) if prev is not None else []
    return pl.pallas_call(
        body, name=name, grid=(n_rows // tile,),
        in_specs=[spec, spec, spec, pl.BlockSpec((n_slots, tile, width), lambda i: (0, i, 0))]
        + [ANY_SPEC] * len(prev),
        out_specs=[spec] * 4, out_shape=[jax.ShapeDtypeStruct(w.shape, F32)] * 4,
        input_output_aliases={4 + k: k for k in range(len(prev))},
        compiler_params=_params(dimension_semantics=("parallel",)),
    )(w, m, v, slots, *prev)


def sum_slots(slots, *, name):
    n_slots, n_rows, width = slots.shape

    def body(s_ref, o_ref):
        @pl.when(pl.program_id(0) == 0)
        def _():
            o_ref[...] = s_ref[...]

        @pl.when(pl.program_id(0) > 0)
        def _():
            o_ref[...] += s_ref[...]

    return pl.pallas_call(
        body, name=name, grid=(n_slots,),
        in_specs=[pl.BlockSpec((None, n_rows, width), lambda k: (k, 0, 0))],
        out_specs=pl.BlockSpec((n_rows, width), lambda k: (0, 0)),
        out_shape=jax.ShapeDtypeStruct((n_rows, width), F32),
        compiler_params=_params(dimension_semantics=("arbitrary",)),
    )(slots)


def adamw_nd(w, m, v, slots, *, name):
    shp = w.shape
    two = (-1, shp[-1])
    res = adamw(w.reshape(two), m.reshape(two), v.reshape(two), slots.reshape((slots.shape[0],) + (w.size // shp[-1], shp[-1])),
                name=name)
    return [r.reshape(shp) for r in res]


def _pack(arrs):
    parts = []
    for a in arrs:
        flat = a.reshape(-1).astype(F32)
        parts.append(jnp.pad(flat, (0, (-flat.shape[0]) % 1024)))
    return jnp.concatenate(parts).reshape(-1, 128)


def _unpack(buf, shapes):
    outs, at = [], 0
    lead = buf.shape[:-2]
    flat = buf.reshape(lead + (-1,))
    for shp in shapes:
        n = int(np.prod(shp))
        outs.append(flat[..., at:at + n].reshape(lead + tuple(shp)))
        at += n + (-n) % 1024
    return outs


ARG_NAMES = ["x", "c", "ada_w", "ada_b", "ln_g", "ln_b", "ffn_w1", "ffn_w3", "ffn_w2", "hgrn_lb_logits", "ab_w_in",
             "ab_w_out", "hgrn_norm_g", "mlstm_conv_w", "mlstm_conv_b", "mlstm_wq", "mlstm_wk", "mlstm_gate_b",
             "mlstm_skip", "mlstm_norm_g", "rglru_w_in", "rglru_conv_w", "rglru_conv_b", "rglru_wa", "rglru_ba",
             "rglru_wx", "rglru_bx", "rglru_lambda", "rglru_w_out", "loss_target"]
WEIGHTS = ARG_NAMES[2:-1]
BIG = ["ffn_w1", "ffn_w3", "ffn_w2", "ab_w_in", "ab_w_out", "rglru_w_in", "rglru_w_out"]
REPLICATED = ["ada_b", "hgrn_lb_logits", "hgrn_norm_g", "mlstm_conv_b", "mlstm_wq", "mlstm_wk", "mlstm_gate_b",
              "mlstm_skip", "mlstm_norm_g", "rglru_wa", "rglru_wx"]
SHARDED_SMALL = ["ln_g", "ln_b", "mlstm_conv_w", "rglru_conv_w", "rglru_conv_b", "rglru_ba", "rglru_bx", "rglru_lambda"]


def _unshard_last(gathered):
    moved = jnp.moveaxis(gathered, 0, -2)
    return moved.reshape(moved.shape[:-2] + (NDEV * moved.shape[-1],))


def _shard_last(full):
    split = full.reshape(full.shape[:-1] + (NDEV, full.shape[-1] // NDEV))
    return jnp.moveaxis(split, -2, 0)


def kernel(x, c, ada_w, ada_b, ln_g, ln_b, ffn_w1, ffn_w3, ffn_w2, hgrn_lb_logits, ab_w_in, ab_w_out, hgrn_norm_g, mlstm_conv_w, mlstm_conv_b, mlstm_wq, mlstm_wk, mlstm_gate_b, mlstm_skip, mlstm_norm_g, rglru_w_in, rglru_conv_w, rglru_conv_b, rglru_wa, rglru_ba, rglru_wx, rglru_bx, rglru_lambda, rglru_w_out, loss_target, m_ada_w, m_ada_b, m_ln_g, m_ln_b, m_ffn_w1, m_ffn_w3, m_ffn_w2, m_hgrn_lb_logits, m_ab_w_in, m_ab_w_out, m_hgrn_norm_g, m_mlstm_conv_w, m_mlstm_conv_b, m_mlstm_wq, m_mlstm_wk, m_mlstm_gate_b, m_mlstm_skip, m_mlstm_norm_g, m_rglru_w_in, m_rglru_conv_w, m_rglru_conv_b, m_rglru_wa, m_rglru_ba, m_rglru_wx, m_rglru_bx, m_rglru_lambda, m_rglru_w_out, v_ada_w, v_ada_b, v_ln_g, v_ln_b, v_ffn_w1, v_ffn_w3, v_ffn_w2, v_hgrn_lb_logits, v_ab_w_in, v_ab_w_out, v_hgrn_norm_g, v_mlstm_conv_w, v_mlstm_conv_b, v_mlstm_wq, v_mlstm_wk, v_mlstm_gate_b, v_mlstm_skip, v_mlstm_norm_g, v_rglru_w_in, v_rglru_conv_w, v_rglru_conv_b, v_rglru_wa, v_rglru_ba, v_rglru_wx, v_rglru_bx, v_rglru_lambda, v_rglru_w_out):
    args = locals()
    p = {n: args[n] for n in ARG_NAMES}
    mom = {n: (args["m_" + n], args["v_" + n]) for n in WEIGHTS}
    me = _flat_index(_my_position())

    keys = [("ffn", 0, 0), ("ab",), ("ffn", 0, 1), ("ffn", 1, 0), ("rg",), ("ffn", 1, 1)]
    names = {("ab",): ("ab_w_in", "ab_w_out"), ("rg",): ("rglru_w_in", "rglru_w_out")}
    for l in range(DEPTH):
        for i in range(2):
            names[("ffn", l, i)] = ("ffn_w1", "ffn_w3", "ffn_w2")

    def part(key):
        return (lambda a: a[key[1], key[2]]) if key[0] == "ffn" else (lambda a: a[0])

    gather_handles = {}

    def landed(handles, own_of, after, name):
        sources, lands = exchange_wait(handles, after, name)
        return [lax.dynamic_update_index_in_dim(ld, own_of(src), me, 0) for src, ld in zip(sources, lands)]

    starts_next = {("ffn", 0, 0): [("ab",)], ("ab",): [("ffn", 0, 1), ("ffn", 1, 0)],
                   ("ffn", 0, 1): [("rg",), ("ffn", 1, 1)]}

    def start_gather(key, after):
        shards = [part(key)(p[n]).astype(BF16) for n in names[key]]
        if after is not None:
            shards, _ = lax.optimization_barrier((shards, after))
        gather_handles[key], token = exchange_start(shards, True, "gather_start_" + "_".join(map(str, key)))
        return token

    def get_weights(key, after):
        got = landed(gather_handles[key], lambda src: src, after, "gather_wait_" + "_".join(map(str, key)))
        tokens = tuple(start_gather(nxt, got) for nxt in starts_next.get(key, []))
        if key[0] == "ffn":
            return (_unshard_last(got[0]), _unshard_last(got[1]), got[2].reshape(D_FF, D_MODEL)), tokens
        w_in = _unshard_last(got[0])
        if key[0] == "ab":
            w_in = jnp.concatenate([w_in[:, :AB_MAIN], jnp.pad(w_in[:, AB_MAIN:], ((0, 0), (0, 120)))], axis=1)
        return (w_in, got[1].reshape(D_MODEL, D_MODEL)), tokens

    scatter_handles = {}

    def put_grads(key, grads):
        if key[0] == "ffn":
            slots = [_shard_last(grads[0]), _shard_last(grads[1]), grads[2].reshape(NDEV, D_FF // NDEV, D_MODEL)]
        else:
            g_in = grads[0][:, :AB_MAIN + 8] if key[0] == "ab" else grads[0]
            slots = [_shard_last(g_in), grads[1].reshape(NDEV, D_MODEL // NDEV, D_MODEL)]
        scatter_handles[key], token = exchange_start(slots, False, "scatter_start_" + "_".join(map(str, key)))
        return token

    sharded_shapes = [p[n].shape for n in SHARDED_SMALL]
    small = all_gather(_pack([p[n] for n in SHARDED_SMALL] + [c]), "gather_small")
    started = [start_gather(keys[0], small)]
    small, _ = lax.optimization_barrier((small, started))
    per_dev = _unpack(small, sharded_shapes + [c.shape])
    full_small = {n: _unshard_last(per_dev[i]) for i, n in enumerate(SHARDED_SMALL)}
    c_all = per_dev[-1].reshape(NDEV, D_MODEL)

    c16 = jnp.pad(c_all, ((0, 8), (0, 0)))
    (c_act,), _ = rowwise(lambda r, q: ([_silu(r[0])], []), [Rows(c16)], [], [(D_MODEL, BF16)], [], tile=16,
                          name="cond_act")
    n_ada = ada_w.shape[-1]
    ada_b_mine = lax.dynamic_slice_in_dim(ada_b, me * n_ada, n_ada, axis=1)
    ada_cols = []
    for l in range(DEPTH):
        bias = jnp.broadcast_to(ada_b_mine[l][None, :], (16, n_ada))
        (cols,) = matmul([[(c_act, ada_w[l])]], "nn", [F32], tm=16, tn=n_ada, tk=D_MODEL, extras=[bias],
                         epi=lambda accs, ex: [accs[0] + ex[0]], name="ada_fwd")
        ada_cols.append(cols[:8])
    ada_mine = all_to_all(jnp.stack(ada_cols, axis=1), "ada_to_owner")
    mod = jnp.moveaxis(ada_mine, 0, 1).reshape(DEPTH, 9, D_MODEL)


    w = {"ln_g": full_small["ln_g"], "ln_b": full_small["ln_b"], "hgrn_lb_logits": hgrn_lb_logits}
    for n in ("hgrn_norm_g", "mlstm_conv_b", "mlstm_wq", "mlstm_wk", "mlstm_gate_b", "mlstm_skip", "mlstm_norm_g",
              "rglru_wa", "rglru_wx"):
        w[n] = p[n][0]
    for n in ("mlstm_conv_w", "rglru_conv_w", "rglru_conv_b", "rglru_ba", "rglru_bx", "rglru_lambda"):
        w[n] = full_small[n][0]

    small_names = REPLICATED + SHARDED_SMALL
    small_handles = []

    def put_small(g_small_parts, d_modulation):
        parts_ = dict(g_small_parts, ada_b=d_modulation.reshape(DEPTH, 9 * D_MODEL))
        handles, token = exchange_start([_pack([parts_[n] for n in small_names])], True, "gather_start_small_grads")
        small_handles.append(handles)
        return token

    loss, grad_x, d_mod, g, last_sent = local_step(x[0], loss_target[0], mod, w, get_weights, put_grads, put_small,
                                                   started)
    loss = lax.psum(loss, ("x", "y", "c"))

    outs = {}
    def update_group(key, after):
        slots = landed(scatter_handles[key], lambda src: lax.dynamic_index_in_dim(src, me, 0, keepdims=False), after,
                       "scatter_wait_" + "_".join(map(str, key)))
        for n, sl in zip(names[key], slots):
            index = key[1:] if key[0] == "ffn" else (0,)
            outs[n] = adamw(p[n], mom[n][0], mom[n][1], sl, name="adamw_" + n, index=index, prev=outs.get(n))
        return [outs[n][0] for n in names[key]]

    full_shapes = [p[n].shape for n in REPLICATED] + [full_small[n].shape for n in SHARDED_SMALL]
    (all_small,) = landed(small_handles[0], lambda src: src, (last_sent, grad_x), "gather_wait_small_grads")
    summed = sum_slots(all_small, name="sum_small_grads")
    g_small = dict(zip(small_names, _unpack(summed, full_shapes)))
    rep = adamw(*[_pack([t[n] for n in REPLICATED]) for t in (p, {n: mom[n][0] for n in WEIGHTS},
                                                               {n: mom[n][1] for n in WEIGHTS})],
                _pack([g_small[n] for n in REPLICATED])[None], name="adamw_replicated")
    rep = [_unpack(r, [p[n].shape for n in REPLICATED]) for r in rep]
    for i, n in enumerate(REPLICATED):
        outs[n] = [r[i] for r in rep]
    g_mine = {n: lax.dynamic_slice_in_dim(g_small[n], me * p[n].shape[-1], p[n].shape[-1], axis=-1)
              for n in SHARDED_SMALL}
    shd = adamw(*[_pack([t[n] for n in SHARDED_SMALL]) for t in (p, {n: mom[n][0] for n in WEIGHTS},
                                                                  {n: mom[n][1] for n in WEIGHTS})],
                _pack([g_mine[n] for n in SHARDED_SMALL])[None], name="adamw_sharded_small")
    done = shd[0]
    for key in keys[:0:-1] + keys[:1]:
        done = update_group(key, done)
    shd = [_unpack(r, sharded_shapes) for r in shd]
    for i, n in enumerate(SHARDED_SMALL):
        outs[n] = [r[i] for r in shd]

    d_ada = all_small[:, :DEPTH * 9 * D_MODEL // 128].reshape(NDEV, DEPTH, 9 * D_MODEL)
    d_mine = lax.dynamic_slice_in_dim(d_ada, me * n_ada, n_ada, axis=2)
    g_ada = []
    for l in range(DEPTH):
        d16 = jnp.pad(d_mine[:, l], ((0, 8), (0, 0)))
        (gl,) = matmul([[(c_act, d16)]], "tn", [F32], tm=D_MODEL, tn=n_ada, tk=16, name="ada_bwd")
        g_ada.append(gl)
    outs["ada_w"] = adamw_nd(ada_w, *mom["ada_w"], jnp.stack(g_ada)[None], name="adamw_ada_w")

    result = [loss, grad_x[None]]
    for k in range(4):
        result += [outs[n][k].reshape(p[n].shape) for n in WEIGHTS]
    return tuple(result)
```
